```python
import jax, jax.numpy as jnp
from jax import lax
import numpy as np


D_MODEL = 2048
BATCH = 8
SEQ = 2048
DEPTH = 1

D_MIX = D_MODEL
NSA_HEADS = 8
NSA_KV_GROUPS = 2
NSA_GROUP_HEADS = NSA_HEADS // NSA_KV_GROUPS
NSA_HEAD_DIM = 128
RET_HEADS = 8
RET_HEAD_DIM = 128
CMP_BLOCK = 32
CMP_STRIDE = 16
SEL_BLOCK = 64
SEL_TOPK = 8
WINDOW = 512
Q_BLOCK = 128
SEL_Q_BLOCK = 64
RET_CHUNK = 128
ROPE_BASE = 10000.0
N_GROUPS = 4
EXPERTS_PER_GROUP = 8
N_EXPERTS = N_GROUPS * EXPERTS_PER_GROUP
TOPK_IN_GROUP = 2
D_EXPERT = 512
MOE_BLOCK = 256
RMS_EPS = 1e-6
GN_EPS = 1e-5
NEG_INF = -1e30
FORCED_SCORE = 1e6
NSA_Q_WIDTH = NSA_HEADS * NSA_HEAD_DIM
NSA_KV_WIDTH = NSA_KV_GROUPS * NSA_HEAD_DIM
NSA_GATE_WIDTH = 3 * NSA_HEADS
RET_WIDTH = RET_HEADS * RET_HEAD_DIM
IN_PROJ_WIDTH = NSA_Q_WIDTH + 6 * NSA_KV_WIDTH + NSA_GATE_WIDTH + 4 * RET_WIDTH

kernel_name = 'hymba_nsa_retnet_hmoe_block'


def rms_norm(x, g, eps=RMS_EPS):
    xf = x.astype(jnp.float32)
    y = xf * lax.rsqrt(jnp.mean(xf * xf, axis=-1, keepdims=True) + eps)
    return (y * g.astype(jnp.float32)).astype(x.dtype)


def masked_softmax(s, mask):
    return jax.nn.softmax(jnp.where(mask, s, NEG_INF), axis=-1)


def nsa_attention(q, k_cmp, v_cmp, k_slc, v_slc, k_win, v_win, gate_logits,
                  cmp_pos, cmp_w1, cmp_w2, q_norm_g, k_norm_g):
    B, S = q.shape[0], q.shape[1]
    G, HG, hd = NSA_KV_GROUPS, NSA_GROUP_HEADS, NSA_HEAD_DIM
    dt = q.dtype
    scale = hd ** -0.5
    pos = jnp.arange(S)
    qg = rms_norm(q, q_norm_g).reshape(B, S, G, HG, hd).transpose(0, 2, 3, 1, 4)

    n_cmp = (S - CMP_BLOCK) // CMP_STRIDE + 1
    cmp_start = jnp.arange(n_cmp) * CMP_STRIDE
    blk_idx = cmp_start[:, None] + jnp.arange(CMP_BLOCK)[None, :]

    def compress(t, i):
        tb = t[:, blk_idx] + cmp_pos[i][None, None, :, None, :]
        hid = jax.nn.silu(jnp.einsum('bnlgd,lde->bnge', tb, cmp_w1[i]))
        return jnp.einsum('bnge,ef->bgnf', hid, cmp_w2[i])

    kc = rms_norm(compress(k_cmp, 0), k_norm_g[0])
    vc = compress(v_cmp, 1)
    s_cmp = jnp.einsum('bghsd,bgnd->bghsn', qg, kc).astype(jnp.float32) * scale
    cmask = (cmp_start + CMP_BLOCK - 1)[None, :] <= pos[:, None]
    p_cmp = masked_softmax(s_cmp, cmask) * cmask
    o_cmp = jnp.einsum('bghsn,bgnd->bghsd', p_cmp.astype(dt), vc)

    n_sblk = S // SEL_BLOCK
    n_sel = min(SEL_TOPK, n_sblk)
    sel_start = jnp.arange(n_sblk) * SEL_BLOCK
    overlap = ((cmp_start[:, None] < (sel_start + SEL_BLOCK)[None, :]) &
               ((cmp_start + CMP_BLOCK)[:, None] > sel_start[None, :])).astype(jnp.float32)
    importance = jnp.einsum('bghsn,nj->bgsj', p_cmp, overlap)
    jb = jnp.arange(n_sblk)[None, :]
    cur = (pos // SEL_BLOCK)[:, None]
    sel_valid = sel_start[None, :] <= pos[:, None]
    forced = (jb == 0) | (jb == cur) | (jb == cur - 1)
    score = jnp.where(sel_valid, jnp.where(forced, FORCED_SCORE, importance), -jnp.inf)
    _, sel_idx = lax.top_k(score, n_sel)

    k_blocks = rms_norm(k_slc, k_norm_g[1]).transpose(0, 2, 1, 3).reshape(B, G, n_sblk, SEL_BLOCK, hd)
    v_blocks = v_slc.transpose(0, 2, 1, 3).reshape(B, G, n_sblk, SEL_BLOCK, hd)
    n_sqc = S // SEL_Q_BLOCK
    q_chunks = qg.reshape(B, G, HG, n_sqc, SEL_Q_BLOCK, hd).transpose(3, 0, 1, 2, 4, 5)
    idx_chunks = sel_idx.reshape(B, G, n_sqc, SEL_Q_BLOCK, n_sel).transpose(2, 0, 1, 3, 4)
    pos_chunks = pos.reshape(n_sqc, SEL_Q_BLOCK)
    bi = jnp.arange(B)[:, None, None, None]
    gi = jnp.arange(G)[None, :, None, None]

    def sel_chunk(args):
        qc, ic, tc = args
        kg = k_blocks[bi, gi, ic].reshape(B, G, SEL_Q_BLOCK, n_sel * SEL_BLOCK, hd)
        vg = v_blocks[bi, gi, ic].reshape(B, G, SEL_Q_BLOCK, n_sel * SEL_BLOCK, hd)
        kpos = (ic[..., None] * SEL_BLOCK + jnp.arange(SEL_BLOCK)).reshape(B, G, SEL_Q_BLOCK, -1)
        m = (kpos <= tc[None, None, :, None])[:, :, None]
        s = jnp.einsum('bghqd,bgqkd->bghqk', qc, kg).astype(jnp.float32) * scale
        p = masked_softmax(s, m)
        return jnp.einsum('bghqk,bgqkd->bghqd', p.astype(dt), vg)

    o_slc = lax.map(sel_chunk, (q_chunks, idx_chunks, pos_chunks))
    o_slc = o_slc.transpose(1, 2, 3, 0, 4, 5).reshape(B, G, HG, S, hd)

    n_qc = S // Q_BLOCK
    n_wb = WINDOW // Q_BLOCK + 1
    kw = rms_norm(k_win, k_norm_g[2]).transpose(0, 2, 1, 3)
    vw = v_win.transpose(0, 2, 1, 3)
    pad = ((0, 0), (0, 0), (WINDOW, 0), (0, 0))
    kp = jnp.pad(kw, pad).reshape(B, G, (S + WINDOW) // Q_BLOCK, Q_BLOCK, hd)
    vp = jnp.pad(vw, pad).reshape(B, G, (S + WINDOW) // Q_BLOCK, Q_BLOCK, hd)
    kband = jnp.concatenate([kp[:, :, i:i + n_qc] for i in range(n_wb)], axis=3)
    vband = jnp.concatenate([vp[:, :, i:i + n_qc] for i in range(n_wb)], axis=3)
    kpos = jnp.arange(n_qc)[:, None] * Q_BLOCK - WINDOW + jnp.arange(n_wb * Q_BLOCK)[None, :]
    qpos = pos.reshape(n_qc, Q_BLOCK)
    kpb, qpb = kpos[:, None, :], qpos[:, :, None]
    wmask = (kpb <= qpb) & (kpb > qpb - WINDOW) & (kpb >= 0)
    qb = qg.reshape(B, G, HG, n_qc, Q_BLOCK, hd)
    s_win = jnp.einsum('bghcqd,bgckd->bghcqk', qb, kband).astype(jnp.float32) * scale
    p_win = masked_softmax(s_win, wmask)
    o_win = jnp.einsum('bghcqk,bgckd->bghcqd', p_win.astype(dt), vband).reshape(B, G, HG, S, hd)

    g = jax.nn.sigmoid(gate_logits).reshape(B, S, G, HG, 3).transpose(0, 2, 3, 1, 4)
    o = o_cmp * g[..., 0:1] + o_slc * g[..., 1:2] + o_win * g[..., 2:3]
    return o.transpose(0, 3, 1, 2, 4).reshape(B, S, NSA_Q_WIDTH)


def retention(q, k, v, gate, gn_g, gn_b):
    B, S, H, hd = q.shape
    dt = gate.dtype
    half = hd // 2
    inv_freq = ROPE_BASE ** (-jnp.arange(half, dtype=jnp.float32) / half)
    ang = jnp.arange(S, dtype=jnp.float32)[:, None] * inv_freq[None, :]
    cos = jnp.cos(ang)[None, :, None, :]
    sin = jnp.sin(ang)[None, :, None, :]

    def rot(t):
        t = t.astype(jnp.float32)
        t1, t2 = t[..., :half], t[..., half:]
        return jnp.concatenate([t1 * cos - t2 * sin, t1 * sin + t2 * cos], axis=-1)

    qf = rot(q)
    kf = rot(k) * (hd ** -0.5)
    vf = v.astype(jnp.float32)
    C = RET_CHUNK
    n_ch = S // C
    log_g = jnp.log1p(-jnp.exp2(-5.0 - jnp.arange(H, dtype=jnp.float32)))
    n = jnp.arange(C, dtype=jnp.float32)
    diff = n[:, None] - n[None, :]
    inner_decay = jnp.where(diff >= 0, jnp.exp(log_g[:, None, None] * jnp.maximum(diff, 0.0)), 0.0)
    xi = jnp.exp(log_g[:, None] * (n + 1.0))[None, :, :, None]
    zeta = jnp.exp(log_g[:, None] * (C - 1.0 - n))[None, :, :, None]
    chunk_decay = jnp.exp(log_g * C)[None, :, None, None]

    def to_chunks(t):
        return t.reshape(B, n_ch, C, H, hd).transpose(1, 0, 3, 2, 4)

    def step(R, qkv):
        qc, kc, vc = qkv
        a = jnp.einsum('bhnd,bhmd->bhnm', qc, kc) * inner_decay
        o = jnp.einsum('bhnm,bhme->bhne', a, vc) + jnp.einsum('bhnd,bhde->bhne', qc, R) * xi
        R = R * chunk_decay + jnp.einsum('bhmd,bhme->bhde', kc * zeta, vc)
        return R, o

    R0 = jnp.zeros((B, H, hd, hd), jnp.float32)
    _, o = lax.scan(step, R0, (to_chunks(qf), to_chunks(kf), to_chunks(vf)))
    o = o.transpose(1, 0, 3, 2, 4)
    mu = jnp.mean(o, axis=-1, keepdims=True)
    var = jnp.mean(jnp.square(o - mu), axis=-1, keepdims=True)
    y = (o - mu) * lax.rsqrt(var + GN_EPS) * gn_g.astype(jnp.float32) + gn_b.astype(jnp.float32)
    y = y.reshape(B, S, H * hd).astype(dt)
    return jax.nn.silu(gate) * y


def hier_moe(h, w_rg, b_rg, w_re, b_re, w_gate, w_up, w_down):
    B, S, D = h.shape
    T = B * S
    xt = h.reshape(T, D)
    pg = jax.nn.softmax((xt @ w_rg + b_rg).astype(jnp.float32), axis=-1)
    g_top, g_idx = lax.top_k(pg, 1)
    le = (xt @ w_re + b_re).astype(jnp.float32).reshape(T, N_GROUPS, EXPERTS_PER_GROUP)
    le_sel = jnp.take_along_axis(le, g_idx[:, :, None], axis=1)[:, 0]
    pe = jax.nn.softmax(le_sel, axis=-1)
    e_top, e_loc = lax.top_k(pe, TOPK_IN_GROUP)
    w = g_top * e_top / jnp.sum(e_top, axis=-1, keepdims=True)
    e_glob = g_idx * EXPERTS_PER_GROUP + e_loc

    n_assign = T * TOPK_IN_GROUP
    e_flat = e_glob.reshape(-1)
    tok_flat = jnp.repeat(jnp.arange(T, dtype=jnp.int32), TOPK_IN_GROUP)
    w_flat = w.reshape(-1)
    order = jnp.argsort(e_flat)
    e_sorted = e_flat[order]
    counts = jnp.bincount(e_flat, length=N_EXPERTS)
    padded = (counts + MOE_BLOCK - 1) // MOE_BLOCK * MOE_BLOCK
    pad_end = jnp.cumsum(padded)
    pad_start = pad_end - padded
    start = jnp.cumsum(counts) - counts
    dest = pad_start[e_sorted] + (jnp.arange(n_assign) - start[e_sorted])
    n_rows = (n_assign + N_EXPERTS * (MOE_BLOCK - 1) + MOE_BLOCK - 1) // MOE_BLOCK * MOE_BLOCK
    n_blocks = n_rows // MOE_BLOCK
    row_tok = jnp.zeros((n_rows,), jnp.int32).at[dest].set(tok_flat[order])
    row_w = jnp.zeros((n_rows,), w_flat.dtype).at[dest].set(w_flat[order])
    block_expert = jnp.minimum(
        jnp.searchsorted(pad_end, jnp.arange(n_blocks) * MOE_BLOCK, side='right'), N_EXPERTS - 1)

    def expert_block(args):
        tok, wt, e = args
        xb = xt[tok]
        hb = jax.nn.silu(xb @ w_gate[e]) * (xb @ w_up[e])
        return (hb @ w_down[e]) * wt[:, None].astype(xb.dtype)

    y_rows = lax.map(expert_block, (row_tok.reshape(n_blocks, MOE_BLOCK),
                                    row_w.reshape(n_blocks, MOE_BLOCK), block_expert))
    out = jnp.zeros((T, D), h.dtype).at[row_tok].add(y_rows.reshape(n_rows, D))
    return out.reshape(B, S, D)


def setup_inputs(seed: int = 0) -> dict:
    key = jax.random.key(seed)
    ks = jax.random.split(key, 20)
    f32 = jnp.float32
    L = DEPTH
    hd = NSA_HEAD_DIM

    def nrm(k, shape, scale):
        return jax.random.normal(k, shape, f32) * scale

    return {
        'x': nrm(ks[0], (BATCH, SEQ, D_MODEL), 1.0),
        'norm1_g': 1.0 + nrm(ks[1], (L, D_MODEL), 0.02),
        'w_in': nrm(ks[2], (L, D_MODEL, IN_PROJ_WIDTH), D_MODEL ** -0.5),
        'cmp_pos': nrm(ks[3], (L, 2, CMP_BLOCK, hd), 0.02),
        'cmp_w1': nrm(ks[4], (L, 2, CMP_BLOCK, hd, hd), (CMP_BLOCK * hd) ** -0.5),
        'cmp_w2': nrm(ks[5], (L, 2, hd, hd), hd ** -0.5),
        'q_norm_g': 1.0 + nrm(ks[6], (L, hd), 0.02),
        'k_norm_g': 1.0 + nrm(ks[7], (L, 3, hd), 0.02),
        'ret_gn_g': 1.0 + nrm(ks[8], (L, RET_HEADS, RET_HEAD_DIM), 0.02),
        'ret_gn_b': nrm(ks[9], (L, RET_HEADS, RET_HEAD_DIM), 0.02),
        'w_out': nrm(ks[10], (L, D_MIX, D_MODEL), D_MIX ** -0.5),
        'norm2_g': 1.0 + nrm(ks[11], (L, D_MODEL), 0.02),
        'w_router_group': nrm(ks[12], (L, D_MODEL, N_GROUPS), D_MODEL ** -0.5),
        'b_router_group': nrm(ks[13], (L, N_GROUPS), 0.01),
        'w_router_expert': nrm(ks[14], (L, D_MODEL, N_EXPERTS), D_MODEL ** -0.5),
        'b_router_expert': nrm(ks[15], (L, N_EXPERTS), 0.01),
        'w_exp_gate': nrm(ks[16], (L, N_EXPERTS, D_MODEL, D_EXPERT), D_MODEL ** -0.5),
        'w_exp_up': nrm(ks[17], (L, N_EXPERTS, D_MODEL, D_EXPERT), D_MODEL ** -0.5),
        'w_exp_down': nrm(ks[18], (L, N_EXPERTS, D_EXPERT, D_MODEL), D_EXPERT ** -0.5),
    }


def reference(x, norm1_g, w_in, cmp_pos, cmp_w1, cmp_w2, q_norm_g, k_norm_g, ret_gn_g, ret_gn_b,
              w_out, norm2_g, w_router_group, b_router_group, w_router_expert, b_router_expert,
              w_exp_gate, w_exp_up, w_exp_down):
    B, S, _ = x.shape
    sizes = [NSA_Q_WIDTH] + [NSA_KV_WIDTH] * 6 + [NSA_GATE_WIDTH] + [RET_WIDTH] * 4
    offsets = np.cumsum(sizes)[:-1].tolist()
    for l in range(DEPTH):
        h = rms_norm(x, norm1_g[l])
        proj = h @ w_in[l]
        (nq, nkc, nvc, nks, nvs, nkw, nvw, ngate, rq, rk, rv, rg) = jnp.split(proj, offsets, axis=-1)
        kv = lambda t: t.reshape(B, S, NSA_KV_GROUPS, NSA_HEAD_DIM)
        o_nsa = nsa_attention(nq.reshape(B, S, NSA_HEADS, NSA_HEAD_DIM), kv(nkc), kv(nvc), kv(nks), kv(nvs),
                              kv(nkw), kv(nvw), ngate, cmp_pos[l], cmp_w1[l], cmp_w2[l],
                              q_norm_g[l], k_norm_g[l])
        rh = lambda t: t.reshape(B, S, RET_HEADS, RET_HEAD_DIM)
        o_ret = retention(rh(rq), rh(rk), rh(rv), rg, ret_gn_g[l], ret_gn_b[l])
        x = x + jnp.concatenate([o_nsa, o_ret], axis=-1) @ w_out[l]
        h2 = rms_norm(x, norm2_g[l])
        x = x + hier_moe(h2, w_router_group[l], b_router_group[l], w_router_expert[l], b_router_expert[l],
                         w_exp_gate[l], w_exp_up[l], w_exp_down[l])
    return x
```

```python
import functools

import numpy as np
import jax
import jax.numpy as jnp
from jax import lax
from jax.experimental import pallas as pl
from jax.experimental.pallas import tpu as pltpu

F32 = jnp.float32
BF16 = jnp.bfloat16

D_MODEL = 2048
NSA_HEADS = 8
NSA_KV_GROUPS = 2
NSA_GROUP_HEADS = NSA_HEADS // NSA_KV_GROUPS
HEAD_DIM = 128
RET_HEADS = 8
CMP_BLOCK = 32
CMP_STRIDE = 16
SEL_BLOCK = 64
SEL_TOPK = 8
WINDOW = 512
RET_CHUNK = 128
ROPE_BASE = 10000.0
N_GROUPS = 4
EXPERTS_PER_GROUP = 8
N_EXPERTS = N_GROUPS * EXPERTS_PER_GROUP
TOPK_IN_GROUP = 2
D_EXPERT = 512
MOE_BLOCK = 256
RMS_EPS = 1e-6
GN_EPS = 1e-5
NEG_INF = -1e30
FORCED_SCORE = 1e6

NSA_Q_WIDTH = NSA_HEADS * HEAD_DIM
NSA_KV_WIDTH = NSA_KV_GROUPS * HEAD_DIM
NSA_GATE_WIDTH = 3 * NSA_HEADS
RET_WIDTH = RET_HEADS * HEAD_DIM
GATE_COL0 = NSA_Q_WIDTH + 6 * NSA_KV_WIDTH
MAIN_WIDTH = GATE_COL0 + 4 * RET_WIDTH
LANES = 128
VMEM_LIMIT = 56 * 1024 * 1024

KC_BLK, VC_BLK, KS_BLK, VS_BLK, KW_BLK, VW_BLK = 8, 10, 12, 14, 16, 18


def _rms(xf, g):
    return xf * lax.rsqrt(jnp.mean(xf * xf, axis=-1, keepdims=True) + RMS_EPS) * g


def _dot(a, b):
    return jnp.dot(a, b, preferred_element_type=F32)


def _dot_nt(a, b):
    return lax.dot_general(a, b, (((1,), (1,)), ((), ())), preferred_element_type=F32)


def _dot_tn(a, b):
    return lax.dot_general(a, b, (((0,), (0,)), ((), ())), preferred_element_type=F32)


def _split3(p):
    p1 = p.astype(BF16)
    r1 = p - p1.astype(F32)
    p2 = r1.astype(BF16)
    p3 = (r1 - p2.astype(F32)).astype(BF16)
    return p1, p2, p3


def _params(sem):
    return pltpu.CompilerParams(dimension_semantics=sem, vmem_limit_bytes=VMEM_LIMIT)


def _in_proj_kernel(x_ref, g_ref, w_ref, wg_ref, o_ref, og_ref, h_scr):
    @pl.when(pl.program_id(1) == 0)
    def _():
        h = _rms(x_ref[...], g_ref[...]).astype(BF16)
        h_scr[...] = h
        og_ref[...] = _dot(h, wg_ref[...])

    o_ref[...] = _dot(h_scr[...], w_ref[...])


def _in_proj(xt, g1, w_main, w_gate, tm, tn):
    T, D = xt.shape
    n_main = w_main.shape[1]
    n_gate = w_gate.shape[1]
    return pl.pallas_call(
        _in_proj_kernel,
        grid=(T // tm, n_main // tn),
        in_specs=[
            pl.BlockSpec((tm, D), lambda m, n: (m, 0)),
            pl.BlockSpec((1, D), lambda m, n: (0, 0)),
            pl.BlockSpec((D, tn), lambda m, n: (0, n)),
            pl.BlockSpec((D, n_gate), lambda m, n: (0, 0)),
        ],
        out_specs=[
            pl.BlockSpec((tm, tn), lambda m, n: (m, n)),
            pl.BlockSpec((tm, n_gate), lambda m, n: (m, 0)),
        ],
        out_shape=[
            jax.ShapeDtypeStruct((T, n_main), F32),
            jax.ShapeDtypeStruct((T, n_gate), F32),
        ],
        scratch_shapes=[pltpu.VMEM((tm, D), BF16)],
        compiler_params=_params(("parallel", "arbitrary")),
        name="in_proj",
    )(xt, g1, w_main, w_gate)


def _compress_kernel(x_ref, pos_ref, w1_ref, w2_ref, kg_ref, o_ref, xs_scr, *, S, n_pad):
    kv = pl.program_id(1)
    xs_scr[0:S, :] = x_ref[...]
    xs_scr[S:S + CMP_STRIDE, :] = jnp.zeros((CMP_STRIDE, HEAD_DIM), F32)
    acc = jnp.zeros((n_pad, HEAD_DIM), F32)
    for l in range(CMP_BLOCK):
        tb = xs_scr[pl.ds(l, n_pad, stride=CMP_STRIDE), :] + pos_ref[0, l:l + 1, :]
        acc = acc + _dot(tb.astype(BF16), w1_ref[0, l].astype(BF16))
    hid = acc * jax.nn.sigmoid(acc)
    out = _dot(hid.astype(BF16), w2_ref[0].astype(BF16))
    normed = _rms(out, kg_ref[0:1, :])
    o_ref[0, 0, 0] = jnp.where(kv == 0, normed, out).astype(BF16)


def _compress(proj, cmp_pos, cmp_w1, cmp_w2, k_norm_g, B, S):
    n_pad = S // CMP_STRIDE
    G = NSA_KV_GROUPS
    kern = functools.partial(_compress_kernel, S=S, n_pad=n_pad)
    return pl.pallas_call(
        kern,
        grid=(B, 2, G),
        in_specs=[
            pl.BlockSpec((S, HEAD_DIM), lambda b, kv, g: (b, KC_BLK + 2 * kv + g)),
            pl.BlockSpec((1, CMP_BLOCK, HEAD_DIM), lambda b, kv, g: (kv, 0, 0)),
            pl.BlockSpec((1, CMP_BLOCK, HEAD_DIM, HEAD_DIM), lambda b, kv, g: (kv, 0, 0, 0)),
            pl.BlockSpec((1, HEAD_DIM, HEAD_DIM), lambda b, kv, g: (kv, 0, 0)),
            pl.BlockSpec((3, HEAD_DIM), lambda b, kv, g: (0, 0)),
        ],
        out_specs=pl.BlockSpec((1, 1, 1, n_pad, HEAD_DIM), lambda b, kv, g: (b, kv, g, 0, 0)),
        out_shape=jax.ShapeDtypeStruct((B, 2, G, n_pad, HEAD_DIM), BF16),
        scratch_shapes=[pltpu.VMEM((S + CMP_STRIDE, HEAD_DIM), F32)],
        compiler_params=_params(("parallel", "parallel", "parallel")),
        name="compress",
    )(proj, cmp_pos, cmp_w1, cmp_w2, k_norm_g)


def _masked_softmax(s, msk):
    sm = jnp.where(msk, s, NEG_INF)
    m = jnp.max(sm, axis=-1, keepdims=True)
    e = jnp.where(msk, jnp.exp(sm - m), 0.0)
    den = jnp.sum(e, axis=-1, keepdims=True)
    return e / jnp.where(den > 0.0, den, 1.0)


def _nsa_kernel(q_ref, kc_ref, vc_ref, ks_ref, vs_ref, kw_ref, vw_ref, gate_ref, qg_ref, kg_ref,
                o_ref, ksn, vsb, kwn, vwb, *, S, tq, wk):
    i = pl.program_id(2)
    scale = HEAD_DIM ** -0.5
    n_cp = S // CMP_STRIDE
    n_sblk = S // SEL_BLOCK

    @pl.when(i == 0)
    def _():
        ksn[...] = _rms(ks_ref[...], kg_ref[1:2, :]).astype(BF16)
        kwn[...] = _rms(kw_ref[...], kg_ref[2:3, :]).astype(BF16)
        vsb[...] = vs_ref[...].astype(BF16)
        vwb[...] = vw_ref[...].astype(BF16)

    pos = i * tq + lax.broadcasted_iota(jnp.int32, (tq, 1), 0)
    qg = qg_ref[...]
    qs = [_rms(q_ref[:, h * HEAD_DIM:(h + 1) * HEAD_DIM], qg).astype(BF16)
          for h in range(NSA_GROUP_HEADS)]

    kc = kc_ref[0, 0, 0]
    vc = vc_ref[0, 0, 0]
    ncol = lax.broadcasted_iota(jnp.int32, (1, n_cp), 1)
    cmask = (ncol * CMP_STRIDE + (CMP_BLOCK - 1)) <= pos
    psum = jnp.zeros((tq, n_cp), F32)
    o_cmp = []
    for h in range(NSA_GROUP_HEADS):
        p = _masked_softmax(_dot_nt(qs[h], kc) * scale, cmask)
        psum = psum + p
        o_cmp.append(_dot(p.astype(BF16), vc))

    nrow = lax.broadcasted_iota(jnp.int32, (n_cp, LANES), 0) * CMP_STRIDE
    jcol = lax.broadcasted_iota(jnp.int32, (n_cp, LANES), 1) * SEL_BLOCK
    ov = jnp.where((nrow < jcol + SEL_BLOCK) & (nrow + CMP_BLOCK > jcol), 1.0, 0.0).astype(BF16)
    p1, p2, p3 = _split3(psum)
    imp = _dot(p1, ov) + _dot(p2, ov) + _dot(p3, ov)
    j = lax.broadcasted_iota(jnp.int32, (1, LANES), 1)
    cur = pos // SEL_BLOCK
    valid = (j * SEL_BLOCK <= pos) & (j < n_sblk)
    forced = (j == 0) | (j == cur) | (j == cur - 1)
    score = jnp.where(valid, jnp.where(forced, FORCED_SCORE, imp), -jnp.inf)
    rank = jnp.zeros((tq, LANES), jnp.int32)
    for i2 in range(n_sblk):
        si = score[:, i2:i2 + 1]
        ahead = (si > score) | ((si == score) & (j > i2))
        rank = rank + jnp.where(ahead, 1, 0)
    sel = jnp.where((rank < SEL_TOPK) & (j < n_sblk), 1.0, 0.0).astype(BF16)
    erow = lax.broadcasted_iota(jnp.int32, (LANES, S), 0)
    ecol = lax.broadcasted_iota(jnp.int32, (LANES, S), 1) // SEL_BLOCK
    expand = jnp.where(erow == ecol, 1.0, 0.0).astype(BF16)
    kpos = lax.broadcasted_iota(jnp.int32, (1, S), 1)
    smask = (_dot(sel, expand) > 0.5) & (kpos <= pos)

    kst = pl.multiple_of(jnp.clip(i * tq - WINDOW, 0, S - wk), tq)
    kwt = kwn[pl.ds(kst, wk), :]
    vwt = vwb[pl.ds(kst, wk), :]
    wpos = kst + lax.broadcasted_iota(jnp.int32, (1, wk), 1)
    wmask = (wpos <= pos) & (wpos > pos - WINDOW)
    gates = jax.nn.sigmoid(gate_ref[...])
    ksn_all = ksn[...]
    vsb_all = vsb[...]
    for h in range(NSA_GROUP_HEADS):
        p_s = _masked_softmax(_dot_nt(qs[h], ksn_all) * scale, smask)
        o_s = _dot(p_s.astype(BF16), vsb_all)
        p_w = _masked_softmax(_dot_nt(qs[h], kwt) * scale, wmask)
        o_w = _dot(p_w.astype(BF16), vwt)
        c = 3 * h
        o = (o_cmp[h] * gates[:, c:c + 1] + o_s * gates[:, c + 1:c + 2]
             + o_w * gates[:, c + 2:c + 3])
        o_ref[:, h * HEAD_DIM:(h + 1) * HEAD_DIM] = o.astype(BF16)


def _nsa(proj, gate, cmp_kv, q_norm_g, k_norm_g, B, S, tq):
    G = NSA_KV_GROUPS
    n_cp = S // CMP_STRIDE
    wk = min(S, WINDOW + tq)
    nq = S // tq
    gw = NSA_GROUP_HEADS * HEAD_DIM
    kern = functools.partial(_nsa_kernel, S=S, tq=tq, wk=wk)
    kv_spec = lambda blk: pl.BlockSpec((S, HEAD_DIM), lambda b, g, i: (b, blk + g))
    return pl.pallas_call(
        kern,
        grid=(B, G, nq),
        in_specs=[
            pl.BlockSpec((tq, gw), lambda b, g, i: (b * nq + i, g)),
            pl.BlockSpec((1, 1, 1, n_cp, HEAD_DIM), lambda b, g, i: (b, 0, g, 0, 0)),
            pl.BlockSpec((1, 1, 1, n_cp, HEAD_DIM), lambda b, g, i: (b, 1, g, 0, 0)),
            kv_spec(KS_BLK), kv_spec(VS_BLK), kv_spec(KW_BLK), kv_spec(VW_BLK),
            pl.BlockSpec((tq, LANES), lambda b, g, i: (b * nq + i, g)),
            pl.BlockSpec((1, HEAD_DIM), lambda b, g, i: (0, 0)),
            pl.BlockSpec((3, HEAD_DIM), lambda b, g, i: (0, 0)),
        ],
        out_specs=pl.BlockSpec((tq, gw), lambda b, g, i: (b * nq + i, g)),
        out_shape=jax.ShapeDtypeStruct((B * S, NSA_Q_WIDTH), BF16),
        scratch_shapes=[pltpu.VMEM((S, HEAD_DIM), BF16)] * 4,
        compiler_params=_params(("parallel", "parallel", "arbitrary")),
        name="nsa",
    )(proj, cmp_kv, cmp_kv, proj, proj, proj, proj, gate, q_norm_g, k_norm_g)


RET_HB = 4


def _ret_kernel(lg_ref, rq_ref, rk_ref, rv_ref, rg_ref, cos_ref, sin_ref, gg_ref, gb_ref,
                o_ref, r_scr):
    hg = pl.program_id(1)
    C = RET_CHUNK
    scale = HEAD_DIM ** -0.5

    @pl.when(pl.program_id(2) == 0)
    def _():
        r_scr[...] = jnp.zeros(r_scr.shape, F32)

    cos = cos_ref[...]
    sin = sin_ref[...]
    n_col = lax.broadcasted_iota(jnp.int32, (C, 1), 0).astype(F32)
    n_row = lax.broadcasted_iota(jnp.int32, (1, C), 1).astype(F32)
    diff = n_col - n_row
    for h in range(RET_HB):
        sl = slice(h * HEAD_DIM, (h + 1) * HEAD_DIM)
        hh = hg * RET_HB + h
        lg = lg_ref[hh]
        q = rq_ref[:, sl]
        k = rk_ref[:, sl]
        qf = q * cos + pltpu.roll(q, HEAD_DIM // 2, 1) * sin
        kf = (k * cos + pltpu.roll(k, HEAD_DIM // 2, 1) * sin) * scale
        v = rv_ref[:, sl].astype(BF16)
        dec = jnp.where(diff >= 0.0, jnp.exp(lg * jnp.maximum(diff, 0.0)), 0.0)
        xi = jnp.exp(lg * (n_col + 1.0))
        zeta = jnp.exp(lg * (C - 1.0 - n_col))
        cd = jnp.exp(jnp.full((1, HEAD_DIM), lg * float(C), F32))
        qb = qf.astype(BF16)
        r_old = r_scr[h]
        a = _dot_nt(qb, kf.astype(BF16)) * dec
        o = _dot(a.astype(BF16), v) + _dot(qb, r_old.astype(BF16)) * xi
        r_scr[h] = r_old * cd + _dot_tn((kf * zeta).astype(BF16), v)
        mu = jnp.mean(o, axis=-1, keepdims=True)
        d = o - mu
        var = jnp.mean(d * d, axis=-1, keepdims=True)
        y = d * lax.rsqrt(var + GN_EPS) * gg_ref[pl.ds(hh, 1), :] + gb_ref[pl.ds(hh, 1), :]
        gt = rg_ref[:, sl]
        o_ref[:, sl] = (gt * jax.nn.sigmoid(gt) * y).astype(BF16)


def _retention(proj, log_g, cos2, sin2, gn_g, gn_b, B, S):
    C = RET_CHUNK
    n_ch = S // C
    bw = RET_HB * HEAD_DIM
    base = GATE_COL0 // bw
    nhb = RET_HEADS // RET_HB
    spec = lambda k: pl.BlockSpec((C, bw), lambda b, hg, c, lg: (b * n_ch + c, base + k * nhb + hg))
    grid_spec = pltpu.PrefetchScalarGridSpec(
        num_scalar_prefetch=1,
        grid=(B, nhb, n_ch),
        in_specs=[
            spec(0), spec(1), spec(2), spec(3),
            pl.BlockSpec((C, HEAD_DIM), lambda b, hg, c, lg: (c, 0)),
            pl.BlockSpec((C, HEAD_DIM), lambda b, hg, c, lg: (c, 0)),
            pl.BlockSpec((RET_HEADS, HEAD_DIM), lambda b, hg, c, lg: (0, 0)),
            pl.BlockSpec((RET_HEADS, HEAD_DIM), lambda b, hg, c, lg: (0, 0)),
        ],
        out_specs=pl.BlockSpec((C, bw), lambda b, hg, c, lg: (b * n_ch + c, hg)),
        scratch_shapes=[pltpu.VMEM((RET_HB, HEAD_DIM, HEAD_DIM), F32)],
    )
    return pl.pallas_call(
        _ret_kernel,
        grid_spec=grid_spec,
        out_shape=jax.ShapeDtypeStruct((B * S, RET_WIDTH), BF16),
        compiler_params=_params(("parallel", "parallel", "arbitrary")),
        name="retention",
    )(log_g, proj, proj, proj, proj, cos2, sin2, gn_g, gn_b)


def _lane_max(v):
    return jnp.max(v, axis=-1, keepdims=True)


def _lane_min(v):
    return jnp.min(v, axis=-1, keepdims=True)


def _out_kernel(on_ref, or_ref, w_ref, x_ref, g2_ref, wr_ref, br_ref, x1_ref, h2_ref, rt_ref):
    half = on_ref.shape[1]
    acc = _dot(on_ref[...], w_ref[0:half, :]) + _dot(or_ref[...], w_ref[half:2 * half, :])
    x1 = x_ref[...] + acc
    x1_ref[...] = x1
    h2 = _rms(x1, g2_ref[...])
    h2_ref[...] = h2

    h_hi = h2.astype(BF16)
    h_lo = (h2 - h_hi.astype(F32)).astype(BF16)
    wr = wr_ref[...]
    w_hi = wr.astype(BF16)
    w_lo = (wr - w_hi.astype(F32)).astype(BF16)
    logits = _dot(h_hi, w_hi) + _dot(h_lo, w_hi) + _dot(h_hi, w_lo) + br_ref[...]

    tm = logits.shape[0]
    lane = lax.broadcasted_iota(jnp.int32, (tm, LANES), 1)
    big = jnp.int32(LANES)
    gm = lane < N_GROUPS
    gl = jnp.where(gm, logits, -jnp.inf)
    ge = jnp.where(gm, jnp.exp(gl - _lane_max(gl)), 0.0)
    pg = ge / jnp.sum(ge, axis=-1, keepdims=True)
    g_top = _lane_max(pg)
    g_idx = _lane_min(jnp.where(gm & (pg == g_top), lane, big))
    e0 = N_GROUPS + g_idx * EXPERTS_PER_GROUP
    em = (lane >= e0) & (lane < e0 + EXPERTS_PER_GROUP)
    el = jnp.where(em, logits, -jnp.inf)
    ee = jnp.where(em, jnp.exp(el - _lane_max(el)), 0.0)
    pe = ee / jnp.sum(ee, axis=-1, keepdims=True)
    t1 = _lane_max(jnp.where(em, pe, -1.0))
    i1 = _lane_min(jnp.where(em & (pe == t1), lane, big))
    em2 = em & (lane != i1)
    t2 = _lane_max(jnp.where(em2, pe, -1.0))
    i2 = _lane_min(jnp.where(em2 & (pe == t2), lane, big))
    tsum = t1 + t2
    w1 = g_top * t1 / tsum
    w2 = g_top * t2 / tsum
    rt = jnp.where(lane == 0, (i1 - N_GROUPS).astype(F32),
                   jnp.where(lane == 1, (i2 - N_GROUPS).astype(F32),
                             jnp.where(lane == 2, w1, jnp.where(lane == 3, w2, 0.0))))
    rt_ref[...] = rt


def _out_proj(o_nsa, o_ret, w_out, xt, g2, w_router, b_router, tm):
    T, D = xt.shape
    half = o_nsa.shape[1]
    return pl.pallas_call(
        _out_kernel,
        grid=(T // tm,),
        in_specs=[
            pl.BlockSpec((tm, half), lambda m: (m, 0)),
            pl.BlockSpec((tm, half), lambda m: (m, 0)),
            pl.BlockSpec((2 * half, D), lambda m: (0, 0)),
            pl.BlockSpec((tm, D), lambda m: (m, 0)),
            pl.BlockSpec((1, D), lambda m: (0, 0)),
            pl.BlockSpec((D, LANES), lambda m: (0, 0)),
            pl.BlockSpec((1, LANES), lambda m: (0, 0)),
        ],
        out_specs=[
            pl.BlockSpec((tm, D), lambda m: (m, 0)),
            pl.BlockSpec((tm, D), lambda m: (m, 0)),
            pl.BlockSpec((tm, LANES), lambda m: (m, 0)),
        ],
        out_shape=[
            jax.ShapeDtypeStruct((T, D), F32),
            jax.ShapeDtypeStruct((T, D), F32),
            jax.ShapeDtypeStruct((T, LANES), F32),
        ],
        compiler_params=_params(("parallel",)),
        name="out_proj",
    )(o_nsa, o_ret, w_out, xt, g2, w_router, b_router)


def _row_copy(src_hbm, row, dst, slot, sem):
    return pltpu.make_async_copy(src_hbm.at[pl.ds(row, 1), :], dst.at[pl.ds(slot, 1), :], sem)


def _expert_kernel(be_ref, nused_ref, tok_ref, h2_hbm, wg_ref, wu_ref, wd_ref, rw_ref,
                   o_ref, xbuf, sem):
    i = pl.program_id(0)
    M = MOE_BLOCK

    @pl.when(i < nused_ref[0])
    def _():
        def issue(r, carry):
            _row_copy(h2_hbm, tok_ref[i * M + r], xbuf, r, sem).start()
            return carry

        lax.fori_loop(0, M, issue, 0)

        def drain(r, carry):
            _row_copy(h2_hbm, 0, xbuf, r, sem).wait()
            return carry

        lax.fori_loop(0, M, drain, 0)
        xb = xbuf[...].astype(BF16)
        hg = _dot(xb, wg_ref[0].astype(BF16))
        hu = _dot(xb, wu_ref[0].astype(BF16))
        hb = (hg * jax.nn.sigmoid(hg) * hu).astype(BF16)
        o_ref[...] = _dot(hb, wd_ref[0].astype(BF16)) * rw_ref[...]

    @pl.when(i >= nused_ref[0])
    def _():
        o_ref[...] = jnp.zeros(o_ref.shape, F32)


def _experts(h2, block_expert, n_used, row_tok, row_w, w_gate, w_up, w_down):
    T, D = h2.shape
    n_rows = row_tok.shape[0]
    n_blocks = n_rows // MOE_BLOCK
    grid_spec = pltpu.PrefetchScalarGridSpec(
        num_scalar_prefetch=3,
        grid=(n_blocks,),
        in_specs=[
            pl.BlockSpec(memory_space=pl.ANY),
            pl.BlockSpec((1, D, D_EXPERT), lambda i, be, nu, tok: (be[i], 0, 0)),
            pl.BlockSpec((1, D, D_EXPERT), lambda i, be, nu, tok: (be[i], 0, 0)),
            pl.BlockSpec((1, D_EXPERT, D), lambda i, be, nu, tok: (be[i], 0, 0)),
            pl.BlockSpec((MOE_BLOCK, 1), lambda i, be, nu, tok: (i, 0)),
        ],
        out_specs=pl.BlockSpec((MOE_BLOCK, D), lambda i, be, nu, tok: (i, 0)),
        scratch_shapes=[pltpu.VMEM((MOE_BLOCK, D), F32), pltpu.SemaphoreType.DMA(())],
    )
    return pl.pallas_call(
        _expert_kernel,
        grid_spec=grid_spec,
        out_shape=jax.ShapeDtypeStruct((n_rows, D), F32),
        compiler_params=_params(("arbitrary",)),
        name="experts",
    )(block_expert, n_used, row_tok, h2, w_gate, w_up, w_down, row_w)


def _combine_kernel(pos_ref, y_hbm, x1_ref, o_ref, ybuf, sem):
    i = pl.program_id(0)
    tm = x1_ref.shape[0]

    def issue(r, carry):
        a = (i * tm + r) * TOPK_IN_GROUP
        _row_copy(y_hbm, pos_ref[a], ybuf.at[0], r, sem).start()
        _row_copy(y_hbm, pos_ref[a + 1], ybuf.at[1], r, sem).start()
        return carry

    lax.fori_loop(0, tm, issue, 0)

    def drain(r, carry):
        _row_copy(y_hbm, 0, ybuf.at[0], r, sem).wait()
        _row_copy(y_hbm, 0, ybuf.at[1], r, sem).wait()
        return carry

    lax.fori_loop(0, tm, drain, 0)
    o_ref[...] = x1_ref[...] + (ybuf[0] + ybuf[1])


def _combine(pos, y_rows, x1, tm):
    T, D = x1.shape
    grid_spec = pltpu.PrefetchScalarGridSpec(
        num_scalar_prefetch=1,
        grid=(T // tm,),
        in_specs=[
            pl.BlockSpec(memory_space=pl.ANY),
            pl.BlockSpec((tm, D), lambda i, pos: (i, 0)),
        ],
        out_specs=pl.BlockSpec((tm, D), lambda i, pos: (i, 0)),
        scratch_shapes=[pltpu.VMEM((TOPK_IN_GROUP, tm, D), F32), pltpu.SemaphoreType.DMA(())],
    )
    return pl.pallas_call(
        _combine_kernel,
        grid_spec=grid_spec,
        out_shape=jax.ShapeDtypeStruct((T, D), F32),
        compiler_params=_params(("arbitrary",)),
        name="combine",
    )(pos, y_rows, x1)


def _dispatch_plan(rt, T):
    n_assign = T * TOPK_IN_GROUP
    e_flat = rt[:, 0:TOPK_IN_GROUP].astype(jnp.int32).reshape(-1)
    w_flat = rt[:, TOPK_IN_GROUP:2 * TOPK_IN_GROUP].reshape(-1)
    order = jnp.argsort(e_flat)
    e_sorted = e_flat[order]
    counts = jnp.bincount(e_flat, length=N_EXPERTS)
    padded = (counts + MOE_BLOCK - 1) // MOE_BLOCK * MOE_BLOCK
    pad_end = jnp.cumsum(padded)
    pad_start = pad_end - padded
    start = jnp.cumsum(counts) - counts
    dest = (pad_start[e_sorted] + (jnp.arange(n_assign) - start[e_sorted])).astype(jnp.int32)
    n_rows = (n_assign + N_EXPERTS * (MOE_BLOCK - 1) + MOE_BLOCK - 1) // MOE_BLOCK * MOE_BLOCK
    n_blocks = n_rows // MOE_BLOCK
    row_tok = jnp.zeros((n_rows,), jnp.int32).at[dest].set((order // TOPK_IN_GROUP).astype(jnp.int32))
    row_w = jnp.zeros((n_rows,), F32).at[dest].set(w_flat[order])
    block_expert = jnp.minimum(
        jnp.searchsorted(pad_end, jnp.arange(n_blocks) * MOE_BLOCK, side='right'),
        N_EXPERTS - 1).astype(jnp.int32)
    n_used = (pad_end[-1:] // MOE_BLOCK).astype(jnp.int32)
    pos = jnp.zeros((n_assign,), jnp.int32).at[order].set(dest)
    return block_expert, n_used, row_tok, row_w.reshape(n_rows, 1), pos


def _layer(x, norm1_g, w_in, cmp_pos, cmp_w1, cmp_w2, q_norm_g, k_norm_g, ret_gn_g, ret_gn_b,
           w_out, norm2_g, w_rg, b_rg, w_re, b_re, w_eg, w_eu, w_ed, tiles):
    B, S, D = x.shape
    T = B * S
    xt = x.reshape(T, D)

    w_main = jnp.concatenate([w_in[:, :GATE_COL0], w_in[:, GATE_COL0 + NSA_GATE_WIDTH:]], axis=1).astype(BF16)
    gpg = NSA_GATE_WIDTH // NSA_KV_GROUPS
    w_gate = jnp.concatenate(
        [jnp.pad(w_in[:, GATE_COL0 + g * gpg:GATE_COL0 + (g + 1) * gpg], ((0, 0), (0, LANES - gpg)))
         for g in range(NSA_KV_GROUPS)], axis=1).astype(BF16)
    proj, gate = _in_proj(xt, norm1_g.reshape(1, D), w_main, w_gate, tiles["tm_in"], tiles["tn_in"])

    cmp_kv = _compress(proj, cmp_pos, cmp_w1, cmp_w2, k_norm_g, B, S)
    o_nsa = _nsa(proj, gate, cmp_kv, q_norm_g.reshape(1, HEAD_DIM), k_norm_g, B, S, tiles["tq"])

    half = HEAD_DIM // 2
    inv_freq = ROPE_BASE ** (-jnp.arange(half, dtype=F32) / half)
    ang = jnp.arange(S, dtype=F32)[:, None] * inv_freq[None, :]
    cos2 = jnp.concatenate([jnp.cos(ang), jnp.cos(ang)], axis=1)
    sin2 = jnp.concatenate([-jnp.sin(ang), jnp.sin(ang)], axis=1)
    log_g = jnp.log1p(-jnp.exp2(-5.0 - jnp.arange(RET_HEADS, dtype=F32)))
    o_ret = _retention(proj, log_g, cos2, sin2, ret_gn_g, ret_gn_b, B, S)

    n_r = N_GROUPS + N_EXPERTS
    w_router = jnp.pad(jnp.concatenate([w_rg, w_re], axis=1), ((0, 0), (0, LANES - n_r)))
    b_router = jnp.pad(jnp.concatenate([b_rg, b_re]), (0, LANES - n_r)).reshape(1, LANES)
    x1, h2, rt = _out_proj(o_nsa, o_ret, w_out.astype(BF16), xt, norm2_g.reshape(1, D),
                           w_router, b_router, tiles["tm_out"])

    block_expert, n_used, row_tok, row_w, pos = _dispatch_plan(rt, T)
    y_rows = _experts(h2, block_expert, n_used, row_tok, row_w, w_eg, w_eu, w_ed)
    out = _combine(pos, y_rows, x1, tiles["tm_cmb"])
    return out.reshape(B, S, D)


def _tiles(T, S):
    return {
        "tm_in": min(1024, T), "tn_in": 512,
        "tq": min(128, S),
        "tm_out": min(256, T),
        "tm_cmb": min(256, T),
    }


def kernel(x, norm1_g, w_in, cmp_pos, cmp_w1, cmp_w2, q_norm_g, k_norm_g, ret_gn_g, ret_gn_b, w_out, norm2_g, w_router_group, b_router_group, w_router_expert, b_router_expert, w_exp_gate, w_exp_up, w_exp_down):
    B, S, _ = x.shape
    tiles = _tiles(B * S, S)
    for l in range(norm1_g.shape[0]):
        x = _layer(x, norm1_g[l], w_in[l], cmp_pos[l], cmp_w1[l], cmp_w2[l], q_norm_g[l], k_norm_g[l],
                   ret_gn_g[l], ret_gn_b[l], w_out[l], norm2_g[l], w_router_group[l], b_router_group[l],
                   w_router_expert[l], b_router_expert[l], w_exp_gate[l], w_exp_up[l], w_exp_down[l], tiles)
    return x
```

```python
import functools

import numpy as np
import jax
import jax.numpy as jnp
from jax import lax
from jax.experimental import pallas as pl
from jax.experimental.pallas import tpu as pltpu

F32 = jnp.float32
BF16 = jnp.bfloat16

D_MODEL = 2048
NSA_HEADS = 8
NSA_KV_GROUPS = 2
NSA_GROUP_HEADS = NSA_HEADS // NSA_KV_GROUPS
HEAD_DIM = 128
RET_HEADS = 8
CMP_BLOCK = 32
CMP_STRIDE = 16
SEL_BLOCK = 64
SEL_TOPK = 8
WINDOW = 512
RET_CHUNK = 128
ROPE_BASE = 10000.0
N_GROUPS = 4
EXPERTS_PER_GROUP = 8
N_EXPERTS = N_GROUPS * EXPERTS_PER_GROUP
TOPK_IN_GROUP = 2
D_EXPERT = 512
MOE_BLOCK = 256
RMS_EPS = 1e-6
GN_EPS = 1e-5
NEG_INF = -1e30
FORCED_SCORE = 1e6
LOG2E = 1.4426950408889634

NSA_Q_WIDTH = NSA_HEADS * HEAD_DIM
NSA_KV_WIDTH = NSA_KV_GROUPS * HEAD_DIM
NSA_GATE_WIDTH = 3 * NSA_HEADS
RET_WIDTH = RET_HEADS * HEAD_DIM
GATE_COL0 = NSA_Q_WIDTH + 6 * NSA_KV_WIDTH
MAIN_WIDTH = GATE_COL0 + 4 * RET_WIDTH
LANES = 128
VMEM_LIMIT = 56 * 1024 * 1024

KC_BLK, VC_BLK, KS_BLK, VS_BLK, KW_BLK, VW_BLK = 8, 10, 12, 14, 16, 18


def _rms(xf, g):
    return xf * lax.rsqrt(jnp.mean(xf * xf, axis=-1, keepdims=True) + RMS_EPS) * g


def _dot(a, b):
    return jnp.dot(a, b, preferred_element_type=F32)


def _dot_nt(a, b):
    return lax.dot_general(a, b, (((1,), (1,)), ((), ())), preferred_element_type=F32)


def _dot_tn(a, b):
    return lax.dot_general(a, b, (((0,), (0,)), ((), ())), preferred_element_type=F32)


def _split3(p):
    p1 = p.astype(BF16)
    r1 = p - p1.astype(F32)
    p2 = r1.astype(BF16)
    p3 = (r1 - p2.astype(F32)).astype(BF16)
    return p1, p2, p3


def _params(sem):
    return pltpu.CompilerParams(dimension_semantics=sem, vmem_limit_bytes=VMEM_LIMIT)


def _in_proj_kernel(x_ref, g_ref, w_ref, wg_ref, o_ref, og_ref, h_scr):
    @pl.when(pl.program_id(1) == 0)
    def _():
        h = _rms(x_ref[...], g_ref[...]).astype(BF16)
        h_scr[...] = h
        og_ref[...] = _dot(h, wg_ref[...])

    o_ref[...] = _dot(h_scr[...], w_ref[...])


def _in_proj(xt, g1, w_main, w_gate, tm, tn):
    T, D = xt.shape
    n_main = w_main.shape[1]
    n_gate = w_gate.shape[1]
    return pl.pallas_call(
        _in_proj_kernel,
        grid=(T // tm, n_main // tn),
        in_specs=[
            pl.BlockSpec((tm, D), lambda m, n: (m, 0)),
            pl.BlockSpec((1, D), lambda m, n: (0, 0)),
            pl.BlockSpec((D, tn), lambda m, n: (0, n)),
            pl.BlockSpec((D, n_gate), lambda m, n: (0, 0)),
        ],
        out_specs=[
            pl.BlockSpec((tm, tn), lambda m, n: (m, n)),
            pl.BlockSpec((tm, n_gate), lambda m, n: (m, 0)),
        ],
        out_shape=[
            jax.ShapeDtypeStruct((T, n_main), F32),
            jax.ShapeDtypeStruct((T, n_gate), F32),
        ],
        scratch_shapes=[pltpu.VMEM((tm, D), BF16)],
        compiler_params=_params(("parallel", "arbitrary")),
        name="in_proj",
    )(xt, g1, w_main, w_gate)


def _compress_kernel(x_ref, pos_ref, w1_ref, w2_ref, kg_ref, o_ref, xs_scr, *, S, n_pad):
    kv = pl.program_id(1)
    xs_scr[0:S, :] = x_ref[...]
    xs_scr[S:S + CMP_STRIDE, :] = jnp.zeros((CMP_STRIDE, HEAD_DIM), F32)
    acc = jnp.zeros((n_pad, HEAD_DIM), F32)
    for l in range(CMP_BLOCK):
        tb = xs_scr[pl.ds(l, n_pad, stride=CMP_STRIDE), :] + pos_ref[0, l:l + 1, :]
        acc = acc + _dot(tb.astype(BF16), w1_ref[0, l].astype(BF16))
    hid = acc * jax.nn.sigmoid(acc)
    out = _dot(hid.astype(BF16), w2_ref[0].astype(BF16))
    normed = _rms(out, kg_ref[0:1, :])
    o_ref[0, 0, 0] = jnp.where(kv == 0, normed, out).astype(BF16)


def _compress(proj, cmp_pos, cmp_w1, cmp_w2, k_norm_g, B, S):
    n_pad = S // CMP_STRIDE
    G = NSA_KV_GROUPS
    kern = functools.partial(_compress_kernel, S=S, n_pad=n_pad)
    return pl.pallas_call(
        kern,
        grid=(B, 2, G),
        in_specs=[
            pl.BlockSpec((S, HEAD_DIM), lambda b, kv, g: (b, KC_BLK + 2 * kv + g)),
            pl.BlockSpec((1, CMP_BLOCK, HEAD_DIM), lambda b, kv, g: (kv, 0, 0)),
            pl.BlockSpec((1, CMP_BLOCK, HEAD_DIM, HEAD_DIM), lambda b, kv, g: (kv, 0, 0, 0)),
            pl.BlockSpec((1, HEAD_DIM, HEAD_DIM), lambda b, kv, g: (kv, 0, 0)),
            pl.BlockSpec((3, HEAD_DIM), lambda b, kv, g: (0, 0)),
        ],
        out_specs=pl.BlockSpec((1, 1, 1, n_pad, HEAD_DIM), lambda b, kv, g: (b, kv, g, 0, 0)),
        out_shape=jax.ShapeDtypeStruct((B, 2, G, n_pad, HEAD_DIM), BF16),
        scratch_shapes=[pltpu.VMEM((S + CMP_STRIDE, HEAD_DIM), F32)],
        compiler_params=_params(("parallel", "parallel", "parallel")),
        name="compress",
    )(proj, cmp_pos, cmp_w1, cmp_w2, k_norm_g)


def _masked_softmax(s, msk):
    sm = jnp.where(msk, s, NEG_INF)
    m = jnp.max(sm, axis=-1, keepdims=True)
    e = jnp.where(msk, jnp.exp(sm - m), 0.0)
    den = jnp.sum(e, axis=-1, keepdims=True)
    return e / jnp.where(den > 0.0, den, 1.0)


def _nsa_kernel(q_ref, kc_ref, vc_ref, ks_ref, vs_ref, kw_ref, vw_ref, gate_ref, qg_ref, kg_ref,
                ovt_ref, ex_ref, o_ref, ksn, vsb, kwn, vwb, l_scr, acc_scr, *, S, tq, tk, wk):
    i = pl.program_id(2)
    H = NSA_GROUP_HEADS
    scale = HEAD_DIM ** -0.5
    c2 = scale * LOG2E
    n_cp = S // CMP_STRIDE
    n_sblk = S // SEL_BLOCK

    @pl.when(i == 0)
    def _():
        ksn[...] = _rms(ks_ref[...], kg_ref[1:2, :]).astype(BF16)
        kwn[...] = _rms(kw_ref[...], kg_ref[2:3, :]).astype(BF16)
        vsb[...] = vs_ref[...].astype(BF16)
        vwb[...] = vw_ref[...].astype(BF16)

    pos = i * tq + lax.broadcasted_iota(jnp.int32, (tq, 1), 0)
    qg = qg_ref[...]
    q4 = jnp.concatenate([_rms(q_ref[:, h * HEAD_DIM:(h + 1) * HEAD_DIM], qg).astype(BF16)
                          for h in range(H)], axis=0)
    rows = [slice(h * tq, (h + 1) * tq) for h in range(H)]

    ncol = lax.broadcasted_iota(jnp.int32, (1, n_cp), 1)
    cmask = (ncol * CMP_STRIDE + (CMP_BLOCK - 1)) <= pos
    s_c4 = _dot_nt(q4, kc_ref[0, 0, 0]) * scale
    ps = [_masked_softmax(s_c4[r], cmask) for r in rows]
    psum = ps[0] + ps[1] + ps[2] + ps[3]
    o_c4 = _dot(jnp.concatenate([p.astype(BF16) for p in ps], axis=0), vc_ref[0, 0, 0])

    ovt = ovt_ref[...]
    p1, p2, p3 = _split3(psum)
    imp_t = (_dot_nt(ovt, p1) + _dot_nt(ovt, p2) + _dot_nt(ovt, p3))[0:n_sblk, :]
    pos_t = i * tq + lax.broadcasted_iota(jnp.int32, (1, tq), 1)
    jrow = lax.broadcasted_iota(jnp.int32, (n_sblk, 1), 0)
    cur_t = pos_t // SEL_BLOCK
    valid_t = jrow * SEL_BLOCK <= pos_t
    forced_t = (jrow == 0) | (jrow == cur_t) | (jrow == cur_t - 1)
    score = jnp.where(valid_t, jnp.where(forced_t, FORCED_SCORE, imp_t), -jnp.inf)
    rank = jnp.zeros((n_sblk, tq), jnp.int32)
    for i2 in range(n_sblk):
        si = score[i2:i2 + 1, :]
        ahead = (si > score) | ((si == score) & (jrow > i2))
        rank = rank + jnp.where(ahead, 1, 0)
    sel_t = jnp.where((rank < SEL_TOPK) & valid_t, 1.0, 0.0)
    sel_t = jnp.concatenate([sel_t, jnp.zeros((LANES - n_sblk, tq), F32)], axis=0)
    sel = sel_t.T.astype(BF16)

    def slc_prefix(n):
        kpos = lax.broadcasted_iota(jnp.int32, (1, n), 1)
        bias = jnp.where((_dot(sel, ex_ref[:, 0:n]) > 0.5) & (kpos <= pos), 0.0, NEG_INF)
        s4 = _dot_nt(q4, ksn[0:n, :])
        es = []
        for r in rows:
            t = s4[r] * c2 + bias
            e = jnp.exp2(t - jnp.max(t, axis=-1, keepdims=True))
            l_scr[r] = jnp.sum(e, axis=-1, keepdims=True)
            es.append(e.astype(BF16))
        acc_scr[...] = _dot(jnp.concatenate(es, axis=0), vsb[0:n, :])

    n_cls = S // tk
    cls = ((i + 1) * tq - 1) // tk
    for c in range(n_cls):
        pl.when(cls == c)(functools.partial(slc_prefix, (c + 1) * tk))

    kst = pl.multiple_of(jnp.clip(i * tq - WINDOW, 0, S - wk), tq)
    wpos = kst + lax.broadcasted_iota(jnp.int32, (1, wk), 1)
    wbias = jnp.where((wpos <= pos) & (wpos > pos - WINDOW), 0.0, NEG_INF)
    s_w4 = _dot_nt(q4, kwn[pl.ds(kst, wk), :])
    ews, lws = [], []
    for r in rows:
        t = s_w4[r] * c2 + wbias
        e = jnp.exp2(t - jnp.max(t, axis=-1, keepdims=True))
        lws.append(jnp.sum(e, axis=-1, keepdims=True))
        ews.append(e.astype(BF16))
    o_w4 = _dot(jnp.concatenate(ews, axis=0), vwb[pl.ds(kst, wk), :])

    gates = jax.nn.sigmoid(gate_ref[...])
    for h, r in enumerate(rows):
        c = 3 * h
        o = (o_c4[r] * gates[:, c:c + 1]
             + acc_scr[r] * (gates[:, c + 1:c + 2] / l_scr[r])
             + o_w4[r] * (gates[:, c + 2:c + 3] / lws[h]))
        o_ref[:, h * HEAD_DIM:(h + 1) * HEAD_DIM] = o.astype(BF16)


def _nsa_tables(S):
    n_cp, n_cmp, n_sblk = S // CMP_STRIDE, (S - CMP_BLOCK) // CMP_STRIDE + 1, S // SEL_BLOCK
    n = np.arange(n_cp)[None, :] * CMP_STRIDE
    j = np.arange(LANES)[:, None]
    ovt = ((n < (j + 1) * SEL_BLOCK) & (n + CMP_BLOCK > j * SEL_BLOCK)
           & (np.arange(n_cp)[None, :] < n_cmp) & (j < n_sblk))
    expand = (np.arange(S) // SEL_BLOCK)[None, :] == np.arange(LANES)[:, None]
    return jnp.asarray(ovt, BF16), jnp.asarray(expand, BF16)


def _nsa(proj, gate, cmp_kv, q_norm_g, k_norm_g, B, S, tq, tk):
    G = NSA_KV_GROUPS
    H = NSA_GROUP_HEADS
    n_cp = S // CMP_STRIDE
    wk = min(S, WINDOW + tq)
    nq = S // tq
    gw = H * HEAD_DIM
    ovt, expand = _nsa_tables(S)
    kern = functools.partial(_nsa_kernel, S=S, tq=tq, tk=tk, wk=wk)
    kv_spec = lambda blk: pl.BlockSpec((S, HEAD_DIM), lambda b, g, i: (b, blk + g))
    return pl.pallas_call(
        kern,
        grid=(B, G, nq),
        in_specs=[
            pl.BlockSpec((tq, gw), lambda b, g, i: (b * nq + i, g)),
            pl.BlockSpec((1, 1, 1, n_cp, HEAD_DIM), lambda b, g, i: (b, 0, g, 0, 0)),
            pl.BlockSpec((1, 1, 1, n_cp, HEAD_DIM), lambda b, g, i: (b, 1, g, 0, 0)),
            kv_spec(KS_BLK), kv_spec(VS_BLK), kv_spec(KW_BLK), kv_spec(VW_BLK),
            pl.BlockSpec((tq, LANES), lambda b, g, i: (b * nq + i, g)),
            pl.BlockSpec((1, HEAD_DIM), lambda b, g, i: (0, 0)),
            pl.BlockSpec((3, HEAD_DIM), lambda b, g, i: (0, 0)),
            pl.BlockSpec((LANES, n_cp), lambda b, g, i: (0, 0)),
            pl.BlockSpec((LANES, S), lambda b, g, i: (0, 0)),
        ],
        out_specs=pl.BlockSpec((tq, gw), lambda b, g, i: (b * nq + i, g)),
        out_shape=jax.ShapeDtypeStruct((B * S, NSA_Q_WIDTH), BF16),
        scratch_shapes=[pltpu.VMEM((S, HEAD_DIM), BF16)] * 4 + [
            pltpu.VMEM((H * tq, 1), F32), pltpu.VMEM((H * tq, HEAD_DIM), F32)],
        compiler_params=_params(("parallel", "parallel", "arbitrary")),
        name="nsa",
    )(proj, cmp_kv, cmp_kv, proj, proj, proj, proj, gate, q_norm_g, k_norm_g, ovt, expand)


RET_HB = 4


def _ret_kernel(lg_ref, rq_ref, rk_ref, rv_ref, rg_ref, cos_ref, sin_ref, gg_ref, gb_ref,
                o_ref, r_scr):
    hg = pl.program_id(1)
    C = RET_CHUNK
    scale = HEAD_DIM ** -0.5

    @pl.when(pl.program_id(2) == 0)
    def _():
        r_scr[...] = jnp.zeros(r_scr.shape, F32)

    cos = cos_ref[...]
    sin = sin_ref[...]
    n_col = lax.broadcasted_iota(jnp.int32, (C, 1), 0).astype(F32)
    n_row = lax.broadcasted_iota(jnp.int32, (1, C), 1).astype(F32)
    diff = n_col - n_row
    for h in range(RET_HB):
        sl = slice(h * HEAD_DIM, (h + 1) * HEAD_DIM)
        hh = hg * RET_HB + h
        lg = lg_ref[hh]
        q = rq_ref[:, sl]
        k = rk_ref[:, sl]
        qf = q * cos + pltpu.roll(q, HEAD_DIM // 2, 1) * sin
        kf = (k * cos + pltpu.roll(k, HEAD_DIM // 2, 1) * sin) * scale
        v = rv_ref[:, sl].astype(BF16)
        dec = jnp.where(diff >= 0.0, jnp.exp(lg * jnp.maximum(diff, 0.0)), 0.0)
        xi = jnp.exp(lg * (n_col + 1.0))
        zeta = jnp.exp(lg * (C - 1.0 - n_col))
        cd = jnp.exp(jnp.full((1, HEAD_DIM), lg * float(C), F32))
        qb = qf.astype(BF16)
        r_old = r_scr[h]
        a = _dot_nt(qb, kf.astype(BF16)) * dec
        o = _dot(a.astype(BF16), v) + _dot(qb, r_old.astype(BF16)) * xi
        r_scr[h] = r_old * cd + _dot_tn((kf * zeta).astype(BF16), v)
        mu = jnp.mean(o, axis=-1, keepdims=True)
        d = o - mu
        var = jnp.mean(d * d, axis=-1, keepdims=True)
        y = d * lax.rsqrt(var + GN_EPS) * gg_ref[pl.ds(hh, 1), :] + gb_ref[pl.ds(hh, 1), :]
        gt = rg_ref[:, sl]
        o_ref[:, sl] = (gt * jax.nn.sigmoid(gt) * y).astype(BF16)


def _retention(proj, log_g, cos2, sin2, gn_g, gn_b, B, S):
    C = RET_CHUNK
    n_ch = S // C
    bw = RET_HB * HEAD_DIM
    base = GATE_COL0 // bw
    nhb = RET_HEADS // RET_HB
    spec = lambda k: pl.BlockSpec((C, bw), lambda b, hg, c, lg: (b * n_ch + c, base + k * nhb + hg))
    grid_spec = pltpu.PrefetchScalarGridSpec(
        num_scalar_prefetch=1,
        grid=(B, nhb, n_ch),
        in_specs=[
            spec(0), spec(1), spec(2), spec(3),
            pl.BlockSpec((C, HEAD_DIM), lambda b, hg, c, lg: (c, 0)),
            pl.BlockSpec((C, HEAD_DIM), lambda b, hg, c, lg: (c, 0)),
            pl.BlockSpec((RET_HEADS, HEAD_DIM), lambda b, hg, c, lg: (0, 0)),
            pl.BlockSpec((RET_HEADS, HEAD_DIM), lambda b, hg, c, lg: (0, 0)),
        ],
        out_specs=pl.BlockSpec((C, bw), lambda b, hg, c, lg: (b * n_ch + c, hg)),
        scratch_shapes=[pltpu.VMEM((RET_HB, HEAD_DIM, HEAD_DIM), F32)],
    )
    return pl.pallas_call(
        _ret_kernel,
        grid_spec=grid_spec,
        out_shape=jax.ShapeDtypeStruct((B * S, RET_WIDTH), BF16),
        compiler_params=_params(("parallel", "parallel", "arbitrary")),
        name="retention",
    )(log_g, proj, proj, proj, proj, cos2, sin2, gn_g, gn_b)


def _lane_max(v):
    return jnp.max(v, axis=-1, keepdims=True)


def _lane_min(v):
    return jnp.min(v, axis=-1, keepdims=True)


def _out_kernel(on_ref, or_ref, w_ref, x_ref, g2_ref, wr_ref, br_ref, x1_ref, h2_ref, rt_ref):
    half = on_ref.shape[1]
    acc = _dot(on_ref[...], w_ref[0:half, :]) + _dot(or_ref[...], w_ref[half:2 * half, :])
    x1 = x_ref[...] + acc
    x1_ref[...] = x1
    h2 = _rms(x1, g2_ref[...])
    h2_ref[...] = h2

    h_hi = h2.astype(BF16)
    h_lo = (h2 - h_hi.astype(F32)).astype(BF16)
    wr = wr_ref[...]
    w_hi = wr.astype(BF16)
    w_lo = (wr - w_hi.astype(F32)).astype(BF16)
    logits = _dot(h_hi, w_hi) + _dot(h_lo, w_hi) + _dot(h_hi, w_lo) + br_ref[...]

    tm = logits.shape[0]
    lane = lax.broadcasted_iota(jnp.int32, (tm, LANES), 1)
    big = jnp.int32(LANES)
    gm = lane < N_GROUPS
    gl = jnp.where(gm, logits, -jnp.inf)
    ge = jnp.where(gm, jnp.exp(gl - _lane_max(gl)), 0.0)
    pg = ge / jnp.sum(ge, axis=-1, keepdims=True)
    g_top = _lane_max(pg)
    g_idx = _lane_min(jnp.where(gm & (pg == g_top), lane, big))
    e0 = N_GROUPS + g_idx * EXPERTS_PER_GROUP
    em = (lane >= e0) & (lane < e0 + EXPERTS_PER_GROUP)
    el = jnp.where(em, logits, -jnp.inf)
    ee = jnp.where(em, jnp.exp(el - _lane_max(el)), 0.0)
    pe = ee / jnp.sum(ee, axis=-1, keepdims=True)
    t1 = _lane_max(jnp.where(em, pe, -1.0))
    i1 = _lane_min(jnp.where(em & (pe == t1), lane, big))
    em2 = em & (lane != i1)
    t2 = _lane_max(jnp.where(em2, pe, -1.0))
    i2 = _lane_min(jnp.where(em2 & (pe == t2), lane, big))
    tsum = t1 + t2
    w1 = g_top * t1 / tsum
    w2 = g_top * t2 / tsum
    rt = jnp.where(lane == 0, (i1 - N_GROUPS).astype(F32),
                   jnp.where(lane == 1, (i2 - N_GROUPS).astype(F32),
                             jnp.where(lane == 2, w1, jnp.where(lane == 3, w2, 0.0))))
    rt_ref[...] = rt


def _out_proj(o_nsa, o_ret, w_out, xt, g2, w_router, b_router, tm):
    T, D = xt.shape
    half = o_nsa.shape[1]
    return pl.pallas_call(
        _out_kernel,
        grid=(T // tm,),
        in_specs=[
            pl.BlockSpec((tm, half), lambda m: (m, 0)),
            pl.BlockSpec((tm, half), lambda m: (m, 0)),
            pl.BlockSpec((2 * half, D), lambda m: (0, 0)),
            pl.BlockSpec((tm, D), lambda m: (m, 0)),
            pl.BlockSpec((1, D), lambda m: (0, 0)),
            pl.BlockSpec((D, LANES), lambda m: (0, 0)),
            pl.BlockSpec((1, LANES), lambda m: (0, 0)),
        ],
        out_specs=[
            pl.BlockSpec((tm, D), lambda m: (m, 0)),
            pl.BlockSpec((tm, D), lambda m: (m, 0)),
            pl.BlockSpec((tm, LANES), lambda m: (m, 0)),
        ],
        out_shape=[
            jax.ShapeDtypeStruct((T, D), F32),
            jax.ShapeDtypeStruct((T, D), F32),
            jax.ShapeDtypeStruct((T, LANES), F32),
        ],
        compiler_params=_params(("parallel",)),
        name="out_proj",
    )(o_nsa, o_ret, w_out, xt, g2, w_router, b_router)


def _row_copy(src_hbm, row, dst, slot, sem):
    return pltpu.make_async_copy(src_hbm.at[pl.ds(row, 1), :], dst.at[pl.ds(slot, 1), :], sem)


def _expert_kernel(be_ref, nused_ref, tok_ref, h2_hbm, wg_ref, wu_ref, wd_ref, rw_ref,
                   o_ref, xbuf, sem):
    i = pl.program_id(0)
    M = MOE_BLOCK

    @pl.when(i < nused_ref[0])
    def _():
        def issue(r, carry):
            _row_copy(h2_hbm, tok_ref[i * M + r], xbuf, r, sem).start()
            return carry

        lax.fori_loop(0, M, issue, 0)

        def drain(r, carry):
            _row_copy(h2_hbm, 0, xbuf, r, sem).wait()
            return carry

        lax.fori_loop(0, M, drain, 0)
        xb = xbuf[...].astype(BF16)
        hg = _dot(xb, wg_ref[0].astype(BF16))
        hu = _dot(xb, wu_ref[0].astype(BF16))
        hb = (hg * jax.nn.sigmoid(hg) * hu).astype(BF16)
        o_ref[...] = _dot(hb, wd_ref[0].astype(BF16)) * rw_ref[...]

    @pl.when(i >= nused_ref[0])
    def _():
        o_ref[...] = jnp.zeros(o_ref.shape, F32)


def _experts(h2, block_expert, n_used, row_tok, row_w, w_gate, w_up, w_down):
    T, D = h2.shape
    n_rows = row_tok.shape[0]
    n_blocks = n_rows // MOE_BLOCK
    grid_spec = pltpu.PrefetchScalarGridSpec(
        num_scalar_prefetch=3,
        grid=(n_blocks,),
        in_specs=[
            pl.BlockSpec(memory_space=pl.ANY),
            pl.BlockSpec((1, D, D_EXPERT), lambda i, be, nu, tok: (be[i], 0, 0)),
            pl.BlockSpec((1, D, D_EXPERT), lambda i, be, nu, tok: (be[i], 0, 0)),
            pl.BlockSpec((1, D_EXPERT, D), lambda i, be, nu, tok: (be[i], 0, 0)),
            pl.BlockSpec((MOE_BLOCK, 1), lambda i, be, nu, tok: (i, 0)),
        ],
        out_specs=pl.BlockSpec((MOE_BLOCK, D), lambda i, be, nu, tok: (i, 0)),
        scratch_shapes=[pltpu.VMEM((MOE_BLOCK, D), F32), pltpu.SemaphoreType.DMA(())],
    )
    return pl.pallas_call(
        _expert_kernel,
        grid_spec=grid_spec,
        out_shape=jax.ShapeDtypeStruct((n_rows, D), F32),
        compiler_params=_params(("arbitrary",)),
        name="experts",
    )(block_expert, n_used, row_tok, h2, w_gate, w_up, w_down, row_w)


def _combine_kernel(pos_ref, y_hbm, x1_ref, o_ref, ybuf, sem):
    i = pl.program_id(0)
    tm = x1_ref.shape[0]

    def issue(r, carry):
        a = (i * tm + r) * TOPK_IN_GROUP
        _row_copy(y_hbm, pos_ref[a], ybuf.at[0], r, sem).start()
        _row_copy(y_hbm, pos_ref[a + 1], ybuf.at[1], r, sem).start()
        return carry

    lax.fori_loop(0, tm, issue, 0)

    def drain(r, carry):
        _row_copy(y_hbm, 0, ybuf.at[0], r, sem).wait()
        _row_copy(y_hbm, 0, ybuf.at[1], r, sem).wait()
        return carry

    lax.fori_loop(0, tm, drain, 0)
    o_ref[...] = x1_ref[...] + (ybuf[0] + ybuf[1])


def _combine(pos, y_rows, x1, tm):
    T, D = x1.shape
    grid_spec = pltpu.PrefetchScalarGridSpec(
        num_scalar_prefetch=1,
        grid=(T // tm,),
        in_specs=[
            pl.BlockSpec(memory_space=pl.ANY),
            pl.BlockSpec((tm, D), lambda i, pos: (i, 0)),
        ],
        out_specs=pl.BlockSpec((tm, D), lambda i, pos: (i, 0)),
        scratch_shapes=[pltpu.VMEM((TOPK_IN_GROUP, tm, D), F32), pltpu.SemaphoreType.DMA(())],
    )
    return pl.pallas_call(
        _combine_kernel,
        grid_spec=grid_spec,
        out_shape=jax.ShapeDtypeStruct((T, D), F32),
        compiler_params=_params(("arbitrary",)),
        name="combine",
    )(pos, y_rows, x1)


def _dispatch_plan(rt, T):
    n_assign = T * TOPK_IN_GROUP
    e_flat = rt[:, 0:TOPK_IN_GROUP].astype(jnp.int32).reshape(-1)
    w_flat = rt[:, TOPK_IN_GROUP:2 * TOPK_IN_GROUP].reshape(-1)
    order = jnp.argsort(e_flat)
    e_sorted = e_flat[order]
    counts = jnp.bincount(e_flat, length=N_EXPERTS)
    padded = (counts + MOE_BLOCK - 1) // MOE_BLOCK * MOE_BLOCK
    pad_end = jnp.cumsum(padded)
    pad_start = pad_end - padded
    start = jnp.cumsum(counts) - counts
    dest = (pad_start[e_sorted] + (jnp.arange(n_assign) - start[e_sorted])).astype(jnp.int32)
    n_rows = (n_assign + N_EXPERTS * (MOE_BLOCK - 1) + MOE_BLOCK - 1) // MOE_BLOCK * MOE_BLOCK
    n_blocks = n_rows // MOE_BLOCK
    row_tok = jnp.zeros((n_rows,), jnp.int32).at[dest].set((order // TOPK_IN_GROUP).astype(jnp.int32))
    row_w = jnp.zeros((n_rows,), F32).at[dest].set(w_flat[order])
    block_expert = jnp.minimum(
        jnp.searchsorted(pad_end, jnp.arange(n_blocks) * MOE_BLOCK, side='right'),
        N_EXPERTS - 1).astype(jnp.int32)
    n_used = (pad_end[-1:] // MOE_BLOCK).astype(jnp.int32)
    pos = jnp.zeros((n_assign,), jnp.int32).at[order].set(dest)
    return block_expert, n_used, row_tok, row_w.reshape(n_rows, 1), pos


def _layer(x, norm1_g, w_in, cmp_pos, cmp_w1, cmp_w2, q_norm_g, k_norm_g, ret_gn_g, ret_gn_b,
           w_out, norm2_g, w_rg, b_rg, w_re, b_re, w_eg, w_eu, w_ed, tiles):
    B, S, D = x.shape
    T = B * S
    xt = x.reshape(T, D)

    w_main = jnp.concatenate([w_in[:, :GATE_COL0], w_in[:, GATE_COL0 + NSA_GATE_WIDTH:]], axis=1).astype(BF16)
    gpg = NSA_GATE_WIDTH // NSA_KV_GROUPS
    w_gate = jnp.concatenate(
        [jnp.pad(w_in[:, GATE_COL0 + g * gpg:GATE_COL0 + (g + 1) * gpg], ((0, 0), (0, LANES - gpg)))
         for g in range(NSA_KV_GROUPS)], axis=1).astype(BF16)
    proj, gate = _in_proj(xt, norm1_g.reshape(1, D), w_main, w_gate, tiles["tm_in"], tiles["tn_in"])

    cmp_kv = _compress(proj, cmp_pos, cmp_w1, cmp_w2, k_norm_g, B, S)
    o_nsa = _nsa(proj, gate, cmp_kv, q_norm_g.reshape(1, HEAD_DIM), k_norm_g, B, S, tiles["tq"], tiles["tk"])

    half = HEAD_DIM // 2
    inv_freq = ROPE_BASE ** (-jnp.arange(half, dtype=F32) / half)
    ang = jnp.arange(S, dtype=F32)[:, None] * inv_freq[None, :]
    cos2 = jnp.concatenate([jnp.cos(ang), jnp.cos(ang)], axis=1)
    sin2 = jnp.concatenate([-jnp.sin(ang), jnp.sin(ang)], axis=1)
    log_g = jnp.log1p(-jnp.exp2(-5.0 - jnp.arange(RET_HEADS, dtype=F32)))
    o_ret = _retention(proj, log_g, cos2, sin2, ret_gn_g, ret_gn_b, B, S)

    n_r = N_GROUPS + N_EXPERTS
    w_router = jnp.pad(jnp.concatenate([w_rg, w_re], axis=1), ((0, 0), (0, LANES - n_r)))
    b_router = jnp.pad(jnp.concatenate([b_rg, b_re]), (0, LANES - n_r)).reshape(1, LANES)
    x1, h2, rt = _out_proj(o_nsa, o_ret, w_out.astype(BF16), xt, norm2_g.reshape(1, D),
                           w_router, b_router, tiles["tm_out"])

    block_expert, n_used, row_tok, row_w, pos = _dispatch_plan(rt, T)
    y_rows = _experts(h2, block_expert, n_used, row_tok, row_w, w_eg, w_eu, w_ed)
    out = _combine(pos, y_rows, x1, tiles["tm_cmb"])
    return out.reshape(B, S, D)


def _tiles(T, S):
    return {
        "tm_in": min(1024, T), "tn_in": 512,
        "tq": min(128, S), "tk": min(512, S),
        "tm_out": min(256, T),
        "tm_cmb": min(256, T),
    }


def kernel(x, norm1_g, w_in, cmp_pos, cmp_w1, cmp_w2, q_norm_g, k_norm_g, ret_gn_g, ret_gn_b, w_out, norm2_g, w_router_group, b_router_group, w_router_expert, b_router_expert, w_exp_gate, w_exp_up, w_exp_down):
    B, S, _ = x.shape
    tiles = _tiles(B * S, S)
    for l in range(norm1_g.shape[0]):
        x = _layer(x, norm1_g[l], w_in[l], cmp_pos[l], cmp_w1[l], cmp_w2[l], q_norm_g[l], k_norm_g[l],
                   ret_gn_g[l], ret_gn_b[l], w_out[l], norm2_g[l], w_router_group[l], b_router_group[l],
                   w_router_expert[l], b_router_expert[l], w_exp_gate[l], w_exp_up[l], w_exp_down[l], tiles)
    return x
```

```python
import functools

import numpy as np
import jax
import jax.numpy as jnp
from jax import lax
from jax.experimental import pallas as pl
from jax.experimental.pallas import tpu as pltpu

F32 = jnp.float32
BF16 = jnp.bfloat16

D_MODEL = 2048
NSA_HEADS = 8
NSA_KV_GROUPS = 2
NSA_GROUP_HEADS = NSA_HEADS // NSA_KV_GROUPS
HEAD_DIM = 128
RET_HEADS = 8
CMP_BLOCK = 32
CMP_STRIDE = 16
SEL_BLOCK = 64
SEL_TOPK = 8
WINDOW = 512
RET_CHUNK = 128
ROPE_BASE = 10000.0
N_GROUPS = 4
EXPERTS_PER_GROUP = 8
N_EXPERTS = N_GROUPS * EXPERTS_PER_GROUP
TOPK_IN_GROUP = 2
D_EXPERT = 512
MOE_BLOCK = 256
RMS_EPS = 1e-6
GN_EPS = 1e-5
NEG_INF = -1e30
FORCED_SCORE = 1e6
LOG2E = 1.4426950408889634

NSA_Q_WIDTH = NSA_HEADS * HEAD_DIM
NSA_KV_WIDTH = NSA_KV_GROUPS * HEAD_DIM
NSA_GATE_WIDTH = 3 * NSA_HEADS
RET_WIDTH = RET_HEADS * HEAD_DIM
GATE_COL0 = NSA_Q_WIDTH + 6 * NSA_KV_WIDTH
MAIN_WIDTH = GATE_COL0 + 4 * RET_WIDTH
LANES = 128
VMEM_LIMIT = 56 * 1024 * 1024

KC_BLK, VC_BLK, KS_BLK, VS_BLK, KW_BLK, VW_BLK = 8, 10, 12, 14, 16, 18


def _rms(xf, g):
    return xf * lax.rsqrt(jnp.mean(xf * xf, axis=-1, keepdims=True) + RMS_EPS) * g


def _dot(a, b):
    return jnp.dot(a, b, preferred_element_type=F32)


def _dot_nt(a, b):
    return lax.dot_general(a, b, (((1,), (1,)), ((), ())), preferred_element_type=F32)


def _dot_tn(a, b):
    return lax.dot_general(a, b, (((0,), (0,)), ((), ())), preferred_element_type=F32)


def _split3(p):
    p1 = p.astype(BF16)
    r1 = p - p1.astype(F32)
    p2 = r1.astype(BF16)
    p3 = (r1 - p2.astype(F32)).astype(BF16)
    return p1, p2, p3


def _params(sem):
    return pltpu.CompilerParams(dimension_semantics=sem, vmem_limit_bytes=VMEM_LIMIT)


def _in_proj_kernel(x_ref, g_ref, w_ref, wg_ref, o_ref, og_ref, h_scr):
    @pl.when(pl.program_id(1) == 0)
    def _():
        h = _rms(x_ref[...], g_ref[...]).astype(BF16)
        h_scr[...] = h
        og_ref[...] = _dot(h, wg_ref[...])

    o_ref[...] = _dot(h_scr[...], w_ref[...])


def _in_proj(xt, g1, w_main, w_gate, tm, tn):
    T, D = xt.shape
    n_main = w_main.shape[1]
    n_gate = w_gate.shape[1]
    return pl.pallas_call(
        _in_proj_kernel,
        grid=(T // tm, n_main // tn),
        in_specs=[
            pl.BlockSpec((tm, D), lambda m, n: (m, 0)),
            pl.BlockSpec((1, D), lambda m, n: (0, 0)),
            pl.BlockSpec((D, tn), lambda m, n: (0, n)),
            pl.BlockSpec((D, n_gate), lambda m, n: (0, 0)),
        ],
        out_specs=[
            pl.BlockSpec((tm, tn), lambda m, n: (m, n)),
            pl.BlockSpec((tm, n_gate), lambda m, n: (m, 0)),
        ],
        out_shape=[
            jax.ShapeDtypeStruct((T, n_main), F32),
            jax.ShapeDtypeStruct((T, n_gate), F32),
        ],
        scratch_shapes=[pltpu.VMEM((tm, D), BF16)],
        compiler_params=_params(("parallel", "arbitrary")),
        name="in_proj",
    )(xt, g1, w_main, w_gate)


def _compress_kernel(x_ref, pos_ref, w1_ref, w2_ref, kg_ref, o_ref, xs_scr, *, S, n_pad):
    kv = pl.program_id(1)
    xs_scr[0:S, :] = x_ref[...]
    xs_scr[S:S + CMP_STRIDE, :] = jnp.zeros((CMP_STRIDE, HEAD_DIM), F32)
    acc = jnp.zeros((n_pad, HEAD_DIM), F32)
    for l in range(CMP_BLOCK):
        tb = xs_scr[pl.ds(l, n_pad, stride=CMP_STRIDE), :] + pos_ref[0, l:l + 1, :]
        acc = acc + _dot(tb.astype(BF16), w1_ref[0, l].astype(BF16))
    hid = acc * jax.nn.sigmoid(acc)
    out = _dot(hid.astype(BF16), w2_ref[0].astype(BF16))
    normed = _rms(out, kg_ref[0:1, :])
    o_ref[0, 0, 0] = jnp.where(kv == 0, normed, out).astype(BF16)


def _compress(proj, cmp_pos, cmp_w1, cmp_w2, k_norm_g, B, S):
    n_pad = S // CMP_STRIDE
    G = NSA_KV_GROUPS
    kern = functools.partial(_compress_kernel, S=S, n_pad=n_pad)
    return pl.pallas_call(
        kern,
        grid=(B, 2, G),
        in_specs=[
            pl.BlockSpec((S, HEAD_DIM), lambda b, kv, g: (b, KC_BLK + 2 * kv + g)),
            pl.BlockSpec((1, CMP_BLOCK, HEAD_DIM), lambda b, kv, g: (kv, 0, 0)),
            pl.BlockSpec((1, CMP_BLOCK, HEAD_DIM, HEAD_DIM), lambda b, kv, g: (kv, 0, 0, 0)),
            pl.BlockSpec((1, HEAD_DIM, HEAD_DIM), lambda b, kv, g: (kv, 0, 0)),
            pl.BlockSpec((3, HEAD_DIM), lambda b, kv, g: (0, 0)),
        ],
        out_specs=pl.BlockSpec((1, 1, 1, n_pad, HEAD_DIM), lambda b, kv, g: (b, kv, g, 0, 0)),
        out_shape=jax.ShapeDtypeStruct((B, 2, G, n_pad, HEAD_DIM), BF16),
        scratch_shapes=[pltpu.VMEM((S + CMP_STRIDE, HEAD_DIM), F32)],
        compiler_params=_params(("parallel", "parallel", "parallel")),
        name="compress",
    )(proj, cmp_pos, cmp_w1, cmp_w2, k_norm_g)


def _masked_softmax(s, msk):
    sm = jnp.where(msk, s, NEG_INF)
    m = jnp.max(sm, axis=-1, keepdims=True)
    e = jnp.where(msk, jnp.exp(sm - m), 0.0)
    den = jnp.sum(e, axis=-1, keepdims=True)
    return e / jnp.where(den > 0.0, den, 1.0)


def _nsa_kernel(q_ref, kc_ref, vc_ref, ks_ref, vs_ref, kw_ref, vw_ref, gate_ref, qg_ref, kg_ref,
                ovt_ref, ex_ref, o_ref, ksn, vsb, kwn, vwb, l_scr, acc_scr, *, S, tq, tk, wk):
    i = pl.program_id(2)
    H = NSA_GROUP_HEADS
    scale = HEAD_DIM ** -0.5
    c2 = scale * LOG2E
    n_cp = S // CMP_STRIDE
    n_sblk = S // SEL_BLOCK

    @pl.when(i == 0)
    def _():
        ksn[...] = _rms(ks_ref[...], kg_ref[1:2, :]).astype(BF16)
        kwn[...] = _rms(kw_ref[...], kg_ref[2:3, :]).astype(BF16)
        vsb[...] = vs_ref[...].astype(BF16)
        vwb[...] = vw_ref[...].astype(BF16)

    pos = i * tq + lax.broadcasted_iota(jnp.int32, (tq, 1), 0)
    qg = qg_ref[...]
    q4 = jnp.concatenate([_rms(q_ref[:, h * HEAD_DIM:(h + 1) * HEAD_DIM], qg).astype(BF16)
                          for h in range(H)], axis=0)
    rows = [slice(h * tq, (h + 1) * tq) for h in range(H)]

    ncol = lax.broadcasted_iota(jnp.int32, (1, n_cp), 1)
    cmask = (ncol * CMP_STRIDE + (CMP_BLOCK - 1)) <= pos
    s_c4 = _dot_nt(q4, kc_ref[0, 0, 0]) * scale
    ps = [_masked_softmax(s_c4[r], cmask) for r in rows]
    psum = ps[0] + ps[1] + ps[2] + ps[3]
    o_c4 = _dot(jnp.concatenate([p.astype(BF16) for p in ps], axis=0), vc_ref[0, 0, 0])

    ovt = ovt_ref[...]
    p1, p2, p3 = _split3(psum)
    imp_t = (_dot_nt(ovt, p1) + _dot_nt(ovt, p2) + _dot_nt(ovt, p3))[0:n_sblk, :]
    pos_t = i * tq + lax.broadcasted_iota(jnp.int32, (1, tq), 1)
    jrow = lax.broadcasted_iota(jnp.int32, (n_sblk, 1), 0)
    cur_t = pos_t // SEL_BLOCK
    valid_t = jrow * SEL_BLOCK <= pos_t
    forced_t = (jrow == 0) | (jrow == cur_t) | (jrow == cur_t - 1)
    score = jnp.where(valid_t, jnp.where(forced_t, FORCED_SCORE, imp_t), -jnp.inf)
    rank = jnp.zeros((n_sblk, tq), jnp.int32)
    for i2 in range(n_sblk):
        si = score[i2:i2 + 1, :]
        ahead = (si > score) | ((si == score) & (jrow > i2))
        rank = rank + jnp.where(ahead, 1, 0)
    sel_t = jnp.where((rank < SEL_TOPK) & valid_t, 1.0, 0.0)
    sel_t = jnp.concatenate([sel_t, jnp.zeros((LANES - n_sblk, tq), F32)], axis=0)
    sel = sel_t.T.astype(BF16)

    def slc_prefix(n):
        kpos = lax.broadcasted_iota(jnp.int32, (1, n), 1)
        bias = jnp.where((_dot(sel, ex_ref[:, 0:n]) > 0.5) & (kpos <= pos), 0.0, NEG_INF)
        s4 = _dot_nt(q4, ksn[0:n, :])
        es = []
        for r in rows:
            t = s4[r] * c2 + bias
            e = jnp.exp2(t - jnp.max(t, axis=-1, keepdims=True))
            l_scr[r] = jnp.sum(e, axis=-1, keepdims=True)
            es.append(e.astype(BF16))
        acc_scr[...] = _dot(jnp.concatenate(es, axis=0), vsb[0:n, :])

    n_cls = S // tk
    cls = ((i + 1) * tq - 1) // tk
    for c in range(n_cls):
        pl.when(cls == c)(functools.partial(slc_prefix, (c + 1) * tk))

    kst = pl.multiple_of(jnp.clip(i * tq - WINDOW, 0, S - wk), tq)
    wpos = kst + lax.broadcasted_iota(jnp.int32, (1, wk), 1)
    wbias = jnp.where((wpos <= pos) & (wpos > pos - WINDOW), 0.0, NEG_INF)
    s_w4 = _dot_nt(q4, kwn[pl.ds(kst, wk), :])
    ews, lws = [], []
    for r in rows:
        t = s_w4[r] * c2 + wbias
        e = jnp.exp2(t - jnp.max(t, axis=-1, keepdims=True))
        lws.append(jnp.sum(e, axis=-1, keepdims=True))
        ews.append(e.astype(BF16))
    o_w4 = _dot(jnp.concatenate(ews, axis=0), vwb[pl.ds(kst, wk), :])

    gates = jax.nn.sigmoid(gate_ref[...])
    for h, r in enumerate(rows):
        c = 3 * h
        o = (o_c4[r] * gates[:, c:c + 1]
             + acc_scr[r] * (gates[:, c + 1:c + 2] / l_scr[r])
             + o_w4[r] * (gates[:, c + 2:c + 3] / lws[h]))
        o_ref[:, h * HEAD_DIM:(h + 1) * HEAD_DIM] = o.astype(BF16)


def _nsa_tables(S):
    n_cp, n_cmp, n_sblk = S // CMP_STRIDE, (S - CMP_BLOCK) // CMP_STRIDE + 1, S // SEL_BLOCK
    n = np.arange(n_cp)[None, :] * CMP_STRIDE
    j = np.arange(LANES)[:, None]
    ovt = ((n < (j + 1) * SEL_BLOCK) & (n + CMP_BLOCK > j * SEL_BLOCK)
           & (np.arange(n_cp)[None, :] < n_cmp) & (j < n_sblk))
    expand = (np.arange(S) // SEL_BLOCK)[None, :] == np.arange(LANES)[:, None]
    return jnp.asarray(ovt, BF16), jnp.asarray(expand, BF16)


def _nsa(proj, gate, cmp_kv, q_norm_g, k_norm_g, B, S, tq, tk):
    G = NSA_KV_GROUPS
    H = NSA_GROUP_HEADS
    n_cp = S // CMP_STRIDE
    wk = min(S, WINDOW + tq)
    nq = S // tq
    gw = H * HEAD_DIM
    ovt, expand = _nsa_tables(S)
    kern = functools.partial(_nsa_kernel, S=S, tq=tq, tk=tk, wk=wk)
    kv_spec = lambda blk: pl.BlockSpec((S, HEAD_DIM), lambda b, g, i: (b, blk + g))
    return pl.pallas_call(
        kern,
        grid=(B, G, nq),
        in_specs=[
            pl.BlockSpec((tq, gw), lambda b, g, i: (b * nq + i, g)),
            pl.BlockSpec((1, 1, 1, n_cp, HEAD_DIM), lambda b, g, i: (b, 0, g, 0, 0)),
            pl.BlockSpec((1, 1, 1, n_cp, HEAD_DIM), lambda b, g, i: (b, 1, g, 0, 0)),
            kv_spec(KS_BLK), kv_spec(VS_BLK), kv_spec(KW_BLK), kv_spec(VW_BLK),
            pl.BlockSpec((tq, LANES), lambda b, g, i: (b * nq + i, g)),
            pl.BlockSpec((1, HEAD_DIM), lambda b, g, i: (0, 0)),
            pl.BlockSpec((3, HEAD_DIM), lambda b, g, i: (0, 0)),
            pl.BlockSpec((LANES, n_cp), lambda b, g, i: (0, 0)),
            pl.BlockSpec((LANES, S), lambda b, g, i: (0, 0)),
        ],
        out_specs=pl.BlockSpec((tq, gw), lambda b, g, i: (b * nq + i, g)),
        out_shape=jax.ShapeDtypeStruct((B * S, NSA_Q_WIDTH), BF16),
        scratch_shapes=[pltpu.VMEM((S, HEAD_DIM), BF16)] * 4 + [
            pltpu.VMEM((H * tq, 1), F32), pltpu.VMEM((H * tq, HEAD_DIM), F32)],
        compiler_params=_params(("parallel", "parallel", "arbitrary")),
        name="nsa",
    )(proj, cmp_kv, cmp_kv, proj, proj, proj, proj, gate, q_norm_g, k_norm_g, ovt, expand)


RET_HB = 4


def _ret_kernel(lg_ref, rq_ref, rk_ref, rv_ref, rg_ref, cos_ref, sin_ref, gg_ref, gb_ref,
                o_ref, r_scr):
    hg = pl.program_id(1)
    C = RET_CHUNK
    scale = HEAD_DIM ** -0.5

    @pl.when(pl.program_id(2) == 0)
    def _():
        r_scr[...] = jnp.zeros(r_scr.shape, F32)

    cos = cos_ref[...]
    sin = sin_ref[...]
    n_col = lax.broadcasted_iota(jnp.int32, (C, 1), 0).astype(F32)
    n_row = lax.broadcasted_iota(jnp.int32, (1, C), 1).astype(F32)
    diff = n_col - n_row
    for h in range(RET_HB):
        sl = slice(h * HEAD_DIM, (h + 1) * HEAD_DIM)
        hh = hg * RET_HB + h
        lg = lg_ref[hh]
        q = rq_ref[:, sl]
        k = rk_ref[:, sl]
        qf = q * cos + pltpu.roll(q, HEAD_DIM // 2, 1) * sin
        kf = (k * cos + pltpu.roll(k, HEAD_DIM // 2, 1) * sin) * scale
        v = rv_ref[:, sl].astype(BF16)
        dec = jnp.where(diff >= 0.0, jnp.exp(lg * jnp.maximum(diff, 0.0)), 0.0)
        xi = jnp.exp(lg * (n_col + 1.0))
        zeta = jnp.exp(lg * (C - 1.0 - n_col))
        cd = jnp.exp(jnp.full((1, HEAD_DIM), lg * float(C), F32))
        qb = qf.astype(BF16)
        r_old = r_scr[h]
        a = _dot_nt(qb, kf.astype(BF16)) * dec
        o = _dot(a.astype(BF16), v) + _dot(qb, r_old.astype(BF16)) * xi
        r_scr[h] = r_old * cd + _dot_tn((kf * zeta).astype(BF16), v)
        mu = jnp.mean(o, axis=-1, keepdims=True)
        d = o - mu
        var = jnp.mean(d * d, axis=-1, keepdims=True)
        y = d * lax.rsqrt(var + GN_EPS) * gg_ref[pl.ds(hh, 1), :] + gb_ref[pl.ds(hh, 1), :]
        gt = rg_ref[:, sl]
        o_ref[:, sl] = (gt * jax.nn.sigmoid(gt) * y).astype(BF16)


def _retention(proj, log_g, cos2, sin2, gn_g, gn_b, B, S):
    C = RET_CHUNK
    n_ch = S // C
    bw = RET_HB * HEAD_DIM
    base = GATE_COL0 // bw
    nhb = RET_HEADS // RET_HB
    spec = lambda k: pl.BlockSpec((C, bw), lambda b, hg, c, lg: (b * n_ch + c, base + k * nhb + hg))
    grid_spec = pltpu.PrefetchScalarGridSpec(
        num_scalar_prefetch=1,
        grid=(B, nhb, n_ch),
        in_specs=[
            spec(0), spec(1), spec(2), spec(3),
            pl.BlockSpec((C, HEAD_DIM), lambda b, hg, c, lg: (c, 0)),
            pl.BlockSpec((C, HEAD_DIM), lambda b, hg, c, lg: (c, 0)),
            pl.BlockSpec((RET_HEADS, HEAD_DIM), lambda b, hg, c, lg: (0, 0)),
            pl.BlockSpec((RET_HEADS, HEAD_DIM), lambda b, hg, c, lg: (0, 0)),
        ],
        out_specs=pl.BlockSpec((C, bw), lambda b, hg, c, lg: (b * n_ch + c, hg)),
        scratch_shapes=[pltpu.VMEM((RET_HB, HEAD_DIM, HEAD_DIM), F32)],
    )
    return pl.pallas_call(
        _ret_kernel,
        grid_spec=grid_spec,
        out_shape=jax.ShapeDtypeStruct((B * S, RET_WIDTH), BF16),
        compiler_params=_params(("parallel", "parallel", "arbitrary")),
        name="retention",
    )(log_g, proj, proj, proj, proj, cos2, sin2, gn_g, gn_b)


def _lane_max(v):
    return jnp.max(v, axis=-1, keepdims=True)


def _lane_min(v):
    return jnp.min(v, axis=-1, keepdims=True)


def _out_kernel(on_ref, or_ref, w_ref, x_ref, g2_ref, wr_ref, br_ref, tri_ref,
                x1_ref, h2_ref, rt_ref, cnt_ref, cnt_scr):
    @pl.when(pl.program_id(0) == 0)
    def _():
        cnt_scr[...] = jnp.zeros(cnt_scr.shape, F32)

    half = on_ref.shape[1]
    acc = _dot(on_ref[...], w_ref[0:half, :]) + _dot(or_ref[...], w_ref[half:2 * half, :])
    x1 = x_ref[...] + acc
    x1_ref[...] = x1
    h2 = _rms(x1, g2_ref[...])
    h2_ref[...] = h2

    h_hi = h2.astype(BF16)
    h_lo = (h2 - h_hi.astype(F32)).astype(BF16)
    wr = wr_ref[...]
    w_hi = wr.astype(BF16)
    w_lo = (wr - w_hi.astype(F32)).astype(BF16)
    logits = _dot(h_hi, w_hi) + _dot(h_lo, w_hi) + _dot(h_hi, w_lo) + br_ref[...]

    tm = logits.shape[0]
    lane = lax.broadcasted_iota(jnp.int32, (tm, LANES), 1)
    big = jnp.int32(LANES)
    gm = lane < N_GROUPS
    gl = jnp.where(gm, logits, -jnp.inf)
    ge = jnp.where(gm, jnp.exp(gl - _lane_max(gl)), 0.0)
    pg = ge / jnp.sum(ge, axis=-1, keepdims=True)
    g_top = _lane_max(pg)
    g_idx = _lane_min(jnp.where(gm & (pg == g_top), lane, big))
    e0 = N_GROUPS + g_idx * EXPERTS_PER_GROUP
    em = (lane >= e0) & (lane < e0 + EXPERTS_PER_GROUP)
    el = jnp.where(em, logits, -jnp.inf)
    ee = jnp.where(em, jnp.exp(el - _lane_max(el)), 0.0)
    pe = ee / jnp.sum(ee, axis=-1, keepdims=True)
    t1 = _lane_max(jnp.where(em, pe, -1.0))
    i1 = _lane_min(jnp.where(em & (pe == t1), lane, big))
    em2 = em & (lane != i1)
    t2 = _lane_max(jnp.where(em2, pe, -1.0))
    i2 = _lane_min(jnp.where(em2 & (pe == t2), lane, big))
    tsum = t1 + t2
    w1 = g_top * t1 / tsum
    w2 = g_top * t2 / tsum
    e1 = i1 - N_GROUPS
    e2 = i2 - N_GROUPS
    oh1 = jnp.where(lane == e1, 1.0, 0.0)
    oh2 = jnp.where(lane == e2, 1.0, 0.0)
    both = oh1 + oh2
    before = _dot(tri_ref[...], both.astype(BF16)) + cnt_scr[...]
    r1 = jnp.sum(before * oh1, axis=-1, keepdims=True)
    r2 = jnp.sum(before * oh2, axis=-1, keepdims=True)
    cnt_scr[...] = cnt_scr[...] + jnp.sum(both, axis=0, keepdims=True)
    cnt_ref[...] = jnp.broadcast_to(cnt_scr[...], cnt_ref.shape)

    cols = (e1.astype(F32), e2.astype(F32), w1, w2, r1, r2)
    rt = jnp.zeros((tm, LANES), F32)
    for c, v in enumerate(cols):
        rt = jnp.where(lane == c, v, rt)
    rt_ref[...] = rt


def _out_proj(o_nsa, o_ret, w_out, xt, g2, w_router, b_router, tm):
    T, D = xt.shape
    half = o_nsa.shape[1]
    tri = jnp.asarray(np.tril(np.ones((tm, tm), np.float32), -1), BF16)
    return pl.pallas_call(
        _out_kernel,
        grid=(T // tm,),
        in_specs=[
            pl.BlockSpec((tm, half), lambda m: (m, 0)),
            pl.BlockSpec((tm, half), lambda m: (m, 0)),
            pl.BlockSpec((2 * half, D), lambda m: (0, 0)),
            pl.BlockSpec((tm, D), lambda m: (m, 0)),
            pl.BlockSpec((1, D), lambda m: (0, 0)),
            pl.BlockSpec((D, LANES), lambda m: (0, 0)),
            pl.BlockSpec((1, LANES), lambda m: (0, 0)),
            pl.BlockSpec((tm, tm), lambda m: (0, 0)),
        ],
        out_specs=[
            pl.BlockSpec((tm, D), lambda m: (m, 0)),
            pl.BlockSpec((tm, D), lambda m: (m, 0)),
            pl.BlockSpec((tm, LANES), lambda m: (m, 0)),
            pl.BlockSpec((8, LANES), lambda m: (0, 0)),
        ],
        out_shape=[
            jax.ShapeDtypeStruct((T, D), F32),
            jax.ShapeDtypeStruct((T, D), F32),
            jax.ShapeDtypeStruct((T, LANES), F32),
            jax.ShapeDtypeStruct((8, LANES), F32),
        ],
        scratch_shapes=[pltpu.VMEM((1, LANES), F32)],
        compiler_params=_params(("arbitrary",)),
        name="out_proj",
    )(o_nsa, o_ret, w_out, xt, g2, w_router, b_router, tri)


def _plan_kernel(pos_ref, tok_ref):
    def clear(r, carry):
        tok_ref[r] = 0
        return carry

    lax.fori_loop(0, tok_ref.shape[0], clear, 0, unroll=8)

    def place(a, carry):
        tok_ref[pos_ref[a]] = lax.shift_right_logical(a, 1)
        return carry

    lax.fori_loop(0, pos_ref.shape[0], place, 0, unroll=8)


def _plan(pos, n_rows):
    return pl.pallas_call(
        _plan_kernel,
        in_specs=[pl.BlockSpec(memory_space=pltpu.SMEM)],
        out_specs=pl.BlockSpec(memory_space=pltpu.SMEM),
        out_shape=jax.ShapeDtypeStruct((n_rows,), jnp.int32),
        name="plan",
    )(pos)


def _row_copy(src_hbm, row, dst, slot, sem):
    return pltpu.make_async_copy(src_hbm.at[pl.ds(row, 1), :], dst.at[pl.ds(slot, 1), :], sem)


def _expert_kernel(be_ref, nused_ref, tok_ref, h2_hbm, wg_ref, wu_ref, wd_ref,
                   o_ref, xbuf, wg_b, wu_b, wd_b, sems):
    i = pl.program_id(0)
    M = MOE_BLOCK
    n_used = nused_ref[0]
    slot = lax.rem(i, 2)
    nslot = 1 - slot

    def wait_block(s):
        pltpu.make_async_copy(h2_hbm.at[pl.ds(0, M), :], xbuf.at[s], sems.at[s]).wait()

    @pl.when(i == 0)
    def _():
        def issue(r, carry):
            _row_copy(h2_hbm, tok_ref[r], xbuf.at[0], r, sems.at[0]).start()
            return carry

        lax.fori_loop(0, M, issue, 0, unroll=8)

    @pl.when(i < n_used)
    def _():
        wait_block(slot)
        nbase = (i + 1) * M
        for r in range(M):
            _row_copy(h2_hbm, tok_ref[nbase + r], xbuf.at[nslot], r, sems.at[nslot]).start()

        prev = jnp.maximum(i - 1, 0)

        @pl.when((i == 0) | (be_ref[i] != be_ref[prev]))
        def _():
            wg_b[...] = wg_ref[0].astype(BF16)
            wu_b[...] = wu_ref[0].astype(BF16)
            wd_b[...] = wd_ref[0].astype(BF16)

        xb = xbuf[slot].astype(BF16)
        hg = _dot(xb, wg_b[...])
        hu = _dot(xb, wu_b[...])
        hb = (hg * jax.nn.sigmoid(hg) * hu).astype(BF16)
        o_ref[...] = _dot(hb, wd_b[...])

    @pl.when(i == n_used)
    def _():
        wait_block(slot)

    @pl.when(i >= n_used)
    def _():
        o_ref[...] = jnp.zeros(o_ref.shape, F32)


def _experts(h2, block_expert, n_used, row_tok, w_gate, w_up, w_down):
    T, D = h2.shape
    n_rows = row_tok.shape[0]
    n_blocks = n_rows // MOE_BLOCK
    assert (TOPK_IN_GROUP * T + N_EXPERTS * (MOE_BLOCK - 1)) // MOE_BLOCK < n_blocks
    grid_spec = pltpu.PrefetchScalarGridSpec(
        num_scalar_prefetch=3,
        grid=(n_blocks,),
        in_specs=[
            pl.BlockSpec(memory_space=pl.ANY),
            pl.BlockSpec((1, D, D_EXPERT), lambda i, be, nu, tok: (be[i], 0, 0)),
            pl.BlockSpec((1, D, D_EXPERT), lambda i, be, nu, tok: (be[i], 0, 0)),
            pl.BlockSpec((1, D_EXPERT, D), lambda i, be, nu, tok: (be[i], 0, 0)),
        ],
        out_specs=pl.BlockSpec((MOE_BLOCK, D), lambda i, be, nu, tok: (i, 0)),
        scratch_shapes=[
            pltpu.VMEM((2, MOE_BLOCK, D), F32),
            pltpu.VMEM((D, D_EXPERT), BF16), pltpu.VMEM((D, D_EXPERT), BF16),
            pltpu.VMEM((D_EXPERT, D), BF16),
            pltpu.SemaphoreType.DMA((2,)),
        ],
    )
    return pl.pallas_call(
        _expert_kernel,
        grid_spec=grid_spec,
        out_shape=jax.ShapeDtypeStruct((n_rows, D), F32),
        compiler_params=_params(("arbitrary",)),
        name="experts",
    )(block_expert, n_used, row_tok, h2, w_gate, w_up, w_down)


def _combine_kernel(pos_ref, y_hbm, x1_ref, rt_ref, o_ref, ybuf, sems):
    i = pl.program_id(0)
    n = pl.num_programs(0)
    tm = x1_ref.shape[0]
    K = TOPK_IN_GROUP
    slot = lax.rem(i, 2)

    def issue_tile(t, s):
        def issue(r, carry):
            a = (t * tm + r) * K
            for k in range(K):
                _row_copy(y_hbm, pos_ref[a + k], ybuf.at[s, k], r, sems.at[s]).start()
            return carry

        lax.fori_loop(0, tm, issue, 0, unroll=8)

    @pl.when(i == 0)
    def _():
        issue_tile(0, 0)

    @pl.when(i + 1 < n)
    def _():
        issue_tile(i + 1, 1 - slot)

    for k in range(K):
        pltpu.make_async_copy(y_hbm.at[pl.ds(0, tm), :], ybuf.at[slot, k], sems.at[slot]).wait()
    w = rt_ref[...]
    yb = ybuf[slot]
    o_ref[...] = x1_ref[...] + (yb[0] * w[:, K:K + 1] + yb[1] * w[:, K + 1:K + 2])


def _combine(pos, y_rows, x1, rt, tm):
    T, D = x1.shape
    grid_spec = pltpu.PrefetchScalarGridSpec(
        num_scalar_prefetch=1,
        grid=(T // tm,),
        in_specs=[
            pl.BlockSpec(memory_space=pl.ANY),
            pl.BlockSpec((tm, D), lambda i, pos: (i, 0)),
            pl.BlockSpec((tm, LANES), lambda i, pos: (i, 0)),
        ],
        out_specs=pl.BlockSpec((tm, D), lambda i, pos: (i, 0)),
        scratch_shapes=[pltpu.VMEM((2, TOPK_IN_GROUP, tm, D), F32), pltpu.SemaphoreType.DMA((2,))],
    )
    return pl.pallas_call(
        _combine_kernel,
        grid_spec=grid_spec,
        out_shape=jax.ShapeDtypeStruct((T, D), F32),
        compiler_params=_params(("arbitrary",)),
        name="combine",
    )(pos, y_rows, x1, rt)


def _block_layout(rt, counts, T):
    K = TOPK_IN_GROUP
    n_rows = (K * T + N_EXPERTS * (MOE_BLOCK - 1) + MOE_BLOCK - 1) // MOE_BLOCK * MOE_BLOCK
    n_blocks = n_rows // MOE_BLOCK
    cnt = counts[0, :N_EXPERTS].astype(jnp.int32)
    padded = (cnt + MOE_BLOCK - 1) // MOE_BLOCK * MOE_BLOCK
    pad_end = jnp.cumsum(padded)
    pad_start = pad_end - padded
    e = rt[:, 0:K].astype(jnp.int32)
    rank = rt[:, 2 * K:3 * K].astype(jnp.int32)
    pos = (pad_start[e] + rank).reshape(-1)
    first_row = jnp.arange(n_blocks, dtype=jnp.int32) * MOE_BLOCK
    block_expert = jnp.minimum(
        jnp.sum((pad_end[None, :] <= first_row[:, None]).astype(jnp.int32), axis=1), N_EXPERTS - 1)
    n_used = (pad_end[-1:] // MOE_BLOCK).astype(jnp.int32)
    return block_expert, n_used, pos, n_rows


def _layer(x, norm1_g, w_in, cmp_pos, cmp_w1, cmp_w2, q_norm_g, k_norm_g, ret_gn_g, ret_gn_b,
           w_out, norm2_g, w_rg, b_rg, w_re, b_re, w_eg, w_eu, w_ed, tiles):
    B, S, D = x.shape
    T = B * S
    xt = x.reshape(T, D)

    w_main = jnp.concatenate([w_in[:, :GATE_COL0], w_in[:, GATE_COL0 + NSA_GATE_WIDTH:]], axis=1).astype(BF16)
    gpg = NSA_GATE_WIDTH // NSA_KV_GROUPS
    w_gate = jnp.concatenate(
        [jnp.pad(w_in[:, GATE_COL0 + g * gpg:GATE_COL0 + (g + 1) * gpg], ((0, 0), (0, LANES - gpg)))
         for g in range(NSA_KV_GROUPS)], axis=1).astype(BF16)
    proj, gate = _in_proj(xt, norm1_g.reshape(1, D), w_main, w_gate, tiles["tm_in"], tiles["tn_in"])

    cmp_kv = _compress(proj, cmp_pos, cmp_w1, cmp_w2, k_norm_g, B, S)
    o_nsa = _nsa(proj, gate, cmp_kv, q_norm_g.reshape(1, HEAD_DIM), k_norm_g, B, S, tiles["tq"], tiles["tk"])

    half = HEAD_DIM // 2
    inv_freq = ROPE_BASE ** (-jnp.arange(half, dtype=F32) / half)
    ang = jnp.arange(S, dtype=F32)[:, None] * inv_freq[None, :]
    cos2 = jnp.concatenate([jnp.cos(ang), jnp.cos(ang)], axis=1)
    sin2 = jnp.concatenate([-jnp.sin(ang), jnp.sin(ang)], axis=1)
    log_g = jnp.log1p(-jnp.exp2(-5.0 - jnp.arange(RET_HEADS, dtype=F32)))
    o_ret = _retention(proj, log_g, cos2, sin2, ret_gn_g, ret_gn_b, B, S)

    n_r = N_GROUPS + N_EXPERTS
    w_router = jnp.pad(jnp.concatenate([w_rg, w_re], axis=1), ((0, 0), (0, LANES - n_r)))
    b_router = jnp.pad(jnp.concatenate([b_rg, b_re]), (0, LANES - n_r)).reshape(1, LANES)
    x1, h2, rt, counts = _out_proj(o_nsa, o_ret, w_out.astype(BF16), xt, norm2_g.reshape(1, D),
                                   w_router, b_router, tiles["tm_out"])

    block_expert, n_used, pos, n_rows = _block_layout(rt, counts, T)
    row_tok = _plan(pos, n_rows)
    y_rows = _experts(h2, block_expert, n_used, row_tok, w_eg, w_eu, w_ed)
    out = _combine(pos, y_rows, x1, rt, tiles["tm_cmb"])
    return out.reshape(B, S, D)


def _tiles(T, S):
    return {
        "tm_in": min(1024, T), "tn_in": 512,
        "tq": min(128, S), "tk": min(512, S),
        "tm_out": min(256, T),
        "tm_cmb": min(256, T),
    }


def kernel(x, norm1_g, w_in, cmp_pos, cmp_w1, cmp_w2, q_norm_g, k_norm_g, ret_gn_g, ret_gn_b, w_out, norm2_g, w_router_group, b_router_group, w_router_expert, b_router_expert, w_exp_gate, w_exp_up, w_exp_down):
    B, S, _ = x.shape
    tiles = _tiles(B * S, S)
    for l in range(norm1_g.shape[0]):
        x = _layer(x, norm1_g[l], w_in[l], cmp_pos[l], cmp_w1[l], cmp_w2[l], q_norm_g[l], k_norm_g[l],
                   ret_gn_g[l], ret_gn_b[l], w_out[l], norm2_g[l], w_router_group[l], b_router_group[l],
                   w_router_expert[l], b_router_expert[l], w_exp_gate[l], w_exp_up[l], w_exp_down[l], tiles)
    return x
```

```python
import functools

import numpy as np
import jax
import jax.numpy as jnp
from jax import lax
from jax.experimental import pallas as pl
from jax.experimental.pallas import tpu as pltpu

F32 = jnp.float32
BF16 = jnp.bfloat16

D_MODEL = 2048
NSA_HEADS = 8
NSA_KV_GROUPS = 2
NSA_GROUP_HEADS = NSA_HEADS // NSA_KV_GROUPS
HEAD_DIM = 128
RET_HEADS = 8
CMP_BLOCK = 32
CMP_STRIDE = 16
SEL_BLOCK = 64
SEL_TOPK = 8
WINDOW = 512
RET_CHUNK = 128
ROPE_BASE = 10000.0
N_GROUPS = 4
EXPERTS_PER_GROUP = 8
N_EXPERTS = N_GROUPS * EXPERTS_PER_GROUP
TOPK_IN_GROUP = 2
D_EXPERT = 512
MOE_BLOCK = 256
RMS_EPS = 1e-6
GN_EPS = 1e-5
NEG_INF = -1e30
FORCED_SCORE = 1e6
LOG2E = 1.4426950408889634

NSA_Q_WIDTH = NSA_HEADS * HEAD_DIM
NSA_KV_WIDTH = NSA_KV_GROUPS * HEAD_DIM
NSA_GATE_WIDTH = 3 * NSA_HEADS
RET_WIDTH = RET_HEADS * HEAD_DIM
GATE_COL0 = NSA_Q_WIDTH + 6 * NSA_KV_WIDTH
MAIN_WIDTH = GATE_COL0 + 4 * RET_WIDTH
LANES = 128
VMEM_LIMIT = 56 * 1024 * 1024

KC_BLK, VC_BLK, KS_BLK, VS_BLK, KW_BLK, VW_BLK = 8, 10, 12, 14, 16, 18


def _rms(xf, g):
    return xf * lax.rsqrt(jnp.mean(xf * xf, axis=-1, keepdims=True) + RMS_EPS) * g


def _dot(a, b):
    return jnp.dot(a, b, preferred_element_type=F32)


def _dot_nt(a, b):
    return lax.dot_general(a, b, (((1,), (1,)), ((), ())), preferred_element_type=F32)


def _dot_tn(a, b):
    return lax.dot_general(a, b, (((0,), (0,)), ((), ())), preferred_element_type=F32)


def _split3(p):
    p1 = p.astype(BF16)
    r1 = p - p1.astype(F32)
    p2 = r1.astype(BF16)
    p3 = (r1 - p2.astype(F32)).astype(BF16)
    return p1, p2, p3


def _params(sem):
    return pltpu.CompilerParams(dimension_semantics=sem, vmem_limit_bytes=VMEM_LIMIT)


def _in_proj_kernel(x_ref, g_ref, w_ref, wg_ref, o_ref, og_ref, h_scr):
    @pl.when(pl.program_id(1) == 0)
    def _():
        h = _rms(x_ref[...], g_ref[...]).astype(BF16)
        h_scr[...] = h
        og_ref[...] = _dot(h, wg_ref[...])

    o_ref[...] = _dot(h_scr[...], w_ref[...])


def _in_proj(xt, g1, w_main, w_gate, tm, tn):
    T, D = xt.shape
    n_main = w_main.shape[1]
    n_gate = w_gate.shape[1]
    return pl.pallas_call(
        _in_proj_kernel,
        grid=(T // tm, n_main // tn),
        in_specs=[
            pl.BlockSpec((tm, D), lambda m, n: (m, 0)),
            pl.BlockSpec((1, D), lambda m, n: (0, 0)),
            pl.BlockSpec((D, tn), lambda m, n: (0, n)),
            pl.BlockSpec((D, n_gate), lambda m, n: (0, 0)),
        ],
        out_specs=[
            pl.BlockSpec((tm, tn), lambda m, n: (m, n)),
            pl.BlockSpec((tm, n_gate), lambda m, n: (m, 0)),
        ],
        out_shape=[
            jax.ShapeDtypeStruct((T, n_main), F32),
            jax.ShapeDtypeStruct((T, n_gate), F32),
        ],
        scratch_shapes=[pltpu.VMEM((tm, D), BF16)],
        compiler_params=_params(("parallel", "arbitrary")),
        name="in_proj",
    )(xt, g1, w_main, w_gate)


def _compress_kernel(x_ref, pos_ref, w1_ref, w2_ref, kg_ref, o_ref, xs_scr, *, S, n_pad):
    kv = pl.program_id(1)
    xs_scr[0:S, :] = x_ref[...]
    xs_scr[S:S + CMP_STRIDE, :] = jnp.zeros((CMP_STRIDE, HEAD_DIM), F32)
    acc = jnp.zeros((n_pad, HEAD_DIM), F32)
    for l in range(CMP_BLOCK):
        tb = xs_scr[pl.ds(l, n_pad, stride=CMP_STRIDE), :] + pos_ref[0, l:l + 1, :]
        acc = acc + _dot(tb.astype(BF16), w1_ref[0, l].astype(BF16))
    hid = acc * jax.nn.sigmoid(acc)
    out = _dot(hid.astype(BF16), w2_ref[0].astype(BF16))
    normed = _rms(out, kg_ref[0:1, :])
    o_ref[0, 0, 0] = jnp.where(kv == 0, normed, out).astype(BF16)


def _compress(proj, cmp_pos, cmp_w1, cmp_w2, k_norm_g, B, S):
    n_pad = S // CMP_STRIDE
    G = NSA_KV_GROUPS
    kern = functools.partial(_compress_kernel, S=S, n_pad=n_pad)
    return pl.pallas_call(
        kern,
        grid=(B, 2, G),
        in_specs=[
            pl.BlockSpec((S, HEAD_DIM), lambda b, kv, g: (b, KC_BLK + 2 * kv + g)),
            pl.BlockSpec((1, CMP_BLOCK, HEAD_DIM), lambda b, kv, g: (kv, 0, 0)),
            pl.BlockSpec((1, CMP_BLOCK, HEAD_DIM, HEAD_DIM), lambda b, kv, g: (kv, 0, 0, 0)),
            pl.BlockSpec((1, HEAD_DIM, HEAD_DIM), lambda b, kv, g: (kv, 0, 0)),
            pl.BlockSpec((3, HEAD_DIM), lambda b, kv, g: (0, 0)),
        ],
        out_specs=pl.BlockSpec((1, 1, 1, n_pad, HEAD_DIM), lambda b, kv, g: (b, kv, g, 0, 0)),
        out_shape=jax.ShapeDtypeStruct((B, 2, G, n_pad, HEAD_DIM), BF16),
        scratch_shapes=[pltpu.VMEM((S + CMP_STRIDE, HEAD_DIM), F32)],
        compiler_params=_params(("parallel", "parallel", "parallel")),
        name="compress",
    )(proj, cmp_pos, cmp_w1, cmp_w2, k_norm_g)


def _masked_softmax(s, msk):
    sm = jnp.where(msk, s, NEG_INF)
    m = jnp.max(sm, axis=-1, keepdims=True)
    e = jnp.where(msk, jnp.exp(sm - m), 0.0)
    den = jnp.sum(e, axis=-1, keepdims=True)
    return e / jnp.where(den > 0.0, den, 1.0)


def _nsa_kernel(q_ref, kc_ref, vc_ref, ks_ref, vs_ref, kw_ref, vw_ref, gate_ref, qg_ref, kg_ref,
                ovt_ref, ext_ref, o_ref, ksa, vsa, kwn, vwa, l_scr, acc_scr, *, S, tq, tk, wk):
    i = pl.program_id(2)
    H = NSA_GROUP_HEADS
    scale = HEAD_DIM ** -0.5
    c2 = scale * LOG2E
    n_cp = S // CMP_STRIDE
    n_sblk = S // SEL_BLOCK

    @pl.when(i == 0)
    def _():
        ksa[:, 0:HEAD_DIM] = _rms(ks_ref[...], kg_ref[1:2, :]).astype(BF16)
        ksa[:, HEAD_DIM:2 * HEAD_DIM] = ext_ref[...]
        kwn[...] = _rms(kw_ref[...], kg_ref[2:3, :]).astype(BF16)
        ones = jnp.ones((S, HEAD_DIM), BF16)
        vsa[:, 0:HEAD_DIM] = vs_ref[...].astype(BF16)
        vsa[:, HEAD_DIM:2 * HEAD_DIM] = ones
        vwa[:, 0:HEAD_DIM] = vw_ref[...].astype(BF16)
        vwa[:, HEAD_DIM:2 * HEAD_DIM] = ones

    pos = i * tq + lax.broadcasted_iota(jnp.int32, (tq, 1), 0)
    qg = qg_ref[...]
    q4 = jnp.concatenate([_rms(q_ref[:, h * HEAD_DIM:(h + 1) * HEAD_DIM], qg).astype(BF16)
                          for h in range(H)], axis=0)
    rows = [slice(h * tq, (h + 1) * tq) for h in range(H)]

    def softmax_pv(q_pairs, k, v, tail_bias):
        head = k.shape[0] - tail_bias.shape[1]
        outs, sums = [], []
        for q2 in q_pairs:
            s2 = _dot_nt(q2, k)
            es = []
            for hh in range(2):
                t = s2[hh * tq:(hh + 1) * tq] * c2
                tb = t[:, head:] + tail_bias
                m = jnp.max(tb, axis=-1, keepdims=True)
                if head:
                    ta = t[:, :head]
                    m = jnp.maximum(m, jnp.max(ta, axis=-1, keepdims=True))
                    es.append(jnp.concatenate([jnp.exp2(ta - m).astype(BF16),
                                               jnp.exp2(tb - m).astype(BF16)], axis=1))
                else:
                    es.append(jnp.exp2(tb - m).astype(BF16))
            o2 = _dot(jnp.concatenate(es, axis=0), v)
            outs += [o2[0:tq, 0:HEAD_DIM], o2[tq:2 * tq, 0:HEAD_DIM]]
            sums += [o2[0:tq, HEAD_DIM:2 * HEAD_DIM], o2[tq:2 * tq, HEAD_DIM:2 * HEAD_DIM]]
        return outs, sums

    ncol = lax.broadcasted_iota(jnp.int32, (1, n_cp), 1)
    cmask = (ncol * CMP_STRIDE + (CMP_BLOCK - 1)) <= pos
    s_c4 = _dot_nt(q4, kc_ref[0, 0, 0]) * scale
    ps = [_masked_softmax(s_c4[r], cmask) for r in rows]
    psum = ps[0] + ps[1] + ps[2] + ps[3]
    o_c4 = _dot(jnp.concatenate([p.astype(BF16) for p in ps], axis=0), vc_ref[0, 0, 0])

    ovt = ovt_ref[...]
    p1, p2, p3 = _split3(psum)
    imp_t = (_dot_nt(ovt, p1) + _dot_nt(ovt, p2) + _dot_nt(ovt, p3))[0:n_sblk, :]
    pos_t = i * tq + lax.broadcasted_iota(jnp.int32, (1, tq), 1)
    jrow = lax.broadcasted_iota(jnp.int32, (n_sblk, 1), 0)
    cur_t = pos_t // SEL_BLOCK
    valid_t = jrow * SEL_BLOCK <= pos_t
    forced_t = (jrow == 0) | (jrow == cur_t) | (jrow == cur_t - 1)
    score = jnp.where(valid_t, jnp.where(forced_t, FORCED_SCORE, imp_t), -jnp.inf)
    rank = jnp.zeros((n_sblk, tq), jnp.int32)
    for i2 in range(n_sblk):
        si = score[i2:i2 + 1, :]
        ahead = (si > score) | ((si == score) & (jrow > i2))
        rank = rank + jnp.where(ahead, 1, 0)
    drop_t = jnp.where((rank < SEL_TOPK) & valid_t, 0.0, NEG_INF)
    drop_t = jnp.concatenate([drop_t, jnp.full((LANES - n_sblk, tq), NEG_INF, F32)], axis=0)
    drop = drop_t.T.astype(BF16)
    drop2 = jnp.concatenate([drop, drop], axis=0)
    qa_pairs = [jnp.concatenate([q4[0:2 * tq], drop2], axis=1),
                jnp.concatenate([q4[2 * tq:4 * tq], drop2], axis=1)]

    kst = pl.multiple_of(jnp.clip(i * tq - WINDOW, 0, S - wk), tq)
    wpos = kst + lax.broadcasted_iota(jnp.int32, (1, wk), 1)
    wbias = jnp.where((wpos <= pos) & (wpos > pos - WINDOW), 0.0, NEG_INF)
    o_ws, lws = softmax_pv([q4[0:2 * tq], q4[2 * tq:4 * tq]],
                           kwn[pl.ds(kst, wk), :], vwa[pl.ds(kst, wk), :], wbias)

    def slc_prefix(n):
        kpos = (n - tk) + lax.broadcasted_iota(jnp.int32, (1, tk), 1)
        causal = jnp.where(kpos <= pos, 0.0, NEG_INF)
        outs, sums = softmax_pv(qa_pairs, ksa[0:n, :], vsa[0:n, :], causal)
        for r, o, l in zip(rows, outs, sums):
            acc_scr[r] = o
            l_scr[r] = l

    n_cls = S // tk
    cls = ((i + 1) * tq - 1) // tk
    for c in range(n_cls):
        pl.when(cls == c)(functools.partial(slc_prefix, (c + 1) * tk))

    gates = jax.nn.sigmoid(gate_ref[...])
    for h, r in enumerate(rows):
        c = 3 * h
        o = (o_c4[r] * gates[:, c:c + 1]
             + acc_scr[r] * (gates[:, c + 1:c + 2] / l_scr[r])
             + o_ws[h] * (gates[:, c + 2:c + 3] / lws[h]))
        o_ref[:, h * HEAD_DIM:(h + 1) * HEAD_DIM] = o.astype(BF16)


def _nsa_tables(S):
    n_cp, n_cmp, n_sblk = S // CMP_STRIDE, (S - CMP_BLOCK) // CMP_STRIDE + 1, S // SEL_BLOCK
    n = np.arange(n_cp)[None, :] * CMP_STRIDE
    j = np.arange(LANES)[:, None]
    ovt = ((n < (j + 1) * SEL_BLOCK) & (n + CMP_BLOCK > j * SEL_BLOCK)
           & (np.arange(n_cp)[None, :] < n_cmp) & (j < n_sblk))
    key_block = (np.arange(S) // SEL_BLOCK)[:, None] == np.arange(LANES)[None, :]
    return jnp.asarray(ovt, BF16), jnp.asarray(key_block, BF16)


def _nsa(proj, gate, cmp_kv, q_norm_g, k_norm_g, B, S, tq, tk):
    G = NSA_KV_GROUPS
    H = NSA_GROUP_HEADS
    n_cp = S // CMP_STRIDE
    wk = min(S, WINDOW + tq)
    nq = S // tq
    gw = H * HEAD_DIM
    ovt, key_block = _nsa_tables(S)
    kern = functools.partial(_nsa_kernel, S=S, tq=tq, tk=tk, wk=wk)
    kv_spec = lambda blk: pl.BlockSpec((S, HEAD_DIM), lambda b, g, i: (b, blk + g))
    return pl.pallas_call(
        kern,
        grid=(B, G, nq),
        in_specs=[
            pl.BlockSpec((tq, gw), lambda b, g, i: (b * nq + i, g)),
            pl.BlockSpec((1, 1, 1, n_cp, HEAD_DIM), lambda b, g, i: (b, 0, g, 0, 0)),
            pl.BlockSpec((1, 1, 1, n_cp, HEAD_DIM), lambda b, g, i: (b, 1, g, 0, 0)),
            kv_spec(KS_BLK), kv_spec(VS_BLK), kv_spec(KW_BLK), kv_spec(VW_BLK),
            pl.BlockSpec((tq, LANES), lambda b, g, i: (b * nq + i, g)),
            pl.BlockSpec((1, HEAD_DIM), lambda b, g, i: (0, 0)),
            pl.BlockSpec((3, HEAD_DIM), lambda b, g, i: (0, 0)),
            pl.BlockSpec((LANES, n_cp), lambda b, g, i: (0, 0)),
            pl.BlockSpec((S, LANES), lambda b, g, i: (0, 0)),
        ],
        out_specs=pl.BlockSpec((tq, gw), lambda b, g, i: (b * nq + i, g)),
        out_shape=jax.ShapeDtypeStruct((B * S, NSA_Q_WIDTH), BF16),
        scratch_shapes=[
            pltpu.VMEM((S, 2 * HEAD_DIM), BF16), pltpu.VMEM((S, 2 * HEAD_DIM), BF16),
            pltpu.VMEM((S, HEAD_DIM), BF16), pltpu.VMEM((S, 2 * HEAD_DIM), BF16),
            pltpu.VMEM((H * tq, HEAD_DIM), F32), pltpu.VMEM((H * tq, HEAD_DIM), F32)],
        compiler_params=_params(("parallel", "parallel", "arbitrary")),
        name="nsa",
    )(proj, cmp_kv, cmp_kv, proj, proj, proj, proj, gate, q_norm_g, k_norm_g, ovt, key_block)


RET_HB = 4


def _ret_kernel(lg_ref, rq_ref, rk_ref, rv_ref, rg_ref, cos_ref, sin_ref, gg_ref, gb_ref,
                o_ref, r_scr):
    hg = pl.program_id(1)
    C = RET_CHUNK
    scale = HEAD_DIM ** -0.5

    @pl.when(pl.program_id(2) == 0)
    def _():
        r_scr[...] = jnp.zeros(r_scr.shape, F32)

    cos = cos_ref[...]
    sin = sin_ref[...]
    n_col = lax.broadcasted_iota(jnp.int32, (C, 1), 0).astype(F32)
    n_row = lax.broadcasted_iota(jnp.int32, (1, C), 1).astype(F32)
    diff = n_col - n_row
    for h in range(RET_HB):
        sl = slice(h * HEAD_DIM, (h + 1) * HEAD_DIM)
        hh = hg * RET_HB + h
        lg = lg_ref[hh]
        q = rq_ref[:, sl]
        k = rk_ref[:, sl]
        qf = q * cos + pltpu.roll(q, HEAD_DIM // 2, 1) * sin
        kf = (k * cos + pltpu.roll(k, HEAD_DIM // 2, 1) * sin) * scale
        v = rv_ref[:, sl].astype(BF16)
        dec = jnp.where(diff >= 0.0, jnp.exp(lg * jnp.maximum(diff, 0.0)), 0.0)
        xi = jnp.exp(lg * (n_col + 1.0))
        zeta = jnp.exp(lg * (C - 1.0 - n_col))
        cd = jnp.exp(jnp.full((1, HEAD_DIM), lg * float(C), F32))
        qb = qf.astype(BF16)
        r_old = r_scr[h]
        a = _dot_nt(qb, kf.astype(BF16)) * dec
        o = _dot(a.astype(BF16), v) + _dot(qb, r_old.astype(BF16)) * xi
        r_scr[h] = r_old * cd + _dot_tn((kf * zeta).astype(BF16), v)
        mu = jnp.mean(o, axis=-1, keepdims=True)
        d = o - mu
        var = jnp.mean(d * d, axis=-1, keepdims=True)
        y = d * lax.rsqrt(var + GN_EPS) * gg_ref[pl.ds(hh, 1), :] + gb_ref[pl.ds(hh, 1), :]
        gt = rg_ref[:, sl]
        o_ref[:, sl] = (gt * jax.nn.sigmoid(gt) * y).astype(BF16)


def _retention(proj, log_g, cos2, sin2, gn_g, gn_b, B, S):
    C = RET_CHUNK
    n_ch = S // C
    bw = RET_HB * HEAD_DIM
    base = GATE_COL0 // bw
    nhb = RET_HEADS // RET_HB
    spec = lambda k: pl.BlockSpec((C, bw), lambda b, hg, c, lg: (b * n_ch + c, base + k * nhb + hg))
    grid_spec = pltpu.PrefetchScalarGridSpec(
        num_scalar_prefetch=1,
        grid=(B, nhb, n_ch),
        in_specs=[
            spec(0), spec(1), spec(2), spec(3),
            pl.BlockSpec((C, HEAD_DIM), lambda b, hg, c, lg: (c, 0)),
            pl.BlockSpec((C, HEAD_DIM), lambda b, hg, c, lg: (c, 0)),
            pl.BlockSpec((RET_HEADS, HEAD_DIM), lambda b, hg, c, lg: (0, 0)),
            pl.BlockSpec((RET_HEADS, HEAD_DIM), lambda b, hg, c, lg: (0, 0)),
        ],
        out_specs=pl.BlockSpec((C, bw), lambda b, hg, c, lg: (b * n_ch + c, hg)),
        scratch_shapes=[pltpu.VMEM((RET_HB, HEAD_DIM, HEAD_DIM), F32)],
    )
    return pl.pallas_call(
        _ret_kernel,
        grid_spec=grid_spec,
        out_shape=jax.ShapeDtypeStruct((B * S, RET_WIDTH), BF16),
        compiler_params=_params(("parallel", "parallel", "arbitrary")),
        name="retention",
    )(log_g, proj, proj, proj, proj, cos2, sin2, gn_g, gn_b)


def _lane_max(v):
    return jnp.max(v, axis=-1, keepdims=True)


def _lane_min(v):
    return jnp.min(v, axis=-1, keepdims=True)


def _out_kernel(on_ref, or_ref, w_ref, x_ref, g2_ref, wr_ref, br_ref, tri_ref,
                x1_ref, h2_ref, rt_ref, cnt_ref, cnt_scr):
    @pl.when(pl.program_id(0) == 0)
    def _():
        cnt_scr[...] = jnp.zeros(cnt_scr.shape, F32)

    half = on_ref.shape[1]
    acc = _dot(on_ref[...], w_ref[0:half, :]) + _dot(or_ref[...], w_ref[half:2 * half, :])
    x1 = x_ref[...] + acc
    x1_ref[...] = x1
    h2 = _rms(x1, g2_ref[...])
    h2_ref[...] = h2

    h_hi = h2.astype(BF16)
    h_lo = (h2 - h_hi.astype(F32)).astype(BF16)
    wr = wr_ref[...]
    w_hi = wr.astype(BF16)
    w_lo = (wr - w_hi.astype(F32)).astype(BF16)
    logits = _dot(h_hi, w_hi) + _dot(h_lo, w_hi) + _dot(h_hi, w_lo) + br_ref[...]

    tm = logits.shape[0]
    lane = lax.broadcasted_iota(jnp.int32, (tm, LANES), 1)
    big = jnp.int32(LANES)
    gm = lane < N_GROUPS
    gl = jnp.where(gm, logits, -jnp.inf)
    ge = jnp.where(gm, jnp.exp(gl - _lane_max(gl)), 0.0)
    pg = ge / jnp.sum(ge, axis=-1, keepdims=True)
    g_top = _lane_max(pg)
    g_idx = _lane_min(jnp.where(gm & (pg == g_top), lane, big))
    e0 = N_GROUPS + g_idx * EXPERTS_PER_GROUP
    em = (lane >= e0) & (lane < e0 + EXPERTS_PER_GROUP)
    el = jnp.where(em, logits, -jnp.inf)
    ee = jnp.where(em, jnp.exp(el - _lane_max(el)), 0.0)
    pe = ee / jnp.sum(ee, axis=-1, keepdims=True)
    t1 = _lane_max(jnp.where(em, pe, -1.0))
    i1 = _lane_min(jnp.where(em & (pe == t1), lane, big))
    em2 = em & (lane != i1)
    t2 = _lane_max(jnp.where(em2, pe, -1.0))
    i2 = _lane_min(jnp.where(em2 & (pe == t2), lane, big))
    tsum = t1 + t2
    w1 = g_top * t1 / tsum
    w2 = g_top * t2 / tsum
    e1 = i1 - N_GROUPS
    e2 = i2 - N_GROUPS
    oh1 = jnp.where(lane == e1, 1.0, 0.0)
    oh2 = jnp.where(lane == e2, 1.0, 0.0)
    both = oh1 + oh2
    before = _dot(tri_ref[...], both.astype(BF16)) + cnt_scr[...]
    r1 = jnp.sum(before * oh1, axis=-1, keepdims=True)
    r2 = jnp.sum(before * oh2, axis=-1, keepdims=True)
    cnt_scr[...] = cnt_scr[...] + jnp.sum(both, axis=0, keepdims=True)
    cnt_ref[...] = jnp.broadcast_to(cnt_scr[...], cnt_ref.shape)

    cols = (e1.astype(F32), e2.astype(F32), w1, w2, r1, r2)
    rt = jnp.zeros((tm, LANES), F32)
    for c, v in enumerate(cols):
        rt = jnp.where(lane == c, v, rt)
    rt_ref[...] = rt


def _out_proj(o_nsa, o_ret, w_out, xt, g2, w_router, b_router, tm):
    T, D = xt.shape
    half = o_nsa.shape[1]
    tri = jnp.asarray(np.tril(np.ones((tm, tm), np.float32), -1), BF16)
    return pl.pallas_call(
        _out_kernel,
        grid=(T // tm,),
        in_specs=[
            pl.BlockSpec((tm, half), lambda m: (m, 0)),
            pl.BlockSpec((tm, half), lambda m: (m, 0)),
            pl.BlockSpec((2 * half, D), lambda m: (0, 0)),
            pl.BlockSpec((tm, D), lambda m: (m, 0)),
            pl.BlockSpec((1, D), lambda m: (0, 0)),
            pl.BlockSpec((D, LANES), lambda m: (0, 0)),
            pl.BlockSpec((1, LANES), lambda m: (0, 0)),
            pl.BlockSpec((tm, tm), lambda m: (0, 0)),
        ],
        out_specs=[
            pl.BlockSpec((tm, D), lambda m: (m, 0)),
            pl.BlockSpec((tm, D), lambda m: (m, 0)),
            pl.BlockSpec((tm, LANES), lambda m: (m, 0)),
            pl.BlockSpec((8, LANES), lambda m: (0, 0)),
        ],
        out_shape=[
            jax.ShapeDtypeStruct((T, D), F32),
            jax.ShapeDtypeStruct((T, D), F32),
            jax.ShapeDtypeStruct((T, LANES), F32),
            jax.ShapeDtypeStruct((8, LANES), F32),
        ],
        scratch_shapes=[pltpu.VMEM((1, LANES), F32)],
        compiler_params=_params(("arbitrary",)),
        name="out_proj",
    )(o_nsa, o_ret, w_out, xt, g2, w_router, b_router, tri)


def _plan_kernel(pos_ref, tok_ref):
    def clear(r, carry):
        tok_ref[r] = 0
        return carry

    lax.fori_loop(0, tok_ref.shape[0], clear, 0, unroll=8)

    def place(a, carry):
        tok_ref[pos_ref[a]] = lax.shift_right_logical(a, 1)
        return carry

    lax.fori_loop(0, pos_ref.shape[0], place, 0, unroll=8)


def _plan(pos, n_rows):
    return pl.pallas_call(
        _plan_kernel,
        in_specs=[pl.BlockSpec(memory_space=pltpu.SMEM)],
        out_specs=pl.BlockSpec(memory_space=pltpu.SMEM),
        out_shape=jax.ShapeDtypeStruct((n_rows,), jnp.int32),
        name="plan",
    )(pos)


def _row_copy(src_hbm, row, dst, slot, sem):
    return pltpu.make_async_copy(src_hbm.at[pl.ds(row, 1), :], dst.at[pl.ds(slot, 1), :], sem)


def _expert_kernel(be_ref, nused_ref, tok_ref, h2_hbm, wg_ref, wu_ref, wd_ref,
                   o_ref, xbuf, wg_b, wu_b, wd_b, sems):
    i = pl.program_id(0)
    M = MOE_BLOCK
    n_used = nused_ref[0]
    slot = lax.rem(i, 2)
    nslot = 1 - slot

    def wait_block(s):
        pltpu.make_async_copy(h2_hbm.at[pl.ds(0, M), :], xbuf.at[s], sems.at[s]).wait()

    @pl.when(i == 0)
    def _():
        def issue(r, carry):
            _row_copy(h2_hbm, tok_ref[r], xbuf.at[0], r, sems.at[0]).start()
            return carry

        lax.fori_loop(0, M, issue, 0, unroll=8)

    @pl.when(i < n_used)
    def _():
        wait_block(slot)
        nbase = (i + 1) * M
        for r in range(M):
            _row_copy(h2_hbm, tok_ref[nbase + r], xbuf.at[nslot], r, sems.at[nslot]).start()

        prev = jnp.maximum(i - 1, 0)

        @pl.when((i == 0) | (be_ref[i] != be_ref[prev]))
        def _():
            wg_b[...] = wg_ref[0].astype(BF16)
            wu_b[...] = wu_ref[0].astype(BF16)
            wd_b[...] = wd_ref[0].astype(BF16)

        xb = xbuf[slot].astype(BF16)
        hg = _dot(xb, wg_b[...])
        hu = _dot(xb, wu_b[...])
        hb = (hg * jax.nn.sigmoid(hg) * hu).astype(BF16)
        o_ref[...] = _dot(hb, wd_b[...])

    @pl.when(i == n_used)
    def _():
        wait_block(slot)

    @pl.when(i >= n_used)
    def _():
        o_ref[...] = jnp.zeros(o_ref.shape, F32)


def _experts(h2, block_expert, n_used, row_tok, w_gate, w_up, w_down):
    T, D = h2.shape
    n_rows = row_tok.shape[0]
    n_blocks = n_rows // MOE_BLOCK
    assert (TOPK_IN_GROUP * T + N_EXPERTS * (MOE_BLOCK - 1)) // MOE_BLOCK < n_blocks
    grid_spec = pltpu.PrefetchScalarGridSpec(
        num_scalar_prefetch=3,
        grid=(n_blocks,),
        in_specs=[
            pl.BlockSpec(memory_space=pl.ANY),
            pl.BlockSpec((1, D, D_EXPERT), lambda i, be, nu, tok: (be[i], 0, 0)),
            pl.BlockSpec((1, D, D_EXPERT), lambda i, be, nu, tok: (be[i], 0, 0)),
            pl.BlockSpec((1, D_EXPERT, D), lambda i, be, nu, tok: (be[i], 0, 0)),
        ],
        out_specs=pl.BlockSpec((MOE_BLOCK, D), lambda i, be, nu, tok: (i, 0)),
        scratch_shapes=[
            pltpu.VMEM((2, MOE_BLOCK, D), F32),
            pltpu.VMEM((D, D_EXPERT), BF16), pltpu.VMEM((D, D_EXPERT), BF16),
            pltpu.VMEM((D_EXPERT, D), BF16),
            pltpu.SemaphoreType.DMA((2,)),
        ],
    )
    return pl.pallas_call(
        _expert_kernel,
        grid_spec=grid_spec,
        out_shape=jax.ShapeDtypeStruct((n_rows, D), F32),
        compiler_params=_params(("arbitrary",)),
        name="experts",
    )(block_expert, n_used, row_tok, h2, w_gate, w_up, w_down)


def _combine_kernel(pos_ref, y_hbm, x1_ref, rt_ref, o_ref, ybuf, sems):
    i = pl.program_id(0)
    n = pl.num_programs(0)
    tm = x1_ref.shape[0]
    K = TOPK_IN_GROUP
    slot = lax.rem(i, 2)

    def issue_tile(t, s):
        def issue(r, carry):
            a = (t * tm + r) * K
            for k in range(K):
                _row_copy(y_hbm, pos_ref[a + k], ybuf.at[s, k], r, sems.at[s]).start()
            return carry

        lax.fori_loop(0, tm, issue, 0, unroll=8)

    @pl.when(i == 0)
    def _():
        issue_tile(0, 0)

    @pl.when(i + 1 < n)
    def _():
        issue_tile(i + 1, 1 - slot)

    for k in range(K):
        pltpu.make_async_copy(y_hbm.at[pl.ds(0, tm), :], ybuf.at[slot, k], sems.at[slot]).wait()
    w = rt_ref[...]
    yb = ybuf[slot]
    o_ref[...] = x1_ref[...] + (yb[0] * w[:, K:K + 1] + yb[1] * w[:, K + 1:K + 2])


def _combine(pos, y_rows, x1, rt, tm):
    T, D = x1.shape
    grid_spec = pltpu.PrefetchScalarGridSpec(
        num_scalar_prefetch=1,
        grid=(T // tm,),
        in_specs=[
            pl.BlockSpec(memory_space=pl.ANY),
            pl.BlockSpec((tm, D), lambda i, pos: (i, 0)),
            pl.BlockSpec((tm, LANES), lambda i, pos: (i, 0)),
        ],
        out_specs=pl.BlockSpec((tm, D), lambda i, pos: (i, 0)),
        scratch_shapes=[pltpu.VMEM((2, TOPK_IN_GROUP, tm, D), F32), pltpu.SemaphoreType.DMA((2,))],
    )
    return pl.pallas_call(
        _combine_kernel,
        grid_spec=grid_spec,
        out_shape=jax.ShapeDtypeStruct((T, D), F32),
        compiler_params=_params(("arbitrary",)),
        name="combine",
    )(pos, y_rows, x1, rt)


def _block_layout(rt, counts, T):
    K = TOPK_IN_GROUP
    n_rows = (K * T + N_EXPERTS * (MOE_BLOCK - 1) + MOE_BLOCK - 1) // MOE_BLOCK * MOE_BLOCK
    n_blocks = n_rows // MOE_BLOCK
    cnt = counts[0, :N_EXPERTS].astype(jnp.int32)
    padded = (cnt + MOE_BLOCK - 1) // MOE_BLOCK * MOE_BLOCK
    pad_end = jnp.cumsum(padded)
    pad_start = pad_end - padded
    e = rt[:, 0:K].astype(jnp.int32)
    rank = rt[:, 2 * K:3 * K].astype(jnp.int32)
    pos = (pad_start[e] + rank).reshape(-1)
    first_row = jnp.arange(n_blocks, dtype=jnp.int32) * MOE_BLOCK
    block_expert = jnp.minimum(
        jnp.sum((pad_end[None, :] <= first_row[:, None]).astype(jnp.int32), axis=1), N_EXPERTS - 1)
    n_used = (pad_end[-1:] // MOE_BLOCK).astype(jnp.int32)
    return block_expert, n_used, pos, n_rows


def _layer(x, norm1_g, w_in, cmp_pos, cmp_w1, cmp_w2, q_norm_g, k_norm_g, ret_gn_g, ret_gn_b,
           w_out, norm2_g, w_rg, b_rg, w_re, b_re, w_eg, w_eu, w_ed, tiles):
    B, S, D = x.shape
    T = B * S
    xt = x.reshape(T, D)

    w_main = jnp.concatenate([w_in[:, :GATE_COL0], w_in[:, GATE_COL0 + NSA_GATE_WIDTH:]], axis=1).astype(BF16)
    gpg = NSA_GATE_WIDTH // NSA_KV_GROUPS
    w_gate = jnp.concatenate(
        [jnp.pad(w_in[:, GATE_COL0 + g * gpg:GATE_COL0 + (g + 1) * gpg], ((0, 0), (0, LANES - gpg)))
         for g in range(NSA_KV_GROUPS)], axis=1).astype(BF16)
    proj, gate = _in_proj(xt, norm1_g.reshape(1, D), w_main, w_gate, tiles["tm_in"], tiles["tn_in"])

    cmp_kv = _compress(proj, cmp_pos, cmp_w1, cmp_w2, k_norm_g, B, S)
    o_nsa = _nsa(proj, gate, cmp_kv, q_norm_g.reshape(1, HEAD_DIM), k_norm_g, B, S, tiles["tq"], tiles["tk"])

    half = HEAD_DIM // 2
    inv_freq = ROPE_BASE ** (-jnp.arange(half, dtype=F32) / half)
    ang = jnp.arange(S, dtype=F32)[:, None] * inv_freq[None, :]
    cos2 = jnp.concatenate([jnp.cos(ang), jnp.cos(ang)], axis=1)
    sin2 = jnp.concatenate([-jnp.sin(ang), jnp.sin(ang)], axis=1)
    log_g = jnp.log1p(-jnp.exp2(-5.0 - jnp.arange(RET_HEADS, dtype=F32)))
    o_ret = _retention(proj, log_g, cos2, sin2, ret_gn_g, ret_gn_b, B, S)

    n_r = N_GROUPS + N_EXPERTS
    w_router = jnp.pad(jnp.concatenate([w_rg, w_re], axis=1), ((0, 0), (0, LANES - n_r)))
    b_router = jnp.pad(jnp.concatenate([b_rg, b_re]), (0, LANES - n_r)).reshape(1, LANES)
    x1, h2, rt, counts = _out_proj(o_nsa, o_ret, w_out.astype(BF16), xt, norm2_g.reshape(1, D),
                                   w_router, b_router, tiles["tm_out"])

    block_expert, n_used, pos, n_rows = _block_layout(rt, counts, T)
    row_tok = _plan(pos, n_rows)
    y_rows = _experts(h2, block_expert, n_used, row_tok, w_eg, w_eu, w_ed)
    out = _combine(pos, y_rows, x1, rt, tiles["tm_cmb"])
    return out.reshape(B, S, D)


def _tiles(T, S):
    return {
        "tm_in": min(1024, T), "tn_in": 512,
        "tq": min(128, S), "tk": min(512, S),
        "tm_out": min(256, T),
        "tm_cmb": min(256, T),
    }


def kernel(x, norm1_g, w_in, cmp_pos, cmp_w1, cmp_w2, q_norm_g, k_norm_g, ret_gn_g, ret_gn_b, w_out, norm2_g, w_router_group, b_router_group, w_router_expert, b_router_expert, w_exp_gate, w_exp_up, w_exp_down):
    B, S, _ = x.shape
    tiles = _tiles(B * S, S)
    for l in range(norm1_g.shape[0]):
        x = _layer(x, norm1_g[l], w_in[l], cmp_pos[l], cmp_w1[l], cmp_w2[l], q_norm_g[l], k_norm_g[l],
                   ret_gn_g[l], ret_gn_b[l], w_out[l], norm2_g[l], w_router_group[l], b_router_group[l],
                   w_router_expert[l], b_router_expert[l], w_exp_gate[l], w_exp_up[l], w_exp_down[l], tiles)
    return x
```

```python
import functools

import numpy as np
import jax
import jax.numpy as jnp
from jax import lax
from jax.experimental import pallas as pl
from jax.experimental.pallas import tpu as pltpu

F32 = jnp.float32
BF16 = jnp.bfloat16

D_MODEL = 2048
NSA_HEADS = 8
NSA_KV_GROUPS = 2
NSA_GROUP_HEADS = NSA_HEADS // NSA_KV_GROUPS
HEAD_DIM = 128
RET_HEADS = 8
CMP_BLOCK = 32
CMP_STRIDE = 16
SEL_BLOCK = 64
SEL_TOPK = 8
WINDOW = 512
RET_CHUNK = 128
ROPE_BASE = 10000.0
N_GROUPS = 4
EXPERTS_PER_GROUP = 8
N_EXPERTS = N_GROUPS * EXPERTS_PER_GROUP
TOPK_IN_GROUP = 2
D_EXPERT = 512
MOE_BLOCK = 256
RMS_EPS = 1e-6
GN_EPS = 1e-5
NEG_INF = -1e30
FORCED_SCORE = 1e6
LOG2E = 1.4426950408889634

NSA_Q_WIDTH = NSA_HEADS * HEAD_DIM
NSA_KV_WIDTH = NSA_KV_GROUPS * HEAD_DIM
NSA_GATE_WIDTH = 3 * NSA_HEADS
RET_WIDTH = RET_HEADS * HEAD_DIM
GATE_COL0 = NSA_Q_WIDTH + 6 * NSA_KV_WIDTH
LANES = 128
VMEM_LIMIT = 56 * 1024 * 1024

Q_COL0 = 4 * RET_WIDTH
KV_COL0 = Q_COL0 + NSA_Q_WIDTH
KC_BLK, VC_BLK, KS_BLK, VS_BLK, KW_BLK, VW_BLK = [KV_COL0 // LANES + 2 * t for t in range(6)]


def _rms(xf, g):
    return xf * lax.rsqrt(jnp.mean(xf * xf, axis=-1, keepdims=True) + RMS_EPS) * g


def _dot(a, b):
    return jnp.dot(a, b, preferred_element_type=F32)


def _dot_nt(a, b):
    return lax.dot_general(a, b, (((1,), (1,)), ((), ())), preferred_element_type=F32)


def _dot_tn(a, b):
    return lax.dot_general(a, b, (((0,), (0,)), ((), ())), preferred_element_type=F32)


def _split3(p):
    p1 = p.astype(BF16)
    r1 = p - p1.astype(F32)
    p2 = r1.astype(BF16)
    p3 = (r1 - p2.astype(F32)).astype(BF16)
    return p1, p2, p3


def _params(sem):
    return pltpu.CompilerParams(dimension_semantics=sem, vmem_limit_bytes=VMEM_LIMIT)


def _in_proj_kernel(x_ref, g_ref, w_ref, wg_ref, o_ref, og_ref, h_scr):
    @pl.when(pl.program_id(1) == 0)
    def _():
        h = _rms(x_ref[...], g_ref[...]).astype(BF16)
        h_scr[...] = h
        og_ref[...] = _dot(h, wg_ref[...])

    o_ref[...] = _dot(h_scr[...], w_ref[...])


def _in_proj(xt, g1, w_main, w_gate, tm, tn):
    T, D = xt.shape
    n_main = w_main.shape[1]
    n_gate = w_gate.shape[1]
    return pl.pallas_call(
        _in_proj_kernel,
        grid=(T // tm, n_main // tn),
        in_specs=[
            pl.BlockSpec((tm, D), lambda m, n: (m, 0)),
            pl.BlockSpec((1, D), lambda m, n: (0, 0)),
            pl.BlockSpec((D, tn), lambda m, n: (0, n)),
            pl.BlockSpec((D, n_gate), lambda m, n: (0, 0)),
        ],
        out_specs=[
            pl.BlockSpec((tm, tn), lambda m, n: (m, n)),
            pl.BlockSpec((tm, n_gate), lambda m, n: (m, 0)),
        ],
        out_shape=[
            jax.ShapeDtypeStruct((T, n_main), F32),
            jax.ShapeDtypeStruct((T, n_gate), F32),
        ],
        scratch_shapes=[pltpu.VMEM((tm, D), BF16)],
        compiler_params=_params(("parallel", "arbitrary")),
        name="in_proj",
    )(xt, g1, w_main, w_gate)


def _compress_kernel(x_ref, pos_ref, w1_ref, w2_ref, kg_ref, o_ref, xs_scr, *, S, n_pad):
    kv = pl.program_id(1)
    xs_scr[0:S, :] = x_ref[...]
    xs_scr[S:S + CMP_STRIDE, :] = jnp.zeros((CMP_STRIDE, HEAD_DIM), F32)
    acc = jnp.zeros((n_pad, HEAD_DIM), F32)
    for l in range(CMP_BLOCK):
        tb = xs_scr[pl.ds(l, n_pad, stride=CMP_STRIDE), :] + pos_ref[0, l:l + 1, :]
        acc = acc + _dot(tb.astype(BF16), w1_ref[0, l].astype(BF16))
    hid = acc * jax.nn.sigmoid(acc)
    out = _dot(hid.astype(BF16), w2_ref[0].astype(BF16))
    normed = _rms(out, kg_ref[0:1, :])
    o_ref[0, 0, 0] = jnp.where(kv == 0, normed, out).astype(BF16)


def _compress(proj, cmp_pos, cmp_w1, cmp_w2, k_norm_g, B, S):
    n_pad = S // CMP_STRIDE
    G = NSA_KV_GROUPS
    kern = functools.partial(_compress_kernel, S=S, n_pad=n_pad)
    return pl.pallas_call(
        kern,
        grid=(B, 2, G),
        in_specs=[
            pl.BlockSpec((S, HEAD_DIM), lambda b, kv, g: (b, KC_BLK + 2 * kv + g)),
            pl.BlockSpec((1, CMP_BLOCK, HEAD_DIM), lambda b, kv, g: (kv, 0, 0)),
            pl.BlockSpec((1, CMP_BLOCK, HEAD_DIM, HEAD_DIM), lambda b, kv, g: (kv, 0, 0, 0)),
            pl.BlockSpec((1, HEAD_DIM, HEAD_DIM), lambda b, kv, g: (kv, 0, 0)),
            pl.BlockSpec((3, HEAD_DIM), lambda b, kv, g: (0, 0)),
        ],
        out_specs=pl.BlockSpec((1, 1, 1, n_pad, HEAD_DIM), lambda b, kv, g: (b, kv, g, 0, 0)),
        out_shape=jax.ShapeDtypeStruct((B, 2, G, n_pad, HEAD_DIM), BF16),
        scratch_shapes=[pltpu.VMEM((S + CMP_STRIDE, HEAD_DIM), F32)],
        compiler_params=_params(("parallel", "parallel", "parallel")),
        name="compress",
    )(proj, cmp_pos, cmp_w1, cmp_w2, k_norm_g)


def _masked_softmax(s, msk):
    sm = jnp.where(msk, s, NEG_INF)
    m = jnp.max(sm, axis=-1, keepdims=True)
    e = jnp.where(msk, jnp.exp(sm - m), 0.0)
    den = jnp.sum(e, axis=-1, keepdims=True)
    return e / jnp.where(den > 0.0, den, 1.0)


def _nsa_kernel(q_ref, kc_ref, vc_ref, ks_ref, vs_ref, kw_ref, vw_ref, gate_ref, qg_ref, kg_ref,
                ovt_ref, ext_ref, o_ref, ksa, vsa, kwn, vwa, l_scr, acc_scr, *, S, tq, tk, wk):
    i = pl.program_id(2)
    H = NSA_GROUP_HEADS
    scale = HEAD_DIM ** -0.5
    c2 = scale * LOG2E
    n_cp = S // CMP_STRIDE
    n_sblk = S // SEL_BLOCK

    @pl.when(i == 0)
    def _():
        ksa[:, 0:HEAD_DIM] = _rms(ks_ref[...], kg_ref[1:2, :]).astype(BF16)
        ksa[:, HEAD_DIM:2 * HEAD_DIM] = ext_ref[...]
        kwn[...] = _rms(kw_ref[...], kg_ref[2:3, :]).astype(BF16)
        ones = jnp.ones((S, HEAD_DIM), BF16)
        vsa[:, 0:HEAD_DIM] = vs_ref[...].astype(BF16)
        vsa[:, HEAD_DIM:2 * HEAD_DIM] = ones
        vwa[:, 0:HEAD_DIM] = vw_ref[...].astype(BF16)
        vwa[:, HEAD_DIM:2 * HEAD_DIM] = ones

    pos = i * tq + lax.broadcasted_iota(jnp.int32, (tq, 1), 0)
    qg = qg_ref[...]
    q4 = jnp.concatenate([_rms(q_ref[:, h * HEAD_DIM:(h + 1) * HEAD_DIM], qg).astype(BF16)
                          for h in range(H)], axis=0)
    rows = [slice(h * tq, (h + 1) * tq) for h in range(H)]

    def softmax_pv(q_pairs, k, v, tail_bias):
        head = k.shape[0] - tail_bias.shape[1]
        outs, sums = [], []
        for q2 in q_pairs:
            s2 = _dot_nt(q2, k)
            es = []
            for hh in range(2):
                t = s2[hh * tq:(hh + 1) * tq] * c2
                tb = t[:, head:] + tail_bias
                m = jnp.max(tb, axis=-1, keepdims=True)
                if head:
                    ta = t[:, :head]
                    m = jnp.maximum(m, jnp.max(ta, axis=-1, keepdims=True))
                    es.append(jnp.concatenate([jnp.exp2(ta - m).astype(BF16),
                                               jnp.exp2(tb - m).astype(BF16)], axis=1))
                else:
                    es.append(jnp.exp2(tb - m).astype(BF16))
            o2 = _dot(jnp.concatenate(es, axis=0), v)
            outs += [o2[0:tq, 0:HEAD_DIM], o2[tq:2 * tq, 0:HEAD_DIM]]
            sums += [o2[0:tq, HEAD_DIM:2 * HEAD_DIM], o2[tq:2 * tq, HEAD_DIM:2 * HEAD_DIM]]
        return outs, sums

    ncol = lax.broadcasted_iota(jnp.int32, (1, n_cp), 1)
    cmask = (ncol * CMP_STRIDE + (CMP_BLOCK - 1)) <= pos
    s_c4 = _dot_nt(q4, kc_ref[0, 0, 0]) * scale
    ps = [_masked_softmax(s_c4[r], cmask) for r in rows]
    psum = ps[0] + ps[1] + ps[2] + ps[3]
    o_c4 = _dot(jnp.concatenate([p.astype(BF16) for p in ps], axis=0), vc_ref[0, 0, 0])

    ovt = ovt_ref[...]
    p1, p2, p3 = _split3(psum)
    imp_t = (_dot_nt(ovt, p1) + _dot_nt(ovt, p2) + _dot_nt(ovt, p3))[0:n_sblk, :]
    pos_t = i * tq + lax.broadcasted_iota(jnp.int32, (1, tq), 1)
    jrow = lax.broadcasted_iota(jnp.int32, (n_sblk, 1), 0)
    cur_t = pos_t // SEL_BLOCK
    valid_t = jrow * SEL_BLOCK <= pos_t
    forced_t = (jrow == 0) | (jrow == cur_t) | (jrow == cur_t - 1)
    score = jnp.where(valid_t, jnp.where(forced_t, FORCED_SCORE, imp_t), -jnp.inf)
    rank = jnp.zeros((n_sblk, tq), jnp.int32)
    for i2 in range(n_sblk):
        si = score[i2:i2 + 1, :]
        ahead = (si > score) | ((si == score) & (jrow > i2))
        rank = rank + jnp.where(ahead, 1, 0)
    drop_t = jnp.where((rank < SEL_TOPK) & valid_t, 0.0, NEG_INF)
    drop_t = jnp.concatenate([drop_t, jnp.full((LANES - n_sblk, tq), NEG_INF, F32)], axis=0)
    drop = drop_t.T.astype(BF16)
    drop2 = jnp.concatenate([drop, drop], axis=0)
    qa_pairs = [jnp.concatenate([q4[0:2 * tq], drop2], axis=1),
                jnp.concatenate([q4[2 * tq:4 * tq], drop2], axis=1)]

    kst = pl.multiple_of(jnp.clip(i * tq - WINDOW, 0, S - wk), tq)
    wpos = kst + lax.broadcasted_iota(jnp.int32, (1, wk), 1)
    wbias = jnp.where((wpos <= pos) & (wpos > pos - WINDOW), 0.0, NEG_INF)
    o_ws, lws = softmax_pv([q4[0:2 * tq], q4[2 * tq:4 * tq]],
                           kwn[pl.ds(kst, wk), :], vwa[pl.ds(kst, wk), :], wbias)

    def slc_prefix(n):
        kpos = (n - tk) + lax.broadcasted_iota(jnp.int32, (1, tk), 1)
        causal = jnp.where(kpos <= pos, 0.0, NEG_INF)
        outs, sums = softmax_pv(qa_pairs, ksa[0:n, :], vsa[0:n, :], causal)
        for r, o, l in zip(rows, outs, sums):
            acc_scr[r] = o
            l_scr[r] = l

    n_cls = S // tk
    cls = ((i + 1) * tq - 1) // tk
    for c in range(n_cls):
        pl.when(cls == c)(functools.partial(slc_prefix, (c + 1) * tk))

    gates = jax.nn.sigmoid(gate_ref[...])
    for h, r in enumerate(rows):
        c = 3 * h
        o = (o_c4[r] * gates[:, c:c + 1]
             + acc_scr[r] * (gates[:, c + 1:c + 2] / l_scr[r])
             + o_ws[h] * (gates[:, c + 2:c + 3] / lws[h]))
        o_ref[:, h * HEAD_DIM:(h + 1) * HEAD_DIM] = o.astype(BF16)


def _nsa_tables(S):
    n_cp, n_cmp, n_sblk = S // CMP_STRIDE, (S - CMP_BLOCK) // CMP_STRIDE + 1, S // SEL_BLOCK
    n = np.arange(n_cp)[None, :] * CMP_STRIDE
    j = np.arange(LANES)[:, None]
    ovt = ((n < (j + 1) * SEL_BLOCK) & (n + CMP_BLOCK > j * SEL_BLOCK)
           & (np.arange(n_cp)[None, :] < n_cmp) & (j < n_sblk))
    key_block = (np.arange(S) // SEL_BLOCK)[:, None] == np.arange(LANES)[None, :]
    return jnp.asarray(ovt, BF16), jnp.asarray(key_block, BF16)


def _nsa(proj, gate, cmp_kv, q_norm_g, k_norm_g, B, S, tq, tk):
    G = NSA_KV_GROUPS
    H = NSA_GROUP_HEADS
    n_cp = S // CMP_STRIDE
    wk = min(S, WINDOW + tq)
    nq = S // tq
    gw = H * HEAD_DIM
    ovt, key_block = _nsa_tables(S)
    kern = functools.partial(_nsa_kernel, S=S, tq=tq, tk=tk, wk=wk)
    kv_spec = lambda blk: pl.BlockSpec((S, HEAD_DIM), lambda b, g, i: (b, blk + g))
    return pl.pallas_call(
        kern,
        grid=(B, G, nq),
        in_specs=[
            pl.BlockSpec((tq, gw), lambda b, g, i: (b * nq + i, Q_COL0 // gw + g)),
            pl.BlockSpec((1, 1, 1, n_cp, HEAD_DIM), lambda b, g, i: (b, 0, g, 0, 0)),
            pl.BlockSpec((1, 1, 1, n_cp, HEAD_DIM), lambda b, g, i: (b, 1, g, 0, 0)),
            kv_spec(KS_BLK), kv_spec(VS_BLK), kv_spec(KW_BLK), kv_spec(VW_BLK),
            pl.BlockSpec((tq, LANES), lambda b, g, i: (b * nq + i, g)),
            pl.BlockSpec((1, HEAD_DIM), lambda b, g, i: (0, 0)),
            pl.BlockSpec((3, HEAD_DIM), lambda b, g, i: (0, 0)),
            pl.BlockSpec((LANES, n_cp), lambda b, g, i: (0, 0)),
            pl.BlockSpec((S, LANES), lambda b, g, i: (0, 0)),
        ],
        out_specs=pl.BlockSpec((tq, gw), lambda b, g, i: (b * nq + i, g)),
        out_shape=jax.ShapeDtypeStruct((B * S, NSA_Q_WIDTH), BF16),
        scratch_shapes=[
            pltpu.VMEM((S, 2 * HEAD_DIM), BF16), pltpu.VMEM((S, 2 * HEAD_DIM), BF16),
            pltpu.VMEM((S, HEAD_DIM), BF16), pltpu.VMEM((S, 2 * HEAD_DIM), BF16),
            pltpu.VMEM((H * tq, HEAD_DIM), F32), pltpu.VMEM((H * tq, HEAD_DIM), F32)],
        compiler_params=_params(("parallel", "parallel", "arbitrary")),
        name="nsa",
    )(proj, cmp_kv, cmp_kv, proj, proj, proj, proj, gate, q_norm_g, k_norm_g, ovt, key_block)


RET_HB = 8


def _ret_kernel(lg_ref, rq_ref, rk_ref, rv_ref, rg_ref, cos_ref, sin_ref, gg_ref, gb_ref,
                o_ref, r_scr):
    hg = pl.program_id(1)
    C = RET_CHUNK
    scale = HEAD_DIM ** -0.5

    @pl.when(pl.program_id(2) == 0)
    def _():
        r_scr[...] = jnp.zeros(r_scr.shape, F32)

    cos = cos_ref[...]
    sin = sin_ref[...]
    n_col = lax.broadcasted_iota(jnp.int32, (C, 1), 0).astype(F32)
    n_row = lax.broadcasted_iota(jnp.int32, (1, C), 1).astype(F32)
    diff = n_col - n_row
    for h in range(RET_HB):
        sl = slice(h * HEAD_DIM, (h + 1) * HEAD_DIM)
        hh = hg * RET_HB + h
        lg = lg_ref[hh]
        q = rq_ref[:, sl]
        k = rk_ref[:, sl]
        qf = q * cos + pltpu.roll(q, HEAD_DIM // 2, 1) * sin
        kf = (k * cos + pltpu.roll(k, HEAD_DIM // 2, 1) * sin) * scale
        v = rv_ref[:, sl].astype(BF16)
        dec = jnp.where(diff >= 0.0, jnp.exp(lg * jnp.maximum(diff, 0.0)), 0.0)
        xi = jnp.exp(lg * (n_col + 1.0))
        zeta = jnp.exp(lg * (C - 1.0 - n_col))
        cd = jnp.exp(jnp.full((1, HEAD_DIM), lg * float(C), F32))
        qb = qf.astype(BF16)
        r_old = r_scr[h]
        a = _dot_nt(qb, kf.astype(BF16)) * dec
        o = _dot(a.astype(BF16), v) + _dot(qb, r_old.astype(BF16)) * xi
        r_scr[h] = r_old * cd + _dot_tn((kf * zeta).astype(BF16), v)
        mu = jnp.mean(o, axis=-1, keepdims=True)
        d = o - mu
        var = jnp.mean(d * d, axis=-1, keepdims=True)
        y = d * lax.rsqrt(var + GN_EPS) * gg_ref[pl.ds(hh, 1), :] + gb_ref[pl.ds(hh, 1), :]
        gt = rg_ref[:, sl]
        o_ref[:, sl] = (gt * jax.nn.sigmoid(gt) * y).astype(BF16)


def _retention(proj, log_g, cos2, sin2, gn_g, gn_b, B, S):
    C = RET_CHUNK
    n_ch = S // C
    bw = RET_HB * HEAD_DIM
    nhb = RET_HEADS // RET_HB
    spec = lambda k: pl.BlockSpec((C, bw), lambda b, hg, c, lg: (b * n_ch + c, k * nhb + hg))
    grid_spec = pltpu.PrefetchScalarGridSpec(
        num_scalar_prefetch=1,
        grid=(B, nhb, n_ch),
        in_specs=[
            spec(0), spec(1), spec(2), spec(3),
            pl.BlockSpec((C, HEAD_DIM), lambda b, hg, c, lg: (c, 0)),
            pl.BlockSpec((C, HEAD_DIM), lambda b, hg, c, lg: (c, 0)),
            pl.BlockSpec((RET_HEADS, HEAD_DIM), lambda b, hg, c, lg: (0, 0)),
            pl.BlockSpec((RET_HEADS, HEAD_DIM), lambda b, hg, c, lg: (0, 0)),
        ],
        out_specs=pl.BlockSpec((C, bw), lambda b, hg, c, lg: (b * n_ch + c, hg)),
        scratch_shapes=[pltpu.VMEM((RET_HB, HEAD_DIM, HEAD_DIM), F32)],
    )
    return pl.pallas_call(
        _ret_kernel,
        grid_spec=grid_spec,
        out_shape=jax.ShapeDtypeStruct((B * S, RET_WIDTH), BF16),
        compiler_params=_params(("parallel", "parallel", "arbitrary")),
        name="retention",
    )(log_g, proj, proj, proj, proj, cos2, sin2, gn_g, gn_b)


def _lane_max(v):
    return jnp.max(v, axis=-1, keepdims=True)


def _lane_min(v):
    return jnp.min(v, axis=-1, keepdims=True)


def _out_kernel(on_ref, or_ref, w_ref, x_ref, g2_ref, wr_ref, br_ref, tri_ref,
                x1_ref, h2_ref, rt_ref, cnt_ref, cnt_scr):
    @pl.when(pl.program_id(0) == 0)
    def _():
        cnt_scr[...] = jnp.zeros(cnt_scr.shape, F32)

    half = on_ref.shape[1]
    acc = _dot(on_ref[...], w_ref[0:half, :]) + _dot(or_ref[...], w_ref[half:2 * half, :])
    x1 = x_ref[...] + acc
    x1_ref[...] = x1
    h2 = _rms(x1, g2_ref[...])
    h2_ref[...] = h2

    h_hi = h2.astype(BF16)
    h_lo = (h2 - h_hi.astype(F32)).astype(BF16)
    wr = wr_ref[...]
    w_hi = wr.astype(BF16)
    w_lo = (wr - w_hi.astype(F32)).astype(BF16)
    logits = _dot(h_hi, w_hi) + _dot(h_lo, w_hi) + _dot(h_hi, w_lo) + br_ref[...]

    tm = logits.shape[0]
    lane = lax.broadcasted_iota(jnp.int32, (tm, LANES), 1)
    big = jnp.int32(LANES)
    gm = lane < N_GROUPS
    gl = jnp.where(gm, logits, -jnp.inf)
    ge = jnp.where(gm, jnp.exp(gl - _lane_max(gl)), 0.0)
    pg = ge / jnp.sum(ge, axis=-1, keepdims=True)
    g_top = _lane_max(pg)
    g_idx = _lane_min(jnp.where(gm & (pg == g_top), lane, big))
    e0 = N_GROUPS + g_idx * EXPERTS_PER_GROUP
    em = (lane >= e0) & (lane < e0 + EXPERTS_PER_GROUP)
    el = jnp.where(em, logits, -jnp.inf)
    ee = jnp.where(em, jnp.exp(el - _lane_max(el)), 0.0)
    pe = ee / jnp.sum(ee, axis=-1, keepdims=True)
    t1 = _lane_max(jnp.where(em, pe, -1.0))
    i1 = _lane_min(jnp.where(em & (pe == t1), lane, big))
    em2 = em & (lane != i1)
    t2 = _lane_max(jnp.where(em2, pe, -1.0))
    i2 = _lane_min(jnp.where(em2 & (pe == t2), lane, big))
    tsum = t1 + t2
    w1 = g_top * t1 / tsum
    w2 = g_top * t2 / tsum
    e1 = i1 - N_GROUPS
    e2 = i2 - N_GROUPS
    oh1 = jnp.where(lane == e1, 1.0, 0.0)
    oh2 = jnp.where(lane == e2, 1.0, 0.0)
    both = oh1 + oh2
    before = _dot(tri_ref[...], both.astype(BF16)) + cnt_scr[...]
    r1 = jnp.sum(before * oh1, axis=-1, keepdims=True)
    r2 = jnp.sum(before * oh2, axis=-1, keepdims=True)
    cnt_scr[...] = cnt_scr[...] + jnp.sum(both, axis=0, keepdims=True)
    cnt_ref[...] = jnp.broadcast_to(cnt_scr[...], cnt_ref.shape)

    cols = (e1.astype(F32), e2.astype(F32), w1, w2, r1, r2)
    rt = jnp.zeros((tm, LANES), F32)
    for c, v in enumerate(cols):
        rt = jnp.where(lane == c, v, rt)
    rt_ref[...] = rt


def _out_proj(o_nsa, o_ret, w_out, xt, g2, w_router, b_router, tm):
    T, D = xt.shape
    half = o_nsa.shape[1]
    tri = jnp.asarray(np.tril(np.ones((tm, tm), np.float32), -1), BF16)
    return pl.pallas_call(
        _out_kernel,
        grid=(T // tm,),
        in_specs=[
            pl.BlockSpec((tm, half), lambda m: (m, 0)),
            pl.BlockSpec((tm, half), lambda m: (m, 0)),
            pl.BlockSpec((2 * half, D), lambda m: (0, 0), pipeline_mode=pl.Buffered(1)),
            pl.BlockSpec((tm, D), lambda m: (m, 0)),
            pl.BlockSpec((1, D), lambda m: (0, 0)),
            pl.BlockSpec((D, LANES), lambda m: (0, 0)),
            pl.BlockSpec((1, LANES), lambda m: (0, 0)),
            pl.BlockSpec((tm, tm), lambda m: (0, 0)),
        ],
        out_specs=[
            pl.BlockSpec((tm, D), lambda m: (m, 0)),
            pl.BlockSpec((tm, D), lambda m: (m, 0)),
            pl.BlockSpec((tm, LANES), lambda m: (m, 0)),
            pl.BlockSpec((8, LANES), lambda m: (0, 0)),
        ],
        out_shape=[
            jax.ShapeDtypeStruct((T, D), F32),
            jax.ShapeDtypeStruct((T, D), F32),
            jax.ShapeDtypeStruct((T, LANES), F32),
            jax.ShapeDtypeStruct((8, LANES), F32),
        ],
        scratch_shapes=[pltpu.VMEM((1, LANES), F32)],
        compiler_params=_params(("arbitrary",)),
        name="out_proj",
    )(o_nsa, o_ret, w_out, xt, g2, w_router, b_router, tri)


def _plan_kernel(pos_ref, tok_ref):
    def clear(r, carry):
        tok_ref[r] = 0
        return carry

    lax.fori_loop(0, tok_ref.shape[0], clear, 0, unroll=8)

    def place(a, carry):
        tok_ref[pos_ref[a]] = lax.shift_right_logical(a, 1)
        return carry

    lax.fori_loop(0, pos_ref.shape[0], place, 0, unroll=8)


def _plan(pos, n_rows):
    return pl.pallas_call(
        _plan_kernel,
        in_specs=[pl.BlockSpec(memory_space=pltpu.SMEM)],
        out_specs=pl.BlockSpec(memory_space=pltpu.SMEM),
        out_shape=jax.ShapeDtypeStruct((n_rows,), jnp.int32),
        name="plan",
    )(pos)


def _row_copy(src_hbm, row, dst, slot, sem):
    return pltpu.make_async_copy(src_hbm.at[pl.ds(row, 1), :], dst.at[pl.ds(slot, 1), :], sem)


def _expert_kernel(ts_ref, tok_ref, h2_hbm, wg_ref, wu_ref, wd_ref, y_hbm,
                   xbuf, ybuf, wg_b, wu_b, wd_b, gsem, osem, *, n_blocks):
    e = pl.program_id(0)
    M = MOE_BLOCK
    t0 = ts_ref[e]
    t1 = ts_ref[e + 1]
    n_used = ts_ref[N_EXPERTS]

    def gather_wait(s):
        pltpu.make_async_copy(h2_hbm.at[pl.ds(0, M), :], xbuf.at[s], gsem.at[s]).wait()

    def out_copy(t, s):
        return pltpu.make_async_copy(ybuf.at[s], y_hbm.at[pl.ds(pl.multiple_of(t * M, M), M), :], osem.at[s])

    @pl.when(e == 0)
    def _():
        def issue(r, carry):
            _row_copy(h2_hbm, tok_ref[r], xbuf.at[0], r, gsem.at[0]).start()
            return carry

        lax.fori_loop(0, M, issue, 0, unroll=8)

    @pl.when(t1 > t0)
    def _():
        wg_b[...] = wg_ref[0].astype(BF16)
        wu_b[...] = wu_ref[0].astype(BF16)
        wd_b[...] = wd_ref[0].astype(BF16)

    def tile(t, carry):
        s = lax.rem(t, 2)
        gather_wait(s)

        @pl.when(t >= 2)
        def _():
            out_copy(t - 2, s).wait()

        nbase = (t + 1) * M
        for r in range(M):
            _row_copy(h2_hbm, tok_ref[nbase + r], xbuf.at[1 - s], r, gsem.at[1 - s]).start()
        xb = xbuf[s].astype(BF16)
        hg = _dot(xb, wg_b[...])
        hu = _dot(xb, wu_b[...])
        hb = (hg * jax.nn.sigmoid(hg) * hu).astype(BF16)
        ybuf[s] = _dot(hb, wd_b[...])
        out_copy(t, s).start()
        return carry

    lax.fori_loop(t0, t1, tile, 0)

    @pl.when(e == N_EXPERTS - 1)
    def _():
        gather_wait(lax.rem(n_used, 2))
        out_copy(n_used - 1, lax.rem(n_used - 1, 2)).wait()

        @pl.when(n_used >= 2)
        def _():
            out_copy(n_used - 2, lax.rem(n_used, 2)).wait()

        ybuf[0] = jnp.zeros((M, ybuf.shape[2]), F32)

        def clear(t, carry):
            out_copy(t, 0).start()
            return carry

        lax.fori_loop(n_used, n_blocks, clear, 0)

        def clear_wait(t, carry):
            out_copy(t, 0).wait()
            return carry

        lax.fori_loop(n_used, n_blocks, clear_wait, 0)


def _experts(h2, tile_start, row_tok, w_gate, w_up, w_down):
    T, D = h2.shape
    n_rows = row_tok.shape[0]
    n_blocks = n_rows // MOE_BLOCK
    assert (TOPK_IN_GROUP * T + N_EXPERTS * (MOE_BLOCK - 1)) // MOE_BLOCK < n_blocks
    grid_spec = pltpu.PrefetchScalarGridSpec(
        num_scalar_prefetch=2,
        grid=(N_EXPERTS,),
        in_specs=[
            pl.BlockSpec(memory_space=pl.ANY),
            pl.BlockSpec((1, D, D_EXPERT), lambda e, ts, tok: (e, 0, 0)),
            pl.BlockSpec((1, D, D_EXPERT), lambda e, ts, tok: (e, 0, 0)),
            pl.BlockSpec((1, D_EXPERT, D), lambda e, ts, tok: (e, 0, 0)),
        ],
        out_specs=pl.BlockSpec(memory_space=pl.ANY),
        scratch_shapes=[
            pltpu.VMEM((2, MOE_BLOCK, D), F32), pltpu.VMEM((2, MOE_BLOCK, D), F32),
            pltpu.VMEM((D, D_EXPERT), BF16), pltpu.VMEM((D, D_EXPERT), BF16),
            pltpu.VMEM((D_EXPERT, D), BF16),
            pltpu.SemaphoreType.DMA((2,)), pltpu.SemaphoreType.DMA((2,)),
        ],
    )
    return pl.pallas_call(
        functools.partial(_expert_kernel, n_blocks=n_blocks),
        grid_spec=grid_spec,
        out_shape=jax.ShapeDtypeStruct((n_rows, D), F32),
        compiler_params=_params(("arbitrary",)),
        name="experts",
    )(tile_start, row_tok, h2, w_gate, w_up, w_down)


def _combine_kernel(pos_ref, y_hbm, x1_ref, rt_ref, o_ref, ybuf, sems):
    i = pl.program_id(0)
    n = pl.num_programs(0)
    tm = x1_ref.shape[0]
    K = TOPK_IN_GROUP
    slot = lax.rem(i, 2)

    def issue_tile(t, s):
        def issue(r, carry):
            a = (t * tm + r) * K
            for k in range(K):
                _row_copy(y_hbm, pos_ref[a + k], ybuf.at[s, k], r, sems.at[s]).start()
            return carry

        lax.fori_loop(0, tm, issue, 0, unroll=8)

    @pl.when(i == 0)
    def _():
        issue_tile(0, 0)

    @pl.when(i + 1 < n)
    def _():
        issue_tile(i + 1, 1 - slot)

    for k in range(K):
        pltpu.make_async_copy(y_hbm.at[pl.ds(0, tm), :], ybuf.at[slot, k], sems.at[slot]).wait()
    w = rt_ref[...]
    yb = ybuf[slot]
    o_ref[...] = x1_ref[...] + (yb[0] * w[:, K:K + 1] + yb[1] * w[:, K + 1:K + 2])


def _combine(pos, y_rows, x1, rt, tm):
    T, D = x1.shape
    grid_spec = pltpu.PrefetchScalarGridSpec(
        num_scalar_prefetch=1,
        grid=(T // tm,),
        in_specs=[
            pl.BlockSpec(memory_space=pl.ANY),
            pl.BlockSpec((tm, D), lambda i, pos: (i, 0)),
            pl.BlockSpec((tm, LANES), lambda i, pos: (i, 0)),
        ],
        out_specs=pl.BlockSpec((tm, D), lambda i, pos: (i, 0)),
        scratch_shapes=[pltpu.VMEM((2, TOPK_IN_GROUP, tm, D), F32), pltpu.SemaphoreType.DMA((2,))],
    )
    return pl.pallas_call(
        _combine_kernel,
        grid_spec=grid_spec,
        out_shape=jax.ShapeDtypeStruct((T, D), F32),
        compiler_params=_params(("arbitrary",)),
        name="combine",
    )(pos, y_rows, x1, rt)


def _block_layout(rt, counts, T):
    K = TOPK_IN_GROUP
    n_rows = (K * T + N_EXPERTS * (MOE_BLOCK - 1) + MOE_BLOCK - 1) // MOE_BLOCK * MOE_BLOCK
    n_blocks = n_rows // MOE_BLOCK
    cnt = counts[0, :N_EXPERTS].astype(jnp.int32)
    padded = (cnt + MOE_BLOCK - 1) // MOE_BLOCK * MOE_BLOCK
    pad_end = jnp.cumsum(padded)
    pad_start = pad_end - padded
    e = rt[:, 0:K].astype(jnp.int32)
    rank = rt[:, 2 * K:3 * K].astype(jnp.int32)
    pos = (pad_start[e] + rank).reshape(-1)
    tile_start = (jnp.concatenate([pad_start, pad_end[-1:]]) // MOE_BLOCK).astype(jnp.int32)
    return tile_start, pos, n_rows


def _layer(x, norm1_g, w_in, cmp_pos, cmp_w1, cmp_w2, q_norm_g, k_norm_g, ret_gn_g, ret_gn_b,
           w_out, norm2_g, w_rg, b_rg, w_re, b_re, w_eg, w_eu, w_ed, tiles):
    B, S, D = x.shape
    T = B * S
    xt = x.reshape(T, D)

    w_main = jnp.concatenate([w_in[:, GATE_COL0 + NSA_GATE_WIDTH:], w_in[:, :GATE_COL0]], axis=1).astype(BF16)
    gpg = NSA_GATE_WIDTH // NSA_KV_GROUPS
    w_gate = jnp.concatenate(
        [jnp.pad(w_in[:, GATE_COL0 + g * gpg:GATE_COL0 + (g + 1) * gpg], ((0, 0), (0, LANES - gpg)))
         for g in range(NSA_KV_GROUPS)], axis=1).astype(BF16)
    proj, gate = _in_proj(xt, norm1_g.reshape(1, D), w_main, w_gate, tiles["tm_in"], tiles["tn_in"])

    cmp_kv = _compress(proj, cmp_pos, cmp_w1, cmp_w2, k_norm_g, B, S)
    o_nsa = _nsa(proj, gate, cmp_kv, q_norm_g.reshape(1, HEAD_DIM), k_norm_g, B, S, tiles["tq"], tiles["tk"])

    half = HEAD_DIM // 2
    inv_freq = ROPE_BASE ** (-jnp.arange(half, dtype=F32) / half)
    ang = jnp.arange(S, dtype=F32)[:, None] * inv_freq[None, :]
    cos2 = jnp.concatenate([jnp.cos(ang), jnp.cos(ang)], axis=1)
    sin2 = jnp.concatenate([-jnp.sin(ang), jnp.sin(ang)], axis=1)
    log_g = jnp.log1p(-jnp.exp2(-5.0 - jnp.arange(RET_HEADS, dtype=F32)))
    o_ret = _retention(proj, log_g, cos2, sin2, ret_gn_g, ret_gn_b, B, S)

    n_r = N_GROUPS + N_EXPERTS
    w_router = jnp.pad(jnp.concatenate([w_rg, w_re], axis=1), ((0, 0), (0, LANES - n_r)))
    b_router = jnp.pad(jnp.concatenate([b_rg, b_re]), (0, LANES - n_r)).reshape(1, LANES)
    x1, h2, rt, counts = _out_proj(o_nsa, o_ret, w_out.astype(BF16), xt, norm2_g.reshape(1, D),
                                   w_router, b_router, tiles["tm_out"])

    tile_start, pos, n_rows = _block_layout(rt, counts, T)
    row_tok = _plan(pos, n_rows)
    y_rows = _experts(h2, tile_start, row_tok, w_eg, w_eu, w_ed)
    out = _combine(pos, y_rows, x1, rt, tiles["tm_cmb"])
    return out.reshape(B, S, D)


def _tiles(T, S):
    return {
        "tm_in": min(1024, T), "tn_in": 512,
        "tq": min(128, S), "tk": min(512, S),
        "tm_out": min(512, T),
        "tm_cmb": min(256, T),
    }


def kernel(x, norm1_g, w_in, cmp_pos, cmp_w1, cmp_w2, q_norm_g, k_norm_g, ret_gn_g, ret_gn_b, w_out, norm2_g, w_router_group, b_router_group, w_router_expert, b_router_expert, w_exp_gate, w_exp_up, w_exp_down):
    B, S, _ = x.shape
    tiles = _tiles(B * S, S)
    for l in range(norm1_g.shape[0]):
        x = _layer(x, norm1_g[l], w_in[l], cmp_pos[l], cmp_w1[l], cmp_w2[l], q_norm_g[l], k_norm_g[l],
                   ret_gn_g[l], ret_gn_b[l], w_out[l], norm2_g[l], w_router_group[l], b_router_group[l],
                   w_router_expert[l], b_router_expert[l], w_exp_gate[l], w_exp_up[l], w_exp_down[l], tiles)
    return x
```

```python
import functools

import numpy as np
import jax
import jax.numpy as jnp
from jax import lax
from jax.experimental import pallas as pl
from jax.experimental.pallas import tpu as pltpu

F32 = jnp.float32
BF16 = jnp.bfloat16

D_MODEL = 2048
NSA_HEADS = 8
NSA_KV_GROUPS = 2
NSA_GROUP_HEADS = NSA_HEADS // NSA_KV_GROUPS
HEAD_DIM = 128
RET_HEADS = 8
CMP_BLOCK = 32
CMP_STRIDE = 16
SEL_BLOCK = 64
SEL_TOPK = 8
WINDOW = 512
RET_CHUNK = 128
ROPE_BASE = 10000.0
N_GROUPS = 4
EXPERTS_PER_GROUP = 8
N_EXPERTS = N_GROUPS * EXPERTS_PER_GROUP
TOPK_IN_GROUP = 2
D_EXPERT = 512
MOE_BLOCK = 256
RMS_EPS = 1e-6
GN_EPS = 1e-5
NEG_INF = -1e30
FORCED_SCORE = 1e6
LOG2E = 1.4426950408889634

NSA_Q_WIDTH = NSA_HEADS * HEAD_DIM
NSA_KV_WIDTH = NSA_KV_GROUPS * HEAD_DIM
NSA_GATE_WIDTH = 3 * NSA_HEADS
RET_WIDTH = RET_HEADS * HEAD_DIM
GATE_COL0 = NSA_Q_WIDTH + 6 * NSA_KV_WIDTH
LANES = 128
VMEM_LIMIT = 56 * 1024 * 1024

Q_COL0 = 4 * RET_WIDTH
KV_COL0 = Q_COL0 + NSA_Q_WIDTH
KC_BLK, VC_BLK, KS_BLK, VS_BLK, KW_BLK, VW_BLK = [KV_COL0 // LANES + 2 * t for t in range(6)]


def _rms(xf, g):
    return xf * lax.rsqrt(jnp.mean(xf * xf, axis=-1, keepdims=True) + RMS_EPS) * g


def _dot(a, b):
    return jnp.dot(a, b, preferred_element_type=F32)


def _dot_nt(a, b):
    return lax.dot_general(a, b, (((1,), (1,)), ((), ())), preferred_element_type=F32)


def _dot_tn(a, b):
    return lax.dot_general(a, b, (((0,), (0,)), ((), ())), preferred_element_type=F32)


def _split3(p):
    p1 = p.astype(BF16)
    r1 = p - p1.astype(F32)
    p2 = r1.astype(BF16)
    p3 = (r1 - p2.astype(F32)).astype(BF16)
    return p1, p2, p3


def _params(sem):
    return pltpu.CompilerParams(dimension_semantics=sem, vmem_limit_bytes=VMEM_LIMIT)


def _in_proj_kernel(x_ref, g_ref, w_ref, wg_ref, o_ref, og_ref, h_scr):
    @pl.when(pl.program_id(1) == 0)
    def _():
        h = _rms(x_ref[...], g_ref[...]).astype(BF16)
        h_scr[...] = h
        og_ref[...] = _dot(h, wg_ref[...])

    o_ref[...] = _dot(h_scr[...], w_ref[...])


def _in_proj(xt, g1, w_main, w_gate, tm, tn):
    T, D = xt.shape
    n_main = w_main.shape[1]
    n_gate = w_gate.shape[1]
    return pl.pallas_call(
        _in_proj_kernel,
        grid=(T // tm, n_main // tn),
        in_specs=[
            pl.BlockSpec((tm, D), lambda m, n: (m, 0)),
            pl.BlockSpec((1, D), lambda m, n: (0, 0)),
            pl.BlockSpec((D, tn), lambda m, n: (0, n)),
            pl.BlockSpec((D, n_gate), lambda m, n: (0, 0)),
        ],
        out_specs=[
            pl.BlockSpec((tm, tn), lambda m, n: (m, n)),
            pl.BlockSpec((tm, n_gate), lambda m, n: (m, 0)),
        ],
        out_shape=[
            jax.ShapeDtypeStruct((T, n_main), F32),
            jax.ShapeDtypeStruct((T, n_gate), F32),
        ],
        scratch_shapes=[pltpu.VMEM((tm, D), BF16)],
        compiler_params=_params(("parallel", "arbitrary")),
        name="in_proj",
    )(xt, g1, w_main, w_gate)


def _compress_kernel(x_ref, pos_ref, w1_ref, w2_ref, kg_ref, o_ref, xs_scr, *, S, n_pad):
    kv = pl.program_id(1)
    xs_scr[0:S, :] = x_ref[...]
    xs_scr[S:S + CMP_STRIDE, :] = jnp.zeros((CMP_STRIDE, HEAD_DIM), F32)
    acc = jnp.zeros((n_pad, HEAD_DIM), F32)
    for l in range(CMP_BLOCK):
        tb = xs_scr[pl.ds(l, n_pad, stride=CMP_STRIDE), :] + pos_ref[0, l:l + 1, :]
        acc = acc + _dot(tb.astype(BF16), w1_ref[0, l].astype(BF16))
    hid = acc * jax.nn.sigmoid(acc)
    out = _dot(hid.astype(BF16), w2_ref[0].astype(BF16))
    normed = _rms(out, kg_ref[0:1, :])
    o_ref[0, 0, 0] = jnp.where(kv == 0, normed, out).astype(BF16)


def _compress(proj, cmp_pos, cmp_w1, cmp_w2, k_norm_g, B, S):
    n_pad = S // CMP_STRIDE
    G = NSA_KV_GROUPS
    kern = functools.partial(_compress_kernel, S=S, n_pad=n_pad)
    return pl.pallas_call(
        kern,
        grid=(B, 2, G),
        in_specs=[
            pl.BlockSpec((S, HEAD_DIM), lambda b, kv, g: (b, KC_BLK + 2 * kv + g)),
            pl.BlockSpec((1, CMP_BLOCK, HEAD_DIM), lambda b, kv, g: (kv, 0, 0)),
            pl.BlockSpec((1, CMP_BLOCK, HEAD_DIM, HEAD_DIM), lambda b, kv, g: (kv, 0, 0, 0)),
            pl.BlockSpec((1, HEAD_DIM, HEAD_DIM), lambda b, kv, g: (kv, 0, 0)),
            pl.BlockSpec((3, HEAD_DIM), lambda b, kv, g: (0, 0)),
        ],
        out_specs=pl.BlockSpec((1, 1, 1, n_pad, HEAD_DIM), lambda b, kv, g: (b, kv, g, 0, 0)),
        out_shape=jax.ShapeDtypeStruct((B, 2, G, n_pad, HEAD_DIM), BF16),
        scratch_shapes=[pltpu.VMEM((S + CMP_STRIDE, HEAD_DIM), F32)],
        compiler_params=_params(("parallel", "parallel", "parallel")),
        name="compress",
    )(proj, cmp_pos, cmp_w1, cmp_w2, k_norm_g)


def _masked_softmax(s, msk):
    sm = jnp.where(msk, s, NEG_INF)
    m = jnp.max(sm, axis=-1, keepdims=True)
    e = jnp.where(msk, jnp.exp(sm - m), 0.0)
    den = jnp.sum(e, axis=-1, keepdims=True)
    return e / jnp.where(den > 0.0, den, 1.0)


def _nsa_kernel(q_ref, kc_ref, vc_ref, ks_ref, vs_ref, kw_ref, vw_ref, gate_ref, qg_ref, kg_ref,
                ovt_ref, ext_ref, o_ref, ksa, vsa, kwn, vwa, l_scr, acc_scr, *, S, tq, tk, wk):
    i = pl.program_id(2)
    H = NSA_GROUP_HEADS
    scale = HEAD_DIM ** -0.5
    c2 = scale * LOG2E
    n_cp = S // CMP_STRIDE
    n_sblk = S // SEL_BLOCK

    @pl.when(i == 0)
    def _():
        ksa[:, 0:HEAD_DIM] = _rms(ks_ref[...], kg_ref[1:2, :]).astype(BF16)
        ksa[:, HEAD_DIM:2 * HEAD_DIM] = ext_ref[...]
        kwn[...] = _rms(kw_ref[...], kg_ref[2:3, :]).astype(BF16)
        ones = jnp.ones((S, HEAD_DIM), BF16)
        vsa[:, 0:HEAD_DIM] = vs_ref[...].astype(BF16)
        vsa[:, HEAD_DIM:2 * HEAD_DIM] = ones
        vwa[:, 0:HEAD_DIM] = vw_ref[...].astype(BF16)
        vwa[:, HEAD_DIM:2 * HEAD_DIM] = ones

    pos = i * tq + lax.broadcasted_iota(jnp.int32, (tq, 1), 0)
    qg = qg_ref[...]
    q4 = jnp.concatenate([_rms(q_ref[:, h * HEAD_DIM:(h + 1) * HEAD_DIM], qg).astype(BF16)
                          for h in range(H)], axis=0)
    rows = [slice(h * tq, (h + 1) * tq) for h in range(H)]

    def softmax_pv(q_pairs, k, v, tail_bias):
        head = k.shape[0] - tail_bias.shape[1]
        outs, sums = [], []
        for q2 in q_pairs:
            s2 = _dot_nt(q2, k)
            es = []
            for hh in range(2):
                t = s2[hh * tq:(hh + 1) * tq] * c2
                tb = t[:, head:] + tail_bias
                m = jnp.max(tb, axis=-1, keepdims=True)
                if head:
                    ta = t[:, :head]
                    m = jnp.maximum(m, jnp.max(ta, axis=-1, keepdims=True))
                    es.append(jnp.concatenate([jnp.exp2(ta - m).astype(BF16),
                                               jnp.exp2(tb - m).astype(BF16)], axis=1))
                else:
                    es.append(jnp.exp2(tb - m).astype(BF16))
            o2 = _dot(jnp.concatenate(es, axis=0), v)
            outs += [o2[0:tq, 0:HEAD_DIM], o2[tq:2 * tq, 0:HEAD_DIM]]
            sums += [o2[0:tq, HEAD_DIM:2 * HEAD_DIM], o2[tq:2 * tq, HEAD_DIM:2 * HEAD_DIM]]
        return outs, sums

    ncol = lax.broadcasted_iota(jnp.int32, (1, n_cp), 1)
    cmask = (ncol * CMP_STRIDE + (CMP_BLOCK - 1)) <= pos
    s_c4 = _dot_nt(q4, kc_ref[0, 0, 0]) * scale
    ps = [_masked_softmax(s_c4[r], cmask) for r in rows]
    psum = ps[0] + ps[1] + ps[2] + ps[3]
    o_c4 = _dot(jnp.concatenate([p.astype(BF16) for p in ps], axis=0), vc_ref[0, 0, 0])

    ovt = ovt_ref[...]
    p1, p2, p3 = _split3(psum)
    imp_t = (_dot_nt(ovt, p1) + _dot_nt(ovt, p2) + _dot_nt(ovt, p3))[0:n_sblk, :]
    pos_t = i * tq + lax.broadcasted_iota(jnp.int32, (1, tq), 1)
    jrow = lax.broadcasted_iota(jnp.int32, (n_sblk, 1), 0)
    cur_t = pos_t // SEL_BLOCK
    valid_t = jrow * SEL_BLOCK <= pos_t
    forced_t = (jrow == 0) | (jrow == cur_t) | (jrow == cur_t - 1)
    score = jnp.where(valid_t, jnp.where(forced_t, FORCED_SCORE, imp_t), -jnp.inf)
    rank = jnp.zeros((n_sblk, tq), jnp.int32)
    for i2 in range(n_sblk):
        si = score[i2:i2 + 1, :]
        ahead = (si > score) | ((si == score) & (jrow > i2))
        rank = rank + jnp.where(ahead, 1, 0)
    drop_t = jnp.where((rank < SEL_TOPK) & valid_t, 0.0, NEG_INF)
    drop_t = jnp.concatenate([drop_t, jnp.full((LANES - n_sblk, tq), NEG_INF, F32)], axis=0)
    drop = drop_t.T.astype(BF16)
    drop2 = jnp.concatenate([drop, drop], axis=0)
    qa_pairs = [jnp.concatenate([q4[0:2 * tq], drop2], axis=1),
                jnp.concatenate([q4[2 * tq:4 * tq], drop2], axis=1)]

    kst = pl.multiple_of(jnp.clip(i * tq - WINDOW, 0, S - wk), tq)
    wpos = kst + lax.broadcasted_iota(jnp.int32, (1, wk), 1)
    wbias = jnp.where((wpos <= pos) & (wpos > pos - WINDOW), 0.0, NEG_INF)
    o_ws, lws = softmax_pv([q4[0:2 * tq], q4[2 * tq:4 * tq]],
                           kwn[pl.ds(kst, wk), :], vwa[pl.ds(kst, wk), :], wbias)

    def slc_prefix(n):
        kpos = (n - tk) + lax.broadcasted_iota(jnp.int32, (1, tk), 1)
        causal = jnp.where(kpos <= pos, 0.0, NEG_INF)
        outs, sums = softmax_pv(qa_pairs, ksa[0:n, :], vsa[0:n, :], causal)
        for r, o, l in zip(rows, outs, sums):
            acc_scr[r] = o
            l_scr[r] = l

    n_cls = S // tk
    cls = ((i + 1) * tq - 1) // tk
    for c in range(n_cls):
        pl.when(cls == c)(functools.partial(slc_prefix, (c + 1) * tk))

    gates = jax.nn.sigmoid(gate_ref[...])
    for h, r in enumerate(rows):
        c = 3 * h
        o = (o_c4[r] * gates[:, c:c + 1]
             + acc_scr[r] * (gates[:, c + 1:c + 2] / l_scr[r])
             + o_ws[h] * (gates[:, c + 2:c + 3] / lws[h]))
        o_ref[:, h * HEAD_DIM:(h + 1) * HEAD_DIM] = o.astype(BF16)


def _nsa_tables(S):
    n_cp, n_cmp, n_sblk = S // CMP_STRIDE, (S - CMP_BLOCK) // CMP_STRIDE + 1, S // SEL_BLOCK
    n = np.arange(n_cp)[None, :] * CMP_STRIDE
    j = np.arange(LANES)[:, None]
    ovt = ((n < (j + 1) * SEL_BLOCK) & (n + CMP_BLOCK > j * SEL_BLOCK)
           & (np.arange(n_cp)[None, :] < n_cmp) & (j < n_sblk))
    key_block = (np.arange(S) // SEL_BLOCK)[:, None] == np.arange(LANES)[None, :]
    return jnp.asarray(ovt, BF16), jnp.asarray(key_block, BF16)


def _nsa(proj, gate, cmp_kv, q_norm_g, k_norm_g, B, S, tq, tk):
    G = NSA_KV_GROUPS
    H = NSA_GROUP_HEADS
    n_cp = S // CMP_STRIDE
    wk = min(S, WINDOW + tq)
    nq = S // tq
    gw = H * HEAD_DIM
    ovt, key_block = _nsa_tables(S)
    kern = functools.partial(_nsa_kernel, S=S, tq=tq, tk=tk, wk=wk)
    kv_spec = lambda blk: pl.BlockSpec((S, HEAD_DIM), lambda b, g, i: (b, blk + g))
    return pl.pallas_call(
        kern,
        grid=(B, G, nq),
        in_specs=[
            pl.BlockSpec((tq, gw), lambda b, g, i: (b * nq + i, Q_COL0 // gw + g)),
            pl.BlockSpec((1, 1, 1, n_cp, HEAD_DIM), lambda b, g, i: (b, 0, g, 0, 0)),
            pl.BlockSpec((1, 1, 1, n_cp, HEAD_DIM), lambda b, g, i: (b, 1, g, 0, 0)),
            kv_spec(KS_BLK), kv_spec(VS_BLK), kv_spec(KW_BLK), kv_spec(VW_BLK),
            pl.BlockSpec((tq, LANES), lambda b, g, i: (b * nq + i, g)),
            pl.BlockSpec((1, HEAD_DIM), lambda b, g, i: (0, 0)),
            pl.BlockSpec((3, HEAD_DIM), lambda b, g, i: (0, 0)),
            pl.BlockSpec((LANES, n_cp), lambda b, g, i: (0, 0)),
            pl.BlockSpec((S, LANES), lambda b, g, i: (0, 0)),
        ],
        out_specs=pl.BlockSpec((tq, gw), lambda b, g, i: (b * nq + i, g)),
        out_shape=jax.ShapeDtypeStruct((B * S, NSA_Q_WIDTH), BF16),
        scratch_shapes=[
            pltpu.VMEM((S, 2 * HEAD_DIM), BF16), pltpu.VMEM((S, 2 * HEAD_DIM), BF16),
            pltpu.VMEM((S, HEAD_DIM), BF16), pltpu.VMEM((S, 2 * HEAD_DIM), BF16),
            pltpu.VMEM((H * tq, HEAD_DIM), F32), pltpu.VMEM((H * tq, HEAD_DIM), F32)],
        compiler_params=_params(("parallel", "parallel", "arbitrary")),
        name="nsa",
    )(proj, cmp_kv, cmp_kv, proj, proj, proj, proj, gate, q_norm_g, k_norm_g, ovt, key_block)


RET_HB = 8


def _ret_kernel(lg_ref, rq_ref, rk_ref, rv_ref, rg_ref, cos_ref, sin_ref, gg_ref, gb_ref,
                o_ref, r_scr):
    hg = pl.program_id(1)
    C = RET_CHUNK
    scale = HEAD_DIM ** -0.5

    @pl.when(pl.program_id(2) == 0)
    def _():
        r_scr[...] = jnp.zeros(r_scr.shape, F32)

    cos = cos_ref[...]
    sin = sin_ref[...]
    n_col = lax.broadcasted_iota(jnp.int32, (C, 1), 0).astype(F32)
    n_row = lax.broadcasted_iota(jnp.int32, (1, C), 1).astype(F32)
    diff = n_col - n_row
    for h in range(RET_HB):
        sl = slice(h * HEAD_DIM, (h + 1) * HEAD_DIM)
        hh = hg * RET_HB + h
        lg = lg_ref[hh]
        q = rq_ref[:, sl]
        k = rk_ref[:, sl]
        qf = q * cos + pltpu.roll(q, HEAD_DIM // 2, 1) * sin
        kf = (k * cos + pltpu.roll(k, HEAD_DIM // 2, 1) * sin) * scale
        v = rv_ref[:, sl].astype(BF16)
        dec = jnp.where(diff >= 0.0, jnp.exp(lg * jnp.maximum(diff, 0.0)), 0.0)
        xi = jnp.exp(lg * (n_col + 1.0))
        zeta = jnp.exp(lg * (C - 1.0 - n_col))
        cd = jnp.exp(jnp.full((1, HEAD_DIM), lg * float(C), F32))
        qb = qf.astype(BF16)
        r_old = r_scr[h]
        a = _dot_nt(qb, kf.astype(BF16)) * dec
        o = _dot(a.astype(BF16), v) + _dot(qb, r_old.astype(BF16)) * xi
        r_scr[h] = r_old * cd + _dot_tn((kf * zeta).astype(BF16), v)
        mu = jnp.mean(o, axis=-1, keepdims=True)
        d = o - mu
        var = jnp.mean(d * d, axis=-1, keepdims=True)
        y = d * lax.rsqrt(var + GN_EPS) * gg_ref[pl.ds(hh, 1), :] + gb_ref[pl.ds(hh, 1), :]
        gt = rg_ref[:, sl]
        o_ref[:, sl] = (gt * jax.nn.sigmoid(gt) * y).astype(BF16)


def _retention(proj, log_g, cos2, sin2, gn_g, gn_b, B, S):
    C = RET_CHUNK
    n_ch = S // C
    bw = RET_HB * HEAD_DIM
    nhb = RET_HEADS // RET_HB
    spec = lambda k: pl.BlockSpec((C, bw), lambda b, hg, c, lg: (b * n_ch + c, k * nhb + hg))
    grid_spec = pltpu.PrefetchScalarGridSpec(
        num_scalar_prefetch=1,
        grid=(B, nhb, n_ch),
        in_specs=[
            spec(0), spec(1), spec(2), spec(3),
            pl.BlockSpec((C, HEAD_DIM), lambda b, hg, c, lg: (c, 0)),
            pl.BlockSpec((C, HEAD_DIM), lambda b, hg, c, lg: (c, 0)),
            pl.BlockSpec((RET_HEADS, HEAD_DIM), lambda b, hg, c, lg: (0, 0)),
            pl.BlockSpec((RET_HEADS, HEAD_DIM), lambda b, hg, c, lg: (0, 0)),
        ],
        out_specs=pl.BlockSpec((C, bw), lambda b, hg, c, lg: (b * n_ch + c, hg)),
        scratch_shapes=[pltpu.VMEM((RET_HB, HEAD_DIM, HEAD_DIM), F32)],
    )
    return pl.pallas_call(
        _ret_kernel,
        grid_spec=grid_spec,
        out_shape=jax.ShapeDtypeStruct((B * S, RET_WIDTH), BF16),
        compiler_params=_params(("parallel", "parallel", "arbitrary")),
        name="retention",
    )(log_g, proj, proj, proj, proj, cos2, sin2, gn_g, gn_b)


def _lane_max(v):
    return jnp.max(v, axis=-1, keepdims=True)


def _lane_min(v):
    return jnp.min(v, axis=-1, keepdims=True)


def _out_kernel(on_ref, or_ref, w_ref, x_ref, g2_ref, wr_ref, br_ref, tri_ref,
                x1_ref, h2_ref, rt_ref, cnt_ref, cnt_scr):
    @pl.when(pl.program_id(0) == 0)
    def _():
        cnt_scr[...] = jnp.zeros(cnt_scr.shape, F32)

    half = on_ref.shape[1]
    acc = _dot(on_ref[...], w_ref[0:half, :]) + _dot(or_ref[...], w_ref[half:2 * half, :])
    x1 = x_ref[...] + acc
    x1_ref[...] = x1
    h2 = _rms(x1, g2_ref[...])
    h2_ref[...] = h2

    h_hi = h2.astype(BF16)
    h_lo = (h2 - h_hi.astype(F32)).astype(BF16)
    wr = wr_ref[...]
    w_hi = wr.astype(BF16)
    w_lo = (wr - w_hi.astype(F32)).astype(BF16)
    logits = _dot(h_hi, w_hi) + _dot(h_lo, w_hi) + _dot(h_hi, w_lo) + br_ref[...]

    tm = logits.shape[0]
    lane = lax.broadcasted_iota(jnp.int32, (tm, LANES), 1)
    big = jnp.int32(LANES)
    gm = lane < N_GROUPS
    gl = jnp.where(gm, logits, -jnp.inf)
    ge = jnp.where(gm, jnp.exp(gl - _lane_max(gl)), 0.0)
    pg = ge / jnp.sum(ge, axis=-1, keepdims=True)
    g_top = _lane_max(pg)
    g_idx = _lane_min(jnp.where(gm & (pg == g_top), lane, big))
    e0 = N_GROUPS + g_idx * EXPERTS_PER_GROUP
    em = (lane >= e0) & (lane < e0 + EXPERTS_PER_GROUP)
    el = jnp.where(em, logits, -jnp.inf)
    ee = jnp.where(em, jnp.exp(el - _lane_max(el)), 0.0)
    pe = ee / jnp.sum(ee, axis=-1, keepdims=True)
    t1 = _lane_max(jnp.where(em, pe, -1.0))
    i1 = _lane_min(jnp.where(em & (pe == t1), lane, big))
    em2 = em & (lane != i1)
    t2 = _lane_max(jnp.where(em2, pe, -1.0))
    i2 = _lane_min(jnp.where(em2 & (pe == t2), lane, big))
    tsum = t1 + t2
    w1 = g_top * t1 / tsum
    w2 = g_top * t2 / tsum
    e1 = i1 - N_GROUPS
    e2 = i2 - N_GROUPS
    oh1 = jnp.where(lane == e1, 1.0, 0.0)
    oh2 = jnp.where(lane == e2, 1.0, 0.0)
    both = oh1 + oh2
    before = _dot(tri_ref[...], both.astype(BF16)) + cnt_scr[...]
    r1 = jnp.sum(before * oh1, axis=-1, keepdims=True)
    r2 = jnp.sum(before * oh2, axis=-1, keepdims=True)
    cnt_scr[...] = cnt_scr[...] + jnp.sum(both, axis=0, keepdims=True)
    cnt_ref[...] = jnp.broadcast_to(cnt_scr[...], cnt_ref.shape)

    cols = (e1.astype(F32), e2.astype(F32), w1, w2, r1, r2)
    rt = jnp.zeros((tm, LANES), F32)
    for c, v in enumerate(cols):
        rt = jnp.where(lane == c, v, rt)
    rt_ref[...] = rt


def _out_proj(o_nsa, o_ret, w_out, xt, g2, w_router, b_router, tm):
    T, D = xt.shape
    half = o_nsa.shape[1]
    tri = jnp.asarray(np.tril(np.ones((tm, tm), np.float32), -1), BF16)
    return pl.pallas_call(
        _out_kernel,
        grid=(T // tm,),
        in_specs=[
            pl.BlockSpec((tm, half), lambda m: (m, 0)),
            pl.BlockSpec((tm, half), lambda m: (m, 0)),
            pl.BlockSpec((2 * half, D), lambda m: (0, 0), pipeline_mode=pl.Buffered(1)),
            pl.BlockSpec((tm, D), lambda m: (m, 0)),
            pl.BlockSpec((1, D), lambda m: (0, 0)),
            pl.BlockSpec((D, LANES), lambda m: (0, 0)),
            pl.BlockSpec((1, LANES), lambda m: (0, 0)),
            pl.BlockSpec((tm, tm), lambda m: (0, 0)),
        ],
        out_specs=[
            pl.BlockSpec((tm, D), lambda m: (m, 0)),
            pl.BlockSpec((tm, D), lambda m: (m, 0)),
            pl.BlockSpec((tm, LANES), lambda m: (m, 0)),
            pl.BlockSpec((8, LANES), lambda m: (0, 0)),
        ],
        out_shape=[
            jax.ShapeDtypeStruct((T, D), F32),
            jax.ShapeDtypeStruct((T, D), F32),
            jax.ShapeDtypeStruct((T, LANES), F32),
            jax.ShapeDtypeStruct((8, LANES), F32),
        ],
        scratch_shapes=[pltpu.VMEM((1, LANES), F32)],
        compiler_params=_params(("arbitrary",)),
        name="out_proj",
    )(o_nsa, o_ret, w_out, xt, g2, w_router, b_router, tri)


def _plan_kernel(pos_ref, tok_ref):
    def clear(r, carry):
        tok_ref[r] = 0
        return carry

    lax.fori_loop(0, tok_ref.shape[0], clear, 0, unroll=8)

    def place(a, carry):
        tok_ref[pos_ref[a]] = lax.shift_right_logical(a, 1)
        return carry

    lax.fori_loop(0, pos_ref.shape[0], place, 0, unroll=8)


def _plan(pos, n_rows):
    return pl.pallas_call(
        _plan_kernel,
        in_specs=[pl.BlockSpec(memory_space=pltpu.SMEM)],
        out_specs=pl.BlockSpec(memory_space=pltpu.SMEM),
        out_shape=jax.ShapeDtypeStruct((n_rows,), jnp.int32),
        name="plan",
    )(pos)


def _row_copy(src_hbm, row, dst, slot, sem):
    return pltpu.make_async_copy(src_hbm.at[pl.ds(row, 1), :], dst.at[pl.ds(slot, 1), :], sem)


def _expert_kernel(ts_ref, tok_ref, h2_hbm, wg_ref, wu_ref, wd_ref, y_hbm,
                   xbuf, ybuf, wg_b, wu_b, wd_b, gsem, osem, *, n_blocks):
    e = pl.program_id(0)
    M = MOE_BLOCK
    t0 = ts_ref[e]
    t1 = ts_ref[e + 1]
    n_used = ts_ref[N_EXPERTS]

    def gather_wait(s):
        pltpu.make_async_copy(h2_hbm.at[pl.ds(0, M), :], xbuf.at[s], gsem.at[s]).wait()

    def out_copy(t, s):
        return pltpu.make_async_copy(ybuf.at[s], y_hbm.at[pl.ds(pl.multiple_of(t * M, M), M), :], osem.at[s])

    @pl.when(e == 0)
    def _():
        def issue(r, carry):
            _row_copy(h2_hbm, tok_ref[r], xbuf.at[0], r, gsem.at[0]).start()
            return carry

        lax.fori_loop(0, M, issue, 0, unroll=8)

    @pl.when(t1 > t0)
    def _():
        wg_b[...] = wg_ref[0].astype(BF16)
        wu_b[...] = wu_ref[0].astype(BF16)
        wd_b[...] = wd_ref[0].astype(BF16)

    def tile(t, carry):
        s = lax.rem(t, 2)
        gather_wait(s)

        @pl.when(t >= 2)
        def _():
            out_copy(t - 2, s).wait()

        nbase = (t + 1) * M
        for r in range(M):
            _row_copy(h2_hbm, tok_ref[nbase + r], xbuf.at[1 - s], r, gsem.at[1 - s]).start(priority=1)
        xb = xbuf[s].astype(BF16)
        hg = _dot(xb, wg_b[...])
        hu = _dot(xb, wu_b[...])
        hb = (hg * jax.nn.sigmoid(hg) * hu).astype(BF16)
        ybuf[s] = _dot(hb, wd_b[...])
        out_copy(t, s).start()
        return carry

    lax.fori_loop(t0, t1, tile, 0)

    @pl.when(e == N_EXPERTS - 1)
    def _():
        gather_wait(lax.rem(n_used, 2))
        out_copy(n_used - 1, lax.rem(n_used - 1, 2)).wait()

        @pl.when(n_used >= 2)
        def _():
            out_copy(n_used - 2, lax.rem(n_used, 2)).wait()

        ybuf[0] = jnp.zeros((M, ybuf.shape[2]), F32)

        def clear(t, carry):
            out_copy(t, 0).start()
            return carry

        lax.fori_loop(n_used, n_blocks, clear, 0)

        def clear_wait(t, carry):
            out_copy(t, 0).wait()
            return carry

        lax.fori_loop(n_used, n_blocks, clear_wait, 0)


def _experts(h2, tile_start, row_tok, w_gate, w_up, w_down):
    T, D = h2.shape
    n_rows = row_tok.shape[0]
    n_blocks = n_rows // MOE_BLOCK
    assert (TOPK_IN_GROUP * T + N_EXPERTS * (MOE_BLOCK - 1)) // MOE_BLOCK < n_blocks
    grid_spec = pltpu.PrefetchScalarGridSpec(
        num_scalar_prefetch=2,
        grid=(N_EXPERTS,),
        in_specs=[
            pl.BlockSpec(memory_space=pl.ANY),
            pl.BlockSpec((1, D, D_EXPERT), lambda e, ts, tok: (e, 0, 0)),
            pl.BlockSpec((1, D, D_EXPERT), lambda e, ts, tok: (e, 0, 0)),
            pl.BlockSpec((1, D_EXPERT, D), lambda e, ts, tok: (e, 0, 0)),
        ],
        out_specs=pl.BlockSpec(memory_space=pl.ANY),
        scratch_shapes=[
            pltpu.VMEM((2, MOE_BLOCK, D), F32), pltpu.VMEM((2, MOE_BLOCK, D), F32),
            pltpu.VMEM((D, D_EXPERT), BF16), pltpu.VMEM((D, D_EXPERT), BF16),
            pltpu.VMEM((D_EXPERT, D), BF16),
            pltpu.SemaphoreType.DMA((2,)), pltpu.SemaphoreType.DMA((2,)),
        ],
    )
    return pl.pallas_call(
        functools.partial(_expert_kernel, n_blocks=n_blocks),
        grid_spec=grid_spec,
        out_shape=jax.ShapeDtypeStruct((n_rows, D), F32),
        compiler_params=_params(("arbitrary",)),
        name="experts",
    )(tile_start, row_tok, h2, w_gate, w_up, w_down)


def _combine_kernel(pos_ref, y_hbm, x1_ref, rt_ref, o_ref, ybuf, sems):
    i = pl.program_id(0)
    n = pl.num_programs(0)
    tm = x1_ref.shape[0]
    K = TOPK_IN_GROUP
    slot = lax.rem(i, 2)

    def issue_tile(t, s):
        def issue(r, carry):
            a = (t * tm + r) * K
            for k in range(K):
                _row_copy(y_hbm, pos_ref[a + k], ybuf.at[s, k], r, sems.at[s]).start(priority=k)
            return carry

        lax.fori_loop(0, tm, issue, 0, unroll=8)

    @pl.when(i == 0)
    def _():
        issue_tile(0, 0)

    @pl.when(i + 1 < n)
    def _():
        issue_tile(i + 1, 1 - slot)

    for k in range(K):
        pltpu.make_async_copy(y_hbm.at[pl.ds(0, tm), :], ybuf.at[slot, k], sems.at[slot]).wait()
    w = rt_ref[...]
    yb = ybuf[slot]
    o_ref[...] = x1_ref[...] + (yb[0] * w[:, K:K + 1] + yb[1] * w[:, K + 1:K + 2])


def _combine(pos, y_rows, x1, rt, tm):
    T, D = x1.shape
    grid_spec = pltpu.PrefetchScalarGridSpec(
        num_scalar_prefetch=1,
        grid=(T // tm,),
        in_specs=[
            pl.BlockSpec(memory_space=pl.ANY),
            pl.BlockSpec((tm, D), lambda i, pos: (i, 0)),
            pl.BlockSpec((tm, LANES), lambda i, pos: (i, 0)),
        ],
        out_specs=pl.BlockSpec((tm, D), lambda i, pos: (i, 0)),
        scratch_shapes=[pltpu.VMEM((2, TOPK_IN_GROUP, tm, D), F32), pltpu.SemaphoreType.DMA((2,))],
    )
    return pl.pallas_call(
        _combine_kernel,
        grid_spec=grid_spec,
        out_shape=jax.ShapeDtypeStruct((T, D), F32),
        compiler_params=_params(("arbitrary",)),
        name="combine",
    )(pos, y_rows, x1, rt)


def _block_layout(rt, counts, T):
    K = TOPK_IN_GROUP
    n_rows = (K * T + N_EXPERTS * (MOE_BLOCK - 1) + MOE_BLOCK - 1) // MOE_BLOCK * MOE_BLOCK
    n_blocks = n_rows // MOE_BLOCK
    cnt = counts[0, :N_EXPERTS].astype(jnp.int32)
    padded = (cnt + MOE_BLOCK - 1) // MOE_BLOCK * MOE_BLOCK
    pad_end = jnp.cumsum(padded)
    pad_start = pad_end - padded
    e = rt[:, 0:K].astype(jnp.int32)
    rank = rt[:, 2 * K:3 * K].astype(jnp.int32)
    pos = (pad_start[e] + rank).reshape(-1)
    tile_start = (jnp.concatenate([pad_start, pad_end[-1:]]) // MOE_BLOCK).astype(jnp.int32)
    return tile_start, pos, n_rows


def _layer(x, norm1_g, w_in, cmp_pos, cmp_w1, cmp_w2, q_norm_g, k_norm_g, ret_gn_g, ret_gn_b,
           w_out, norm2_g, w_rg, b_rg, w_re, b_re, w_eg, w_eu, w_ed, tiles):
    B, S, D = x.shape
    T = B * S
    xt = x.reshape(T, D)

    w_main = jnp.concatenate([w_in[:, GATE_COL0 + NSA_GATE_WIDTH:], w_in[:, :GATE_COL0]], axis=1).astype(BF16)
    gpg = NSA_GATE_WIDTH // NSA_KV_GROUPS
    w_gate = jnp.concatenate(
        [jnp.pad(w_in[:, GATE_COL0 + g * gpg:GATE_COL0 + (g + 1) * gpg], ((0, 0), (0, LANES - gpg)))
         for g in range(NSA_KV_GROUPS)], axis=1).astype(BF16)
    proj, gate = _in_proj(xt, norm1_g.reshape(1, D), w_main, w_gate, tiles["tm_in"], tiles["tn_in"])

    cmp_kv = _compress(proj, cmp_pos, cmp_w1, cmp_w2, k_norm_g, B, S)
    o_nsa = _nsa(proj, gate, cmp_kv, q_norm_g.reshape(1, HEAD_DIM), k_norm_g, B, S, tiles["tq"], tiles["tk"])

    half = HEAD_DIM // 2
    inv_freq = ROPE_BASE ** (-jnp.arange(half, dtype=F32) / half)
    ang = jnp.arange(S, dtype=F32)[:, None] * inv_freq[None, :]
    cos2 = jnp.concatenate([jnp.cos(ang), jnp.cos(ang)], axis=1)
    sin2 = jnp.concatenate([-jnp.sin(ang), jnp.sin(ang)], axis=1)
    log_g = jnp.log1p(-jnp.exp2(-5.0 - jnp.arange(RET_HEADS, dtype=F32)))
    o_ret = _retention(proj, log_g, cos2, sin2, ret_gn_g, ret_gn_b, B, S)

    n_r = N_GROUPS + N_EXPERTS
    w_router = jnp.pad(jnp.concatenate([w_rg, w_re], axis=1), ((0, 0), (0, LANES - n_r)))
    b_router = jnp.pad(jnp.concatenate([b_rg, b_re]), (0, LANES - n_r)).reshape(1, LANES)
    x1, h2, rt, counts = _out_proj(o_nsa, o_ret, w_out.astype(BF16), xt, norm2_g.reshape(1, D),
                                   w_router, b_router, tiles["tm_out"])

    tile_start, pos, n_rows = _block_layout(rt, counts, T)
    row_tok = _plan(pos, n_rows)
    y_rows = _experts(h2, tile_start, row_tok, w_eg, w_eu, w_ed)
    out = _combine(pos, y_rows, x1, rt, tiles["tm_cmb"])
    return out.reshape(B, S, D)


def _tiles(T, S):
    return {
        "tm_in": min(1024, T), "tn_in": 512,
        "tq": min(128, S), "tk": min(512, S),
        "tm_out": min(512, T),
        "tm_cmb": min(256, T),
    }


def kernel(x, norm1_g, w_in, cmp_pos, cmp_w1, cmp_w2, q_norm_g, k_norm_g, ret_gn_g, ret_gn_b, w_out, norm2_g, w_router_group, b_router_group, w_router_expert, b_router_expert, w_exp_gate, w_exp_up, w_exp_down):
    B, S, _ = x.shape
    tiles = _tiles(B * S, S)
    for l in range(norm1_g.shape[0]):
        x = _layer(x, norm1_g[l], w_in[l], cmp_pos[l], cmp_w1[l], cmp_w2[l], q_norm_g[l], k_norm_g[l],
                   ret_gn_g[l], ret_gn_b[l], w_out[l], norm2_g[l], w_router_group[l], b_router_group[l],
                   w_router_expert[l], b_router_expert[l], w_exp_gate[l], w_exp_up[l], w_exp_down[l], tiles)
    return x
```

```python
import functools

import numpy as np
import jax
import jax.numpy as jnp
from jax import lax
from jax.experimental import pallas as pl
from jax.experimental.pallas import tpu as pltpu

F32 = jnp.float32
BF16 = jnp.bfloat16

D_MODEL = 2048
NSA_HEADS = 8
NSA_KV_GROUPS = 2
NSA_GROUP_HEADS = NSA_HEADS // NSA_KV_GROUPS
HEAD_DIM = 128
RET_HEADS = 8
CMP_BLOCK = 32
CMP_STRIDE = 16
SEL_BLOCK = 64
SEL_TOPK = 8
WINDOW = 512
RET_CHUNK = 128
ROPE_BASE = 10000.0
N_GROUPS = 4
EXPERTS_PER_GROUP = 8
N_EXPERTS = N_GROUPS * EXPERTS_PER_GROUP
TOPK_IN_GROUP = 2
D_EXPERT = 512
MOE_BLOCK = 256
RMS_EPS = 1e-6
GN_EPS = 1e-5
NEG_INF = -1e30
FORCED_SCORE = 1e6
LOG2E = 1.4426950408889634

NSA_Q_WIDTH = NSA_HEADS * HEAD_DIM
NSA_KV_WIDTH = NSA_KV_GROUPS * HEAD_DIM
NSA_GATE_WIDTH = 3 * NSA_HEADS
RET_WIDTH = RET_HEADS * HEAD_DIM
GATE_COL0 = NSA_Q_WIDTH + 6 * NSA_KV_WIDTH
LANES = 128
VMEM_LIMIT = 56 * 1024 * 1024

Q_COL0 = 4 * RET_WIDTH
KV_COL0 = Q_COL0 + NSA_Q_WIDTH
KC_BLK, VC_BLK, KS_BLK, VS_BLK, KW_BLK, VW_BLK = [KV_COL0 // LANES + 2 * t for t in range(6)]


def _rms(xf, g):
    return xf * lax.rsqrt(jnp.mean(xf * xf, axis=-1, keepdims=True) + RMS_EPS) * g


def _dot(a, b):
    return jnp.dot(a, b, preferred_element_type=F32)


def _dot_nt(a, b):
    return lax.dot_general(a, b, (((1,), (1,)), ((), ())), preferred_element_type=F32)


def _dot_tn(a, b):
    return lax.dot_general(a, b, (((0,), (0,)), ((), ())), preferred_element_type=F32)


def _split3(p):
    p1 = p.astype(BF16)
    r1 = p - p1.astype(F32)
    p2 = r1.astype(BF16)
    p3 = (r1 - p2.astype(F32)).astype(BF16)
    return p1, p2, p3


def _params(sem):
    return pltpu.CompilerParams(dimension_semantics=sem, vmem_limit_bytes=VMEM_LIMIT)


def _in_proj_kernel(x_ref, g_ref, w_ref, wg_ref, o_ref, og_ref, h_scr):
    @pl.when(pl.program_id(1) == 0)
    def _():
        h = _rms(x_ref[...], g_ref[...]).astype(BF16)
        h_scr[...] = h
        og_ref[...] = _dot(h, wg_ref[...])

    o_ref[...] = _dot(h_scr[...], w_ref[...])


def _in_proj(xt, g1, w_main, w_gate, tm, tn):
    T, D = xt.shape
    n_main = w_main.shape[1]
    n_gate = w_gate.shape[1]
    return pl.pallas_call(
        _in_proj_kernel,
        grid=(T // tm, n_main // tn),
        in_specs=[
            pl.BlockSpec((tm, D), lambda m, n: (m, 0)),
            pl.BlockSpec((1, D), lambda m, n: (0, 0)),
            pl.BlockSpec((D, tn), lambda m, n: (0, n)),
            pl.BlockSpec((D, n_gate), lambda m, n: (0, 0)),
        ],
        out_specs=[
            pl.BlockSpec((tm, tn), lambda m, n: (m, n)),
            pl.BlockSpec((tm, n_gate), lambda m, n: (m, 0)),
        ],
        out_shape=[
            jax.ShapeDtypeStruct((T, n_main), F32),
            jax.ShapeDtypeStruct((T, n_gate), F32),
        ],
        scratch_shapes=[pltpu.VMEM((tm, D), BF16)],
        compiler_params=_params(("parallel", "arbitrary")),
        name="in_proj",
    )(xt, g1, w_main, w_gate)


def _compress_kernel(x_ref, pos_ref, w1_ref, w2_ref, kg_ref, o_ref, xs_scr, *, S, n_pad):
    kv = pl.program_id(1)
    xs_scr[0:S, :] = x_ref[...]
    xs_scr[S:S + CMP_STRIDE, :] = jnp.zeros((CMP_STRIDE, HEAD_DIM), F32)
    acc = jnp.zeros((n_pad, HEAD_DIM), F32)
    for l in range(CMP_BLOCK):
        tb = xs_scr[pl.ds(l, n_pad, stride=CMP_STRIDE), :] + pos_ref[0, l:l + 1, :]
        acc = acc + _dot(tb.astype(BF16), w1_ref[0, l].astype(BF16))
    hid = acc * jax.nn.sigmoid(acc)
    out = _dot(hid.astype(BF16), w2_ref[0].astype(BF16))
    normed = _rms(out, kg_ref[0:1, :])
    o_ref[0, 0, 0] = jnp.where(kv == 0, normed, out).astype(BF16)


def _compress(proj, cmp_pos, cmp_w1, cmp_w2, k_norm_g, B, S):
    n_pad = S // CMP_STRIDE
    G = NSA_KV_GROUPS
    kern = functools.partial(_compress_kernel, S=S, n_pad=n_pad)
    return pl.pallas_call(
        kern,
        grid=(B, 2, G),
        in_specs=[
            pl.BlockSpec((S, HEAD_DIM), lambda b, kv, g: (b, KC_BLK + 2 * kv + g)),
            pl.BlockSpec((1, CMP_BLOCK, HEAD_DIM), lambda b, kv, g: (kv, 0, 0)),
            pl.BlockSpec((1, CMP_BLOCK, HEAD_DIM, HEAD_DIM), lambda b, kv, g: (kv, 0, 0, 0)),
            pl.BlockSpec((1, HEAD_DIM, HEAD_DIM), lambda b, kv, g: (kv, 0, 0)),
            pl.BlockSpec((3, HEAD_DIM), lambda b, kv, g: (0, 0)),
        ],
        out_specs=pl.BlockSpec((1, 1, 1, n_pad, HEAD_DIM), lambda b, kv, g: (b, kv, g, 0, 0)),
        out_shape=jax.ShapeDtypeStruct((B, 2, G, n_pad, HEAD_DIM), BF16),
        scratch_shapes=[pltpu.VMEM((S + CMP_STRIDE, HEAD_DIM), F32)],
        compiler_params=_params(("parallel", "parallel", "parallel")),
        name="compress",
    )(proj, cmp_pos, cmp_w1, cmp_w2, k_norm_g)


def _masked_softmax(s, msk):
    sm = jnp.where(msk, s, NEG_INF)
    m = jnp.max(sm, axis=-1, keepdims=True)
    e = jnp.where(msk, jnp.exp(sm - m), 0.0)
    den = jnp.sum(e, axis=-1, keepdims=True)
    return e / jnp.where(den > 0.0, den, 1.0)


def _nsa_kernel(q_ref, kc_ref, vc_ref, ks_ref, vs_ref, kw_ref, vw_ref, gate_ref, qg_ref, kg_ref,
                ovt_ref, ext_ref, o_ref, ksa, vsa, kwn, vwa, l_scr, acc_scr, *, S, tq, tk, wk):
    i = pl.program_id(2)
    H = NSA_GROUP_HEADS
    scale = HEAD_DIM ** -0.5
    c2 = scale * LOG2E
    n_cp = S // CMP_STRIDE
    n_sblk = S // SEL_BLOCK

    @pl.when(i == 0)
    def _():
        ksa[:, 0:HEAD_DIM] = _rms(ks_ref[...], kg_ref[1:2, :]).astype(BF16)
        ksa[:, HEAD_DIM:2 * HEAD_DIM] = ext_ref[...]
        kwn[...] = _rms(kw_ref[...], kg_ref[2:3, :]).astype(BF16)
        ones = jnp.ones((S, HEAD_DIM), BF16)
        vsa[:, 0:HEAD_DIM] = vs_ref[...].astype(BF16)
        vsa[:, HEAD_DIM:2 * HEAD_DIM] = ones
        vwa[:, 0:HEAD_DIM] = vw_ref[...].astype(BF16)
        vwa[:, HEAD_DIM:2 * HEAD_DIM] = ones

    pos = i * tq + lax.broadcasted_iota(jnp.int32, (tq, 1), 0)
    qg = qg_ref[...]
    q4 = jnp.concatenate([_rms(q_ref[:, h * HEAD_DIM:(h + 1) * HEAD_DIM], qg).astype(BF16)
                          for h in range(H)], axis=0)
    rows = [slice(h * tq, (h + 1) * tq) for h in range(H)]

    def softmax_pv(q_pairs, k, v, tail_bias):
        head = k.shape[0] - tail_bias.shape[1]
        outs, sums = [], []
        for q2 in q_pairs:
            s2 = _dot_nt(q2, k)
            es = []
            for hh in range(2):
                t = s2[hh * tq:(hh + 1) * tq] * c2
                tb = t[:, head:] + tail_bias
                m = jnp.max(tb, axis=-1, keepdims=True)
                if head:
                    ta = t[:, :head]
                    m = jnp.maximum(m, jnp.max(ta, axis=-1, keepdims=True))
                    es.append(jnp.concatenate([jnp.exp2(ta - m).astype(BF16),
                                               jnp.exp2(tb - m).astype(BF16)], axis=1))
                else:
                    es.append(jnp.exp2(tb - m).astype(BF16))
            o2 = _dot(jnp.concatenate(es, axis=0), v)
            outs += [o2[0:tq, 0:HEAD_DIM], o2[tq:2 * tq, 0:HEAD_DIM]]
            sums += [o2[0:tq, HEAD_DIM:2 * HEAD_DIM], o2[tq:2 * tq, HEAD_DIM:2 * HEAD_DIM]]
        return outs, sums

    ncol = lax.broadcasted_iota(jnp.int32, (1, n_cp), 1)
    cmask = (ncol * CMP_STRIDE + (CMP_BLOCK - 1)) <= pos
    s_c4 = _dot_nt(q4, kc_ref[0, 0, 0]) * scale
    ps = [_masked_softmax(s_c4[r], cmask) for r in rows]
    psum = ps[0] + ps[1] + ps[2] + ps[3]
    o_c4 = _dot(jnp.concatenate([p.astype(BF16) for p in ps], axis=0), vc_ref[0, 0, 0])

    ovt = ovt_ref[...]
    p1, p2, p3 = _split3(psum)
    imp_t = (_dot_nt(ovt, p1) + _dot_nt(ovt, p2) + _dot_nt(ovt, p3))[0:n_sblk, :]
    pos_t = i * tq + lax.broadcasted_iota(jnp.int32, (1, tq), 1)
    jrow = lax.broadcasted_iota(jnp.int32, (n_sblk, 1), 0)
    cur_t = pos_t // SEL_BLOCK
    valid_t = jrow * SEL_BLOCK <= pos_t
    forced_t = (jrow == 0) | (jrow == cur_t) | (jrow == cur_t - 1)
    score = jnp.where(valid_t, jnp.where(forced_t, FORCED_SCORE, imp_t), -jnp.inf)
    rank = jnp.zeros((n_sblk, tq), jnp.int32)
    for i2 in range(n_sblk):
        si = score[i2:i2 + 1, :]
        ahead = (si > score) | ((si == score) & (jrow > i2))
        rank = rank + jnp.where(ahead, 1, 0)
    drop_t = jnp.where((rank < SEL_TOPK) & valid_t, 0.0, NEG_INF)
    drop_t = jnp.concatenate([drop_t, jnp.full((LANES - n_sblk, tq), NEG_INF, F32)], axis=0)
    drop = drop_t.T.astype(BF16)
    drop2 = jnp.concatenate([drop, drop], axis=0)
    qa_pairs = [jnp.concatenate([q4[0:2 * tq], drop2], axis=1),
                jnp.concatenate([q4[2 * tq:4 * tq], drop2], axis=1)]

    kst = pl.multiple_of(jnp.clip(i * tq - WINDOW, 0, S - wk), tq)
    wpos = kst + lax.broadcasted_iota(jnp.int32, (1, wk), 1)
    wbias = jnp.where((wpos <= pos) & (wpos > pos - WINDOW), 0.0, NEG_INF)
    o_ws, lws = softmax_pv([q4[0:2 * tq], q4[2 * tq:4 * tq]],
                           kwn[pl.ds(kst, wk), :], vwa[pl.ds(kst, wk), :], wbias)

    def slc_prefix(n):
        kpos = (n - tk) + lax.broadcasted_iota(jnp.int32, (1, tk), 1)
        causal = jnp.where(kpos <= pos, 0.0, NEG_INF)
        outs, sums = softmax_pv(qa_pairs, ksa[0:n, :], vsa[0:n, :], causal)
        for r, o, l in zip(rows, outs, sums):
            acc_scr[r] = o
            l_scr[r] = l

    n_cls = S // tk
    cls = ((i + 1) * tq - 1) // tk
    for c in range(n_cls):
        pl.when(cls == c)(functools.partial(slc_prefix, (c + 1) * tk))

    gates = jax.nn.sigmoid(gate_ref[...])
    for h, r in enumerate(rows):
        c = 3 * h
        o = (o_c4[r] * gates[:, c:c + 1]
             + acc_scr[r] * (gates[:, c + 1:c + 2] / l_scr[r])
             + o_ws[h] * (gates[:, c + 2:c + 3] / lws[h]))
        o_ref[:, h * HEAD_DIM:(h + 1) * HEAD_DIM] = o.astype(BF16)


def _nsa_tables(S):
    n_cp, n_cmp, n_sblk = S // CMP_STRIDE, (S - CMP_BLOCK) // CMP_STRIDE + 1, S // SEL_BLOCK
    n = np.arange(n_cp)[None, :] * CMP_STRIDE
    j = np.arange(LANES)[:, None]
    ovt = ((n < (j + 1) * SEL_BLOCK) & (n + CMP_BLOCK > j * SEL_BLOCK)
           & (np.arange(n_cp)[None, :] < n_cmp) & (j < n_sblk))
    key_block = (np.arange(S) // SEL_BLOCK)[:, None] == np.arange(LANES)[None, :]
    return jnp.asarray(ovt, BF16), jnp.asarray(key_block, BF16)


def _nsa(proj, gate, cmp_kv, q_norm_g, k_norm_g, B, S, tq, tk):
    G = NSA_KV_GROUPS
    H = NSA_GROUP_HEADS
    n_cp = S // CMP_STRIDE
    wk = min(S, WINDOW + tq)
    nq = S // tq
    gw = H * HEAD_DIM
    ovt, key_block = _nsa_tables(S)
    kern = functools.partial(_nsa_kernel, S=S, tq=tq, tk=tk, wk=wk)
    kv_spec = lambda blk: pl.BlockSpec((S, HEAD_DIM), lambda b, g, i: (b, blk + g))
    return pl.pallas_call(
        kern,
        grid=(B, G, nq),
        in_specs=[
            pl.BlockSpec((tq, gw), lambda b, g, i: (b * nq + i, Q_COL0 // gw + g)),
            pl.BlockSpec((1, 1, 1, n_cp, HEAD_DIM), lambda b, g, i: (b, 0, g, 0, 0)),
            pl.BlockSpec((1, 1, 1, n_cp, HEAD_DIM), lambda b, g, i: (b, 1, g, 0, 0)),
            kv_spec(KS_BLK), kv_spec(VS_BLK), kv_spec(KW_BLK), kv_spec(VW_BLK),
            pl.BlockSpec((tq, LANES), lambda b, g, i: (b * nq + i, g)),
            pl.BlockSpec((1, HEAD_DIM), lambda b, g, i: (0, 0)),
            pl.BlockSpec((3, HEAD_DIM), lambda b, g, i: (0, 0)),
            pl.BlockSpec((LANES, n_cp), lambda b, g, i: (0, 0)),
            pl.BlockSpec((S, LANES), lambda b, g, i: (0, 0)),
        ],
        out_specs=pl.BlockSpec((tq, gw), lambda b, g, i: (b * nq + i, g)),
        out_shape=jax.ShapeDtypeStruct((B * S, NSA_Q_WIDTH), BF16),
        scratch_shapes=[
            pltpu.VMEM((S, 2 * HEAD_DIM), BF16), pltpu.VMEM((S, 2 * HEAD_DIM), BF16),
            pltpu.VMEM((S, HEAD_DIM), BF16), pltpu.VMEM((S, 2 * HEAD_DIM), BF16),
            pltpu.VMEM((H * tq, HEAD_DIM), F32), pltpu.VMEM((H * tq, HEAD_DIM), F32)],
        compiler_params=_params(("parallel", "parallel", "arbitrary")),
        name="nsa",
    )(proj, cmp_kv, cmp_kv, proj, proj, proj, proj, gate, q_norm_g, k_norm_g, ovt, key_block)


RET_HB = 8


def _ret_kernel(lg_ref, rq_ref, rk_ref, rv_ref, rg_ref, cos_ref, sin_ref, gg_ref, gb_ref,
                o_ref, r_scr):
    hg = pl.program_id(1)
    C = RET_CHUNK
    scale = HEAD_DIM ** -0.5

    @pl.when(pl.program_id(2) == 0)
    def _():
        r_scr[...] = jnp.zeros(r_scr.shape, F32)

    cos = cos_ref[...]
    sin = sin_ref[...]
    n_col = lax.broadcasted_iota(jnp.int32, (C, 1), 0).astype(F32)
    n_row = lax.broadcasted_iota(jnp.int32, (1, C), 1).astype(F32)
    diff = n_col - n_row
    for h in range(RET_HB):
        sl = slice(h * HEAD_DIM, (h + 1) * HEAD_DIM)
        hh = hg * RET_HB + h
        lg = lg_ref[hh]
        q = rq_ref[:, sl]
        k = rk_ref[:, sl]
        qf = q * cos + pltpu.roll(q, HEAD_DIM // 2, 1) * sin
        kf = (k * cos + pltpu.roll(k, HEAD_DIM // 2, 1) * sin) * scale
        v = rv_ref[:, sl].astype(BF16)
        dec = jnp.where(diff >= 0.0, jnp.exp(lg * jnp.maximum(diff, 0.0)), 0.0)
        xi = jnp.exp(lg * (n_col + 1.0))
        zeta = jnp.exp(lg * (C - 1.0 - n_col))
        cd = jnp.exp(jnp.full((1, HEAD_DIM), lg * float(C), F32))
        qb = qf.astype(BF16)
        r_old = r_scr[h]
        a = _dot_nt(qb, kf.astype(BF16)) * dec
        o = _dot(a.astype(BF16), v) + _dot(qb, r_old.astype(BF16)) * xi
        r_scr[h] = r_old * cd + _dot_tn((kf * zeta).astype(BF16), v)
        mu = jnp.mean(o, axis=-1, keepdims=True)
        d = o - mu
        var = jnp.mean(d * d, axis=-1, keepdims=True)
        y = d * lax.rsqrt(var + GN_EPS) * gg_ref[pl.ds(hh, 1), :] + gb_ref[pl.ds(hh, 1), :]
        gt = rg_ref[:, sl]
        o_ref[:, sl] = (gt * jax.nn.sigmoid(gt) * y).astype(BF16)


def _retention(proj, log_g, cos2, sin2, gn_g, gn_b, B, S):
    C = RET_CHUNK
    n_ch = S // C
    bw = RET_HB * HEAD_DIM
    nhb = RET_HEADS // RET_HB
    spec = lambda k: pl.BlockSpec((C, bw), lambda b, hg, c, lg: (b * n_ch + c, k * nhb + hg))
    grid_spec = pltpu.PrefetchScalarGridSpec(
        num_scalar_prefetch=1,
        grid=(B, nhb, n_ch),
        in_specs=[
            spec(0), spec(1), spec(2), spec(3),
            pl.BlockSpec((C, HEAD_DIM), lambda b, hg, c, lg: (c, 0)),
            pl.BlockSpec((C, HEAD_DIM), lambda b, hg, c, lg: (c, 0)),
            pl.BlockSpec((RET_HEADS, HEAD_DIM), lambda b, hg, c, lg: (0, 0)),
            pl.BlockSpec((RET_HEADS, HEAD_DIM), lambda b, hg, c, lg: (0, 0)),
        ],
        out_specs=pl.BlockSpec((C, bw), lambda b, hg, c, lg: (b * n_ch + c, hg)),
        scratch_shapes=[pltpu.VMEM((RET_HB, HEAD_DIM, HEAD_DIM), F32)],
    )
    return pl.pallas_call(
        _ret_kernel,
        grid_spec=grid_spec,
        out_shape=jax.ShapeDtypeStruct((B * S, RET_WIDTH), BF16),
        compiler_params=_params(("parallel", "parallel", "arbitrary")),
        name="retention",
    )(log_g, proj, proj, proj, proj, cos2, sin2, gn_g, gn_b)


def _lane_max(v):
    return jnp.max(v, axis=-1, keepdims=True)


def _lane_min(v):
    return jnp.min(v, axis=-1, keepdims=True)


def _out_kernel(on_ref, or_ref, w_ref, x_ref, g2_ref, wr_ref, br_ref, tri_ref,
                x1_ref, h2_ref, rt_ref, cnt_ref, cnt_scr):
    @pl.when(pl.program_id(0) == 0)
    def _():
        cnt_scr[...] = jnp.zeros(cnt_scr.shape, F32)

    half = on_ref.shape[1]
    acc = _dot(on_ref[...], w_ref[0:half, :]) + _dot(or_ref[...], w_ref[half:2 * half, :])
    x1 = x_ref[...] + acc
    x1_ref[...] = x1
    h2 = _rms(x1, g2_ref[...])
    h2_ref[...] = h2

    h_hi = h2.astype(BF16)
    h_lo = (h2 - h_hi.astype(F32)).astype(BF16)
    wr = wr_ref[...]
    w_hi = wr.astype(BF16)
    w_lo = (wr - w_hi.astype(F32)).astype(BF16)
    logits = _dot(h_hi, w_hi) + _dot(h_lo, w_hi) + _dot(h_hi, w_lo) + br_ref[...]

    tm = logits.shape[0]
    lane = lax.broadcasted_iota(jnp.int32, (tm, LANES), 1)
    big = jnp.int32(LANES)
    gm = lane < N_GROUPS
    gl = jnp.where(gm, logits, -jnp.inf)
    ge = jnp.where(gm, jnp.exp(gl - _lane_max(gl)), 0.0)
    pg = ge / jnp.sum(ge, axis=-1, keepdims=True)
    g_top = _lane_max(pg)
    g_idx = _lane_min(jnp.where(gm & (pg == g_top), lane, big))
    e0 = N_GROUPS + g_idx * EXPERTS_PER_GROUP
    em = (lane >= e0) & (lane < e0 + EXPERTS_PER_GROUP)
    el = jnp.where(em, logits, -jnp.inf)
    ee = jnp.where(em, jnp.exp(el - _lane_max(el)), 0.0)
    pe = ee / jnp.sum(ee, axis=-1, keepdims=True)
    t1 = _lane_max(jnp.where(em, pe, -1.0))
    i1 = _lane_min(jnp.where(em & (pe == t1), lane, big))
    em2 = em & (lane != i1)
    t2 = _lane_max(jnp.where(em2, pe, -1.0))
    i2 = _lane_min(jnp.where(em2 & (pe == t2), lane, big))
    tsum = t1 + t2
    w1 = g_top * t1 / tsum
    w2 = g_top * t2 / tsum
    e1 = i1 - N_GROUPS
    e2 = i2 - N_GROUPS
    oh1 = jnp.where(lane == e1, 1.0, 0.0)
    oh2 = jnp.where(lane == e2, 1.0, 0.0)
    both = oh1 + oh2
    before = _dot(tri_ref[...], both.astype(BF16)) + cnt_scr[...]
    r1 = jnp.sum(before * oh1, axis=-1, keepdims=True)
    r2 = jnp.sum(before * oh2, axis=-1, keepdims=True)
    cnt_scr[...] = cnt_scr[...] + jnp.sum(both, axis=0, keepdims=True)
    cnt_ref[...] = jnp.broadcast_to(cnt_scr[...], cnt_ref.shape)

    cols = (e1.astype(F32), e2.astype(F32), w1, w2, r1, r2)
    rt = jnp.zeros((tm, LANES), F32)
    for c, v in enumerate(cols):
        rt = jnp.where(lane == c, v, rt)
    rt_ref[...] = rt


def _out_proj(o_nsa, o_ret, w_out, xt, g2, w_router, b_router, tm):
    T, D = xt.shape
    half = o_nsa.shape[1]
    tri = jnp.asarray(np.tril(np.ones((tm, tm), np.float32), -1), BF16)
    return pl.pallas_call(
        _out_kernel,
        grid=(T // tm,),
        in_specs=[
            pl.BlockSpec((tm, half), lambda m: (m, 0)),
            pl.BlockSpec((tm, half), lambda m: (m, 0)),
            pl.BlockSpec((2 * half, D), lambda m: (0, 0), pipeline_mode=pl.Buffered(1)),
            pl.BlockSpec((tm, D), lambda m: (m, 0)),
            pl.BlockSpec((1, D), lambda m: (0, 0)),
            pl.BlockSpec((D, LANES), lambda m: (0, 0)),
            pl.BlockSpec((1, LANES), lambda m: (0, 0)),
            pl.BlockSpec((tm, tm), lambda m: (0, 0)),
        ],
        out_specs=[
            pl.BlockSpec((tm, D), lambda m: (m, 0)),
            pl.BlockSpec((tm, D), lambda m: (m, 0)),
            pl.BlockSpec((tm, LANES), lambda m: (m, 0)),
            pl.BlockSpec((8, LANES), lambda m: (0, 0)),
        ],
        out_shape=[
            jax.ShapeDtypeStruct((T, D), F32),
            jax.ShapeDtypeStruct((T, D), F32),
            jax.ShapeDtypeStruct((T, LANES), F32),
            jax.ShapeDtypeStruct((8, LANES), F32),
        ],
        scratch_shapes=[pltpu.VMEM((1, LANES), F32)],
        compiler_params=_params(("arbitrary",)),
        name="out_proj",
    )(o_nsa, o_ret, w_out, xt, g2, w_router, b_router, tri)


def _plan_kernel(pos_ref, tok_ref):
    def clear(r, carry):
        tok_ref[r] = 0
        return carry

    lax.fori_loop(0, tok_ref.shape[0], clear, 0, unroll=8)

    def place(a, carry):
        tok_ref[pos_ref[a]] = lax.shift_right_logical(a, 1)
        return carry

    lax.fori_loop(0, pos_ref.shape[0], place, 0, unroll=8)


def _plan(pos, n_rows):
    return pl.pallas_call(
        _plan_kernel,
        in_specs=[pl.BlockSpec(memory_space=pltpu.SMEM)],
        out_specs=pl.BlockSpec(memory_space=pltpu.SMEM),
        out_shape=jax.ShapeDtypeStruct((n_rows,), jnp.int32),
        name="plan",
    )(pos)


def _row_copy(src_hbm, row, dst, slot, sem):
    return pltpu.make_async_copy(src_hbm.at[pl.ds(row, 1), :], dst.at[pl.ds(slot, 1), :], sem)


def _expert_kernel(ts_ref, tok_ref, h2_hbm, wg_ref, wu_ref, wd_ref, y_hbm,
                   xb0, xb1, yb0, yb1, wg_b, wu_b, wd_b, gsem, osem, *, n_blocks):
    e = pl.program_id(0)
    M = MOE_BLOCK
    t0 = ts_ref[e]
    t1 = ts_ref[e + 1]
    n_used = ts_ref[N_EXPERTS]
    xbufs = (xb0, xb1)
    ybufs = (yb0, yb1)

    def gather_wait(p):
        pltpu.make_async_copy(h2_hbm.at[pl.ds(0, M), :], xbufs[p], gsem.at[p]).wait()

    def out_copy(t, p):
        return pltpu.make_async_copy(ybufs[p], y_hbm.at[pl.ds(pl.multiple_of(t * M, M), M), :], osem.at[p])

    @pl.when(e == 0)
    def _():
        def issue(r, carry):
            _row_copy(h2_hbm, tok_ref[r], xb0, r, gsem.at[0]).start(priority=1)
            return carry

        lax.fori_loop(0, M, issue, 0, unroll=8)

    @pl.when(t1 > t0)
    def _():
        wg_b[...] = wg_ref[0].astype(BF16)
        wu_b[...] = wu_ref[0].astype(BF16)
        wd_b[...] = wd_ref[0].astype(BF16)

    def tile_body(t, p):
        gather_wait(p)

        @pl.when(t >= 2)
        def _():
            out_copy(t - 2, p).wait()

        nbase = (t + 1) * M
        chain = 0
        for r in range(M):
            tok = tok_ref[nbase + r + chain]
            _row_copy(h2_hbm, tok, xbufs[1 - p], r, gsem.at[1 - p]).start(priority=1)
            chain = lax.shift_right_logical(tok, 31)
        xb = xbufs[p][...].astype(BF16)
        hg = _dot(xb, wg_b[...])
        hu = _dot(xb, wu_b[...])
        hb = (hg * jax.nn.sigmoid(hg) * hu).astype(BF16)
        ybufs[p][...] = _dot(hb, wd_b[...])
        out_copy(t, p).start()

    def tile(t, carry):
        parity = lax.rem(t, 2)
        for p in range(2):
            pl.when(parity == p)(functools.partial(tile_body, t, p))
        return carry

    lax.fori_loop(t0, t1, tile, 0)

    @pl.when(e == N_EXPERTS - 1)
    def _():
        parity = lax.rem(n_used, 2)
        for p in range(2):
            @pl.when(parity == p)
            def _(p=p):
                gather_wait(p)
                out_copy(n_used - 1, 1 - p).wait()

                @pl.when(n_used >= 2)
                def _():
                    out_copy(n_used - 2, p).wait()

        yb0[...] = jnp.zeros(yb0.shape, F32)

        def clear(t, carry):
            out_copy(t, 0).start()
            return carry

        lax.fori_loop(n_used, n_blocks, clear, 0)

        def clear_wait(t, carry):
            out_copy(t, 0).wait()
            return carry

        lax.fori_loop(n_used, n_blocks, clear_wait, 0)


def _experts(h2, tile_start, row_tok, w_gate, w_up, w_down):
    T, D = h2.shape
    n_rows = row_tok.shape[0]
    n_blocks = n_rows // MOE_BLOCK
    assert (TOPK_IN_GROUP * T + N_EXPERTS * (MOE_BLOCK - 1)) // MOE_BLOCK < n_blocks
    grid_spec = pltpu.PrefetchScalarGridSpec(
        num_scalar_prefetch=2,
        grid=(N_EXPERTS,),
        in_specs=[
            pl.BlockSpec(memory_space=pl.ANY),
            pl.BlockSpec((1, D, D_EXPERT), lambda e, ts, tok: (e, 0, 0)),
            pl.BlockSpec((1, D, D_EXPERT), lambda e, ts, tok: (e, 0, 0)),
            pl.BlockSpec((1, D_EXPERT, D), lambda e, ts, tok: (e, 0, 0)),
        ],
        out_specs=pl.BlockSpec(memory_space=pl.ANY),
        scratch_shapes=[pltpu.VMEM((MOE_BLOCK, D), F32)] * 4 + [
            pltpu.VMEM((D, D_EXPERT), BF16), pltpu.VMEM((D, D_EXPERT), BF16),
            pltpu.VMEM((D_EXPERT, D), BF16),
            pltpu.SemaphoreType.DMA((2,)), pltpu.SemaphoreType.DMA((2,)),
        ],
    )
    return pl.pallas_call(
        functools.partial(_expert_kernel, n_blocks=n_blocks),
        grid_spec=grid_spec,
        out_shape=jax.ShapeDtypeStruct((n_rows, D), F32),
        compiler_params=_params(("arbitrary",)),
        name="experts",
    )(tile_start, row_tok, h2, w_gate, w_up, w_down)


def _combine_kernel(pos_ref, y_hbm, x1_ref, rt_ref, o_ref, ybuf, sems):
    i = pl.program_id(0)
    n = pl.num_programs(0)
    tm = x1_ref.shape[0]
    K = TOPK_IN_GROUP
    slot = lax.rem(i, 2)

    def issue_tile(t, s):
        def issue(r, carry):
            a = (t * tm + r) * K
            for k in range(K):
                _row_copy(y_hbm, pos_ref[a + k], ybuf.at[s, k], r, sems.at[s]).start(priority=k)
            return carry

        lax.fori_loop(0, tm, issue, 0, unroll=8)

    @pl.when(i == 0)
    def _():
        issue_tile(0, 0)

    @pl.when(i + 1 < n)
    def _():
        issue_tile(i + 1, 1 - slot)

    for k in range(K):
        pltpu.make_async_copy(y_hbm.at[pl.ds(0, tm), :], ybuf.at[slot, k], sems.at[slot]).wait()
    w = rt_ref[...]
    yb = ybuf[slot]
    o_ref[...] = x1_ref[...] + (yb[0] * w[:, K:K + 1] + yb[1] * w[:, K + 1:K + 2])


def _combine(pos, y_rows, x1, rt, tm):
    T, D = x1.shape
    grid_spec = pltpu.PrefetchScalarGridSpec(
        num_scalar_prefetch=1,
        grid=(T // tm,),
        in_specs=[
            pl.BlockSpec(memory_space=pl.ANY),
            pl.BlockSpec((tm, D), lambda i, pos: (i, 0)),
            pl.BlockSpec((tm, LANES), lambda i, pos: (i, 0)),
        ],
        out_specs=pl.BlockSpec((tm, D), lambda i, pos: (i, 0)),
        scratch_shapes=[pltpu.VMEM((2, TOPK_IN_GROUP, tm, D), F32), pltpu.SemaphoreType.DMA((2,))],
    )
    return pl.pallas_call(
        _combine_kernel,
        grid_spec=grid_spec,
        out_shape=jax.ShapeDtypeStruct((T, D), F32),
        compiler_params=_params(("arbitrary",)),
        name="combine",
    )(pos, y_rows, x1, rt)


def _block_layout(rt, counts, T):
    K = TOPK_IN_GROUP
    n_rows = (K * T + N_EXPERTS * (MOE_BLOCK - 1) + MOE_BLOCK - 1) // MOE_BLOCK * MOE_BLOCK
    n_blocks = n_rows // MOE_BLOCK
    cnt = counts[0, :N_EXPERTS].astype(jnp.int32)
    padded = (cnt + MOE_BLOCK - 1) // MOE_BLOCK * MOE_BLOCK
    pad_end = jnp.cumsum(padded)
    pad_start = pad_end - padded
    e = rt[:, 0:K].astype(jnp.int32)
    rank = rt[:, 2 * K:3 * K].astype(jnp.int32)
    pos = (pad_start[e] + rank).reshape(-1)
    tile_start = (jnp.concatenate([pad_start, pad_end[-1:]]) // MOE_BLOCK).astype(jnp.int32)
    return tile_start, pos, n_rows


def _layer(x, norm1_g, w_in, cmp_pos, cmp_w1, cmp_w2, q_norm_g, k_norm_g, ret_gn_g, ret_gn_b,
           w_out, norm2_g, w_rg, b_rg, w_re, b_re, w_eg, w_eu, w_ed, tiles):
    B, S, D = x.shape
    T = B * S
    xt = x.reshape(T, D)

    w_main = jnp.concatenate([w_in[:, GATE_COL0 + NSA_GATE_WIDTH:], w_in[:, :GATE_COL0]], axis=1).astype(BF16)
    gpg = NSA_GATE_WIDTH // NSA_KV_GROUPS
    w_gate = jnp.concatenate(
        [jnp.pad(w_in[:, GATE_COL0 + g * gpg:GATE_COL0 + (g + 1) * gpg], ((0, 0), (0, LANES - gpg)))
         for g in range(NSA_KV_GROUPS)], axis=1).astype(BF16)
    proj, gate = _in_proj(xt, norm1_g.reshape(1, D), w_main, w_gate, tiles["tm_in"], tiles["tn_in"])

    cmp_kv = _compress(proj, cmp_pos, cmp_w1, cmp_w2, k_norm_g, B, S)
    o_nsa = _nsa(proj, gate, cmp_kv, q_norm_g.reshape(1, HEAD_DIM), k_norm_g, B, S, tiles["tq"], tiles["tk"])

    half = HEAD_DIM // 2
    inv_freq = ROPE_BASE ** (-jnp.arange(half, dtype=F32) / half)
    ang = jnp.arange(S, dtype=F32)[:, None] * inv_freq[None, :]
    cos2 = jnp.concatenate([jnp.cos(ang), jnp.cos(ang)], axis=1)
    sin2 = jnp.concatenate([-jnp.sin(ang), jnp.sin(ang)], axis=1)
    log_g = jnp.log1p(-jnp.exp2(-5.0 - jnp.arange(RET_HEADS, dtype=F32)))
    o_ret = _retention(proj, log_g, cos2, sin2, ret_gn_g, ret_gn_b, B, S)

    n_r = N_GROUPS + N_EXPERTS
    w_router = jnp.pad(jnp.concatenate([w_rg, w_re], axis=1), ((0, 0), (0, LANES - n_r)))
    b_router = jnp.pad(jnp.concatenate([b_rg, b_re]), (0, LANES - n_r)).reshape(1, LANES)
    x1, h2, rt, counts = _out_proj(o_nsa, o_ret, w_out.astype(BF16), xt, norm2_g.reshape(1, D),
                                   w_router, b_router, tiles["tm_out"])

    tile_start, pos, n_rows = _block_layout(rt, counts, T)
    row_tok = _plan(pos, n_rows)
    y_rows = _experts(h2, tile_start, row_tok, w_eg, w_eu, w_ed)
    out = _combine(pos, y_rows, x1, rt, tiles["tm_cmb"])
    return out.reshape(B, S, D)


def _tiles(T, S):
    return {
        "tm_in": min(1024, T), "tn_in": 512,
        "tq": min(128, S), "tk": min(512, S),
        "tm_out": min(512, T),
        "tm_cmb": min(256, T),
    }


def kernel(x, norm1_g, w_in, cmp_pos, cmp_w1, cmp_w2, q_norm_g, k_norm_g, ret_gn_g, ret_gn_b, w_out, norm2_g, w_router_group, b_router_group, w_router_expert, b_router_expert, w_exp_gate, w_exp_up, w_exp_down):
    B, S, _ = x.shape
    tiles = _tiles(B * S, S)
    for l in range(norm1_g.shape[0]):
        x = _layer(x, norm1_g[l], w_in[l], cmp_pos[l], cmp_w1[l], cmp_w2[l], q_norm_g[l], k_norm_g[l],
                   ret_gn_g[l], ret_gn_b[l], w_out[l], norm2_g[l], w_router_group[l], b_router_group[l],
                   w_router_expert[l], b_router_expert[l], w_exp_gate[l], w_exp_up[l], w_exp_down[l], tiles)
    return x
```

```python
import functools

import numpy as np
import jax
import jax.numpy as jnp
from jax import lax
from jax.experimental import pallas as pl
from jax.experimental.pallas import tpu as pltpu

F32 = jnp.float32
BF16 = jnp.bfloat16

D_MODEL = 2048
NSA_HEADS = 8
NSA_KV_GROUPS = 2
NSA_GROUP_HEADS = NSA_HEADS // NSA_KV_GROUPS
HEAD_DIM = 128
RET_HEADS = 8
CMP_BLOCK = 32
CMP_STRIDE = 16
SEL_BLOCK = 64
SEL_TOPK = 8
WINDOW = 512
RET_CHUNK = 128
ROPE_BASE = 10000.0
N_GROUPS = 4
EXPERTS_PER_GROUP = 8
N_EXPERTS = N_GROUPS * EXPERTS_PER_GROUP
TOPK_IN_GROUP = 2
D_EXPERT = 512
MOE_BLOCK = 256
RMS_EPS = 1e-6
GN_EPS = 1e-5
NEG_INF = -1e30
FORCED_SCORE = 1e6
LOG2E = 1.4426950408889634

NSA_Q_WIDTH = NSA_HEADS * HEAD_DIM
NSA_KV_WIDTH = NSA_KV_GROUPS * HEAD_DIM
NSA_GATE_WIDTH = 3 * NSA_HEADS
RET_WIDTH = RET_HEADS * HEAD_DIM
GATE_COL0 = NSA_Q_WIDTH + 6 * NSA_KV_WIDTH
LANES = 128
VMEM_LIMIT = 56 * 1024 * 1024

Q_COL0 = 4 * RET_WIDTH
KV_COL0 = Q_COL0 + NSA_Q_WIDTH
KC_BLK, VC_BLK, KS_BLK, VS_BLK, KW_BLK, VW_BLK = [KV_COL0 // LANES + 2 * t for t in range(6)]


def _rms(xf, g):
    return xf * lax.rsqrt(jnp.mean(xf * xf, axis=-1, keepdims=True) + RMS_EPS) * g


def _dot(a, b):
    return jnp.dot(a, b, preferred_element_type=F32)


def _dot_nt(a, b):
    return lax.dot_general(a, b, (((1,), (1,)), ((), ())), preferred_element_type=F32)


def _dot_tn(a, b):
    return lax.dot_general(a, b, (((0,), (0,)), ((), ())), preferred_element_type=F32)


def _split3(p):
    p1 = p.astype(BF16)
    r1 = p - p1.astype(F32)
    p2 = r1.astype(BF16)
    p3 = (r1 - p2.astype(F32)).astype(BF16)
    return p1, p2, p3


def _params(sem):
    return pltpu.CompilerParams(dimension_semantics=sem, vmem_limit_bytes=VMEM_LIMIT)


def _in_proj_kernel(x_ref, g_ref, w_ref, wg_ref, o_ref, og_ref, h_scr):
    @pl.when(pl.program_id(1) == 0)
    def _():
        h = _rms(x_ref[...], g_ref[...]).astype(BF16)
        h_scr[...] = h
        og_ref[...] = _dot(h, wg_ref[...])

    o_ref[...] = _dot(h_scr[...], w_ref[...])


def _in_proj(xt, g1, w_main, w_gate, tm, tn):
    T, D = xt.shape
    n_main = w_main.shape[1]
    n_gate = w_gate.shape[1]
    return pl.pallas_call(
        _in_proj_kernel,
        grid=(T // tm, n_main // tn),
        in_specs=[
            pl.BlockSpec((tm, D), lambda m, n: (m, 0)),
            pl.BlockSpec((1, D), lambda m, n: (0, 0)),
            pl.BlockSpec((D, tn), lambda m, n: (0, n)),
            pl.BlockSpec((D, n_gate), lambda m, n: (0, 0)),
        ],
        out_specs=[
            pl.BlockSpec((tm, tn), lambda m, n: (m, n)),
            pl.BlockSpec((tm, n_gate), lambda m, n: (m, 0)),
        ],
        out_shape=[
            jax.ShapeDtypeStruct((T, n_main), F32),
            jax.ShapeDtypeStruct((T, n_gate), F32),
        ],
        scratch_shapes=[pltpu.VMEM((tm, D), BF16)],
        compiler_params=_params(("parallel", "arbitrary")),
        name="in_proj",
    )(xt, g1, w_main, w_gate)


def _compress_kernel(x_ref, pos_ref, w1_ref, w2_ref, kg_ref, o_ref, xs_scr, *, S, n_pad):
    kv = pl.program_id(1)
    xs_scr[0:S, :] = x_ref[...]
    xs_scr[S:S + CMP_STRIDE, :] = jnp.zeros((CMP_STRIDE, HEAD_DIM), F32)
    acc = jnp.zeros((n_pad, HEAD_DIM), F32)
    for l in range(CMP_BLOCK):
        tb = xs_scr[pl.ds(l, n_pad, stride=CMP_STRIDE), :] + pos_ref[0, l:l + 1, :]
        acc = acc + _dot(tb.astype(BF16), w1_ref[0, l].astype(BF16))
    hid = acc * jax.nn.sigmoid(acc)
    out = _dot(hid.astype(BF16), w2_ref[0].astype(BF16))
    normed = _rms(out, kg_ref[0:1, :])
    o_ref[0, 0, 0] = jnp.where(kv == 0, normed, out).astype(BF16)


def _compress(proj, cmp_pos, cmp_w1, cmp_w2, k_norm_g, B, S):
    n_pad = S // CMP_STRIDE
    G = NSA_KV_GROUPS
    kern = functools.partial(_compress_kernel, S=S, n_pad=n_pad)
    return pl.pallas_call(
        kern,
        grid=(B, 2, G),
        in_specs=[
            pl.BlockSpec((S, HEAD_DIM), lambda b, kv, g: (b, KC_BLK + 2 * kv + g)),
            pl.BlockSpec((1, CMP_BLOCK, HEAD_DIM), lambda b, kv, g: (kv, 0, 0)),
            pl.BlockSpec((1, CMP_BLOCK, HEAD_DIM, HEAD_DIM), lambda b, kv, g: (kv, 0, 0, 0)),
            pl.BlockSpec((1, HEAD_DIM, HEAD_DIM), lambda b, kv, g: (kv, 0, 0)),
            pl.BlockSpec((3, HEAD_DIM), lambda b, kv, g: (0, 0)),
        ],
        out_specs=pl.BlockSpec((1, 1, 1, n_pad, HEAD_DIM), lambda b, kv, g: (b, kv, g, 0, 0)),
        out_shape=jax.ShapeDtypeStruct((B, 2, G, n_pad, HEAD_DIM), BF16),
        scratch_shapes=[pltpu.VMEM((S + CMP_STRIDE, HEAD_DIM), F32)],
        compiler_params=_params(("parallel", "parallel", "parallel")),
        name="compress",
    )(proj, cmp_pos, cmp_w1, cmp_w2, k_norm_g)


def _masked_softmax(s, msk):
    sm = jnp.where(msk, s, NEG_INF)
    m = jnp.max(sm, axis=-1, keepdims=True)
    e = jnp.where(msk, jnp.exp(sm - m), 0.0)
    den = jnp.sum(e, axis=-1, keepdims=True)
    return e / jnp.where(den > 0.0, den, 1.0)


def _nsa_kernel(q_ref, kc_ref, vc_ref, ks_ref, vs_ref, kw_ref, vw_ref, gate_ref, qg_ref, kg_ref,
                ovt_ref, ext_ref, o_ref, ksa, vsa, kwn, vwa, l_scr, acc_scr, *, S, tq, tk, wk):
    i = pl.program_id(2)
    H = NSA_GROUP_HEADS
    scale = HEAD_DIM ** -0.5
    c2 = scale * LOG2E
    n_cp = S // CMP_STRIDE
    n_sblk = S // SEL_BLOCK

    @pl.when(i == 0)
    def _():
        ksa[:, 0:HEAD_DIM] = _rms(ks_ref[...], kg_ref[1:2, :]).astype(BF16)
        ksa[:, HEAD_DIM:2 * HEAD_DIM] = ext_ref[...]
        kwn[...] = _rms(kw_ref[...], kg_ref[2:3, :]).astype(BF16)
        ones = jnp.ones((S, HEAD_DIM), BF16)
        vsa[:, 0:HEAD_DIM] = vs_ref[...].astype(BF16)
        vsa[:, HEAD_DIM:2 * HEAD_DIM] = ones
        vwa[:, 0:HEAD_DIM] = vw_ref[...].astype(BF16)
        vwa[:, HEAD_DIM:2 * HEAD_DIM] = ones

    pos = i * tq + lax.broadcasted_iota(jnp.int32, (tq, 1), 0)
    qg = qg_ref[...]
    q4 = jnp.concatenate([_rms(q_ref[:, h * HEAD_DIM:(h + 1) * HEAD_DIM], qg).astype(BF16)
                          for h in range(H)], axis=0)
    rows = [slice(h * tq, (h + 1) * tq) for h in range(H)]

    def softmax_pv(q_pairs, k, v, tail_bias):
        head = k.shape[0] - tail_bias.shape[1]
        outs, sums = [], []
        for q2 in q_pairs:
            s2 = _dot_nt(q2, k)
            es = []
            for hh in range(2):
                t = s2[hh * tq:(hh + 1) * tq] * c2
                tb = t[:, head:] + tail_bias
                m = jnp.max(tb, axis=-1, keepdims=True)
                if head:
                    ta = t[:, :head]
                    m = jnp.maximum(m, jnp.max(ta, axis=-1, keepdims=True))
                    es.append(jnp.concatenate([jnp.exp2(ta - m).astype(BF16),
                                               jnp.exp2(tb - m).astype(BF16)], axis=1))
                else:
                    es.append(jnp.exp2(tb - m).astype(BF16))
            o2 = _dot(jnp.concatenate(es, axis=0), v)
            outs += [o2[0:tq, 0:HEAD_DIM], o2[tq:2 * tq, 0:HEAD_DIM]]
            sums += [o2[0:tq, HEAD_DIM:2 * HEAD_DIM], o2[tq:2 * tq, HEAD_DIM:2 * HEAD_DIM]]
        return outs, sums

    ncol = lax.broadcasted_iota(jnp.int32, (1, n_cp), 1)
    cmask = (ncol * CMP_STRIDE + (CMP_BLOCK - 1)) <= pos
    s_c4 = _dot_nt(q4, kc_ref[0, 0, 0]) * scale
    ps = [_masked_softmax(s_c4[r], cmask) for r in rows]
    psum = ps[0] + ps[1] + ps[2] + ps[3]
    o_c4 = _dot(jnp.concatenate([p.astype(BF16) for p in ps], axis=0), vc_ref[0, 0, 0])

    ovt = ovt_ref[...]
    p1, p2, p3 = _split3(psum)
    imp_t = (_dot_nt(ovt, p1) + _dot_nt(ovt, p2) + _dot_nt(ovt, p3))[0:n_sblk, :]
    pos_t = i * tq + lax.broadcasted_iota(jnp.int32, (1, tq), 1)
    jrow = lax.broadcasted_iota(jnp.int32, (n_sblk, 1), 0)
    cur_t = pos_t // SEL_BLOCK
    valid_t = jrow * SEL_BLOCK <= pos_t
    forced_t = (jrow == 0) | (jrow == cur_t) | (jrow == cur_t - 1)
    score = jnp.where(valid_t, jnp.where(forced_t, FORCED_SCORE, imp_t), -jnp.inf)
    rank = jnp.zeros((n_sblk, tq), jnp.int32)
    for i2 in range(n_sblk):
        si = score[i2:i2 + 1, :]
        ahead = (si > score) | ((si == score) & (jrow > i2))
        rank = rank + jnp.where(ahead, 1, 0)
    drop_t = jnp.where((rank < SEL_TOPK) & valid_t, 0.0, NEG_INF)
    drop_t = jnp.concatenate([drop_t, jnp.full((LANES - n_sblk, tq), NEG_INF, F32)], axis=0)
    drop = drop_t.T.astype(BF16)
    drop2 = jnp.concatenate([drop, drop], axis=0)
    qa_pairs = [jnp.concatenate([q4[0:2 * tq], drop2], axis=1),
                jnp.concatenate([q4[2 * tq:4 * tq], drop2], axis=1)]

    kst = pl.multiple_of(jnp.clip(i * tq - WINDOW, 0, S - wk), tq)
    wpos = kst + lax.broadcasted_iota(jnp.int32, (1, wk), 1)
    wbias = jnp.where((wpos <= pos) & (wpos > pos - WINDOW), 0.0, NEG_INF)
    o_ws, lws = softmax_pv([q4[0:2 * tq], q4[2 * tq:4 * tq]],
                           kwn[pl.ds(kst, wk), :], vwa[pl.ds(kst, wk), :], wbias)

    def slc_prefix(n):
        kpos = (n - tk) + lax.broadcasted_iota(jnp.int32, (1, tk), 1)
        causal = jnp.where(kpos <= pos, 0.0, NEG_INF)
        outs, sums = softmax_pv(qa_pairs, ksa[0:n, :], vsa[0:n, :], causal)
        for r, o, l in zip(rows, outs, sums):
            acc_scr[r] = o
            l_scr[r] = l

    n_cls = S // tk
    cls = ((i + 1) * tq - 1) // tk
    for c in range(n_cls):
        pl.when(cls == c)(functools.partial(slc_prefix, (c + 1) * tk))

    gates = jax.nn.sigmoid(gate_ref[...])
    for h, r in enumerate(rows):
        c = 3 * h
        o = (o_c4[r] * gates[:, c:c + 1]
             + acc_scr[r] * (gates[:, c + 1:c + 2] / l_scr[r])
             + o_ws[h] * (gates[:, c + 2:c + 3] / lws[h]))
        o_ref[:, h * HEAD_DIM:(h + 1) * HEAD_DIM] = o.astype(BF16)


def _nsa_tables(S):
    n_cp, n_cmp, n_sblk = S // CMP_STRIDE, (S - CMP_BLOCK) // CMP_STRIDE + 1, S // SEL_BLOCK
    n = np.arange(n_cp)[None, :] * CMP_STRIDE
    j = np.arange(LANES)[:, None]
    ovt = ((n < (j + 1) * SEL_BLOCK) & (n + CMP_BLOCK > j * SEL_BLOCK)
           & (np.arange(n_cp)[None, :] < n_cmp) & (j < n_sblk))
    key_block = (np.arange(S) // SEL_BLOCK)[:, None] == np.arange(LANES)[None, :]
    return jnp.asarray(ovt, BF16), jnp.asarray(key_block, BF16)


def _nsa(proj, gate, cmp_kv, q_norm_g, k_norm_g, B, S, tq, tk):
    G = NSA_KV_GROUPS
    H = NSA_GROUP_HEADS
    n_cp = S // CMP_STRIDE
    wk = min(S, WINDOW + tq)
    nq = S // tq
    gw = H * HEAD_DIM
    ovt, key_block = _nsa_tables(S)
    kern = functools.partial(_nsa_kernel, S=S, tq=tq, tk=tk, wk=wk)
    kv_spec = lambda blk: pl.BlockSpec((S, HEAD_DIM), lambda b, g, i: (b, blk + g))
    return pl.pallas_call(
        kern,
        grid=(B, G, nq),
        in_specs=[
            pl.BlockSpec((tq, gw), lambda b, g, i: (b * nq + i, Q_COL0 // gw + g)),
            pl.BlockSpec((1, 1, 1, n_cp, HEAD_DIM), lambda b, g, i: (b, 0, g, 0, 0)),
            pl.BlockSpec((1, 1, 1, n_cp, HEAD_DIM), lambda b, g, i: (b, 1, g, 0, 0)),
            kv_spec(KS_BLK), kv_spec(VS_BLK), kv_spec(KW_BLK), kv_spec(VW_BLK),
            pl.BlockSpec((tq, LANES), lambda b, g, i: (b * nq + i, g)),
            pl.BlockSpec((1, HEAD_DIM), lambda b, g, i: (0, 0)),
            pl.BlockSpec((3, HEAD_DIM), lambda b, g, i: (0, 0)),
            pl.BlockSpec((LANES, n_cp), lambda b, g, i: (0, 0)),
            pl.BlockSpec((S, LANES), lambda b, g, i: (0, 0)),
        ],
        out_specs=pl.BlockSpec((tq, gw), lambda b, g, i: (b * nq + i, g)),
        out_shape=jax.ShapeDtypeStruct((B * S, NSA_Q_WIDTH), BF16),
        scratch_shapes=[
            pltpu.VMEM((S, 2 * HEAD_DIM), BF16), pltpu.VMEM((S, 2 * HEAD_DIM), BF16),
            pltpu.VMEM((S, HEAD_DIM), BF16), pltpu.VMEM((S, 2 * HEAD_DIM), BF16),
            pltpu.VMEM((H * tq, HEAD_DIM), F32), pltpu.VMEM((H * tq, HEAD_DIM), F32)],
        compiler_params=_params(("parallel", "parallel", "arbitrary")),
        name="nsa",
    )(proj, cmp_kv, cmp_kv, proj, proj, proj, proj, gate, q_norm_g, k_norm_g, ovt, key_block)


RET_HB = 8


def _ret_kernel(lg_ref, rq_ref, rk_ref, rv_ref, rg_ref, cos_ref, sin_ref, gg_ref, gb_ref,
                o_ref, r_scr):
    hg = pl.program_id(1)
    C = RET_CHUNK
    scale = HEAD_DIM ** -0.5

    @pl.when(pl.program_id(2) == 0)
    def _():
        r_scr[...] = jnp.zeros(r_scr.shape, F32)

    cos = cos_ref[...]
    sin = sin_ref[...]
    n_col = lax.broadcasted_iota(jnp.int32, (C, 1), 0).astype(F32)
    n_row = lax.broadcasted_iota(jnp.int32, (1, C), 1).astype(F32)
    diff = n_col - n_row
    for h in range(RET_HB):
        sl = slice(h * HEAD_DIM, (h + 1) * HEAD_DIM)
        hh = hg * RET_HB + h
        lg = lg_ref[hh]
        q = rq_ref[:, sl]
        k = rk_ref[:, sl]
        qf = q * cos + pltpu.roll(q, HEAD_DIM // 2, 1) * sin
        kf = (k * cos + pltpu.roll(k, HEAD_DIM // 2, 1) * sin) * scale
        v = rv_ref[:, sl].astype(BF16)
        dec = jnp.where(diff >= 0.0, jnp.exp(lg * jnp.maximum(diff, 0.0)), 0.0)
        xi = jnp.exp(lg * (n_col + 1.0))
        zeta = jnp.exp(lg * (C - 1.0 - n_col))
        cd = jnp.exp(jnp.full((1, HEAD_DIM), lg * float(C), F32))
        qb = qf.astype(BF16)
        r_old = r_scr[h]
        a = _dot_nt(qb, kf.astype(BF16)) * dec
        o = _dot(a.astype(BF16), v) + _dot(qb, r_old.astype(BF16)) * xi
        r_scr[h] = r_old * cd + _dot_tn((kf * zeta).astype(BF16), v)
        mu = jnp.mean(o, axis=-1, keepdims=True)
        d = o - mu
        var = jnp.mean(d * d, axis=-1, keepdims=True)
        y = d * lax.rsqrt(var + GN_EPS) * gg_ref[pl.ds(hh, 1), :] + gb_ref[pl.ds(hh, 1), :]
        gt = rg_ref[:, sl]
        o_ref[:, sl] = (gt * jax.nn.sigmoid(gt) * y).astype(BF16)


def _retention(proj, log_g, cos2, sin2, gn_g, gn_b, B, S):
    C = RET_CHUNK
    n_ch = S // C
    bw = RET_HB * HEAD_DIM
    nhb = RET_HEADS // RET_HB
    spec = lambda k: pl.BlockSpec((C, bw), lambda b, hg, c, lg: (b * n_ch + c, k * nhb + hg))
    grid_spec = pltpu.PrefetchScalarGridSpec(
        num_scalar_prefetch=1,
        grid=(B, nhb, n_ch),
        in_specs=[
            spec(0), spec(1), spec(2), spec(3),
            pl.BlockSpec((C, HEAD_DIM), lambda b, hg, c, lg: (c, 0)),
            pl.BlockSpec((C, HEAD_DIM), lambda b, hg, c, lg: (c, 0)),
            pl.BlockSpec((RET_HEADS, HEAD_DIM), lambda b, hg, c, lg: (0, 0)),
            pl.BlockSpec((RET_HEADS, HEAD_DIM), lambda b, hg, c, lg: (0, 0)),
        ],
        out_specs=pl.BlockSpec((C, bw), lambda b, hg, c, lg: (b * n_ch + c, hg)),
        scratch_shapes=[pltpu.VMEM((RET_HB, HEAD_DIM, HEAD_DIM), F32)],
    )
    return pl.pallas_call(
        _ret_kernel,
        grid_spec=grid_spec,
        out_shape=jax.ShapeDtypeStruct((B * S, RET_WIDTH), BF16),
        compiler_params=_params(("parallel", "parallel", "arbitrary")),
        name="retention",
    )(log_g, proj, proj, proj, proj, cos2, sin2, gn_g, gn_b)


def _lane_max(v):
    return jnp.max(v, axis=-1, keepdims=True)


def _lane_min(v):
    return jnp.min(v, axis=-1, keepdims=True)


def _out_kernel(on_ref, or_ref, w_ref, x_ref, g2_ref, wr_ref, br_ref, tri_ref,
                x1_ref, h2_ref, rt_ref, cnt_ref, cnt_scr):
    @pl.when(pl.program_id(0) == 0)
    def _():
        cnt_scr[...] = jnp.zeros(cnt_scr.shape, F32)

    half = on_ref.shape[1]
    acc = _dot(on_ref[...], w_ref[0:half, :]) + _dot(or_ref[...], w_ref[half:2 * half, :])
    x1 = x_ref[...] + acc
    x1_ref[...] = x1
    h2 = _rms(x1, g2_ref[...])
    h2_ref[...] = h2

    h_hi = h2.astype(BF16)
    h_lo = (h2 - h_hi.astype(F32)).astype(BF16)
    wr = wr_ref[...]
    w_hi = wr.astype(BF16)
    w_lo = (wr - w_hi.astype(F32)).astype(BF16)
    logits = _dot(h_hi, w_hi) + _dot(h_lo, w_hi) + _dot(h_hi, w_lo) + br_ref[...]

    tm = logits.shape[0]
    lane = lax.broadcasted_iota(jnp.int32, (tm, LANES), 1)
    big = jnp.int32(LANES)
    gm = lane < N_GROUPS
    gl = jnp.where(gm, logits, -jnp.inf)
    ge = jnp.where(gm, jnp.exp(gl - _lane_max(gl)), 0.0)
    pg = ge / jnp.sum(ge, axis=-1, keepdims=True)
    g_top = _lane_max(pg)
    g_idx = _lane_min(jnp.where(gm & (pg == g_top), lane, big))
    e0 = N_GROUPS + g_idx * EXPERTS_PER_GROUP
    em = (lane >= e0) & (lane < e0 + EXPERTS_PER_GROUP)
    el = jnp.where(em, logits, -jnp.inf)
    ee = jnp.where(em, jnp.exp(el - _lane_max(el)), 0.0)
    pe = ee / jnp.sum(ee, axis=-1, keepdims=True)
    t1 = _lane_max(jnp.where(em, pe, -1.0))
    i1 = _lane_min(jnp.where(em & (pe == t1), lane, big))
    em2 = em & (lane != i1)
    t2 = _lane_max(jnp.where(em2, pe, -1.0))
    i2 = _lane_min(jnp.where(em2 & (pe == t2), lane, big))
    tsum = t1 + t2
    w1 = g_top * t1 / tsum
    w2 = g_top * t2 / tsum
    e1 = i1 - N_GROUPS
    e2 = i2 - N_GROUPS
    oh1 = jnp.where(lane == e1, 1.0, 0.0)
    oh2 = jnp.where(lane == e2, 1.0, 0.0)
    both = oh1 + oh2
    before = _dot(tri_ref[...], both.astype(BF16)) + cnt_scr[...]
    r1 = jnp.sum(before * oh1, axis=-1, keepdims=True)
    r2 = jnp.sum(before * oh2, axis=-1, keepdims=True)
    cnt_scr[...] = cnt_scr[...] + jnp.sum(both, axis=0, keepdims=True)
    cnt_ref[...] = jnp.broadcast_to(cnt_scr[...], cnt_ref.shape)

    cols = (e1.astype(F32), e2.astype(F32), w1, w2, r1, r2)
    rt = jnp.zeros((tm, LANES), F32)
    for c, v in enumerate(cols):
        rt = jnp.where(lane == c, v, rt)
    rt_ref[...] = rt


def _out_proj(o_nsa, o_ret, w_out, xt, g2, w_router, b_router, tm):
    T, D = xt.shape
    half = o_nsa.shape[1]
    tri = jnp.asarray(np.tril(np.ones((tm, tm), np.float32), -1), BF16)
    return pl.pallas_call(
        _out_kernel,
        grid=(T // tm,),
        in_specs=[
            pl.BlockSpec((tm, half), lambda m: (m, 0)),
            pl.BlockSpec((tm, half), lambda m: (m, 0)),
            pl.BlockSpec((2 * half, D), lambda m: (0, 0), pipeline_mode=pl.Buffered(1)),
            pl.BlockSpec((tm, D), lambda m: (m, 0)),
            pl.BlockSpec((1, D), lambda m: (0, 0)),
            pl.BlockSpec((D, LANES), lambda m: (0, 0)),
            pl.BlockSpec((1, LANES), lambda m: (0, 0)),
            pl.BlockSpec((tm, tm), lambda m: (0, 0)),
        ],
        out_specs=[
            pl.BlockSpec((tm, D), lambda m: (m, 0)),
            pl.BlockSpec((tm, D), lambda m: (m, 0)),
            pl.BlockSpec((tm, LANES), lambda m: (m, 0)),
            pl.BlockSpec((8, LANES), lambda m: (0, 0)),
        ],
        out_shape=[
            jax.ShapeDtypeStruct((T, D), F32),
            jax.ShapeDtypeStruct((T, D), F32),
            jax.ShapeDtypeStruct((T, LANES), F32),
            jax.ShapeDtypeStruct((8, LANES), F32),
        ],
        scratch_shapes=[pltpu.VMEM((1, LANES), F32)],
        compiler_params=_params(("arbitrary",)),
        name="out_proj",
    )(o_nsa, o_ret, w_out, xt, g2, w_router, b_router, tri)


def _row_copy(src, row, dst, slot, sem):
    return pltpu.make_async_copy(src.at[pl.ds(row, 1), :], dst.at[pl.ds(slot, 1), :], sem)


def _dispatch_kernel(pos_ref, pad_ref, h2_ref, xs_hbm, zrow, sem, zsem):
    i = pl.program_id(0)
    tm = h2_ref.shape[0]
    K = TOPK_IN_GROUP

    def issue(r, carry):
        a = (i * tm + r) * K
        for k in range(K):
            _row_copy(h2_ref, r, xs_hbm, pos_ref[a + k], sem).start()
        return carry

    lax.fori_loop(0, tm, issue, 0, unroll=8)

    @pl.when(i == pl.num_programs(0) - 1)
    def _():
        zrow[...] = jnp.zeros(zrow.shape, F32)
        for e in range(N_EXPERTS):
            lo, hi = pad_ref[e], pad_ref[N_EXPERTS + e]

            def fill(r, carry):
                _row_copy(zrow, 0, xs_hbm, r, zsem).start()
                return carry

            lax.fori_loop(lo, hi, fill, 0)

            def fill_wait(r, carry):
                _row_copy(zrow, 0, xs_hbm, r, zsem).wait()
                return carry

            lax.fori_loop(lo, hi, fill_wait, 0)

        M = zrow.shape[0]

        def tail_copy(t):
            return pltpu.make_async_copy(zrow, xs_hbm.at[pl.ds(pl.multiple_of(t * M, M), M), :], zsem)

        def tail(t, carry):
            tail_copy(t).start()
            return carry

        lax.fori_loop(pad_ref[2 * N_EXPERTS], xs_hbm.shape[0] // M, tail, 0)

        def tail_wait(t, carry):
            tail_copy(t).wait()
            return carry

        lax.fori_loop(pad_ref[2 * N_EXPERTS], xs_hbm.shape[0] // M, tail_wait, 0)

    for k in range(K):
        pltpu.make_async_copy(h2_ref, xs_hbm.at[pl.ds(0, tm), :], sem).wait()


def _dispatch(pos, pad_rows, h2, n_rows, tm):
    T, D = h2.shape
    grid_spec = pltpu.PrefetchScalarGridSpec(
        num_scalar_prefetch=2,
        grid=(T // tm,),
        in_specs=[pl.BlockSpec((tm, D), lambda i, pos, pad: (i, 0))],
        out_specs=pl.BlockSpec(memory_space=pl.ANY),
        scratch_shapes=[pltpu.VMEM((MOE_BLOCK, D), F32), pltpu.SemaphoreType.DMA, pltpu.SemaphoreType.DMA],
    )
    return pl.pallas_call(
        _dispatch_kernel,
        grid_spec=grid_spec,
        out_shape=jax.ShapeDtypeStruct((n_rows, D), F32),
        compiler_params=_params(("arbitrary",)),
        name="dispatch",
    )(pos, pad_rows, h2)


def _expert_kernel(ts_ref, xs_hbm, wg_ref, wu_ref, wd_ref, y_hbm,
                   xb0, xb1, yb0, yb1, wg_b, wu_b, wd_b, gsem, osem, *, n_blocks):
    e = pl.program_id(0)
    M = MOE_BLOCK
    t0 = ts_ref[e]
    t1 = ts_ref[e + 1]
    n_used = ts_ref[N_EXPERTS]
    xbufs = (xb0, xb1)
    ybufs = (yb0, yb1)

    def rows(t):
        return pl.ds(pl.multiple_of(t * M, M), M)

    def in_copy(t, p):
        return pltpu.make_async_copy(xs_hbm.at[rows(t), :], xbufs[p], gsem.at[p])

    def out_copy(t, p):
        return pltpu.make_async_copy(ybufs[p], y_hbm.at[rows(t), :], osem.at[p])

    @pl.when(e == 0)
    def _():
        in_copy(0, 0).start()

    @pl.when(t1 > t0)
    def _():
        wg_b[...] = wg_ref[0].astype(BF16)
        wu_b[...] = wu_ref[0].astype(BF16)
        wd_b[...] = wd_ref[0].astype(BF16)

    def tile_body(t, p):
        in_copy(t, p).wait()

        @pl.when(t + 1 < n_used)
        def _():
            in_copy(t + 1, 1 - p).start()

        @pl.when(t >= 2)
        def _():
            out_copy(t - 2, p).wait()

        xb = xbufs[p][...].astype(BF16)
        hg = _dot(xb, wg_b[...])
        hu = _dot(xb, wu_b[...])
        hb = (hg * jax.nn.sigmoid(hg) * hu).astype(BF16)
        ybufs[p][...] = _dot(hb, wd_b[...])
        out_copy(t, p).start()

    def tile(t, carry):
        parity = lax.rem(t, 2)
        for p in range(2):
            pl.when(parity == p)(functools.partial(tile_body, t, p))
        return carry

    lax.fori_loop(t0, t1, tile, 0)

    @pl.when(e == N_EXPERTS - 1)
    def _():
        parity = lax.rem(n_used, 2)
        for p in range(2):
            @pl.when(parity == p)
            def _(p=p):
                out_copy(n_used - 1, 1 - p).wait()

                @pl.when(n_used >= 2)
                def _():
                    out_copy(n_used - 2, p).wait()

        yb0[...] = jnp.zeros(yb0.shape, F32)

        def clear(t, carry):
            out_copy(t, 0).start()
            return carry

        lax.fori_loop(n_used, n_blocks, clear, 0)

        def clear_wait(t, carry):
            out_copy(t, 0).wait()
            return carry

        lax.fori_loop(n_used, n_blocks, clear_wait, 0)


def _experts(xs, tile_start, w_gate, w_up, w_down):
    n_rows, D = xs.shape
    n_blocks = n_rows // MOE_BLOCK
    grid_spec = pltpu.PrefetchScalarGridSpec(
        num_scalar_prefetch=1,
        grid=(N_EXPERTS,),
        in_specs=[
            pl.BlockSpec(memory_space=pl.ANY),
            pl.BlockSpec((1, D, D_EXPERT), lambda e, ts: (e, 0, 0)),
            pl.BlockSpec((1, D, D_EXPERT), lambda e, ts: (e, 0, 0)),
            pl.BlockSpec((1, D_EXPERT, D), lambda e, ts: (e, 0, 0)),
        ],
        out_specs=pl.BlockSpec(memory_space=pl.ANY),
        scratch_shapes=[pltpu.VMEM((MOE_BLOCK, D), F32)] * 4 + [
            pltpu.VMEM((D, D_EXPERT), BF16), pltpu.VMEM((D, D_EXPERT), BF16),
            pltpu.VMEM((D_EXPERT, D), BF16),
            pltpu.SemaphoreType.DMA((2,)), pltpu.SemaphoreType.DMA((2,)),
        ],
    )
    return pl.pallas_call(
        functools.partial(_expert_kernel, n_blocks=n_blocks),
        grid_spec=grid_spec,
        out_shape=jax.ShapeDtypeStruct((n_rows, D), F32),
        compiler_params=_params(("arbitrary",)),
        name="experts",
    )(tile_start, xs, w_gate, w_up, w_down)


def _combine_kernel(pos_ref, y_hbm, x1_ref, rt_ref, o_ref, ybuf, sems):
    i = pl.program_id(0)
    n = pl.num_programs(0)
    tm = x1_ref.shape[0]
    K = TOPK_IN_GROUP
    slot = lax.rem(i, 2)

    def issue_tile(t, s):
        def issue(r, carry):
            a = (t * tm + r) * K
            for k in range(K):
                _row_copy(y_hbm, pos_ref[a + k], ybuf.at[s, k], r, sems.at[s]).start(priority=k)
            return carry

        lax.fori_loop(0, tm, issue, 0, unroll=8)

    @pl.when(i == 0)
    def _():
        issue_tile(0, 0)

    @pl.when(i + 1 < n)
    def _():
        issue_tile(i + 1, 1 - slot)

    for k in range(K):
        pltpu.make_async_copy(y_hbm.at[pl.ds(0, tm), :], ybuf.at[slot, k], sems.at[slot]).wait()
    w = rt_ref[...]
    yb = ybuf[slot]
    o_ref[...] = x1_ref[...] + (yb[0] * w[:, K:K + 1] + yb[1] * w[:, K + 1:K + 2])


def _combine(pos, y_rows, x1, rt, tm):
    T, D = x1.shape
    grid_spec = pltpu.PrefetchScalarGridSpec(
        num_scalar_prefetch=1,
        grid=(T // tm,),
        in_specs=[
            pl.BlockSpec(memory_space=pl.ANY),
            pl.BlockSpec((tm, D), lambda i, pos: (i, 0)),
            pl.BlockSpec((tm, LANES), lambda i, pos: (i, 0)),
        ],
        out_specs=pl.BlockSpec((tm, D), lambda i, pos: (i, 0)),
        scratch_shapes=[pltpu.VMEM((2, TOPK_IN_GROUP, tm, D), F32), pltpu.SemaphoreType.DMA((2,))],
    )
    return pl.pallas_call(
        _combine_kernel,
        grid_spec=grid_spec,
        out_shape=jax.ShapeDtypeStruct((T, D), F32),
        compiler_params=_params(("arbitrary",)),
        name="combine",
    )(pos, y_rows, x1, rt)


def _block_layout(rt, counts, T):
    K = TOPK_IN_GROUP
    n_rows = (K * T + N_EXPERTS * (MOE_BLOCK - 1) + MOE_BLOCK - 1) // MOE_BLOCK * MOE_BLOCK
    n_blocks = n_rows // MOE_BLOCK
    cnt = counts[0, :N_EXPERTS].astype(jnp.int32)
    padded = (cnt + MOE_BLOCK - 1) // MOE_BLOCK * MOE_BLOCK
    pad_end = jnp.cumsum(padded)
    pad_start = pad_end - padded
    e = rt[:, 0:K].astype(jnp.int32)
    rank = rt[:, 2 * K:3 * K].astype(jnp.int32)
    pos = (pad_start[e] + rank).reshape(-1)
    tile_start = (jnp.concatenate([pad_start, pad_end[-1:]]) // MOE_BLOCK).astype(jnp.int32)
    pad_rows = jnp.concatenate([pad_start + cnt, pad_end, pad_end[-1:] // MOE_BLOCK]).astype(jnp.int32)
    return tile_start, pos, pad_rows, n_rows


def _layer(x, norm1_g, w_in, cmp_pos, cmp_w1, cmp_w2, q_norm_g, k_norm_g, ret_gn_g, ret_gn_b,
           w_out, norm2_g, w_rg, b_rg, w_re, b_re, w_eg, w_eu, w_ed, tiles):
    B, S, D = x.shape
    T = B * S
    xt = x.reshape(T, D)

    w_main = jnp.concatenate([w_in[:, GATE_COL0 + NSA_GATE_WIDTH:], w_in[:, :GATE_COL0]], axis=1).astype(BF16)
    gpg = NSA_GATE_WIDTH // NSA_KV_GROUPS
    w_gate = jnp.concatenate(
        [jnp.pad(w_in[:, GATE_COL0 + g * gpg:GATE_COL0 + (g + 1) * gpg], ((0, 0), (0, LANES - gpg)))
         for g in range(NSA_KV_GROUPS)], axis=1).astype(BF16)
    proj, gate = _in_proj(xt, norm1_g.reshape(1, D), w_main, w_gate, tiles["tm_in"], tiles["tn_in"])

    cmp_kv = _compress(proj, cmp_pos, cmp_w1, cmp_w2, k_norm_g, B, S)
    o_nsa = _nsa(proj, gate, cmp_kv, q_norm_g.reshape(1, HEAD_DIM), k_norm_g, B, S, tiles["tq"], tiles["tk"])

    half = HEAD_DIM // 2
    inv_freq = ROPE_BASE ** (-jnp.arange(half, dtype=F32) / half)
    ang = jnp.arange(S, dtype=F32)[:, None] * inv_freq[None, :]
    cos2 = jnp.concatenate([jnp.cos(ang), jnp.cos(ang)], axis=1)
    sin2 = jnp.concatenate([-jnp.sin(ang), jnp.sin(ang)], axis=1)
    log_g = jnp.log1p(-jnp.exp2(-5.0 - jnp.arange(RET_HEADS, dtype=F32)))
    o_ret = _retention(proj, log_g, cos2, sin2, ret_gn_g, ret_gn_b, B, S)

    n_r = N_GROUPS + N_EXPERTS
    w_router = jnp.pad(jnp.concatenate([w_rg, w_re], axis=1), ((0, 0), (0, LANES - n_r)))
    b_router = jnp.pad(jnp.concatenate([b_rg, b_re]), (0, LANES - n_r)).reshape(1, LANES)
    x1, h2, rt, counts = _out_proj(o_nsa, o_ret, w_out.astype(BF16), xt, norm2_g.reshape(1, D),
                                   w_router, b_router, tiles["tm_out"])

    tile_start, pos, pad_rows, n_rows = _block_layout(rt, counts, T)
    xs = _dispatch(pos, pad_rows, h2, n_rows, tiles["tm_dsp"])
    y_rows = _experts(xs, tile_start, w_eg, w_eu, w_ed)
    out = _combine(pos, y_rows, x1, rt, tiles["tm_cmb"])
    return out.reshape(B, S, D)


def _tiles(T, S):
    return {
        "tm_in": min(1024, T), "tn_in": 512,
        "tq": min(128, S), "tk": min(512, S),
        "tm_out": min(512, T),
        "tm_dsp": min(512, T),
        "tm_cmb": min(256, T),
    }


def kernel(x, norm1_g, w_in, cmp_pos, cmp_w1, cmp_w2, q_norm_g, k_norm_g, ret_gn_g, ret_gn_b, w_out, norm2_g, w_router_group, b_router_group, w_router_expert, b_router_expert, w_exp_gate, w_exp_up, w_exp_down):
    B, S, _ = x.shape
    tiles = _tiles(B * S, S)
    for l in range(norm1_g.shape[0]):
        x = _layer(x, norm1_g[l], w_in[l], cmp_pos[l], cmp_w1[l], cmp_w2[l], q_norm_g[l], k_norm_g[l],
                   ret_gn_g[l], ret_gn_b[l], w_out[l], norm2_g[l], w_router_group[l], b_router_group[l],
                   w_router_expert[l], b_router_expert[l], w_exp_gate[l], w_exp_up[l], w_exp_down[l], tiles)
    return x
```

```python
import functools

import numpy as np
import jax
import jax.numpy as jnp
from jax import lax
from jax.experimental import pallas as pl
from jax.experimental.pallas import tpu as pltpu

F32 = jnp.float32
BF16 = jnp.bfloat16

D_MODEL = 2048
NSA_HEADS = 8
NSA_KV_GROUPS = 2
NSA_GROUP_HEADS = NSA_HEADS // NSA_KV_GROUPS
HEAD_DIM = 128
RET_HEADS = 8
CMP_BLOCK = 32
CMP_STRIDE = 16
SEL_BLOCK = 64
SEL_TOPK = 8
WINDOW = 512
RET_CHUNK = 128
ROPE_BASE = 10000.0
N_GROUPS = 4
EXPERTS_PER_GROUP = 8
N_EXPERTS = N_GROUPS * EXPERTS_PER_GROUP
TOPK_IN_GROUP = 2
D_EXPERT = 512
MOE_BLOCK = 256
RMS_EPS = 1e-6
GN_EPS = 1e-5
NEG_INF = -1e30
FORCED_SCORE = 1e6
LOG2E = 1.4426950408889634

NSA_Q_WIDTH = NSA_HEADS * HEAD_DIM
NSA_KV_WIDTH = NSA_KV_GROUPS * HEAD_DIM
NSA_GATE_WIDTH = 3 * NSA_HEADS
RET_WIDTH = RET_HEADS * HEAD_DIM
GATE_COL0 = NSA_Q_WIDTH + 6 * NSA_KV_WIDTH
LANES = 128
VMEM_LIMIT = 56 * 1024 * 1024

Q_COL0 = 4 * RET_WIDTH
KV_COL0 = Q_COL0 + NSA_Q_WIDTH
KC_BLK, VC_BLK, KS_BLK, VS_BLK, KW_BLK, VW_BLK = [KV_COL0 // LANES + 2 * t for t in range(6)]


def _rms(xf, g):
    return xf * lax.rsqrt(jnp.mean(xf * xf, axis=-1, keepdims=True) + RMS_EPS) * g


def _dot(a, b):
    return jnp.dot(a, b, preferred_element_type=F32)


def _dot_nt(a, b):
    return lax.dot_general(a, b, (((1,), (1,)), ((), ())), preferred_element_type=F32)


def _dot_tn(a, b):
    return lax.dot_general(a, b, (((0,), (0,)), ((), ())), preferred_element_type=F32)


def _split3(p):
    p1 = p.astype(BF16)
    r1 = p - p1.astype(F32)
    p2 = r1.astype(BF16)
    p3 = (r1 - p2.astype(F32)).astype(BF16)
    return p1, p2, p3


def _params(sem):
    return pltpu.CompilerParams(dimension_semantics=sem, vmem_limit_bytes=VMEM_LIMIT)


def _in_proj_kernel(x_ref, g_ref, w_ref, wg_ref, o_ref, og_ref, h_scr):
    @pl.when(pl.program_id(1) == 0)
    def _():
        h = _rms(x_ref[...], g_ref[...]).astype(BF16)
        h_scr[...] = h
        og_ref[...] = _dot(h, wg_ref[...])

    o_ref[...] = _dot(h_scr[...], w_ref[...])


def _in_proj(xt, g1, w_main, w_gate, tm, tn):
    T, D = xt.shape
    n_main = w_main.shape[1]
    n_gate = w_gate.shape[1]
    return pl.pallas_call(
        _in_proj_kernel,
        grid=(T // tm, n_main // tn),
        in_specs=[
            pl.BlockSpec((tm, D), lambda m, n: (m, 0)),
            pl.BlockSpec((1, D), lambda m, n: (0, 0)),
            pl.BlockSpec((D, tn), lambda m, n: (0, n)),
            pl.BlockSpec((D, n_gate), lambda m, n: (0, 0)),
        ],
        out_specs=[
            pl.BlockSpec((tm, tn), lambda m, n: (m, n)),
            pl.BlockSpec((tm, n_gate), lambda m, n: (m, 0)),
        ],
        out_shape=[
            jax.ShapeDtypeStruct((T, n_main), F32),
            jax.ShapeDtypeStruct((T, n_gate), F32),
        ],
        scratch_shapes=[pltpu.VMEM((tm, D), BF16)],
        compiler_params=_params(("parallel", "arbitrary")),
        name="in_proj",
    )(xt, g1, w_main, w_gate)


def _compress_kernel(x_ref, pos_ref, w1_ref, w2_ref, kg_ref, o_ref, xs_scr, *, S, n_pad):
    kv = pl.program_id(1)
    xs_scr[0:S, :] = x_ref[...]
    xs_scr[S:S + CMP_STRIDE, :] = jnp.zeros((CMP_STRIDE, HEAD_DIM), F32)
    acc = jnp.zeros((n_pad, HEAD_DIM), F32)
    for l in range(CMP_BLOCK):
        tb = xs_scr[pl.ds(l, n_pad, stride=CMP_STRIDE), :] + pos_ref[0, l:l + 1, :]
        acc = acc + _dot(tb.astype(BF16), w1_ref[0, l].astype(BF16))
    hid = acc * jax.nn.sigmoid(acc)
    out = _dot(hid.astype(BF16), w2_ref[0].astype(BF16))
    normed = _rms(out, kg_ref[0:1, :])
    o_ref[0, 0, 0] = jnp.where(kv == 0, normed, out).astype(BF16)


def _compress(proj, cmp_pos, cmp_w1, cmp_w2, k_norm_g, B, S):
    n_pad = S // CMP_STRIDE
    G = NSA_KV_GROUPS
    kern = functools.partial(_compress_kernel, S=S, n_pad=n_pad)
    return pl.pallas_call(
        kern,
        grid=(B, 2, G),
        in_specs=[
            pl.BlockSpec((S, HEAD_DIM), lambda b, kv, g: (b, KC_BLK + 2 * kv + g)),
            pl.BlockSpec((1, CMP_BLOCK, HEAD_DIM), lambda b, kv, g: (kv, 0, 0)),
            pl.BlockSpec((1, CMP_BLOCK, HEAD_DIM, HEAD_DIM), lambda b, kv, g: (kv, 0, 0, 0)),
            pl.BlockSpec((1, HEAD_DIM, HEAD_DIM), lambda b, kv, g: (kv, 0, 0)),
            pl.BlockSpec((3, HEAD_DIM), lambda b, kv, g: (0, 0)),
        ],
        out_specs=pl.BlockSpec((1, 1, 1, n_pad, HEAD_DIM), lambda b, kv, g: (b, kv, g, 0, 0)),
        out_shape=jax.ShapeDtypeStruct((B, 2, G, n_pad, HEAD_DIM), BF16),
        scratch_shapes=[pltpu.VMEM((S + CMP_STRIDE, HEAD_DIM), F32)],
        compiler_params=_params(("parallel", "parallel", "parallel")),
        name="compress",
    )(proj, cmp_pos, cmp_w1, cmp_w2, k_norm_g)


def _masked_softmax(s, msk):
    sm = jnp.where(msk, s, NEG_INF)
    m = jnp.max(sm, axis=-1, keepdims=True)
    e = jnp.where(msk, jnp.exp(sm - m), 0.0)
    den = jnp.sum(e, axis=-1, keepdims=True)
    return e / jnp.where(den > 0.0, den, 1.0)


def _nsa_kernel(q_ref, kc_ref, vc_ref, ks_ref, vs_ref, kw_ref, vw_ref, gate_ref, qg_ref, kg_ref,
                ovt_ref, ext_ref, o_ref, ksa, vsa, kwn, vwa, l_scr, acc_scr, *, S, tq, tk, wk):
    i = pl.program_id(2)
    H = NSA_GROUP_HEADS
    scale = HEAD_DIM ** -0.5
    c2 = scale * LOG2E
    n_cp = S // CMP_STRIDE
    n_sblk = S // SEL_BLOCK

    @pl.when(i == 0)
    def _():
        ksa[:, 0:HEAD_DIM] = _rms(ks_ref[...], kg_ref[1:2, :]).astype(BF16)
        ksa[:, HEAD_DIM:2 * HEAD_DIM] = ext_ref[...]
        kwn[...] = _rms(kw_ref[...], kg_ref[2:3, :]).astype(BF16)
        ones = jnp.ones((S, HEAD_DIM), BF16)
        vsa[:, 0:HEAD_DIM] = vs_ref[...].astype(BF16)
        vsa[:, HEAD_DIM:2 * HEAD_DIM] = ones
        vwa[:, 0:HEAD_DIM] = vw_ref[...].astype(BF16)
        vwa[:, HEAD_DIM:2 * HEAD_DIM] = ones

    pos = i * tq + lax.broadcasted_iota(jnp.int32, (tq, 1), 0)
    qg = qg_ref[...]
    q4 = jnp.concatenate([_rms(q_ref[:, h * HEAD_DIM:(h + 1) * HEAD_DIM], qg).astype(BF16)
                          for h in range(H)], axis=0)
    rows = [slice(h * tq, (h + 1) * tq) for h in range(H)]

    def softmax_pv(q_pairs, k, v, tail_bias):
        head = k.shape[0] - tail_bias.shape[1]
        outs, sums = [], []
        for q2 in q_pairs:
            s2 = _dot_nt(q2, k)
            es = []
            for hh in range(2):
                t = s2[hh * tq:(hh + 1) * tq] * c2
                tb = t[:, head:] + tail_bias
                m = jnp.max(tb, axis=-1, keepdims=True)
                if head:
                    ta = t[:, :head]
                    m = jnp.maximum(m, jnp.max(ta, axis=-1, keepdims=True))
                    es.append(jnp.concatenate([jnp.exp2(ta - m).astype(BF16),
                                               jnp.exp2(tb - m).astype(BF16)], axis=1))
                else:
                    es.append(jnp.exp2(tb - m).astype(BF16))
            o2 = _dot(jnp.concatenate(es, axis=0), v)
            outs += [o2[0:tq, 0:HEAD_DIM], o2[tq:2 * tq, 0:HEAD_DIM]]
            sums += [o2[0:tq, HEAD_DIM:2 * HEAD_DIM], o2[tq:2 * tq, HEAD_DIM:2 * HEAD_DIM]]
        return outs, sums

    ncol = lax.broadcasted_iota(jnp.int32, (1, n_cp), 1)
    cmask = (ncol * CMP_STRIDE + (CMP_BLOCK - 1)) <= pos
    s_c4 = _dot_nt(q4, kc_ref[0, 0, 0]) * scale
    ps = [_masked_softmax(s_c4[r], cmask) for r in rows]
    psum = ps[0] + ps[1] + ps[2] + ps[3]
    o_c4 = _dot(jnp.concatenate([p.astype(BF16) for p in ps], axis=0), vc_ref[0, 0, 0])

    ovt = ovt_ref[...]
    p1, p2, p3 = _split3(psum)
    imp_t = (_dot_nt(ovt, p1) + _dot_nt(ovt, p2) + _dot_nt(ovt, p3))[0:n_sblk, :]
    pos_t = i * tq + lax.broadcasted_iota(jnp.int32, (1, tq), 1)
    jrow = lax.broadcasted_iota(jnp.int32, (n_sblk, 1), 0)
    cur_t = pos_t // SEL_BLOCK
    valid_t = jrow * SEL_BLOCK <= pos_t
    forced_t = (jrow == 0) | (jrow == cur_t) | (jrow == cur_t - 1)
    score = jnp.where(valid_t, jnp.where(forced_t, FORCED_SCORE, imp_t), -jnp.inf)
    rank = jnp.zeros((n_sblk, tq), jnp.int32)
    for i2 in range(n_sblk):
        si = score[i2:i2 + 1, :]
        ahead = (si > score) | ((si == score) & (jrow > i2))
        rank = rank + jnp.where(ahead, 1, 0)
    drop_t = jnp.where((rank < SEL_TOPK) & valid_t, 0.0, NEG_INF)
    drop_t = jnp.concatenate([drop_t, jnp.full((LANES - n_sblk, tq), NEG_INF, F32)], axis=0)
    drop = drop_t.T.astype(BF16)
    drop2 = jnp.concatenate([drop, drop], axis=0)
    qa_pairs = [jnp.concatenate([q4[0:2 * tq], drop2], axis=1),
                jnp.concatenate([q4[2 * tq:4 * tq], drop2], axis=1)]

    kst = pl.multiple_of(jnp.clip(i * tq - WINDOW, 0, S - wk), tq)
    wpos = kst + lax.broadcasted_iota(jnp.int32, (1, wk), 1)
    wbias = jnp.where((wpos <= pos) & (wpos > pos - WINDOW), 0.0, NEG_INF)
    o_ws, lws = softmax_pv([q4[0:2 * tq], q4[2 * tq:4 * tq]],
                           kwn[pl.ds(kst, wk), :], vwa[pl.ds(kst, wk), :], wbias)

    def slc_prefix(n):
        kpos = (n - tk) + lax.broadcasted_iota(jnp.int32, (1, tk), 1)
        causal = jnp.where(kpos <= pos, 0.0, NEG_INF)
        outs, sums = softmax_pv(qa_pairs, ksa[0:n, :], vsa[0:n, :], causal)
        for r, o, l in zip(rows, outs, sums):
            acc_scr[r] = o
            l_scr[r] = l

    n_cls = S // tk
    cls = ((i + 1) * tq - 1) // tk
    for c in range(n_cls):
        pl.when(cls == c)(functools.partial(slc_prefix, (c + 1) * tk))

    gates = jax.nn.sigmoid(gate_ref[...])
    for h, r in enumerate(rows):
        c = 3 * h
        o = (o_c4[r] * gates[:, c:c + 1]
             + acc_scr[r] * (gates[:, c + 1:c + 2] / l_scr[r])
             + o_ws[h] * (gates[:, c + 2:c + 3] / lws[h]))
        o_ref[:, h * HEAD_DIM:(h + 1) * HEAD_DIM] = o.astype(BF16)


def _nsa_tables(S):
    n_cp, n_cmp, n_sblk = S // CMP_STRIDE, (S - CMP_BLOCK) // CMP_STRIDE + 1, S // SEL_BLOCK
    n = np.arange(n_cp)[None, :] * CMP_STRIDE
    j = np.arange(LANES)[:, None]
    ovt = ((n < (j + 1) * SEL_BLOCK) & (n + CMP_BLOCK > j * SEL_BLOCK)
           & (np.arange(n_cp)[None, :] < n_cmp) & (j < n_sblk))
    key_block = (np.arange(S) // SEL_BLOCK)[:, None] == np.arange(LANES)[None, :]
    return jnp.asarray(ovt, BF16), jnp.asarray(key_block, BF16)


def _nsa(proj, gate, cmp_kv, q_norm_g, k_norm_g, B, S, tq, tk):
    G = NSA_KV_GROUPS
    H = NSA_GROUP_HEADS
    n_cp = S // CMP_STRIDE
    wk = min(S, WINDOW + tq)
    nq = S // tq
    gw = H * HEAD_DIM
    ovt, key_block = _nsa_tables(S)
    kern = functools.partial(_nsa_kernel, S=S, tq=tq, tk=tk, wk=wk)
    kv_spec = lambda blk: pl.BlockSpec((S, HEAD_DIM), lambda b, g, i: (b, blk + g))
    return pl.pallas_call(
        kern,
        grid=(B, G, nq),
        in_specs=[
            pl.BlockSpec((tq, gw), lambda b, g, i: (b * nq + i, Q_COL0 // gw + g)),
            pl.BlockSpec((1, 1, 1, n_cp, HEAD_DIM), lambda b, g, i: (b, 0, g, 0, 0)),
            pl.BlockSpec((1, 1, 1, n_cp, HEAD_DIM), lambda b, g, i: (b, 1, g, 0, 0)),
            kv_spec(KS_BLK), kv_spec(VS_BLK), kv_spec(KW_BLK), kv_spec(VW_BLK),
            pl.BlockSpec((tq, LANES), lambda b, g, i: (b * nq + i, g)),
            pl.BlockSpec((1, HEAD_DIM), lambda b, g, i: (0, 0)),
            pl.BlockSpec((3, HEAD_DIM), lambda b, g, i: (0, 0)),
            pl.BlockSpec((LANES, n_cp), lambda b, g, i: (0, 0)),
            pl.BlockSpec((S, LANES), lambda b, g, i: (0, 0)),
        ],
        out_specs=pl.BlockSpec((tq, gw), lambda b, g, i: (b * nq + i, g)),
        out_shape=jax.ShapeDtypeStruct((B * S, NSA_Q_WIDTH), BF16),
        scratch_shapes=[
            pltpu.VMEM((S, 2 * HEAD_DIM), BF16), pltpu.VMEM((S, 2 * HEAD_DIM), BF16),
            pltpu.VMEM((S, HEAD_DIM), BF16), pltpu.VMEM((S, 2 * HEAD_DIM), BF16),
            pltpu.VMEM((H * tq, HEAD_DIM), F32), pltpu.VMEM((H * tq, HEAD_DIM), F32)],
        compiler_params=_params(("parallel", "parallel", "arbitrary")),
        name="nsa",
    )(proj, cmp_kv, cmp_kv, proj, proj, proj, proj, gate, q_norm_g, k_norm_g, ovt, key_block)


RET_HB = 8


def _ret_kernel(lg_ref, rq_ref, rk_ref, rv_ref, rg_ref, cos_ref, sin_ref, gg_ref, gb_ref,
                o_ref, r_scr):
    hg = pl.program_id(1)
    C = RET_CHUNK
    scale = HEAD_DIM ** -0.5

    @pl.when(pl.program_id(2) == 0)
    def _():
        r_scr[...] = jnp.zeros(r_scr.shape, F32)

    cos = cos_ref[...]
    sin = sin_ref[...]
    n_col = lax.broadcasted_iota(jnp.int32, (C, 1), 0).astype(F32)
    n_row = lax.broadcasted_iota(jnp.int32, (1, C), 1).astype(F32)
    diff = n_col - n_row
    for h in range(RET_HB):
        sl = slice(h * HEAD_DIM, (h + 1) * HEAD_DIM)
        hh = hg * RET_HB + h
        lg = lg_ref[hh]
        q = rq_ref[:, sl]
        k = rk_ref[:, sl]
        qf = q * cos + pltpu.roll(q, HEAD_DIM // 2, 1) * sin
        kf = (k * cos + pltpu.roll(k, HEAD_DIM // 2, 1) * sin) * scale
        v = rv_ref[:, sl].astype(BF16)
        dec = jnp.where(diff >= 0.0, jnp.exp(lg * jnp.maximum(diff, 0.0)), 0.0)
        xi = jnp.exp(lg * (n_col + 1.0))
        zeta = jnp.exp(lg * (C - 1.0 - n_col))
        cd = jnp.exp(jnp.full((1, HEAD_DIM), lg * float(C), F32))
        qb = qf.astype(BF16)
        r_old = r_scr[h]
        a = _dot_nt(qb, kf.astype(BF16)) * dec
        o = _dot(a.astype(BF16), v) + _dot(qb, r_old.astype(BF16)) * xi
        r_scr[h] = r_old * cd + _dot_tn((kf * zeta).astype(BF16), v)
        mu = jnp.mean(o, axis=-1, keepdims=True)
        d = o - mu
        var = jnp.mean(d * d, axis=-1, keepdims=True)
        y = d * lax.rsqrt(var + GN_EPS) * gg_ref[pl.ds(hh, 1), :] + gb_ref[pl.ds(hh, 1), :]
        gt = rg_ref[:, sl]
        o_ref[:, sl] = (gt * jax.nn.sigmoid(gt) * y).astype(BF16)


def _retention(proj, log_g, cos2, sin2, gn_g, gn_b, B, S):
    C = RET_CHUNK
    n_ch = S // C
    bw = RET_HB * HEAD_DIM
    nhb = RET_HEADS // RET_HB
    spec = lambda k: pl.BlockSpec((C, bw), lambda b, hg, c, lg: (b * n_ch + c, k * nhb + hg))
    grid_spec = pltpu.PrefetchScalarGridSpec(
        num_scalar_prefetch=1,
        grid=(B, nhb, n_ch),
        in_specs=[
            spec(0), spec(1), spec(2), spec(3),
            pl.BlockSpec((C, HEAD_DIM), lambda b, hg, c, lg: (c, 0)),
            pl.BlockSpec((C, HEAD_DIM), lambda b, hg, c, lg: (c, 0)),
            pl.BlockSpec((RET_HEADS, HEAD_DIM), lambda b, hg, c, lg: (0, 0)),
            pl.BlockSpec((RET_HEADS, HEAD_DIM), lambda b, hg, c, lg: (0, 0)),
        ],
        out_specs=pl.BlockSpec((C, bw), lambda b, hg, c, lg: (b * n_ch + c, hg)),
        scratch_shapes=[pltpu.VMEM((RET_HB, HEAD_DIM, HEAD_DIM), F32)],
    )
    return pl.pallas_call(
        _ret_kernel,
        grid_spec=grid_spec,
        out_shape=jax.ShapeDtypeStruct((B * S, RET_WIDTH), BF16),
        compiler_params=_params(("parallel", "parallel", "arbitrary")),
        name="retention",
    )(log_g, proj, proj, proj, proj, cos2, sin2, gn_g, gn_b)


def _lane_max(v):
    return jnp.max(v, axis=-1, keepdims=True)


def _lane_min(v):
    return jnp.min(v, axis=-1, keepdims=True)


def _out_kernel(on_ref, or_ref, w_ref, x_ref, g2_ref, wr_ref, br_ref, tri_ref,
                x1_ref, h2_ref, rt_ref, cnt_ref, cnt_scr):
    @pl.when(pl.program_id(0) == 0)
    def _():
        cnt_scr[...] = jnp.zeros(cnt_scr.shape, F32)

    half = on_ref.shape[1]
    acc = _dot(on_ref[...], w_ref[0:half, :]) + _dot(or_ref[...], w_ref[half:2 * half, :])
    x1 = x_ref[...] + acc
    x1_ref[...] = x1
    h2 = _rms(x1, g2_ref[...])
    h2_ref[...] = h2

    h_hi = h2.astype(BF16)
    h_lo = (h2 - h_hi.astype(F32)).astype(BF16)
    wr = wr_ref[...]
    w_hi = wr.astype(BF16)
    w_lo = (wr - w_hi.astype(F32)).astype(BF16)
    logits = _dot(h_hi, w_hi) + _dot(h_lo, w_hi) + _dot(h_hi, w_lo) + br_ref[...]

    tm = logits.shape[0]
    lane = lax.broadcasted_iota(jnp.int32, (tm, LANES), 1)
    big = jnp.int32(LANES)
    gm = lane < N_GROUPS
    gl = jnp.where(gm, logits, -jnp.inf)
    ge = jnp.where(gm, jnp.exp(gl - _lane_max(gl)), 0.0)
    pg = ge / jnp.sum(ge, axis=-1, keepdims=True)
    g_top = _lane_max(pg)
    g_idx = _lane_min(jnp.where(gm & (pg == g_top), lane, big))
    e0 = N_GROUPS + g_idx * EXPERTS_PER_GROUP
    em = (lane >= e0) & (lane < e0 + EXPERTS_PER_GROUP)
    el = jnp.where(em, logits, -jnp.inf)
    ee = jnp.where(em, jnp.exp(el - _lane_max(el)), 0.0)
    pe = ee / jnp.sum(ee, axis=-1, keepdims=True)
    t1 = _lane_max(jnp.where(em, pe, -1.0))
    i1 = _lane_min(jnp.where(em & (pe == t1), lane, big))
    em2 = em & (lane != i1)
    t2 = _lane_max(jnp.where(em2, pe, -1.0))
    i2 = _lane_min(jnp.where(em2 & (pe == t2), lane, big))
    tsum = t1 + t2
    w1 = g_top * t1 / tsum
    w2 = g_top * t2 / tsum
    e1 = i1 - N_GROUPS
    e2 = i2 - N_GROUPS
    oh1 = jnp.where(lane == e1, 1.0, 0.0)
    oh2 = jnp.where(lane == e2, 1.0, 0.0)
    both = oh1 + oh2
    before = _dot(tri_ref[...], both.astype(BF16)) + cnt_scr[...]
    r1 = jnp.sum(before * oh1, axis=-1, keepdims=True)
    r2 = jnp.sum(before * oh2, axis=-1, keepdims=True)
    cnt_scr[...] = cnt_scr[...] + jnp.sum(both, axis=0, keepdims=True)
    cnt_ref[...] = jnp.broadcast_to(cnt_scr[...], cnt_ref.shape)

    cols = (e1.astype(F32), e2.astype(F32), w1, w2, r1, r2)
    rt = jnp.zeros((tm, LANES), F32)
    for c, v in enumerate(cols):
        rt = jnp.where(lane == c, v, rt)
    rt_ref[...] = rt


def _out_proj(o_nsa, o_ret, w_out, xt, g2, w_router, b_router, tm):
    T, D = xt.shape
    half = o_nsa.shape[1]
    tri = jnp.asarray(np.tril(np.ones((tm, tm), np.float32), -1), BF16)
    return pl.pallas_call(
        _out_kernel,
        grid=(T // tm,),
        in_specs=[
            pl.BlockSpec((tm, half), lambda m: (m, 0)),
            pl.BlockSpec((tm, half), lambda m: (m, 0)),
            pl.BlockSpec((2 * half, D), lambda m: (0, 0), pipeline_mode=pl.Buffered(1)),
            pl.BlockSpec((tm, D), lambda m: (m, 0)),
            pl.BlockSpec((1, D), lambda m: (0, 0)),
            pl.BlockSpec((D, LANES), lambda m: (0, 0)),
            pl.BlockSpec((1, LANES), lambda m: (0, 0)),
            pl.BlockSpec((tm, tm), lambda m: (0, 0)),
        ],
        out_specs=[
            pl.BlockSpec((tm, D), lambda m: (m, 0)),
            pl.BlockSpec((tm, D), lambda m: (m, 0)),
            pl.BlockSpec((tm, LANES), lambda m: (m, 0)),
            pl.BlockSpec((8, LANES), lambda m: (0, 0)),
        ],
        out_shape=[
            jax.ShapeDtypeStruct((T, D), F32),
            jax.ShapeDtypeStruct((T, D), F32),
            jax.ShapeDtypeStruct((T, LANES), F32),
            jax.ShapeDtypeStruct((8, LANES), F32),
        ],
        scratch_shapes=[pltpu.VMEM((1, LANES), F32)],
        compiler_params=_params(("arbitrary",)),
        name="out_proj",
    )(o_nsa, o_ret, w_out, xt, g2, w_router, b_router, tri)


def _row_copy(src, row, dst, slot, sem):
    return pltpu.make_async_copy(src.at[pl.ds(row, 1), :], dst.at[pl.ds(slot, 1), :], sem)


def _dispatch_kernel(pos_ref, pad_ref, h2_ref, xs_hbm, zrow, sem, zsem):
    i = pl.program_id(0)
    tm = h2_ref.shape[0]
    K = TOPK_IN_GROUP

    def issue(r, carry):
        a = (i * tm + r) * K
        for k in range(K):
            _row_copy(h2_ref, r, xs_hbm, pos_ref[a + k], sem).start()
        return carry

    lax.fori_loop(0, tm, issue, 0, unroll=8)

    @pl.when(i == pl.num_programs(0) - 1)
    def _():
        zrow[...] = jnp.zeros(zrow.shape, F32)
        for e in range(N_EXPERTS):
            lo, hi = pad_ref[e], pad_ref[N_EXPERTS + e]

            def fill(r, carry):
                _row_copy(zrow, 0, xs_hbm, r, zsem).start()
                return carry

            lax.fori_loop(lo, hi, fill, 0)

            def fill_wait(r, carry):
                _row_copy(zrow, 0, xs_hbm, r, zsem).wait()
                return carry

            lax.fori_loop(lo, hi, fill_wait, 0)

        M = zrow.shape[0]

        def tail_copy(t):
            return pltpu.make_async_copy(zrow, xs_hbm.at[pl.ds(pl.multiple_of(t * M, M), M), :], zsem)

        def tail(t, carry):
            tail_copy(t).start()
            return carry

        lax.fori_loop(pad_ref[2 * N_EXPERTS], xs_hbm.shape[0] // M, tail, 0)

        def tail_wait(t, carry):
            tail_copy(t).wait()
            return carry

        lax.fori_loop(pad_ref[2 * N_EXPERTS], xs_hbm.shape[0] // M, tail_wait, 0)

    for k in range(K):
        pltpu.make_async_copy(h2_ref, xs_hbm.at[pl.ds(0, tm), :], sem).wait()


def _dispatch(pos, pad_rows, h2, n_rows, tm):
    T, D = h2.shape
    grid_spec = pltpu.PrefetchScalarGridSpec(
        num_scalar_prefetch=2,
        grid=(T // tm,),
        in_specs=[pl.BlockSpec((tm, D), lambda i, pos, pad: (i, 0))],
        out_specs=pl.BlockSpec(memory_space=pl.ANY),
        scratch_shapes=[pltpu.VMEM((MOE_BLOCK, D), F32), pltpu.SemaphoreType.DMA, pltpu.SemaphoreType.DMA],
    )
    return pl.pallas_call(
        _dispatch_kernel,
        grid_spec=grid_spec,
        out_shape=jax.ShapeDtypeStruct((n_rows, D), F32),
        compiler_params=_params(("arbitrary",)),
        name="dispatch",
    )(pos, pad_rows, h2)


def _expert_kernel(ts_ref, xs_hbm, wg_ref, wu_ref, wd_ref, y_hbm,
                   xb0, xb1, yb0, yb1, wg_b, wu_b, wd_b, gsem, osem, *, n_blocks):
    e = pl.program_id(0)
    M = MOE_BLOCK
    t0 = ts_ref[e]
    t1 = ts_ref[e + 1]
    n_used = ts_ref[N_EXPERTS]
    xbufs = (xb0, xb1)
    ybufs = (yb0, yb1)

    def rows(t):
        return pl.ds(pl.multiple_of(t * M, M), M)

    def in_copy(t, p):
        return pltpu.make_async_copy(xs_hbm.at[rows(t), :], xbufs[p], gsem.at[p])

    def out_copy(t, p):
        return pltpu.make_async_copy(ybufs[p], y_hbm.at[rows(t), :], osem.at[p])

    @pl.when(e == 0)
    def _():
        in_copy(0, 0).start()

    @pl.when(t1 > t0)
    def _():
        wg_b[...] = wg_ref[0].astype(BF16)
        wu_b[...] = wu_ref[0].astype(BF16)
        wd_b[...] = wd_ref[0].astype(BF16)

    def tile_body(t, p):
        in_copy(t, p).wait()

        @pl.when(t + 1 < n_used)
        def _():
            in_copy(t + 1, 1 - p).start()

        @pl.when(t >= 2)
        def _():
            out_copy(t - 2, p).wait()

        xb = xbufs[p][...].astype(BF16)
        hg = _dot(xb, wg_b[...])
        hu = _dot(xb, wu_b[...])
        hb = (hg * jax.nn.sigmoid(hg) * hu).astype(BF16)
        ybufs[p][...] = _dot(hb, wd_b[...])
        out_copy(t, p).start()

    def tile(t, carry):
        parity = lax.rem(t, 2)
        for p in range(2):
            pl.when(parity == p)(functools.partial(tile_body, t, p))
        return carry

    lax.fori_loop(t0, t1, tile, 0)

    @pl.when(e == N_EXPERTS - 1)
    def _():
        parity = lax.rem(n_used, 2)
        for p in range(2):
            @pl.when(parity == p)
            def _(p=p):
                out_copy(n_used - 1, 1 - p).wait()

                @pl.when(n_used >= 2)
                def _():
                    out_copy(n_used - 2, p).wait()

        yb0[...] = jnp.zeros(yb0.shape, F32)

        def clear(t, carry):
            out_copy(t, 0).start()
            return carry

        lax.fori_loop(n_used, n_blocks, clear, 0)

        def clear_wait(t, carry):
            out_copy(t, 0).wait()
            return carry

        lax.fori_loop(n_used, n_blocks, clear_wait, 0)


def _experts(xs, tile_start, w_gate, w_up, w_down):
    n_rows, D = xs.shape
    n_blocks = n_rows // MOE_BLOCK
    grid_spec = pltpu.PrefetchScalarGridSpec(
        num_scalar_prefetch=1,
        grid=(N_EXPERTS,),
        in_specs=[
            pl.BlockSpec(memory_space=pl.ANY),
            pl.BlockSpec((1, D, D_EXPERT), lambda e, ts: (e, 0, 0)),
            pl.BlockSpec((1, D, D_EXPERT), lambda e, ts: (e, 0, 0)),
            pl.BlockSpec((1, D_EXPERT, D), lambda e, ts: (e, 0, 0)),
        ],
        out_specs=pl.BlockSpec(memory_space=pl.ANY),
        scratch_shapes=[pltpu.VMEM((MOE_BLOCK, D), F32)] * 4 + [
            pltpu.VMEM((D, D_EXPERT), BF16), pltpu.VMEM((D, D_EXPERT), BF16),
            pltpu.VMEM((D_EXPERT, D), BF16),
            pltpu.SemaphoreType.DMA((2,)), pltpu.SemaphoreType.DMA((2,)),
        ],
    )
    return pl.pallas_call(
        functools.partial(_expert_kernel, n_blocks=n_blocks),
        grid_spec=grid_spec,
        out_shape=jax.ShapeDtypeStruct((n_rows, D), F32),
        compiler_params=_params(("arbitrary",)),
        name="experts",
    )(tile_start, xs, w_gate, w_up, w_down)


def _combine_kernel(pos_ref, y_hbm, x1_ref, rt_ref, o_ref, ybuf, sems):
    i = pl.program_id(0)
    n = pl.num_programs(0)
    tm = x1_ref.shape[0]
    K = TOPK_IN_GROUP
    slot = lax.rem(i, 2)

    def issue_tile(t, s):
        def issue(r, carry):
            a = (t * tm + r) * K
            for k in range(K):
                _row_copy(y_hbm, pos_ref[a + k], ybuf.at[s, k], r, sems.at[s]).start(priority=k)
            return carry

        lax.fori_loop(0, tm, issue, 0, unroll=8)

    @pl.when(i == 0)
    def _():
        issue_tile(0, 0)

    @pl.when(i + 1 < n)
    def _():
        issue_tile(i + 1, 1 - slot)

    for k in range(K):
        pltpu.make_async_copy(y_hbm.at[pl.ds(0, tm), :], ybuf.at[slot, k], sems.at[slot]).wait()
    w = rt_ref[...]
    yb = ybuf[slot]
    o_ref[...] = x1_ref[...] + (yb[0] * w[:, K:K + 1] + yb[1] * w[:, K + 1:K + 2])


def _combine(pos, y_rows, x1, rt, tm):
    T, D = x1.shape
    grid_spec = pltpu.PrefetchScalarGridSpec(
        num_scalar_prefetch=1,
        grid=(T // tm,),
        in_specs=[
            pl.BlockSpec(memory_space=pl.ANY),
            pl.BlockSpec((tm, D), lambda i, pos: (i, 0)),
            pl.BlockSpec((tm, LANES), lambda i, pos: (i, 0)),
        ],
        out_specs=pl.BlockSpec((tm, D), lambda i, pos: (i, 0)),
        scratch_shapes=[pltpu.VMEM((2, TOPK_IN_GROUP, tm, D), F32), pltpu.SemaphoreType.DMA((2,))],
    )
    return pl.pallas_call(
        _combine_kernel,
        grid_spec=grid_spec,
        out_shape=jax.ShapeDtypeStruct((T, D), F32),
        compiler_params=_params(("arbitrary",)),
        name="combine",
    )(pos, y_rows, x1, rt)


def _block_layout(rt, counts, T):
    K = TOPK_IN_GROUP
    n_rows = (K * T + N_EXPERTS * (MOE_BLOCK - 1) + MOE_BLOCK - 1) // MOE_BLOCK * MOE_BLOCK
    n_blocks = n_rows // MOE_BLOCK
    cnt = counts[0, :N_EXPERTS].astype(jnp.int32)
    padded = (cnt + MOE_BLOCK - 1) // MOE_BLOCK * MOE_BLOCK
    pad_end = jnp.cumsum(padded)
    pad_start = pad_end - padded
    e = rt[:, 0:K].astype(jnp.int32)
    rank = rt[:, 2 * K:3 * K].astype(jnp.int32)
    pos = (pad_start[e] + rank).reshape(-1)
    tile_start = (jnp.concatenate([pad_start, pad_end[-1:]]) // MOE_BLOCK).astype(jnp.int32)
    pad_rows = jnp.concatenate([pad_start + cnt, pad_end, pad_end[-1:] // MOE_BLOCK]).astype(jnp.int32)
    return tile_start, pos, pad_rows, n_rows


def _layer(x, norm1_g, w_in, cmp_pos, cmp_w1, cmp_w2, q_norm_g, k_norm_g, ret_gn_g, ret_gn_b,
           w_out, norm2_g, w_rg, b_rg, w_re, b_re, w_eg, w_eu, w_ed, tiles):
    B, S, D = x.shape
    T = B * S
    xt = x.reshape(T, D)

    w_main = jnp.concatenate([w_in[:, GATE_COL0 + NSA_GATE_WIDTH:], w_in[:, :GATE_COL0]], axis=1).astype(BF16)
    gpg = NSA_GATE_WIDTH // NSA_KV_GROUPS
    w_gate = jnp.concatenate(
        [jnp.pad(w_in[:, GATE_COL0 + g * gpg:GATE_COL0 + (g + 1) * gpg], ((0, 0), (0, LANES - gpg)))
         for g in range(NSA_KV_GROUPS)], axis=1).astype(BF16)
    proj, gate = _in_proj(xt, norm1_g.reshape(1, D), w_main, w_gate, tiles["tm_in"], tiles["tn_in"])

    cmp_kv = _compress(proj, cmp_pos, cmp_w1, cmp_w2, k_norm_g, B, S)
    o_nsa = _nsa(proj, gate, cmp_kv, q_norm_g.reshape(1, HEAD_DIM), k_norm_g, B, S, tiles["tq"], tiles["tk"])

    half = HEAD_DIM // 2
    inv_freq = ROPE_BASE ** (-jnp.arange(half, dtype=F32) / half)
    ang = jnp.arange(S, dtype=F32)[:, None] * inv_freq[None, :]
    cos2 = jnp.concatenate([jnp.cos(ang), jnp.cos(ang)], axis=1)
    sin2 = jnp.concatenate([-jnp.sin(ang), jnp.sin(ang)], axis=1)
    log_g = jnp.log1p(-jnp.exp2(-5.0 - jnp.arange(RET_HEADS, dtype=F32)))
    o_ret = _retention(proj, log_g, cos2, sin2, ret_gn_g, ret_gn_b, B, S)

    n_r = N_GROUPS + N_EXPERTS
    w_router = jnp.pad(jnp.concatenate([w_rg, w_re], axis=1), ((0, 0), (0, LANES - n_r)))
    b_router = jnp.pad(jnp.concatenate([b_rg, b_re]), (0, LANES - n_r)).reshape(1, LANES)
    x1, h2, rt, counts = _out_proj(o_nsa, o_ret, w_out.astype(BF16), xt, norm2_g.reshape(1, D),
                                   w_router, b_router, tiles["tm_out"])

    tile_start, pos, pad_rows, n_rows = _block_layout(rt, counts, T)
    xs = _dispatch(pos, pad_rows, h2, n_rows, tiles["tm_dsp"])
    y_rows = _experts(xs, tile_start, w_eg, w_eu, w_ed)
    out = _combine(pos, y_rows, x1, rt, tiles["tm_cmb"])
    return out.reshape(B, S, D)


def _tiles(T, S):
    return {
        "tm_in": min(1024, T), "tn_in": 512,
        "tq": min(256, S), "tk": min(512, S),
        "tm_out": min(512, T),
        "tm_dsp": min(1024, T),
        "tm_cmb": min(256, T),
    }


def kernel(x, norm1_g, w_in, cmp_pos, cmp_w1, cmp_w2, q_norm_g, k_norm_g, ret_gn_g, ret_gn_b, w_out, norm2_g, w_router_group, b_router_group, w_router_expert, b_router_expert, w_exp_gate, w_exp_up, w_exp_down):
    B, S, _ = x.shape
    tiles = _tiles(B * S, S)
    for l in range(norm1_g.shape[0]):
        x = _layer(x, norm1_g[l], w_in[l], cmp_pos[l], cmp_w1[l], cmp_w2[l], q_norm_g[l], k_norm_g[l],
                   ret_gn_g[l], ret_gn_b[l], w_out[l], norm2_g[l], w_router_group[l], b_router_group[l],
                   w_router_expert[l], b_router_expert[l], w_exp_gate[l], w_exp_up[l], w_exp_down[l], tiles)
    return x
```

```python
import functools

import numpy as np
import jax
import jax.numpy as jnp
from jax import lax
from jax.experimental import pallas as pl
from jax.experimental.pallas import tpu as pltpu

F32 = jnp.float32
BF16 = jnp.bfloat16

D_MODEL = 2048
NSA_HEADS = 8
NSA_KV_GROUPS = 2
NSA_GROUP_HEADS = NSA_HEADS // NSA_KV_GROUPS
HEAD_DIM = 128
RET_HEADS = 8
CMP_BLOCK = 32
CMP_STRIDE = 16
SEL_BLOCK = 64
SEL_TOPK = 8
WINDOW = 512
RET_CHUNK = 128
ROPE_BASE = 10000.0
N_GROUPS = 4
EXPERTS_PER_GROUP = 8
N_EXPERTS = N_GROUPS * EXPERTS_PER_GROUP
TOPK_IN_GROUP = 2
D_EXPERT = 512
MOE_BLOCK = 256
RMS_EPS = 1e-6
GN_EPS = 1e-5
NEG_INF = -1e30
FORCED_SCORE = 1e6
LOG2E = 1.4426950408889634

NSA_Q_WIDTH = NSA_HEADS * HEAD_DIM
NSA_KV_WIDTH = NSA_KV_GROUPS * HEAD_DIM
NSA_GATE_WIDTH = 3 * NSA_HEADS
RET_WIDTH = RET_HEADS * HEAD_DIM
GATE_COL0 = NSA_Q_WIDTH + 6 * NSA_KV_WIDTH
LANES = 128
VMEM_LIMIT = 56 * 1024 * 1024

Q_COL0 = 4 * RET_WIDTH
KV_COL0 = Q_COL0 + NSA_Q_WIDTH
KC_BLK, VC_BLK, KS_BLK, VS_BLK, KW_BLK, VW_BLK = [KV_COL0 // LANES + 2 * t for t in range(6)]


def _rms(xf, g):
    return xf * lax.rsqrt(jnp.mean(xf * xf, axis=-1, keepdims=True) + RMS_EPS) * g


def _dot(a, b):
    return jnp.dot(a, b, preferred_element_type=F32)


def _dot_nt(a, b):
    return lax.dot_general(a, b, (((1,), (1,)), ((), ())), preferred_element_type=F32)


def _dot_tn(a, b):
    return lax.dot_general(a, b, (((0,), (0,)), ((), ())), preferred_element_type=F32)


def _split3(p):
    p1 = p.astype(BF16)
    r1 = p - p1.astype(F32)
    p2 = r1.astype(BF16)
    p3 = (r1 - p2.astype(F32)).astype(BF16)
    return p1, p2, p3


def _params(sem):
    return pltpu.CompilerParams(dimension_semantics=sem, vmem_limit_bytes=VMEM_LIMIT)


def _in_proj_kernel(x_ref, g_ref, wa_ref, wb_ref, wg_ref, o_ref, og_ref, h_scr, *, n_a):
    n = pl.program_id(1)

    @pl.when(n == 0)
    def _():
        h = _rms(x_ref[...], g_ref[...]).astype(BF16)
        h_scr[...] = h
        og_ref[...] = _dot(h, wg_ref[...])

    @pl.when(n < n_a)
    def _():
        o_ref[...] = _dot(h_scr[...], wa_ref[...])

    @pl.when(n >= n_a)
    def _():
        o_ref[...] = _dot(h_scr[...], wb_ref[...])


def _in_proj(xt, g1, w_a, w_b, w_gate, tm, tn):
    T, D = xt.shape
    n_a, n_b = w_a.shape[1] // tn, w_b.shape[1] // tn
    n_main = w_a.shape[1] + w_b.shape[1]
    n_gate = w_gate.shape[1]
    return pl.pallas_call(
        functools.partial(_in_proj_kernel, n_a=n_a),
        grid=(T // tm, n_a + n_b),
        in_specs=[
            pl.BlockSpec((tm, D), lambda m, n: (m, 0)),
            pl.BlockSpec((1, D), lambda m, n: (0, 0)),
            pl.BlockSpec((D, tn), lambda m, n: (0, jnp.minimum(n, n_a - 1))),
            pl.BlockSpec((D, tn), lambda m, n: (0, jnp.maximum(n - n_a, 0))),
            pl.BlockSpec((D, n_gate), lambda m, n: (0, 0)),
        ],
        out_specs=[
            pl.BlockSpec((tm, tn), lambda m, n: (m, n)),
            pl.BlockSpec((tm, n_gate), lambda m, n: (m, 0)),
        ],
        out_shape=[
            jax.ShapeDtypeStruct((T, n_main), F32),
            jax.ShapeDtypeStruct((T, n_gate), F32),
        ],
        scratch_shapes=[pltpu.VMEM((tm, D), BF16)],
        compiler_params=_params(("parallel", "arbitrary")),
        name="in_proj",
    )(xt, g1, w_a, w_b, w_gate)


def _compress_kernel(x_ref, pos_ref, w1_ref, w2_ref, kg_ref, o_ref, xs_scr, *, S, n_pad):
    kv = pl.program_id(1)
    xs_scr[0:S, :] = x_ref[...]
    xs_scr[S:S + CMP_STRIDE, :] = jnp.zeros((CMP_STRIDE, HEAD_DIM), F32)
    acc = jnp.zeros((n_pad, HEAD_DIM), F32)
    for l in range(CMP_BLOCK):
        tb = xs_scr[pl.ds(l, n_pad, stride=CMP_STRIDE), :] + pos_ref[0, l:l + 1, :]
        acc = acc + _dot(tb.astype(BF16), w1_ref[0, l].astype(BF16))
    hid = acc * jax.nn.sigmoid(acc)
    out = _dot(hid.astype(BF16), w2_ref[0].astype(BF16))
    normed = _rms(out, kg_ref[0:1, :])
    o_ref[0, 0, 0] = jnp.where(kv == 0, normed, out).astype(BF16)


def _compress(proj, cmp_pos, cmp_w1, cmp_w2, k_norm_g, B, S):
    n_pad = S // CMP_STRIDE
    G = NSA_KV_GROUPS
    kern = functools.partial(_compress_kernel, S=S, n_pad=n_pad)
    return pl.pallas_call(
        kern,
        grid=(B, 2, G),
        in_specs=[
            pl.BlockSpec((S, HEAD_DIM), lambda b, kv, g: (b, KC_BLK + 2 * kv + g)),
            pl.BlockSpec((1, CMP_BLOCK, HEAD_DIM), lambda b, kv, g: (kv, 0, 0)),
            pl.BlockSpec((1, CMP_BLOCK, HEAD_DIM, HEAD_DIM), lambda b, kv, g: (kv, 0, 0, 0)),
            pl.BlockSpec((1, HEAD_DIM, HEAD_DIM), lambda b, kv, g: (kv, 0, 0)),
            pl.BlockSpec((3, HEAD_DIM), lambda b, kv, g: (0, 0)),
        ],
        out_specs=pl.BlockSpec((1, 1, 1, n_pad, HEAD_DIM), lambda b, kv, g: (b, kv, g, 0, 0)),
        out_shape=jax.ShapeDtypeStruct((B, 2, G, n_pad, HEAD_DIM), BF16),
        scratch_shapes=[pltpu.VMEM((S + CMP_STRIDE, HEAD_DIM), F32)],
        compiler_params=_params(("parallel", "parallel", "parallel")),
        name="compress",
    )(proj, cmp_pos, cmp_w1, cmp_w2, k_norm_g)


def _masked_softmax(s, msk):
    sm = jnp.where(msk, s, NEG_INF)
    m = jnp.max(sm, axis=-1, keepdims=True)
    e = jnp.where(msk, jnp.exp(sm - m), 0.0)
    den = jnp.sum(e, axis=-1, keepdims=True)
    return e / jnp.where(den > 0.0, den, 1.0)


def _nsa_kernel(q_ref, kc_ref, vc_ref, ks_ref, vs_ref, kw_ref, vw_ref, gate_ref, qg_ref, kg_ref,
                ovt_ref, ext_ref, o_ref, ksa, vsa, kwn, vwa, l_scr, acc_scr, *, S, tq, tk, wk):
    i = pl.program_id(2)
    H = NSA_GROUP_HEADS
    scale = HEAD_DIM ** -0.5
    c2 = scale * LOG2E
    n_cp = S // CMP_STRIDE
    n_sblk = S // SEL_BLOCK

    @pl.when(i == 0)
    def _():
        ksa[:, 0:HEAD_DIM] = _rms(ks_ref[...], kg_ref[1:2, :]).astype(BF16)
        ksa[:, HEAD_DIM:2 * HEAD_DIM] = ext_ref[...]
        kwn[...] = _rms(kw_ref[...], kg_ref[2:3, :]).astype(BF16)
        ones = jnp.ones((S, HEAD_DIM), BF16)
        vsa[:, 0:HEAD_DIM] = vs_ref[...].astype(BF16)
        vsa[:, HEAD_DIM:2 * HEAD_DIM] = ones
        vwa[:, 0:HEAD_DIM] = vw_ref[...].astype(BF16)
        vwa[:, HEAD_DIM:2 * HEAD_DIM] = ones

    pos = i * tq + lax.broadcasted_iota(jnp.int32, (tq, 1), 0)
    qg = qg_ref[...]
    q4 = jnp.concatenate([_rms(q_ref[:, h * HEAD_DIM:(h + 1) * HEAD_DIM], qg).astype(BF16)
                          for h in range(H)], axis=0)
    rows = [slice(h * tq, (h + 1) * tq) for h in range(H)]

    def softmax_pv(q_pairs, k, v, tail_bias):
        head = k.shape[0] - tail_bias.shape[1]
        outs, sums = [], []
        for q2 in q_pairs:
            s2 = _dot_nt(q2, k)
            es = []
            for hh in range(2):
                t = s2[hh * tq:(hh + 1) * tq] * c2
                tb = t[:, head:] + tail_bias
                m = jnp.max(tb, axis=-1, keepdims=True)
                if head:
                    ta = t[:, :head]
                    m = jnp.maximum(m, jnp.max(ta, axis=-1, keepdims=True))
                    es.append(jnp.concatenate([jnp.exp2(ta - m).astype(BF16),
                                               jnp.exp2(tb - m).astype(BF16)], axis=1))
                else:
                    es.append(jnp.exp2(tb - m).astype(BF16))
            o2 = _dot(jnp.concatenate(es, axis=0), v)
            outs += [o2[0:tq, 0:HEAD_DIM], o2[tq:2 * tq, 0:HEAD_DIM]]
            sums += [o2[0:tq, HEAD_DIM:2 * HEAD_DIM], o2[tq:2 * tq, HEAD_DIM:2 * HEAD_DIM]]
        return outs, sums

    ncol = lax.broadcasted_iota(jnp.int32, (1, n_cp), 1)
    cmask = (ncol * CMP_STRIDE + (CMP_BLOCK - 1)) <= pos
    s_c4 = _dot_nt(q4, kc_ref[0, 0, 0]) * scale
    ps = [_masked_softmax(s_c4[r], cmask) for r in rows]
    psum = ps[0] + ps[1] + ps[2] + ps[3]
    o_c4 = _dot(jnp.concatenate([p.astype(BF16) for p in ps], axis=0), vc_ref[0, 0, 0])

    ovt = ovt_ref[...]
    p1, p2, p3 = _split3(psum)
    imp_t = (_dot_nt(ovt, p1) + _dot_nt(ovt, p2) + _dot_nt(ovt, p3))[0:n_sblk, :]
    pos_t = i * tq + lax.broadcasted_iota(jnp.int32, (1, tq), 1)
    jrow = lax.broadcasted_iota(jnp.int32, (n_sblk, 1), 0)
    cur_t = pos_t // SEL_BLOCK
    valid_t = jrow * SEL_BLOCK <= pos_t
    forced_t = (jrow == 0) | (jrow == cur_t) | (jrow == cur_t - 1)
    score = jnp.where(valid_t, jnp.where(forced_t, FORCED_SCORE, imp_t), -jnp.inf)
    rank = jnp.zeros((n_sblk, tq), jnp.int32)
    for i2 in range(n_sblk):
        si = score[i2:i2 + 1, :]
        ahead = (si > score) | ((si == score) & (jrow > i2))
        rank = rank + jnp.where(ahead, 1, 0)
    drop_t = jnp.where((rank < SEL_TOPK) & valid_t, 0.0, NEG_INF)
    drop_t = jnp.concatenate([drop_t, jnp.full((LANES - n_sblk, tq), NEG_INF, F32)], axis=0)
    drop = drop_t.T.astype(BF16)
    drop2 = jnp.concatenate([drop, drop], axis=0)
    qa_pairs = [jnp.concatenate([q4[0:2 * tq], drop2], axis=1),
                jnp.concatenate([q4[2 * tq:4 * tq], drop2], axis=1)]

    kst = pl.multiple_of(jnp.clip(i * tq - WINDOW, 0, S - wk), tq)
    wpos = kst + lax.broadcasted_iota(jnp.int32, (1, wk), 1)
    wbias = jnp.where((wpos <= pos) & (wpos > pos - WINDOW), 0.0, NEG_INF)
    o_ws, lws = softmax_pv([q4[0:2 * tq], q4[2 * tq:4 * tq]],
                           kwn[pl.ds(kst, wk), :], vwa[pl.ds(kst, wk), :], wbias)

    def slc_prefix(n):
        kpos = (n - tk) + lax.broadcasted_iota(jnp.int32, (1, tk), 1)
        causal = jnp.where(kpos <= pos, 0.0, NEG_INF)
        outs, sums = softmax_pv(qa_pairs, ksa[0:n, :], vsa[0:n, :], causal)
        for r, o, l in zip(rows, outs, sums):
            acc_scr[r] = o
            l_scr[r] = l

    n_cls = S // tk
    cls = ((i + 1) * tq - 1) // tk
    for c in range(n_cls):
        pl.when(cls == c)(functools.partial(slc_prefix, (c + 1) * tk))

    gates = jax.nn.sigmoid(gate_ref[...])
    for h, r in enumerate(rows):
        c = 3 * h
        o = (o_c4[r] * gates[:, c:c + 1]
             + acc_scr[r] * (gates[:, c + 1:c + 2] / l_scr[r])
             + o_ws[h] * (gates[:, c + 2:c + 3] / lws[h]))
        o_ref[:, h * HEAD_DIM:(h + 1) * HEAD_DIM] = o.astype(BF16)


def _nsa_tables(S):
    n_cp, n_cmp, n_sblk = S // CMP_STRIDE, (S - CMP_BLOCK) // CMP_STRIDE + 1, S // SEL_BLOCK
    n = np.arange(n_cp)[None, :] * CMP_STRIDE
    j = np.arange(LANES)[:, None]
    ovt = ((n < (j + 1) * SEL_BLOCK) & (n + CMP_BLOCK > j * SEL_BLOCK)
           & (np.arange(n_cp)[None, :] < n_cmp) & (j < n_sblk))
    key_block = (np.arange(S) // SEL_BLOCK)[:, None] == np.arange(LANES)[None, :]
    return jnp.asarray(ovt, BF16), jnp.asarray(key_block, BF16)


def _nsa(proj, gate, cmp_kv, q_norm_g, k_norm_g, B, S, tq, tk):
    G = NSA_KV_GROUPS
    H = NSA_GROUP_HEADS
    n_cp = S // CMP_STRIDE
    wk = min(S, WINDOW + tq)
    nq = S // tq
    gw = H * HEAD_DIM
    ovt, key_block = _nsa_tables(S)
    kern = functools.partial(_nsa_kernel, S=S, tq=tq, tk=tk, wk=wk)
    kv_spec = lambda blk: pl.BlockSpec((S, HEAD_DIM), lambda b, g, i: (b, blk + g))
    return pl.pallas_call(
        kern,
        grid=(B, G, nq),
        in_specs=[
            pl.BlockSpec((tq, gw), lambda b, g, i: (b * nq + i, Q_COL0 // gw + g)),
            pl.BlockSpec((1, 1, 1, n_cp, HEAD_DIM), lambda b, g, i: (b, 0, g, 0, 0)),
            pl.BlockSpec((1, 1, 1, n_cp, HEAD_DIM), lambda b, g, i: (b, 1, g, 0, 0)),
            kv_spec(KS_BLK), kv_spec(VS_BLK), kv_spec(KW_BLK), kv_spec(VW_BLK),
            pl.BlockSpec((tq, LANES), lambda b, g, i: (b * nq + i, g)),
            pl.BlockSpec((1, HEAD_DIM), lambda b, g, i: (0, 0)),
            pl.BlockSpec((3, HEAD_DIM), lambda b, g, i: (0, 0)),
            pl.BlockSpec((LANES, n_cp), lambda b, g, i: (0, 0)),
            pl.BlockSpec((S, LANES), lambda b, g, i: (0, 0)),
        ],
        out_specs=pl.BlockSpec((tq, gw), lambda b, g, i: (b * nq + i, g)),
        out_shape=jax.ShapeDtypeStruct((B * S, NSA_Q_WIDTH), BF16),
        scratch_shapes=[
            pltpu.VMEM((S, 2 * HEAD_DIM), BF16), pltpu.VMEM((S, 2 * HEAD_DIM), BF16),
            pltpu.VMEM((S, HEAD_DIM), BF16), pltpu.VMEM((S, 2 * HEAD_DIM), BF16),
            pltpu.VMEM((H * tq, HEAD_DIM), F32), pltpu.VMEM((H * tq, HEAD_DIM), F32)],
        compiler_params=_params(("parallel", "parallel", "arbitrary")),
        name="nsa",
    )(proj, cmp_kv, cmp_kv, proj, proj, proj, proj, gate, q_norm_g, k_norm_g, ovt, key_block)


RET_HB = 8


def _ret_kernel(lg_ref, rq_ref, rk_ref, rv_ref, rg_ref, cos_ref, sin_ref, gg_ref, gb_ref,
                o_ref, r_scr):
    hg = pl.program_id(1)
    C = RET_CHUNK
    scale = HEAD_DIM ** -0.5

    @pl.when(pl.program_id(2) == 0)
    def _():
        r_scr[...] = jnp.zeros(r_scr.shape, F32)

    cos = cos_ref[...]
    sin = sin_ref[...]
    n_col = lax.broadcasted_iota(jnp.int32, (C, 1), 0).astype(F32)
    n_row = lax.broadcasted_iota(jnp.int32, (1, C), 1).astype(F32)
    diff = n_col - n_row
    for h in range(RET_HB):
        sl = slice(h * HEAD_DIM, (h + 1) * HEAD_DIM)
        hh = hg * RET_HB + h
        lg = lg_ref[hh]
        q = rq_ref[:, sl]
        k = rk_ref[:, sl]
        qf = q * cos + pltpu.roll(q, HEAD_DIM // 2, 1) * sin
        kf = (k * cos + pltpu.roll(k, HEAD_DIM // 2, 1) * sin) * scale
        v = rv_ref[:, sl].astype(BF16)
        dec = jnp.where(diff >= 0.0, jnp.exp(lg * jnp.maximum(diff, 0.0)), 0.0)
        xi = jnp.exp(lg * (n_col + 1.0))
        zeta = jnp.exp(lg * (C - 1.0 - n_col))
        cd = jnp.exp(jnp.full((1, HEAD_DIM), lg * float(C), F32))
        qb = qf.astype(BF16)
        r_old = r_scr[h]
        a = _dot_nt(qb, kf.astype(BF16)) * dec
        o = _dot(a.astype(BF16), v) + _dot(qb, r_old.astype(BF16)) * xi
        r_scr[h] = r_old * cd + _dot_tn((kf * zeta).astype(BF16), v)
        mu = jnp.mean(o, axis=-1, keepdims=True)
        d = o - mu
        var = jnp.mean(d * d, axis=-1, keepdims=True)
        y = d * lax.rsqrt(var + GN_EPS) * gg_ref[pl.ds(hh, 1), :] + gb_ref[pl.ds(hh, 1), :]
        gt = rg_ref[:, sl]
        o_ref[:, sl] = (gt * jax.nn.sigmoid(gt) * y).astype(BF16)


def _retention(proj, log_g, cos2, sin2, gn_g, gn_b, B, S):
    C = RET_CHUNK
    n_ch = S // C
    bw = RET_HB * HEAD_DIM
    nhb = RET_HEADS // RET_HB
    spec = lambda k: pl.BlockSpec((C, bw), lambda b, hg, c, lg: (b * n_ch + c, k * nhb + hg))
    grid_spec = pltpu.PrefetchScalarGridSpec(
        num_scalar_prefetch=1,
        grid=(B, nhb, n_ch),
        in_specs=[
            spec(0), spec(1), spec(2), spec(3),
            pl.BlockSpec((C, HEAD_DIM), lambda b, hg, c, lg: (c, 0)),
            pl.BlockSpec((C, HEAD_DIM), lambda b, hg, c, lg: (c, 0)),
            pl.BlockSpec((RET_HEADS, HEAD_DIM), lambda b, hg, c, lg: (0, 0)),
            pl.BlockSpec((RET_HEADS, HEAD_DIM), lambda b, hg, c, lg: (0, 0)),
        ],
        out_specs=pl.BlockSpec((C, bw), lambda b, hg, c, lg: (b * n_ch + c, hg)),
        scratch_shapes=[pltpu.VMEM((RET_HB, HEAD_DIM, HEAD_DIM), F32)],
    )
    return pl.pallas_call(
        _ret_kernel,
        grid_spec=grid_spec,
        out_shape=jax.ShapeDtypeStruct((B * S, RET_WIDTH), BF16),
        compiler_params=_params(("parallel", "parallel", "arbitrary")),
        name="retention",
    )(log_g, proj, proj, proj, proj, cos2, sin2, gn_g, gn_b)


def _lane_max(v):
    return jnp.max(v, axis=-1, keepdims=True)


def _lane_min(v):
    return jnp.min(v, axis=-1, keepdims=True)


def _out_kernel(on_ref, or_ref, w_ref, x_ref, g2_ref, wr_ref, br_ref, tri_ref,
                x1_ref, h2_ref, rt_ref, cnt_ref, cnt_scr):
    @pl.when(pl.program_id(0) == 0)
    def _():
        cnt_scr[...] = jnp.zeros(cnt_scr.shape, F32)

    half = on_ref.shape[1]
    acc = _dot(on_ref[...], w_ref[0:half, :]) + _dot(or_ref[...], w_ref[half:2 * half, :])
    x1 = x_ref[...] + acc
    x1_ref[...] = x1
    h2 = _rms(x1, g2_ref[...])
    h2_ref[...] = h2

    h_hi = h2.astype(BF16)
    h_lo = (h2 - h_hi.astype(F32)).astype(BF16)
    wr = wr_ref[...]
    w_hi = wr.astype(BF16)
    w_lo = (wr - w_hi.astype(F32)).astype(BF16)
    logits = _dot(h_hi, w_hi) + _dot(h_lo, w_hi) + _dot(h_hi, w_lo) + br_ref[...]

    tm = logits.shape[0]
    lane = lax.broadcasted_iota(jnp.int32, (tm, LANES), 1)
    big = jnp.int32(LANES)
    gm = lane < N_GROUPS
    gl = jnp.where(gm, logits, -jnp.inf)
    ge = jnp.where(gm, jnp.exp(gl - _lane_max(gl)), 0.0)
    pg = ge / jnp.sum(ge, axis=-1, keepdims=True)
    g_top = _lane_max(pg)
    g_idx = _lane_min(jnp.where(gm & (pg == g_top), lane, big))
    e0 = N_GROUPS + g_idx * EXPERTS_PER_GROUP
    em = (lane >= e0) & (lane < e0 + EXPERTS_PER_GROUP)
    el = jnp.where(em, logits, -jnp.inf)
    ee = jnp.where(em, jnp.exp(el - _lane_max(el)), 0.0)
    pe = ee / jnp.sum(ee, axis=-1, keepdims=True)
    t1 = _lane_max(jnp.where(em, pe, -1.0))
    i1 = _lane_min(jnp.where(em & (pe == t1), lane, big))
    em2 = em & (lane != i1)
    t2 = _lane_max(jnp.where(em2, pe, -1.0))
    i2 = _lane_min(jnp.where(em2 & (pe == t2), lane, big))
    tsum = t1 + t2
    w1 = g_top * t1 / tsum
    w2 = g_top * t2 / tsum
    e1 = i1 - N_GROUPS
    e2 = i2 - N_GROUPS
    oh1 = jnp.where(lane == e1, 1.0, 0.0)
    oh2 = jnp.where(lane == e2, 1.0, 0.0)
    both = oh1 + oh2
    before = _dot(tri_ref[...], both.astype(BF16)) + cnt_scr[...]
    r1 = jnp.sum(before * oh1, axis=-1, keepdims=True)
    r2 = jnp.sum(before * oh2, axis=-1, keepdims=True)
    cnt_scr[...] = cnt_scr[...] + jnp.sum(both, axis=0, keepdims=True)
    cnt_ref[...] = jnp.broadcast_to(cnt_scr[...], cnt_ref.shape)

    cols = (e1.astype(F32), e2.astype(F32), w1, w2, r1, r2)
    rt = jnp.zeros((tm, LANES), F32)
    for c, v in enumerate(cols):
        rt = jnp.where(lane == c, v, rt)
    rt_ref[...] = rt


def _out_proj(o_nsa, o_ret, w_out, xt, g2, w_router, b_router, tm):
    T, D = xt.shape
    half = o_nsa.shape[1]
    tri = jnp.asarray(np.tril(np.ones((tm, tm), np.float32), -1), BF16)
    return pl.pallas_call(
        _out_kernel,
        grid=(T // tm,),
        in_specs=[
            pl.BlockSpec((tm, half), lambda m: (m, 0)),
            pl.BlockSpec((tm, half), lambda m: (m, 0)),
            pl.BlockSpec((2 * half, D), lambda m: (0, 0), pipeline_mode=pl.Buffered(1)),
            pl.BlockSpec((tm, D), lambda m: (m, 0)),
            pl.BlockSpec((1, D), lambda m: (0, 0)),
            pl.BlockSpec((D, LANES), lambda m: (0, 0)),
            pl.BlockSpec((1, LANES), lambda m: (0, 0)),
            pl.BlockSpec((tm, tm), lambda m: (0, 0)),
        ],
        out_specs=[
            pl.BlockSpec((tm, D), lambda m: (m, 0)),
            pl.BlockSpec((tm, D), lambda m: (m, 0)),
            pl.BlockSpec((tm, LANES), lambda m: (m, 0)),
            pl.BlockSpec((8, LANES), lambda m: (0, 0)),
        ],
        out_shape=[
            jax.ShapeDtypeStruct((T, D), F32),
            jax.ShapeDtypeStruct((T, D), F32),
            jax.ShapeDtypeStruct((T, LANES), F32),
            jax.ShapeDtypeStruct((8, LANES), F32),
        ],
        scratch_shapes=[pltpu.VMEM((1, LANES), F32)],
        compiler_params=_params(("arbitrary",)),
        name="out_proj",
    )(o_nsa, o_ret, w_out, xt, g2, w_router, b_router, tri)


def _row_copy(src, row, dst, slot, sem):
    return pltpu.make_async_copy(src.at[pl.ds(row, 1), :], dst.at[pl.ds(slot, 1), :], sem)


def _dispatch_kernel(e_ref, rk_ref, ps_ref, pad_ref, h2_ref, xs_hbm, zrow, sem, zsem):
    i = pl.program_id(0)
    tm = h2_ref.shape[0]
    K = TOPK_IN_GROUP

    def issue(r, carry):
        a = (i * tm + r) * K
        for k in range(K):
            _row_copy(h2_ref, r, xs_hbm, ps_ref[e_ref[a + k]] + rk_ref[a + k], sem).start()
        return carry

    lax.fori_loop(0, tm, issue, 0, unroll=8)

    @pl.when(i == pl.num_programs(0) - 1)
    def _():
        zrow[...] = jnp.zeros(zrow.shape, F32)
        for e in range(N_EXPERTS):
            lo, hi = pad_ref[e], pad_ref[N_EXPERTS + e]

            def fill(r, carry):
                _row_copy(zrow, 0, xs_hbm, r, zsem).start()
                return carry

            lax.fori_loop(lo, hi, fill, 0)

            def fill_wait(r, carry):
                _row_copy(zrow, 0, xs_hbm, r, zsem).wait()
                return carry

            lax.fori_loop(lo, hi, fill_wait, 0)

        M = zrow.shape[0]

        def tail_copy(t):
            return pltpu.make_async_copy(zrow, xs_hbm.at[pl.ds(pl.multiple_of(t * M, M), M), :], zsem)

        def tail(t, carry):
            tail_copy(t).start()
            return carry

        lax.fori_loop(pad_ref[2 * N_EXPERTS], xs_hbm.shape[0] // M, tail, 0)

        def tail_wait(t, carry):
            tail_copy(t).wait()
            return carry

        lax.fori_loop(pad_ref[2 * N_EXPERTS], xs_hbm.shape[0] // M, tail_wait, 0)

    for k in range(K):
        pltpu.make_async_copy(h2_ref, xs_hbm.at[pl.ds(0, tm), :], sem).wait()


def _dispatch(slots, pad_rows, h2, n_rows, tm):
    T, D = h2.shape
    grid_spec = pltpu.PrefetchScalarGridSpec(
        num_scalar_prefetch=4,
        grid=(T // tm,),
        in_specs=[pl.BlockSpec((tm, D), lambda i, *_: (i, 0))],
        out_specs=pl.BlockSpec(memory_space=pl.ANY),
        scratch_shapes=[pltpu.VMEM((MOE_BLOCK, D), F32), pltpu.SemaphoreType.DMA, pltpu.SemaphoreType.DMA],
    )
    return pl.pallas_call(
        _dispatch_kernel,
        grid_spec=grid_spec,
        out_shape=jax.ShapeDtypeStruct((n_rows, D), F32),
        compiler_params=_params(("arbitrary",)),
        name="dispatch",
    )(*slots, pad_rows, h2)


def _expert_kernel(ts_ref, xs_hbm, wg_ref, wu_ref, wd_ref, y_hbm,
                   xb0, xb1, yb0, yb1, wg_b, wu_b, wd_b, gsem, osem, *, n_blocks):
    e = pl.program_id(0)
    M = MOE_BLOCK
    t0 = ts_ref[e]
    t1 = ts_ref[e + 1]
    n_used = ts_ref[N_EXPERTS]
    xbufs = (xb0, xb1)
    ybufs = (yb0, yb1)

    def rows(t):
        return pl.ds(pl.multiple_of(t * M, M), M)

    def in_copy(t, p):
        return pltpu.make_async_copy(xs_hbm.at[rows(t), :], xbufs[p], gsem.at[p])

    def out_copy(t, p):
        return pltpu.make_async_copy(ybufs[p], y_hbm.at[rows(t), :], osem.at[p])

    @pl.when(e == 0)
    def _():
        in_copy(0, 0).start()

    @pl.when(t1 > t0)
    def _():
        wg_b[...] = wg_ref[0].astype(BF16)
        wu_b[...] = wu_ref[0].astype(BF16)
        wd_b[...] = wd_ref[0].astype(BF16)

    def tile_body(t, p):
        in_copy(t, p).wait()

        @pl.when(t + 1 < n_used)
        def _():
            in_copy(t + 1, 1 - p).start()

        @pl.when(t >= 2)
        def _():
            out_copy(t - 2, p).wait()

        xb = xbufs[p][...].astype(BF16)
        hg = _dot(xb, wg_b[...])
        hu = _dot(xb, wu_b[...])
        hb = (hg * jax.nn.sigmoid(hg) * hu).astype(BF16)
        ybufs[p][...] = _dot(hb, wd_b[...])
        out_copy(t, p).start()

    def tile(t, carry):
        parity = lax.rem(t, 2)
        for p in range(2):
            pl.when(parity == p)(functools.partial(tile_body, t, p))
        return carry

    lax.fori_loop(t0, t1, tile, 0)

    @pl.when(e == N_EXPERTS - 1)
    def _():
        parity = lax.rem(n_used, 2)
        for p in range(2):
            @pl.when(parity == p)
            def _(p=p):
                out_copy(n_used - 1, 1 - p).wait()

                @pl.when(n_used >= 2)
                def _():
                    out_copy(n_used - 2, p).wait()

        yb0[...] = jnp.zeros(yb0.shape, F32)

        def clear(t, carry):
            out_copy(t, 0).start()
            return carry

        lax.fori_loop(n_used, n_blocks, clear, 0)

        def clear_wait(t, carry):
            out_copy(t, 0).wait()
            return carry

        lax.fori_loop(n_used, n_blocks, clear_wait, 0)


def _experts(xs, tile_start, w_gate, w_up, w_down):
    n_rows, D = xs.shape
    n_blocks = n_rows // MOE_BLOCK
    grid_spec = pltpu.PrefetchScalarGridSpec(
        num_scalar_prefetch=1,
        grid=(N_EXPERTS,),
        in_specs=[
            pl.BlockSpec(memory_space=pl.ANY),
            pl.BlockSpec((1, D, D_EXPERT), lambda e, ts: (e, 0, 0)),
            pl.BlockSpec((1, D, D_EXPERT), lambda e, ts: (e, 0, 0)),
            pl.BlockSpec((1, D_EXPERT, D), lambda e, ts: (e, 0, 0)),
        ],
        out_specs=pl.BlockSpec(memory_space=pl.ANY),
        scratch_shapes=[pltpu.VMEM((MOE_BLOCK, D), F32)] * 4 + [
            pltpu.VMEM((D, D_EXPERT), BF16), pltpu.VMEM((D, D_EXPERT), BF16),
            pltpu.VMEM((D_EXPERT, D), BF16),
            pltpu.SemaphoreType.DMA((2,)), pltpu.SemaphoreType.DMA((2,)),
        ],
    )
    return pl.pallas_call(
        functools.partial(_expert_kernel, n_blocks=n_blocks),
        grid_spec=grid_spec,
        out_shape=jax.ShapeDtypeStruct((n_rows, D), F32),
        compiler_params=_params(("arbitrary",)),
        name="experts",
    )(tile_start, xs, w_gate, w_up, w_down)


def _combine_kernel(e_ref, rk_ref, ps_ref, y_hbm, x1_ref, rt_ref, o_ref, ybuf, sems):
    i = pl.program_id(0)
    n = pl.num_programs(0)
    tm = x1_ref.shape[0]
    K = TOPK_IN_GROUP
    slot = lax.rem(i, 2)

    def issue_tile(t, s):
        def issue(r, carry):
            a = (t * tm + r) * K
            for k in range(K):
                _row_copy(y_hbm, ps_ref[e_ref[a + k]] + rk_ref[a + k], ybuf.at[s, k], r,
                          sems.at[s]).start(priority=k)
            return carry

        lax.fori_loop(0, tm, issue, 0, unroll=8)

    @pl.when(i == 0)
    def _():
        issue_tile(0, 0)

    @pl.when(i + 1 < n)
    def _():
        issue_tile(i + 1, 1 - slot)

    for k in range(K):
        pltpu.make_async_copy(y_hbm.at[pl.ds(0, tm), :], ybuf.at[slot, k], sems.at[slot]).wait()
    w = rt_ref[...]
    yb = ybuf[slot]
    o_ref[...] = x1_ref[...] + (yb[0] * w[:, K:K + 1] + yb[1] * w[:, K + 1:K + 2])


def _combine(slots, y_rows, x1, rt, tm):
    T, D = x1.shape
    grid_spec = pltpu.PrefetchScalarGridSpec(
        num_scalar_prefetch=3,
        grid=(T // tm,),
        in_specs=[
            pl.BlockSpec(memory_space=pl.ANY),
            pl.BlockSpec((tm, D), lambda i, *_: (i, 0)),
            pl.BlockSpec((tm, LANES), lambda i, *_: (i, 0)),
        ],
        out_specs=pl.BlockSpec((tm, D), lambda i, *_: (i, 0)),
        scratch_shapes=[pltpu.VMEM((2, TOPK_IN_GROUP, tm, D), F32), pltpu.SemaphoreType.DMA((2,))],
    )
    return pl.pallas_call(
        _combine_kernel,
        grid_spec=grid_spec,
        out_shape=jax.ShapeDtypeStruct((T, D), F32),
        compiler_params=_params(("arbitrary",)),
        name="combine",
    )(*slots, y_rows, x1, rt)


def _block_layout(rt, counts, T):
    K = TOPK_IN_GROUP
    n_rows = (K * T + N_EXPERTS * (MOE_BLOCK - 1) + MOE_BLOCK - 1) // MOE_BLOCK * MOE_BLOCK
    n_blocks = n_rows // MOE_BLOCK
    cnt = counts[0, :N_EXPERTS].astype(jnp.int32)
    padded = (cnt + MOE_BLOCK - 1) // MOE_BLOCK * MOE_BLOCK
    pad_end = jnp.cumsum(padded)
    pad_start = pad_end - padded
    slots = (rt[:, 0:K].astype(jnp.int32).reshape(-1), rt[:, 2 * K:3 * K].astype(jnp.int32).reshape(-1),
             pad_start.astype(jnp.int32))
    tile_start = (jnp.concatenate([pad_start, pad_end[-1:]]) // MOE_BLOCK).astype(jnp.int32)
    pad_rows = jnp.concatenate([pad_start + cnt, pad_end, pad_end[-1:] // MOE_BLOCK]).astype(jnp.int32)
    return tile_start, slots, pad_rows, n_rows


def _layer(x, norm1_g, w_in, cmp_pos, cmp_w1, cmp_w2, q_norm_g, k_norm_g, ret_gn_g, ret_gn_b,
           w_out, norm2_g, w_rg, b_rg, w_re, b_re, w_eg, w_eu, w_ed, tiles):
    B, S, D = x.shape
    T = B * S
    xt = x.reshape(T, D)

    w_ret = w_in[:, GATE_COL0 + NSA_GATE_WIDTH:].astype(BF16)
    w_qkv = w_in[:, :GATE_COL0].astype(BF16)
    gpg = NSA_GATE_WIDTH // NSA_KV_GROUPS
    w_gate = jnp.concatenate(
        [jnp.pad(w_in[:, GATE_COL0 + g * gpg:GATE_COL0 + (g + 1) * gpg], ((0, 0), (0, LANES - gpg)))
         for g in range(NSA_KV_GROUPS)], axis=1).astype(BF16)
    proj, gate = _in_proj(xt, norm1_g.reshape(1, D), w_ret, w_qkv, w_gate, tiles["tm_in"], tiles["tn_in"])

    cmp_kv = _compress(proj, cmp_pos, cmp_w1, cmp_w2, k_norm_g, B, S)
    o_nsa = _nsa(proj, gate, cmp_kv, q_norm_g.reshape(1, HEAD_DIM), k_norm_g, B, S, tiles["tq"], tiles["tk"])

    half = HEAD_DIM // 2
    inv_freq = ROPE_BASE ** (-jnp.arange(half, dtype=F32) / half)
    ang = jnp.arange(S, dtype=F32)[:, None] * inv_freq[None, :]
    cos2 = jnp.concatenate([jnp.cos(ang), jnp.cos(ang)], axis=1)
    sin2 = jnp.concatenate([-jnp.sin(ang), jnp.sin(ang)], axis=1)
    log_g = jnp.log1p(-jnp.exp2(-5.0 - jnp.arange(RET_HEADS, dtype=F32)))
    o_ret = _retention(proj, log_g, cos2, sin2, ret_gn_g, ret_gn_b, B, S)

    n_r = N_GROUPS + N_EXPERTS
    w_router = jnp.pad(jnp.concatenate([w_rg, w_re], axis=1), ((0, 0), (0, LANES - n_r)))
    b_router = jnp.pad(jnp.concatenate([b_rg, b_re]), (0, LANES - n_r)).reshape(1, LANES)
    x1, h2, rt, counts = _out_proj(o_nsa, o_ret, w_out.astype(BF16), xt, norm2_g.reshape(1, D),
                                   w_router, b_router, tiles["tm_out"])

    tile_start, slots, pad_rows, n_rows = _block_layout(rt, counts, T)
    xs = _dispatch(slots, pad_rows, h2, n_rows, tiles["tm_dsp"])
    y_rows = _experts(xs, tile_start, w_eg, w_eu, w_ed)
    out = _combine(slots, y_rows, x1, rt, tiles["tm_cmb"])
    return out.reshape(B, S, D)


def _tiles(T, S):
    return {
        "tm_in": min(1024, T), "tn_in": 512,
        "tq": min(256, S), "tk": min(512, S),
        "tm_out": min(512, T),
        "tm_dsp": min(1024, T),
        "tm_cmb": min(256, T),
    }


def kernel(x, norm1_g, w_in, cmp_pos, cmp_w1, cmp_w2, q_norm_g, k_norm_g, ret_gn_g, ret_gn_b, w_out, norm2_g, w_router_group, b_router_group, w_router_expert, b_router_expert, w_exp_gate, w_exp_up, w_exp_down):
    B, S, _ = x.shape
    tiles = _tiles(B * S, S)
    for l in range(norm1_g.shape[0]):
        x = _layer(x, norm1_g[l], w_in[l], cmp_pos[l], cmp_w1[l], cmp_w2[l], q_norm_g[l], k_norm_g[l],
                   ret_gn_g[l], ret_gn_b[l], w_out[l], norm2_g[l], w_router_group[l], b_router_group[l],
                   w_router_expert[l], b_router_expert[l], w_exp_gate[l], w_exp_up[l], w_exp_down[l], tiles)
    return x
```

```python
import functools

import numpy as np
import jax
import jax.numpy as jnp
from jax import lax
from jax.experimental import pallas as pl
from jax.experimental.pallas import tpu as pltpu

F32 = jnp.float32
BF16 = jnp.bfloat16

D_MODEL = 2048
NSA_HEADS = 8
NSA_KV_GROUPS = 2
NSA_GROUP_HEADS = NSA_HEADS // NSA_KV_GROUPS
HEAD_DIM = 128
RET_HEADS = 8
CMP_BLOCK = 32
CMP_STRIDE = 16
SEL_BLOCK = 64
SEL_TOPK = 8
WINDOW = 512
RET_CHUNK = 128
ROPE_BASE = 10000.0
N_GROUPS = 4
EXPERTS_PER_GROUP = 8
N_EXPERTS = N_GROUPS * EXPERTS_PER_GROUP
TOPK_IN_GROUP = 2
D_EXPERT = 512
MOE_BLOCK = 256
RMS_EPS = 1e-6
GN_EPS = 1e-5
NEG_INF = -1e30
FORCED_SCORE = 1e6
LOG2E = 1.4426950408889634

NSA_Q_WIDTH = NSA_HEADS * HEAD_DIM
NSA_KV_WIDTH = NSA_KV_GROUPS * HEAD_DIM
NSA_GATE_WIDTH = 3 * NSA_HEADS
RET_WIDTH = RET_HEADS * HEAD_DIM
GATE_COL0 = NSA_Q_WIDTH + 6 * NSA_KV_WIDTH
LANES = 128
VMEM_LIMIT = 56 * 1024 * 1024

Q_COL0 = 4 * RET_WIDTH
KV_COL0 = Q_COL0 + NSA_Q_WIDTH
KC_BLK, VC_BLK, KS_BLK, VS_BLK, KW_BLK, VW_BLK = [KV_COL0 // LANES + 2 * t for t in range(6)]


def _rms(xf, g):
    return xf * lax.rsqrt(jnp.mean(xf * xf, axis=-1, keepdims=True) + RMS_EPS) * g


def _dot(a, b):
    return jnp.dot(a, b, preferred_element_type=F32)


def _dot_nt(a, b):
    return lax.dot_general(a, b, (((1,), (1,)), ((), ())), preferred_element_type=F32)


def _dot_tn(a, b):
    return lax.dot_general(a, b, (((0,), (0,)), ((), ())), preferred_element_type=F32)


def _split3(p):
    p1 = p.astype(BF16)
    r1 = p - p1.astype(F32)
    p2 = r1.astype(BF16)
    p3 = (r1 - p2.astype(F32)).astype(BF16)
    return p1, p2, p3


def _params(sem):
    return pltpu.CompilerParams(dimension_semantics=sem, vmem_limit_bytes=VMEM_LIMIT)


def _in_proj_kernel(x_ref, g_ref, w_ref, wg_ref, o_ref, og_ref, h_scr):
    @pl.when(pl.program_id(1) == 0)
    def _():
        h = _rms(x_ref[...], g_ref[...]).astype(BF16)
        h_scr[...] = h
        og_ref[...] = _dot(h, wg_ref[...])

    o_ref[...] = _dot(h_scr[...], w_ref[...])


def _in_proj(xt, g1, w_main, w_gate, tm, tn):
    T, D = xt.shape
    n_main = w_main.shape[1]
    n_gate = w_gate.shape[1]
    return pl.pallas_call(
        _in_proj_kernel,
        grid=(T // tm, n_main // tn),
        in_specs=[
            pl.BlockSpec((tm, D), lambda m, n: (m, 0), pipeline_mode=pl.Buffered(1)),
            pl.BlockSpec((1, D), lambda m, n: (0, 0)),
            pl.BlockSpec((D, tn), lambda m, n: (0, n)),
            pl.BlockSpec((D, n_gate), lambda m, n: (0, 0)),
        ],
        out_specs=[
            pl.BlockSpec((tm, tn), lambda m, n: (m, n)),
            pl.BlockSpec((tm, n_gate), lambda m, n: (m, 0)),
        ],
        out_shape=[
            jax.ShapeDtypeStruct((T, n_main), F32),
            jax.ShapeDtypeStruct((T, n_gate), F32),
        ],
        scratch_shapes=[pltpu.VMEM((tm, D), BF16)],
        compiler_params=_params(("parallel", "arbitrary")),
        name="in_proj",
    )(xt, g1, w_main, w_gate)


def _compress_kernel(x_ref, pos_ref, w1_ref, w2_ref, kg_ref, o_ref, xs_scr, *, S, n_pad):
    kv = pl.program_id(1)
    xs_scr[0:S, :] = x_ref[...]
    xs_scr[S:S + CMP_STRIDE, :] = jnp.zeros((CMP_STRIDE, HEAD_DIM), F32)
    acc = jnp.zeros((n_pad, HEAD_DIM), F32)
    for l in range(CMP_BLOCK):
        tb = xs_scr[pl.ds(l, n_pad, stride=CMP_STRIDE), :] + pos_ref[0, l:l + 1, :]
        acc = acc + _dot(tb.astype(BF16), w1_ref[0, l].astype(BF16))
    hid = acc * jax.nn.sigmoid(acc)
    out = _dot(hid.astype(BF16), w2_ref[0].astype(BF16))
    normed = _rms(out, kg_ref[0:1, :])
    o_ref[0, 0, 0] = jnp.where(kv == 0, normed, out).astype(BF16)


def _compress(proj, cmp_pos, cmp_w1, cmp_w2, k_norm_g, B, S):
    n_pad = S // CMP_STRIDE
    G = NSA_KV_GROUPS
    kern = functools.partial(_compress_kernel, S=S, n_pad=n_pad)
    return pl.pallas_call(
        kern,
        grid=(B, 2, G),
        in_specs=[
            pl.BlockSpec((S, HEAD_DIM), lambda b, kv, g: (b, KC_BLK + 2 * kv + g)),
            pl.BlockSpec((1, CMP_BLOCK, HEAD_DIM), lambda b, kv, g: (kv, 0, 0)),
            pl.BlockSpec((1, CMP_BLOCK, HEAD_DIM, HEAD_DIM), lambda b, kv, g: (kv, 0, 0, 0)),
            pl.BlockSpec((1, HEAD_DIM, HEAD_DIM), lambda b, kv, g: (kv, 0, 0)),
            pl.BlockSpec((3, HEAD_DIM), lambda b, kv, g: (0, 0)),
        ],
        out_specs=pl.BlockSpec((1, 1, 1, n_pad, HEAD_DIM), lambda b, kv, g: (b, kv, g, 0, 0)),
        out_shape=jax.ShapeDtypeStruct((B, 2, G, n_pad, HEAD_DIM), BF16),
        scratch_shapes=[pltpu.VMEM((S + CMP_STRIDE, HEAD_DIM), F32)],
        compiler_params=_params(("parallel", "parallel", "parallel")),
        name="compress",
    )(proj, cmp_pos, cmp_w1, cmp_w2, k_norm_g)


def _masked_softmax(s, msk):
    sm = jnp.where(msk, s, NEG_INF)
    m = jnp.max(sm, axis=-1, keepdims=True)
    e = jnp.where(msk, jnp.exp(sm - m), 0.0)
    den = jnp.sum(e, axis=-1, keepdims=True)
    return e / jnp.where(den > 0.0, den, 1.0)


def _nsa_kernel(q_ref, kc_ref, vc_ref, ks_ref, vs_ref, kw_ref, vw_ref, gate_ref, qg_ref, kg_ref,
                ovt_ref, ext_ref, o_ref, ksa, vsa, kwn, vwa, l_scr, acc_scr, *, S, tq, tk, wk):
    i = pl.program_id(2)
    H = NSA_GROUP_HEADS
    scale = HEAD_DIM ** -0.5
    c2 = scale * LOG2E
    n_cp = S // CMP_STRIDE
    n_sblk = S // SEL_BLOCK

    @pl.when(i == 0)
    def _():
        ksa[:, 0:HEAD_DIM] = _rms(ks_ref[...], kg_ref[1:2, :]).astype(BF16)
        ksa[:, HEAD_DIM:2 * HEAD_DIM] = ext_ref[...]
        kwn[...] = _rms(kw_ref[...], kg_ref[2:3, :]).astype(BF16)
        ones = jnp.ones((S, HEAD_DIM), BF16)
        vsa[:, 0:HEAD_DIM] = vs_ref[...].astype(BF16)
        vsa[:, HEAD_DIM:2 * HEAD_DIM] = ones
        vwa[:, 0:HEAD_DIM] = vw_ref[...].astype(BF16)
        vwa[:, HEAD_DIM:2 * HEAD_DIM] = ones

    pos = i * tq + lax.broadcasted_iota(jnp.int32, (tq, 1), 0)
    qg = qg_ref[...]
    q4 = jnp.concatenate([_rms(q_ref[:, h * HEAD_DIM:(h + 1) * HEAD_DIM], qg).astype(BF16)
                          for h in range(H)], axis=0)
    rows = [slice(h * tq, (h + 1) * tq) for h in range(H)]

    def softmax_pv(q_chains, k, v, tail_bias):
        head = k.shape[0] - tail_bias.shape[1]
        outs, sums = [], []
        for q2 in q_chains:
            s2 = _dot_nt(q2, k)
            es = []
            hpc = q2.shape[0] // tq
            for hh in range(hpc):
                t = s2[hh * tq:(hh + 1) * tq] * c2
                tb = t[:, head:] + tail_bias
                m = jnp.max(tb, axis=-1, keepdims=True)
                if head:
                    ta = t[:, :head]
                    m = jnp.maximum(m, jnp.max(ta, axis=-1, keepdims=True))
                    es.append(jnp.concatenate([jnp.exp2(ta - m).astype(BF16),
                                               jnp.exp2(tb - m).astype(BF16)], axis=1))
                else:
                    es.append(jnp.exp2(tb - m).astype(BF16))
            o2 = _dot(es[0] if hpc == 1 else jnp.concatenate(es, axis=0), v)
            outs += [o2[hh * tq:(hh + 1) * tq, 0:HEAD_DIM] for hh in range(hpc)]
            sums += [o2[hh * tq:(hh + 1) * tq, HEAD_DIM:2 * HEAD_DIM] for hh in range(hpc)]
        return outs, sums

    ncol = lax.broadcasted_iota(jnp.int32, (1, n_cp), 1)
    cmask = (ncol * CMP_STRIDE + (CMP_BLOCK - 1)) <= pos
    s_c4 = _dot_nt(q4, kc_ref[0, 0, 0]) * scale
    ps = [_masked_softmax(s_c4[r], cmask) for r in rows]
    psum = ps[0] + ps[1] + ps[2] + ps[3]
    o_c4 = _dot(jnp.concatenate([p.astype(BF16) for p in ps], axis=0), vc_ref[0, 0, 0])

    ovt = ovt_ref[...]
    p1, p2, p3 = _split3(psum)
    imp_t = (_dot_nt(ovt, p1) + _dot_nt(ovt, p2) + _dot_nt(ovt, p3))[0:n_sblk, :]
    pos_t = i * tq + lax.broadcasted_iota(jnp.int32, (1, tq), 1)
    jrow = lax.broadcasted_iota(jnp.int32, (n_sblk, 1), 0)
    cur_t = pos_t // SEL_BLOCK
    valid_t = jrow * SEL_BLOCK <= pos_t
    forced_t = (jrow == 0) | (jrow == cur_t) | (jrow == cur_t - 1)
    score = jnp.where(valid_t, jnp.where(forced_t, FORCED_SCORE, imp_t), -jnp.inf)
    rank = jnp.zeros((n_sblk, tq), jnp.int32)
    for i2 in range(n_sblk):
        si = score[i2:i2 + 1, :]
        ahead = (si > score) | ((si == score) & (jrow > i2))
        rank = rank + jnp.where(ahead, 1, 0)
    drop_t = jnp.where((rank < SEL_TOPK) & valid_t, 0.0, NEG_INF)
    drop_t = jnp.concatenate([drop_t, jnp.full((LANES - n_sblk, tq), NEG_INF, F32)], axis=0)
    drop = drop_t.T.astype(BF16)
    qa_pairs = [jnp.concatenate([q4[r], drop], axis=1) for r in rows]

    kst = pl.multiple_of(jnp.clip(i * tq - WINDOW, 0, S - wk), tq)
    wpos = kst + lax.broadcasted_iota(jnp.int32, (1, wk), 1)
    wbias = jnp.where((wpos <= pos) & (wpos > pos - WINDOW), 0.0, NEG_INF)
    o_ws, lws = softmax_pv([q4[r] for r in rows],
                           kwn[pl.ds(kst, wk), :], vwa[pl.ds(kst, wk), :], wbias)

    def slc_prefix(n):
        kpos = (n - tk) + lax.broadcasted_iota(jnp.int32, (1, tk), 1)
        causal = jnp.where(kpos <= pos, 0.0, NEG_INF)
        outs, sums = softmax_pv(qa_pairs, ksa[0:n, :], vsa[0:n, :], causal)
        for r, o, l in zip(rows, outs, sums):
            acc_scr[r] = o
            l_scr[r] = l

    n_cls = S // tk
    cls = ((i + 1) * tq - 1) // tk
    for c in range(n_cls):
        pl.when(cls == c)(functools.partial(slc_prefix, (c + 1) * tk))

    gates = jax.nn.sigmoid(gate_ref[...])
    for h, r in enumerate(rows):
        c = 3 * h
        o = (o_c4[r] * gates[:, c:c + 1]
             + acc_scr[r] * (gates[:, c + 1:c + 2] / l_scr[r])
             + o_ws[h] * (gates[:, c + 2:c + 3] / lws[h]))
        o_ref[:, h * HEAD_DIM:(h + 1) * HEAD_DIM] = o.astype(BF16)


def _nsa_tables(S):
    n_cp, n_cmp, n_sblk = S // CMP_STRIDE, (S - CMP_BLOCK) // CMP_STRIDE + 1, S // SEL_BLOCK
    n = np.arange(n_cp)[None, :] * CMP_STRIDE
    j = np.arange(LANES)[:, None]
    ovt = ((n < (j + 1) * SEL_BLOCK) & (n + CMP_BLOCK > j * SEL_BLOCK)
           & (np.arange(n_cp)[None, :] < n_cmp) & (j < n_sblk))
    key_block = (np.arange(S) // SEL_BLOCK)[:, None] == np.arange(LANES)[None, :]
    return jnp.asarray(ovt, BF16), jnp.asarray(key_block, BF16)


def _nsa(proj, gate, cmp_kv, q_norm_g, k_norm_g, B, S, tq, tk):
    G = NSA_KV_GROUPS
    H = NSA_GROUP_HEADS
    n_cp = S // CMP_STRIDE
    wk = min(S, WINDOW + tq)
    nq = S // tq
    gw = H * HEAD_DIM
    ovt, key_block = _nsa_tables(S)
    kern = functools.partial(_nsa_kernel, S=S, tq=tq, tk=tk, wk=wk)
    kv_spec = lambda blk: pl.BlockSpec((S, HEAD_DIM), lambda b, g, i: (b, blk + g))
    return pl.pallas_call(
        kern,
        grid=(B, G, nq),
        in_specs=[
            pl.BlockSpec((tq, gw), lambda b, g, i: (b * nq + i, Q_COL0 // gw + g)),
            pl.BlockSpec((1, 1, 1, n_cp, HEAD_DIM), lambda b, g, i: (b, 0, g, 0, 0)),
            pl.BlockSpec((1, 1, 1, n_cp, HEAD_DIM), lambda b, g, i: (b, 1, g, 0, 0)),
            kv_spec(KS_BLK), kv_spec(VS_BLK), kv_spec(KW_BLK), kv_spec(VW_BLK),
            pl.BlockSpec((tq, LANES), lambda b, g, i: (b * nq + i, g)),
            pl.BlockSpec((1, HEAD_DIM), lambda b, g, i: (0, 0)),
            pl.BlockSpec((3, HEAD_DIM), lambda b, g, i: (0, 0)),
            pl.BlockSpec((LANES, n_cp), lambda b, g, i: (0, 0)),
            pl.BlockSpec((S, LANES), lambda b, g, i: (0, 0)),
        ],
        out_specs=pl.BlockSpec((tq, gw), lambda b, g, i: (b * nq + i, g)),
        out_shape=jax.ShapeDtypeStruct((B * S, NSA_Q_WIDTH), BF16),
        scratch_shapes=[
            pltpu.VMEM((S, 2 * HEAD_DIM), BF16), pltpu.VMEM((S, 2 * HEAD_DIM), BF16),
            pltpu.VMEM((S, HEAD_DIM), BF16), pltpu.VMEM((S, 2 * HEAD_DIM), BF16),
            pltpu.VMEM((H * tq, HEAD_DIM), F32), pltpu.VMEM((H * tq, HEAD_DIM), F32)],
        compiler_params=_params(("parallel", "parallel", "arbitrary")),
        name="nsa",
    )(proj, cmp_kv, cmp_kv, proj, proj, proj, proj, gate, q_norm_g, k_norm_g, ovt, key_block)


RET_HB = 8


def _ret_kernel(lg_ref, rq_ref, rk_ref, rv_ref, rg_ref, cos_ref, sin_ref, gg_ref, gb_ref,
                o_ref, r_scr):
    hg = pl.program_id(1)
    C = RET_CHUNK
    scale = HEAD_DIM ** -0.5

    @pl.when(pl.program_id(2) == 0)
    def _():
        r_scr[...] = jnp.zeros(r_scr.shape, F32)

    cos = cos_ref[...]
    sin = sin_ref[...]
    n_col = lax.broadcasted_iota(jnp.int32, (C, 1), 0).astype(F32)
    n_row = lax.broadcasted_iota(jnp.int32, (1, C), 1).astype(F32)
    diff = n_col - n_row
    for h in range(RET_HB):
        sl = slice(h * HEAD_DIM, (h + 1) * HEAD_DIM)
        hh = hg * RET_HB + h
        lg = lg_ref[hh]
        q = rq_ref[:, sl]
        k = rk_ref[:, sl]
        qf = q * cos + pltpu.roll(q, HEAD_DIM // 2, 1) * sin
        kf = (k * cos + pltpu.roll(k, HEAD_DIM // 2, 1) * sin) * scale
        v = rv_ref[:, sl].astype(BF16)
        dec = jnp.where(diff >= 0.0, jnp.exp(lg * jnp.maximum(diff, 0.0)), 0.0)
        xi = jnp.exp(lg * (n_col + 1.0))
        zeta = jnp.exp(lg * (C - 1.0 - n_col))
        cd = jnp.exp(jnp.full((1, HEAD_DIM), lg * float(C), F32))
        qb = qf.astype(BF16)
        r_old = r_scr[h]
        a = _dot_nt(qb, kf.astype(BF16)) * dec
        o = _dot(a.astype(BF16), v) + _dot(qb, r_old.astype(BF16)) * xi
        r_scr[h] = r_old * cd + _dot_tn((kf * zeta).astype(BF16), v)
        mu = jnp.mean(o, axis=-1, keepdims=True)
        d = o - mu
        var = jnp.mean(d * d, axis=-1, keepdims=True)
        y = d * lax.rsqrt(var + GN_EPS) * gg_ref[pl.ds(hh, 1), :] + gb_ref[pl.ds(hh, 1), :]
        gt = rg_ref[:, sl]
        o_ref[:, sl] = (gt * jax.nn.sigmoid(gt) * y).astype(BF16)


def _retention(proj, log_g, cos2, sin2, gn_g, gn_b, B, S):
    C = RET_CHUNK
    n_ch = S // C
    bw = RET_HB * HEAD_DIM
    nhb = RET_HEADS // RET_HB
    spec = lambda k: pl.BlockSpec((C, bw), lambda b, hg, c, lg: (b * n_ch + c, k * nhb + hg))
    grid_spec = pltpu.PrefetchScalarGridSpec(
        num_scalar_prefetch=1,
        grid=(B, nhb, n_ch),
        in_specs=[
            spec(0), spec(1), spec(2), spec(3),
            pl.BlockSpec((C, HEAD_DIM), lambda b, hg, c, lg: (c, 0)),
            pl.BlockSpec((C, HEAD_DIM), lambda b, hg, c, lg: (c, 0)),
            pl.BlockSpec((RET_HEADS, HEAD_DIM), lambda b, hg, c, lg: (0, 0)),
            pl.BlockSpec((RET_HEADS, HEAD_DIM), lambda b, hg, c, lg: (0, 0)),
        ],
        out_specs=pl.BlockSpec((C, bw), lambda b, hg, c, lg: (b * n_ch + c, hg)),
        scratch_shapes=[pltpu.VMEM((RET_HB, HEAD_DIM, HEAD_DIM), F32)],
    )
    return pl.pallas_call(
        _ret_kernel,
        grid_spec=grid_spec,
        out_shape=jax.ShapeDtypeStruct((B * S, RET_WIDTH), BF16),
        compiler_params=_params(("parallel", "parallel", "arbitrary")),
        name="retention",
    )(log_g, proj, proj, proj, proj, cos2, sin2, gn_g, gn_b)


def _lane_max(v):
    return jnp.max(v, axis=-1, keepdims=True)


def _lane_min(v):
    return jnp.min(v, axis=-1, keepdims=True)


def _out_kernel(on_ref, or_ref, w_ref, x_ref, g2_ref, wr_ref, br_ref, tri_ref,
                x1_ref, h2_ref, rt_ref, cnt_ref, cnt_scr):
    @pl.when(pl.program_id(0) == 0)
    def _():
        cnt_scr[...] = jnp.zeros(cnt_scr.shape, F32)

    half = on_ref.shape[1]
    acc = _dot(on_ref[...], w_ref[0:half, :]) + _dot(or_ref[...], w_ref[half:2 * half, :])
    x1 = x_ref[...] + acc
    x1_ref[...] = x1
    h2 = _rms(x1, g2_ref[...])
    h2_ref[...] = h2

    h_hi = h2.astype(BF16)
    h_lo = (h2 - h_hi.astype(F32)).astype(BF16)
    wr = wr_ref[...]
    w_hi = wr.astype(BF16)
    w_lo = (wr - w_hi.astype(F32)).astype(BF16)
    logits = _dot(h_hi, w_hi) + _dot(h_lo, w_hi) + _dot(h_hi, w_lo) + br_ref[...]

    tm = logits.shape[0]
    lane = lax.broadcasted_iota(jnp.int32, (tm, LANES), 1)
    big = jnp.int32(LANES)
    gm = lane < N_GROUPS
    gl = jnp.where(gm, logits, -jnp.inf)
    ge = jnp.where(gm, jnp.exp(gl - _lane_max(gl)), 0.0)
    pg = ge / jnp.sum(ge, axis=-1, keepdims=True)
    g_top = _lane_max(pg)
    g_idx = _lane_min(jnp.where(gm & (pg == g_top), lane, big))
    e0 = N_GROUPS + g_idx * EXPERTS_PER_GROUP
    em = (lane >= e0) & (lane < e0 + EXPERTS_PER_GROUP)
    el = jnp.where(em, logits, -jnp.inf)
    ee = jnp.where(em, jnp.exp(el - _lane_max(el)), 0.0)
    pe = ee / jnp.sum(ee, axis=-1, keepdims=True)
    t1 = _lane_max(jnp.where(em, pe, -1.0))
    i1 = _lane_min(jnp.where(em & (pe == t1), lane, big))
    em2 = em & (lane != i1)
    t2 = _lane_max(jnp.where(em2, pe, -1.0))
    i2 = _lane_min(jnp.where(em2 & (pe == t2), lane, big))
    tsum = t1 + t2
    w1 = g_top * t1 / tsum
    w2 = g_top * t2 / tsum
    e1 = i1 - N_GROUPS
    e2 = i2 - N_GROUPS
    oh1 = jnp.where(lane == e1, 1.0, 0.0)
    oh2 = jnp.where(lane == e2, 1.0, 0.0)
    both = oh1 + oh2
    before = _dot(tri_ref[...], both.astype(BF16)) + cnt_scr[...]
    r1 = jnp.sum(before * oh1, axis=-1, keepdims=True)
    r2 = jnp.sum(before * oh2, axis=-1, keepdims=True)
    cnt_scr[...] = cnt_scr[...] + jnp.sum(both, axis=0, keepdims=True)
    cnt_ref[...] = jnp.broadcast_to(cnt_scr[...], cnt_ref.shape)

    cols = (e1.astype(F32), e2.astype(F32), w1, w2, r1, r2)
    rt = jnp.zeros((tm, LANES), F32)
    for c, v in enumerate(cols):
        rt = jnp.where(lane == c, v, rt)
    rt_ref[...] = rt


def _out_proj(o_nsa, o_ret, w_out, xt, g2, w_router, b_router, tm):
    T, D = xt.shape
    half = o_nsa.shape[1]
    tri = jnp.asarray(np.tril(np.ones((tm, tm), np.float32), -1), BF16)
    return pl.pallas_call(
        _out_kernel,
        grid=(T // tm,),
        in_specs=[
            pl.BlockSpec((tm, half), lambda m: (m, 0)),
            pl.BlockSpec((tm, half), lambda m: (m, 0)),
            pl.BlockSpec((2 * half, D), lambda m: (0, 0), pipeline_mode=pl.Buffered(1)),
            pl.BlockSpec((tm, D), lambda m: (m, 0)),
            pl.BlockSpec((1, D), lambda m: (0, 0)),
            pl.BlockSpec((D, LANES), lambda m: (0, 0)),
            pl.BlockSpec((1, LANES), lambda m: (0, 0)),
            pl.BlockSpec((tm, tm), lambda m: (0, 0)),
        ],
        out_specs=[
            pl.BlockSpec((tm, D), lambda m: (m, 0)),
            pl.BlockSpec((tm, D), lambda m: (m, 0)),
            pl.BlockSpec((tm, LANES), lambda m: (m, 0)),
            pl.BlockSpec((8, LANES), lambda m: (0, 0)),
        ],
        out_shape=[
            jax.ShapeDtypeStruct((T, D), F32),
            jax.ShapeDtypeStruct((T, D), F32),
            jax.ShapeDtypeStruct((T, LANES), F32),
            jax.ShapeDtypeStruct((8, LANES), F32),
        ],
        scratch_shapes=[pltpu.VMEM((1, LANES), F32)],
        compiler_params=_params(("arbitrary",)),
        name="out_proj",
    )(o_nsa, o_ret, w_out, xt, g2, w_router, b_router, tri)


def _row_copy(src, row, dst, slot, sem):
    return pltpu.make_async_copy(src.at[pl.ds(row, 1), :], dst.at[pl.ds(slot, 1), :], sem)


def _dispatch_kernel(pos_ref, pad_ref, h2_ref, xs_hbm, zrow, sem, zsem):
    i = pl.program_id(0)
    tm = h2_ref.shape[0]
    K = TOPK_IN_GROUP

    def issue(r, carry):
        a = (i * tm + r) * K
        for k in range(K):
            _row_copy(h2_ref, r, xs_hbm, pos_ref[a + k], sem).start()
        return carry

    lax.fori_loop(0, tm, issue, 0, unroll=8)

    @pl.when(i == pl.num_programs(0) - 1)
    def _():
        zrow[...] = jnp.zeros(zrow.shape, F32)
        for e in range(N_EXPERTS):
            lo, hi = pad_ref[e], pad_ref[N_EXPERTS + e]

            def fill(r, carry):
                _row_copy(zrow, 0, xs_hbm, r, zsem).start()
                return carry

            lax.fori_loop(lo, hi, fill, 0)

            def fill_wait(r, carry):
                _row_copy(zrow, 0, xs_hbm, r, zsem).wait()
                return carry

            lax.fori_loop(lo, hi, fill_wait, 0)

        M = zrow.shape[0]

        def tail_copy(t):
            return pltpu.make_async_copy(zrow, xs_hbm.at[pl.ds(pl.multiple_of(t * M, M), M), :], zsem)

        def tail(t, carry):
            tail_copy(t).start()
            return carry

        lax.fori_loop(pad_ref[2 * N_EXPERTS], xs_hbm.shape[0] // M, tail, 0)

        def tail_wait(t, carry):
            tail_copy(t).wait()
            return carry

        lax.fori_loop(pad_ref[2 * N_EXPERTS], xs_hbm.shape[0] // M, tail_wait, 0)

    for k in range(K):
        pltpu.make_async_copy(h2_ref, xs_hbm.at[pl.ds(0, tm), :], sem).wait()


def _dispatch(pos, pad_rows, h2, n_rows, tm):
    T, D = h2.shape
    grid_spec = pltpu.PrefetchScalarGridSpec(
        num_scalar_prefetch=2,
        grid=(T // tm,),
        in_specs=[pl.BlockSpec((tm, D), lambda i, pos, pad: (i, 0))],
        out_specs=pl.BlockSpec(memory_space=pl.ANY),
        scratch_shapes=[pltpu.VMEM((MOE_BLOCK, D), F32), pltpu.SemaphoreType.DMA, pltpu.SemaphoreType.DMA],
    )
    return pl.pallas_call(
        _dispatch_kernel,
        grid_spec=grid_spec,
        out_shape=jax.ShapeDtypeStruct((n_rows, D), F32),
        compiler_params=_params(("arbitrary",)),
        name="dispatch",
    )(pos, pad_rows, h2)


def _expert_kernel(ts_ref, xs_hbm, wg_ref, wu_ref, wd_ref, y_hbm,
                   xb0, xb1, yb0, yb1, wg_b, wu_b, wd_b, gsem, osem, *, n_blocks):
    e = pl.program_id(0)
    M = MOE_BLOCK
    t0 = ts_ref[e]
    t1 = ts_ref[e + 1]
    n_used = ts_ref[N_EXPERTS]
    xbufs = (xb0, xb1)
    ybufs = (yb0, yb1)

    def rows(t):
        return pl.ds(pl.multiple_of(t * M, M), M)

    def in_copy(t, p):
        return pltpu.make_async_copy(xs_hbm.at[rows(t), :], xbufs[p], gsem.at[p])

    def out_copy(t, p):
        return pltpu.make_async_copy(ybufs[p], y_hbm.at[rows(t), :], osem.at[p])

    @pl.when(e == 0)
    def _():
        in_copy(0, 0).start()

    @pl.when(t1 > t0)
    def _():
        wg_b[...] = wg_ref[0].astype(BF16)
        wu_b[...] = wu_ref[0].astype(BF16)
        wd_b[...] = wd_ref[0].astype(BF16)

    def tile_body(t, p):
        in_copy(t, p).wait()

        @pl.when(t + 1 < n_used)
        def _():
            in_copy(t + 1, 1 - p).start()

        @pl.when(t >= 2)
        def _():
            out_copy(t - 2, p).wait()

        xb = xbufs[p][...].astype(BF16)
        hg = _dot(xb, wg_b[...])
        hu = _dot(xb, wu_b[...])
        hb = (hg * jax.nn.sigmoid(hg) * hu).astype(BF16)
        ybufs[p][...] = _dot(hb, wd_b[...])
        out_copy(t, p).start()

    def tile(t, carry):
        parity = lax.rem(t, 2)
        for p in range(2):
            pl.when(parity == p)(functools.partial(tile_body, t, p))
        return carry

    lax.fori_loop(t0, t1, tile, 0)

    @pl.when(e == N_EXPERTS - 1)
    def _():
        parity = lax.rem(n_used, 2)
        for p in range(2):
            @pl.when(parity == p)
            def _(p=p):
                out_copy(n_used - 1, 1 - p).wait()

                @pl.when(n_used >= 2)
                def _():
                    out_copy(n_used - 2, p).wait()

        yb0[...] = jnp.zeros(yb0.shape, F32)

        def clear(t, carry):
            out_copy(t, 0).start()
            return carry

        lax.fori_loop(n_used, n_blocks, clear, 0)

        def clear_wait(t, carry):
            out_copy(t, 0).wait()
            return carry

        lax.fori_loop(n_used, n_blocks, clear_wait, 0)


def _experts(xs, tile_start, w_gate, w_up, w_down):
    n_rows, D = xs.shape
    n_blocks = n_rows // MOE_BLOCK
    grid_spec = pltpu.PrefetchScalarGridSpec(
        num_scalar_prefetch=1,
        grid=(N_EXPERTS,),
        in_specs=[
            pl.BlockSpec(memory_space=pl.ANY),
            pl.BlockSpec((1, D, D_EXPERT), lambda e, ts: (e, 0, 0)),
            pl.BlockSpec((1, D, D_EXPERT), lambda e, ts: (e, 0, 0)),
            pl.BlockSpec((1, D_EXPERT, D), lambda e, ts: (e, 0, 0)),
        ],
        out_specs=pl.BlockSpec(memory_space=pl.ANY),
        scratch_shapes=[pltpu.VMEM((MOE_BLOCK, D), F32)] * 4 + [
            pltpu.VMEM((D, D_EXPERT), BF16), pltpu.VMEM((D, D_EXPERT), BF16),
            pltpu.VMEM((D_EXPERT, D), BF16),
            pltpu.SemaphoreType.DMA((2,)), pltpu.SemaphoreType.DMA((2,)),
        ],
    )
    return pl.pallas_call(
        functools.partial(_expert_kernel, n_blocks=n_blocks),
        grid_spec=grid_spec,
        out_shape=jax.ShapeDtypeStruct((n_rows, D), F32),
        compiler_params=_params(("arbitrary",)),
        name="experts",
    )(tile_start, xs, w_gate, w_up, w_down)


def _combine_kernel(pos_ref, y_hbm, x1_ref, rt_ref, o_ref, ybuf, sems):
    i = pl.program_id(0)
    n = pl.num_programs(0)
    tm = x1_ref.shape[0]
    K = TOPK_IN_GROUP
    slot = lax.rem(i, 2)

    def issue_tile(t, s):
        def issue(r, carry):
            a = (t * tm + r) * K
            for k in range(K):
                _row_copy(y_hbm, pos_ref[a + k], ybuf.at[s, k], r, sems.at[s]).start(priority=k)
            return carry

        lax.fori_loop(0, tm, issue, 0, unroll=8)

    @pl.when(i == 0)
    def _():
        issue_tile(0, 0)

    @pl.when(i + 1 < n)
    def _():
        issue_tile(i + 1, 1 - slot)

    for k in range(K):
        pltpu.make_async_copy(y_hbm.at[pl.ds(0, tm), :], ybuf.at[slot, k], sems.at[slot]).wait()
    w = rt_ref[...]
    yb = ybuf[slot]
    o_ref[...] = x1_ref[...] + (yb[0] * w[:, K:K + 1] + yb[1] * w[:, K + 1:K + 2])


def _combine(pos, y_rows, x1, rt, tm):
    T, D = x1.shape
    grid_spec = pltpu.PrefetchScalarGridSpec(
        num_scalar_prefetch=1,
        grid=(T // tm,),
        in_specs=[
            pl.BlockSpec(memory_space=pl.ANY),
            pl.BlockSpec((tm, D), lambda i, pos: (i, 0)),
            pl.BlockSpec((tm, LANES), lambda i, pos: (i, 0)),
        ],
        out_specs=pl.BlockSpec((tm, D), lambda i, pos: (i, 0)),
        scratch_shapes=[pltpu.VMEM((2, TOPK_IN_GROUP, tm, D), F32), pltpu.SemaphoreType.DMA((2,))],
    )
    return pl.pallas_call(
        _combine_kernel,
        grid_spec=grid_spec,
        out_shape=jax.ShapeDtypeStruct((T, D), F32),
        compiler_params=_params(("arbitrary",)),
        name="combine",
    )(pos, y_rows, x1, rt)


def _block_layout(rt, counts, T):
    K = TOPK_IN_GROUP
    n_rows = (K * T + N_EXPERTS * (MOE_BLOCK - 1) + MOE_BLOCK - 1) // MOE_BLOCK * MOE_BLOCK
    n_blocks = n_rows // MOE_BLOCK
    cnt = counts[0, :N_EXPERTS].astype(jnp.int32)
    padded = (cnt + MOE_BLOCK - 1) // MOE_BLOCK * MOE_BLOCK
    pad_end = jnp.cumsum(padded)
    pad_start = pad_end - padded
    e = rt[:, 0:K].astype(jnp.int32)
    rank = rt[:, 2 * K:3 * K].astype(jnp.int32)
    pos = (pad_start[e] + rank).reshape(-1)
    tile_start = (jnp.concatenate([pad_start, pad_end[-1:]]) // MOE_BLOCK).astype(jnp.int32)
    pad_rows = jnp.concatenate([pad_start + cnt, pad_end, pad_end[-1:] // MOE_BLOCK]).astype(jnp.int32)
    return tile_start, pos, pad_rows, n_rows


def _layer(x, norm1_g, w_in, cmp_pos, cmp_w1, cmp_w2, q_norm_g, k_norm_g, ret_gn_g, ret_gn_b,
           w_out, norm2_g, w_rg, b_rg, w_re, b_re, w_eg, w_eu, w_ed, tiles):
    B, S, D = x.shape
    T = B * S
    xt = x.reshape(T, D)

    w_main = jnp.concatenate([w_in[:, GATE_COL0 + NSA_GATE_WIDTH:], w_in[:, :GATE_COL0]], axis=1).astype(BF16)
    gpg = NSA_GATE_WIDTH // NSA_KV_GROUPS
    w_gate = jnp.concatenate(
        [jnp.pad(w_in[:, GATE_COL0 + g * gpg:GATE_COL0 + (g + 1) * gpg], ((0, 0), (0, LANES - gpg)))
         for g in range(NSA_KV_GROUPS)], axis=1).astype(BF16)
    proj, gate = _in_proj(xt, norm1_g.reshape(1, D), w_main, w_gate, tiles["tm_in"], tiles["tn_in"])

    cmp_kv = _compress(proj, cmp_pos, cmp_w1, cmp_w2, k_norm_g, B, S)
    o_nsa = _nsa(proj, gate, cmp_kv, q_norm_g.reshape(1, HEAD_DIM), k_norm_g, B, S, tiles["tq"], tiles["tk"])

    half = HEAD_DIM // 2
    inv_freq = ROPE_BASE ** (-jnp.arange(half, dtype=F32) / half)
    ang = jnp.arange(S, dtype=F32)[:, None] * inv_freq[None, :]
    cos2 = jnp.concatenate([jnp.cos(ang), jnp.cos(ang)], axis=1)
    sin2 = jnp.concatenate([-jnp.sin(ang), jnp.sin(ang)], axis=1)
    log_g = jnp.log1p(-jnp.exp2(-5.0 - jnp.arange(RET_HEADS, dtype=F32)))
    o_ret = _retention(proj, log_g, cos2, sin2, ret_gn_g, ret_gn_b, B, S)

    n_r = N_GROUPS + N_EXPERTS
    w_router = jnp.pad(jnp.concatenate([w_rg, w_re], axis=1), ((0, 0), (0, LANES - n_r)))
    b_router = jnp.pad(jnp.concatenate([b_rg, b_re]), (0, LANES - n_r)).reshape(1, LANES)
    x1, h2, rt, counts = _out_proj(o_nsa, o_ret, w_out.astype(BF16), xt, norm2_g.reshape(1, D),
                                   w_router, b_router, tiles["tm_out"])

    tile_start, pos, pad_rows, n_rows = _block_layout(rt, counts, T)
    xs = _dispatch(pos, pad_rows, h2, n_rows, tiles["tm_dsp"])
    y_rows = _experts(xs, tile_start, w_eg, w_eu, w_ed)
    out = _combine(pos, y_rows, x1, rt, tiles["tm_cmb"])
    return out.reshape(B, S, D)


def _tiles(T, S):
    return {
        "tm_in": min(2048, T), "tn_in": 512,
        "tq": min(256, S), "tk": min(512, S),
        "tm_out": min(512, T),
        "tm_dsp": min(1024, T),
        "tm_cmb": min(256, T),
    }


def kernel(x, norm1_g, w_in, cmp_pos, cmp_w1, cmp_w2, q_norm_g, k_norm_g, ret_gn_g, ret_gn_b, w_out, norm2_g, w_router_group, b_router_group, w_router_expert, b_router_expert, w_exp_gate, w_exp_up, w_exp_down):
    B, S, _ = x.shape
    tiles = _tiles(B * S, S)
    for l in range(norm1_g.shape[0]):
        x = _layer(x, norm1_g[l], w_in[l], cmp_pos[l], cmp_w1[l], cmp_w2[l], q_norm_g[l], k_norm_g[l],
                   ret_gn_g[l], ret_gn_b[l], w_out[l], norm2_g[l], w_router_group[l], b_router_group[l],
                   w_router_expert[l], b_router_expert[l], w_exp_gate[l], w_exp_up[l], w_exp_down[l], tiles)
    return x
```

```python
import functools

import numpy as np
import jax
import jax.numpy as jnp
from jax import lax
from jax.experimental import pallas as pl
from jax.experimental.pallas import tpu as pltpu

F32 = jnp.float32
BF16 = jnp.bfloat16

D_MODEL = 2048
NSA_HEADS = 8
NSA_KV_GROUPS = 2
NSA_GROUP_HEADS = NSA_HEADS // NSA_KV_GROUPS
HEAD_DIM = 128
RET_HEADS = 8
CMP_BLOCK = 32
CMP_STRIDE = 16
SEL_BLOCK = 64
SEL_TOPK = 8
WINDOW = 512
RET_CHUNK = 128
ROPE_BASE = 10000.0
N_GROUPS = 4
EXPERTS_PER_GROUP = 8
N_EXPERTS = N_GROUPS * EXPERTS_PER_GROUP
TOPK_IN_GROUP = 2
D_EXPERT = 512
MOE_BLOCK = 256
RMS_EPS = 1e-6
GN_EPS = 1e-5
NEG_INF = -1e30
FORCED_SCORE = 1e6
LOG2E = 1.4426950408889634

NSA_Q_WIDTH = NSA_HEADS * HEAD_DIM
NSA_KV_WIDTH = NSA_KV_GROUPS * HEAD_DIM
NSA_GATE_WIDTH = 3 * NSA_HEADS
RET_WIDTH = RET_HEADS * HEAD_DIM
GATE_COL0 = NSA_Q_WIDTH + 6 * NSA_KV_WIDTH
LANES = 128
VMEM_LIMIT = 56 * 1024 * 1024

Q_COL0 = 4 * RET_WIDTH
KV_COL0 = Q_COL0 + NSA_Q_WIDTH
KC_BLK, VC_BLK, KS_BLK, VS_BLK, KW_BLK, VW_BLK = [KV_COL0 // LANES + 2 * t for t in range(6)]


def _rms(xf, g):
    return xf * lax.rsqrt(jnp.mean(xf * xf, axis=-1, keepdims=True) + RMS_EPS) * g


def _dot(a, b):
    return jnp.dot(a, b, preferred_element_type=F32)


def _dot_nt(a, b):
    return lax.dot_general(a, b, (((1,), (1,)), ((), ())), preferred_element_type=F32)


def _dot_tn(a, b):
    return lax.dot_general(a, b, (((0,), (0,)), ((), ())), preferred_element_type=F32)


def _split3(p):
    p1 = p.astype(BF16)
    r1 = p - p1.astype(F32)
    p2 = r1.astype(BF16)
    p3 = (r1 - p2.astype(F32)).astype(BF16)
    return p1, p2, p3


def _params(sem):
    return pltpu.CompilerParams(dimension_semantics=sem, vmem_limit_bytes=VMEM_LIMIT)


IN_CHUNK = 256


def _in_proj_kernel(x_hbm, g_ref, w_ref, wg_ref, o_ref, og_ref, h0, h1, xc0, xc1, sem, *, tm):
    m = pl.program_id(0)
    n = pl.program_id(1)
    ck = IN_CHUNK
    nc = tm // ck
    hs = (h0, h1)
    xcs = (xc0, xc1)
    g = g_ref[...]

    def chunk_copy(tile, c, slot):
        row0 = pl.multiple_of(tile * tm + c * ck, ck)
        return pltpu.make_async_copy(x_hbm.at[pl.ds(row0, ck), :], xcs[slot], sem.at[slot])

    def normalise(slot, dst, c):
        dst[pl.ds(pl.multiple_of(c * ck, ck), ck), :] = _rms(xcs[slot][...], g).astype(BF16)

    @pl.when((m == 0) & (n == 0))
    def _():
        chunk_copy(0, 0, 0).start()
        for c in range(nc):
            if c + 1 < nc:
                chunk_copy(0, c + 1, (c + 1) % 2).start()
            chunk_copy(0, c, c % 2).wait()
            normalise(c % 2, h0, c)

    has_next = m + 1 < pl.num_programs(0)
    for slot in range(2):
        @pl.when(has_next & (n >= 1) & (n <= nc) & (lax.rem(n - 1, 2) == slot))
        def _(slot=slot):
            chunk_copy(m + 1, n - 1, slot).wait()

        @pl.when(has_next & (n < nc) & (lax.rem(n, 2) == slot))
        def _(slot=slot):
            chunk_copy(m + 1, n, slot).start()

    c_prev = jnp.where(n == 0, nc - 1, jnp.clip(n - 1, 0, nc - 1))
    even_chunk = lax.rem(c_prev, 2) == 0
    for cur in range(2):
        for slot in range(2):
            @pl.when((lax.rem(m, 2) == cur) & (even_chunk == (slot == 0)))
            def _(cur=cur, slot=slot):
                @pl.when(n == 0)
                def _():
                    og_ref[...] = _dot(hs[cur][...], wg_ref[...])

                normalise(slot, hs[1 - cur], c_prev)
                o_ref[...] = _dot(hs[cur][...], w_ref[...])


def _in_proj(xt, g1, w_main, w_gate, tm, tn):
    T, D = xt.shape
    n_main = w_main.shape[1]
    n_gate = w_gate.shape[1]
    assert tm % (2 * IN_CHUNK) == 0 and tm // IN_CHUNK < n_main // tn
    return pl.pallas_call(
        functools.partial(_in_proj_kernel, tm=tm),
        grid=(T // tm, n_main // tn),
        in_specs=[
            pl.BlockSpec(memory_space=pl.ANY),
            pl.BlockSpec((1, D), lambda m, n: (0, 0)),
            pl.BlockSpec((D, tn), lambda m, n: (0, n)),
            pl.BlockSpec((D, n_gate), lambda m, n: (0, 0)),
        ],
        out_specs=[
            pl.BlockSpec((tm, tn), lambda m, n: (m, n)),
            pl.BlockSpec((tm, n_gate), lambda m, n: (m, 0)),
        ],
        out_shape=[
            jax.ShapeDtypeStruct((T, n_main), F32),
            jax.ShapeDtypeStruct((T, n_gate), F32),
        ],
        scratch_shapes=[pltpu.VMEM((tm, D), BF16), pltpu.VMEM((tm, D), BF16),
                        pltpu.VMEM((IN_CHUNK, D), F32), pltpu.VMEM((IN_CHUNK, D), F32),
                        pltpu.SemaphoreType.DMA((2,))],
        compiler_params=_params(("arbitrary", "arbitrary")),
        name="in_proj",
    )(xt, g1, w_main, w_gate)


def _compress_kernel(x_ref, pos_ref, w1_ref, w2_ref, kg_ref, o_ref, xs_scr, *, S, n_pad):
    kv = pl.program_id(1)
    xs_scr[0:S, :] = x_ref[...]
    xs_scr[S:S + CMP_STRIDE, :] = jnp.zeros((CMP_STRIDE, HEAD_DIM), F32)
    acc = jnp.zeros((n_pad, HEAD_DIM), F32)
    for l in range(CMP_BLOCK):
        tb = xs_scr[pl.ds(l, n_pad, stride=CMP_STRIDE), :] + pos_ref[0, l:l + 1, :]
        acc = acc + _dot(tb.astype(BF16), w1_ref[0, l].astype(BF16))
    hid = acc * jax.nn.sigmoid(acc)
    out = _dot(hid.astype(BF16), w2_ref[0].astype(BF16))
    normed = _rms(out, kg_ref[0:1, :])
    o_ref[0, 0, 0] = jnp.where(kv == 0, normed, out).astype(BF16)


def _compress(proj, cmp_pos, cmp_w1, cmp_w2, k_norm_g, B, S):
    n_pad = S // CMP_STRIDE
    G = NSA_KV_GROUPS
    kern = functools.partial(_compress_kernel, S=S, n_pad=n_pad)
    return pl.pallas_call(
        kern,
        grid=(B, 2, G),
        in_specs=[
            pl.BlockSpec((S, HEAD_DIM), lambda b, kv, g: (b, KC_BLK + 2 * kv + g)),
            pl.BlockSpec((1, CMP_BLOCK, HEAD_DIM), lambda b, kv, g: (kv, 0, 0)),
            pl.BlockSpec((1, CMP_BLOCK, HEAD_DIM, HEAD_DIM), lambda b, kv, g: (kv, 0, 0, 0)),
            pl.BlockSpec((1, HEAD_DIM, HEAD_DIM), lambda b, kv, g: (kv, 0, 0)),
            pl.BlockSpec((3, HEAD_DIM), lambda b, kv, g: (0, 0)),
        ],
        out_specs=pl.BlockSpec((1, 1, 1, n_pad, HEAD_DIM), lambda b, kv, g: (b, kv, g, 0, 0)),
        out_shape=jax.ShapeDtypeStruct((B, 2, G, n_pad, HEAD_DIM), BF16),
        scratch_shapes=[pltpu.VMEM((S + CMP_STRIDE, HEAD_DIM), F32)],
        compiler_params=_params(("parallel", "parallel", "parallel")),
        name="compress",
    )(proj, cmp_pos, cmp_w1, cmp_w2, k_norm_g)


def _masked_softmax(s, msk):
    sm = jnp.where(msk, s, NEG_INF)
    m = jnp.max(sm, axis=-1, keepdims=True)
    e = jnp.where(msk, jnp.exp(sm - m), 0.0)
    den = jnp.sum(e, axis=-1, keepdims=True)
    return e / jnp.where(den > 0.0, den, 1.0)


def _nsa_kernel(q_ref, kc_ref, vc_ref, ks_ref, vs_ref, kw_ref, vw_ref, gate_ref, qg_ref, kg_ref,
                ovt_ref, ext_ref, o_ref, ksa, vsa, kwn, vwa, l_scr, acc_scr, *, S, tq, tk, wk):
    i = pl.program_id(2)
    H = NSA_GROUP_HEADS
    scale = HEAD_DIM ** -0.5
    c2 = scale * LOG2E
    n_cp = S // CMP_STRIDE
    n_sblk = S // SEL_BLOCK

    @pl.when(i == 0)
    def _():
        ksa[:, 0:HEAD_DIM] = _rms(ks_ref[...], kg_ref[1:2, :]).astype(BF16)
        ksa[:, HEAD_DIM:2 * HEAD_DIM] = ext_ref[...]
        kwn[...] = _rms(kw_ref[...], kg_ref[2:3, :]).astype(BF16)
        ones = jnp.ones((S, HEAD_DIM), BF16)
        vsa[:, 0:HEAD_DIM] = vs_ref[...].astype(BF16)
        vsa[:, HEAD_DIM:2 * HEAD_DIM] = ones
        vwa[:, 0:HEAD_DIM] = vw_ref[...].astype(BF16)
        vwa[:, HEAD_DIM:2 * HEAD_DIM] = ones

    pos = i * tq + lax.broadcasted_iota(jnp.int32, (tq, 1), 0)
    qg = qg_ref[...]
    q4 = jnp.concatenate([_rms(q_ref[:, h * HEAD_DIM:(h + 1) * HEAD_DIM], qg).astype(BF16)
                          for h in range(H)], axis=0)
    rows = [slice(h * tq, (h + 1) * tq) for h in range(H)]

    def softmax_pv(q_chains, k, v, tail_bias):
        head = k.shape[0] - tail_bias.shape[1]
        outs, sums = [], []
        for q2 in q_chains:
            s2 = _dot_nt(q2, k)
            es = []
            hpc = q2.shape[0] // tq
            for hh in range(hpc):
                t = s2[hh * tq:(hh + 1) * tq] * c2
                tb = t[:, head:] + tail_bias
                m = jnp.max(tb, axis=-1, keepdims=True)
                if head:
                    ta = t[:, :head]
                    m = jnp.maximum(m, jnp.max(ta, axis=-1, keepdims=True))
                    es.append(jnp.concatenate([jnp.exp2(ta - m).astype(BF16),
                                               jnp.exp2(tb - m).astype(BF16)], axis=1))
                else:
                    es.append(jnp.exp2(tb - m).astype(BF16))
            o2 = _dot(es[0] if hpc == 1 else jnp.concatenate(es, axis=0), v)
            outs += [o2[hh * tq:(hh + 1) * tq, 0:HEAD_DIM] for hh in range(hpc)]
            sums += [o2[hh * tq:(hh + 1) * tq, HEAD_DIM:2 * HEAD_DIM] for hh in range(hpc)]
        return outs, sums

    ncol = lax.broadcasted_iota(jnp.int32, (1, n_cp), 1)
    cmask = (ncol * CMP_STRIDE + (CMP_BLOCK - 1)) <= pos
    s_c4 = _dot_nt(q4, kc_ref[0, 0, 0]) * scale
    ps = [_masked_softmax(s_c4[r], cmask) for r in rows]
    psum = ps[0] + ps[1] + ps[2] + ps[3]
    o_c4 = _dot(jnp.concatenate([p.astype(BF16) for p in ps], axis=0), vc_ref[0, 0, 0])

    ovt = ovt_ref[...]
    p1, p2, p3 = _split3(psum)
    imp_t = (_dot_nt(ovt, p1) + _dot_nt(ovt, p2) + _dot_nt(ovt, p3))[0:n_sblk, :]
    pos_t = i * tq + lax.broadcasted_iota(jnp.int32, (1, tq), 1)
    jrow = lax.broadcasted_iota(jnp.int32, (n_sblk, 1), 0)
    cur_t = pos_t // SEL_BLOCK
    valid_t = jrow * SEL_BLOCK <= pos_t
    forced_t = (jrow == 0) | (jrow == cur_t) | (jrow == cur_t - 1)
    score = jnp.where(valid_t, jnp.where(forced_t, FORCED_SCORE, imp_t), -jnp.inf)
    rank = jnp.zeros((n_sblk, tq), jnp.int32)
    for i2 in range(n_sblk):
        si = score[i2:i2 + 1, :]
        ahead = (si > score) | ((si == score) & (jrow > i2))
        rank = rank + jnp.where(ahead, 1, 0)
    drop_t = jnp.where((rank < SEL_TOPK) & valid_t, 0.0, NEG_INF)
    drop_t = jnp.concatenate([drop_t, jnp.full((LANES - n_sblk, tq), NEG_INF, F32)], axis=0)
    drop = drop_t.T.astype(BF16)
    qa_pairs = [jnp.concatenate([q4[r], drop], axis=1) for r in rows]

    kst = pl.multiple_of(jnp.clip(i * tq - WINDOW, 0, S - wk), tq)
    wpos = kst + lax.broadcasted_iota(jnp.int32, (1, wk), 1)
    wbias = jnp.where((wpos <= pos) & (wpos > pos - WINDOW), 0.0, NEG_INF)
    o_ws, lws = softmax_pv([q4[r] for r in rows],
                           kwn[pl.ds(kst, wk), :], vwa[pl.ds(kst, wk), :], wbias)

    def slc_prefix(n):
        kpos = (n - tk) + lax.broadcasted_iota(jnp.int32, (1, tk), 1)
        causal = jnp.where(kpos <= pos, 0.0, NEG_INF)
        outs, sums = softmax_pv(qa_pairs, ksa[0:n, :], vsa[0:n, :], causal)
        for r, o, l in zip(rows, outs, sums):
            acc_scr[r] = o
            l_scr[r] = l

    n_cls = S // tk
    cls = ((i + 1) * tq - 1) // tk
    for c in range(n_cls):
        pl.when(cls == c)(functools.partial(slc_prefix, (c + 1) * tk))

    gates = jax.nn.sigmoid(gate_ref[...])
    for h, r in enumerate(rows):
        c = 3 * h
        o = (o_c4[r] * gates[:, c:c + 1]
             + acc_scr[r] * (gates[:, c + 1:c + 2] / l_scr[r])
             + o_ws[h] * (gates[:, c + 2:c + 3] / lws[h]))
        o_ref[:, h * HEAD_DIM:(h + 1) * HEAD_DIM] = o.astype(BF16)


def _nsa_tables(S):
    n_cp, n_cmp, n_sblk = S // CMP_STRIDE, (S - CMP_BLOCK) // CMP_STRIDE + 1, S // SEL_BLOCK
    n = np.arange(n_cp)[None, :] * CMP_STRIDE
    j = np.arange(LANES)[:, None]
    ovt = ((n < (j + 1) * SEL_BLOCK) & (n + CMP_BLOCK > j * SEL_BLOCK)
           & (np.arange(n_cp)[None, :] < n_cmp) & (j < n_sblk))
    key_block = (np.arange(S) // SEL_BLOCK)[:, None] == np.arange(LANES)[None, :]
    return jnp.asarray(ovt, BF16), jnp.asarray(key_block, BF16)


def _nsa(proj, gate, cmp_kv, q_norm_g, k_norm_g, B, S, tq, tk):
    G = NSA_KV_GROUPS
    H = NSA_GROUP_HEADS
    n_cp = S // CMP_STRIDE
    wk = min(S, WINDOW + tq)
    nq = S // tq
    gw = H * HEAD_DIM
    ovt, key_block = _nsa_tables(S)
    kern = functools.partial(_nsa_kernel, S=S, tq=tq, tk=tk, wk=wk)
    kv_spec = lambda blk: pl.BlockSpec((S, HEAD_DIM), lambda b, g, i: (b, blk + g))
    return pl.pallas_call(
        kern,
        grid=(B, G, nq),
        in_specs=[
            pl.BlockSpec((tq, gw), lambda b, g, i: (b * nq + i, Q_COL0 // gw + g)),
            pl.BlockSpec((1, 1, 1, n_cp, HEAD_DIM), lambda b, g, i: (b, 0, g, 0, 0)),
            pl.BlockSpec((1, 1, 1, n_cp, HEAD_DIM), lambda b, g, i: (b, 1, g, 0, 0)),
            kv_spec(KS_BLK), kv_spec(VS_BLK), kv_spec(KW_BLK), kv_spec(VW_BLK),
            pl.BlockSpec((tq, LANES), lambda b, g, i: (b * nq + i, g)),
            pl.BlockSpec((1, HEAD_DIM), lambda b, g, i: (0, 0)),
            pl.BlockSpec((3, HEAD_DIM), lambda b, g, i: (0, 0)),
            pl.BlockSpec((LANES, n_cp), lambda b, g, i: (0, 0)),
            pl.BlockSpec((S, LANES), lambda b, g, i: (0, 0)),
        ],
        out_specs=pl.BlockSpec((tq, gw), lambda b, g, i: (b * nq + i, g)),
        out_shape=jax.ShapeDtypeStruct((B * S, NSA_Q_WIDTH), BF16),
        scratch_shapes=[
            pltpu.VMEM((S, 2 * HEAD_DIM), BF16), pltpu.VMEM((S, 2 * HEAD_DIM), BF16),
            pltpu.VMEM((S, HEAD_DIM), BF16), pltpu.VMEM((S, 2 * HEAD_DIM), BF16),
            pltpu.VMEM((H * tq, HEAD_DIM), F32), pltpu.VMEM((H * tq, HEAD_DIM), F32)],
        compiler_params=_params(("parallel", "parallel", "arbitrary")),
        name="nsa",
    )(proj, cmp_kv, cmp_kv, proj, proj, proj, proj, gate, q_norm_g, k_norm_g, ovt, key_block)


def _ret_kernel(rq_ref, rk_ref, rv_ref, rg_ref, cos_ref, sin_ref, dec_ref, xi_ref, zeta_ref, cd_ref,
                gg_ref, gb_ref, o_ref, r_scr):
    scale = HEAD_DIM ** -0.5

    @pl.when(pl.program_id(1) == 0)
    def _():
        r_scr[...] = jnp.zeros(r_scr.shape, F32)

    cos = cos_ref[...]
    sin = sin_ref[...]
    for h in range(RET_HEADS):
        sl = slice(h * HEAD_DIM, (h + 1) * HEAD_DIM)
        q = rq_ref[:, sl]
        k = rk_ref[:, sl]
        qf = q * cos + pltpu.roll(q, HEAD_DIM // 2, 1) * sin
        kf = (k * cos + pltpu.roll(k, HEAD_DIM // 2, 1) * sin) * scale
        v = rv_ref[:, sl].astype(BF16)
        qb = qf.astype(BF16)
        r_old = r_scr[h]
        a = _dot_nt(qb, kf.astype(BF16)) * dec_ref[h]
        o = _dot(a.astype(BF16), v) + _dot(qb, r_old.astype(BF16)) * xi_ref[h]
        r_scr[h] = r_old * cd_ref[h:h + 1, :] + _dot_tn((kf * zeta_ref[h]).astype(BF16), v)
        mu = jnp.mean(o, axis=-1, keepdims=True)
        d = o - mu
        var = jnp.mean(d * d, axis=-1, keepdims=True)
        y = d * lax.rsqrt(var + GN_EPS) * gg_ref[h:h + 1, :] + gb_ref[h:h + 1, :]
        gt = rg_ref[:, sl]
        o_ref[:, sl] = (gt * jax.nn.sigmoid(gt) * y).astype(BF16)


def _ret_tables(S):
    C, H, half = RET_CHUNK, RET_HEADS, HEAD_DIM // 2
    inv_freq = ROPE_BASE ** (-jnp.arange(half, dtype=F32) / half)
    ang = jnp.arange(S, dtype=F32)[:, None] * inv_freq[None, :]
    cos2 = jnp.concatenate([jnp.cos(ang), jnp.cos(ang)], axis=1)
    sin2 = jnp.concatenate([-jnp.sin(ang), jnp.sin(ang)], axis=1)
    log_g = jnp.log1p(-jnp.exp2(-5.0 - jnp.arange(H, dtype=F32)))
    n = jnp.arange(C, dtype=F32)
    diff = n[:, None] - n[None, :]
    dec = jnp.where(diff >= 0, jnp.exp(log_g[:, None, None] * jnp.maximum(diff, 0.0)), 0.0)
    lanes = lambda t: jnp.broadcast_to(t[..., None], t.shape + (HEAD_DIM,))
    xi = lanes(jnp.exp(log_g[:, None] * (n + 1.0)))
    zeta = lanes(jnp.exp(log_g[:, None] * (C - 1.0 - n)))
    cd = lanes(jnp.exp(log_g * C))
    return cos2, sin2, dec, xi, zeta, cd


def _retention(proj, gn_g, gn_b, B, S):
    C, H = RET_CHUNK, RET_HEADS
    n_ch = S // C
    cos2, sin2, dec, xi, zeta, cd = _ret_tables(S)
    spec = lambda k: pl.BlockSpec((C, RET_WIDTH), lambda b, c: (b * n_ch + c, k))
    whole = lambda a: pl.BlockSpec(a.shape, lambda b, c: (0,) * a.ndim)
    return pl.pallas_call(
        _ret_kernel,
        grid=(B, n_ch),
        in_specs=[
            spec(0), spec(1), spec(2), spec(3),
            pl.BlockSpec((C, HEAD_DIM), lambda b, c: (c, 0)),
            pl.BlockSpec((C, HEAD_DIM), lambda b, c: (c, 0)),
            whole(dec), whole(xi), whole(zeta), whole(cd), whole(gn_g), whole(gn_b),
        ],
        out_specs=pl.BlockSpec((C, RET_WIDTH), lambda b, c: (b * n_ch + c, 0)),
        out_shape=jax.ShapeDtypeStruct((B * S, RET_WIDTH), BF16),
        scratch_shapes=[pltpu.VMEM((H, HEAD_DIM, HEAD_DIM), F32)],
        compiler_params=_params(("parallel", "arbitrary")),
        name="retention",
    )(proj, proj, proj, proj, cos2, sin2, dec, xi, zeta, cd, gn_g, gn_b)


def _lane_max(v):
    return jnp.max(v, axis=-1, keepdims=True)


def _lane_min(v):
    return jnp.min(v, axis=-1, keepdims=True)


def _out_kernel(on_ref, or_ref, w_ref, x_ref, g2_ref, wr_ref, br_ref, tri_ref,
                x1_ref, h2_ref, rt_ref, cnt_ref, cnt_scr):
    @pl.when(pl.program_id(0) == 0)
    def _():
        cnt_scr[...] = jnp.zeros(cnt_scr.shape, F32)

    half = on_ref.shape[1]
    acc = _dot(on_ref[...], w_ref[0:half, :]) + _dot(or_ref[...], w_ref[half:2 * half, :])
    x1 = x_ref[...] + acc
    x1_ref[...] = x1
    h2 = _rms(x1, g2_ref[...])
    h2_ref[...] = h2

    h_hi = h2.astype(BF16)
    h_lo = (h2 - h_hi.astype(F32)).astype(BF16)
    wr = wr_ref[...]
    w_hi = wr.astype(BF16)
    w_lo = (wr - w_hi.astype(F32)).astype(BF16)
    logits = _dot(h_hi, w_hi) + _dot(h_lo, w_hi) + _dot(h_hi, w_lo) + br_ref[...]

    tm = logits.shape[0]
    lane = lax.broadcasted_iota(jnp.int32, (tm, LANES), 1)
    big = jnp.int32(LANES)
    gm = lane < N_GROUPS
    gl = jnp.where(gm, logits, -jnp.inf)
    ge = jnp.where(gm, jnp.exp(gl - _lane_max(gl)), 0.0)
    pg = ge / jnp.sum(ge, axis=-1, keepdims=True)
    g_top = _lane_max(pg)
    g_idx = _lane_min(jnp.where(gm & (pg == g_top), lane, big))
    e0 = N_GROUPS + g_idx * EXPERTS_PER_GROUP
    em = (lane >= e0) & (lane < e0 + EXPERTS_PER_GROUP)
    el = jnp.where(em, logits, -jnp.inf)
    ee = jnp.where(em, jnp.exp(el - _lane_max(el)), 0.0)
    pe = ee / jnp.sum(ee, axis=-1, keepdims=True)
    t1 = _lane_max(jnp.where(em, pe, -1.0))
    i1 = _lane_min(jnp.where(em & (pe == t1), lane, big))
    em2 = em & (lane != i1)
    t2 = _lane_max(jnp.where(em2, pe, -1.0))
    i2 = _lane_min(jnp.where(em2 & (pe == t2), lane, big))
    tsum = t1 + t2
    w1 = g_top * t1 / tsum
    w2 = g_top * t2 / tsum
    e1 = i1 - N_GROUPS
    e2 = i2 - N_GROUPS
    oh1 = jnp.where(lane == e1, 1.0, 0.0)
    oh2 = jnp.where(lane == e2, 1.0, 0.0)
    both = oh1 + oh2
    before = _dot(tri_ref[...], both.astype(BF16)) + cnt_scr[...]
    r1 = jnp.sum(before * oh1, axis=-1, keepdims=True)
    r2 = jnp.sum(before * oh2, axis=-1, keepdims=True)
    cnt_scr[...] = cnt_scr[...] + jnp.sum(both, axis=0, keepdims=True)
    cnt_ref[...] = jnp.broadcast_to(cnt_scr[...], cnt_ref.shape)

    cols = (e1.astype(F32), e2.astype(F32), w1, w2, r1, r2)
    rt = jnp.zeros((tm, LANES), F32)
    for c, v in enumerate(cols):
        rt = jnp.where(lane == c, v, rt)
    rt_ref[...] = rt


def _out_proj(o_nsa, o_ret, w_out, xt, g2, w_router, b_router, tm):
    T, D = xt.shape
    half = o_nsa.shape[1]
    tri = jnp.asarray(np.tril(np.ones((tm, tm), np.float32), -1), BF16)
    return pl.pallas_call(
        _out_kernel,
        grid=(T // tm,),
        in_specs=[
            pl.BlockSpec((tm, half), lambda m: (m, 0)),
            pl.BlockSpec((tm, half), lambda m: (m, 0)),
            pl.BlockSpec((2 * half, D), lambda m: (0, 0), pipeline_mode=pl.Buffered(1)),
            pl.BlockSpec((tm, D), lambda m: (m, 0)),
            pl.BlockSpec((1, D), lambda m: (0, 0)),
            pl.BlockSpec((D, LANES), lambda m: (0, 0)),
            pl.BlockSpec((1, LANES), lambda m: (0, 0)),
            pl.BlockSpec((tm, tm), lambda m: (0, 0)),
        ],
        out_specs=[
            pl.BlockSpec((tm, D), lambda m: (m, 0)),
            pl.BlockSpec((tm, D), lambda m: (m, 0)),
            pl.BlockSpec((tm, LANES), lambda m: (m, 0)),
            pl.BlockSpec((8, LANES), lambda m: (0, 0)),
        ],
        out_shape=[
            jax.ShapeDtypeStruct((T, D), F32),
            jax.ShapeDtypeStruct((T, D), F32),
            jax.ShapeDtypeStruct((T, LANES), F32),
            jax.ShapeDtypeStruct((8, LANES), F32),
        ],
        scratch_shapes=[pltpu.VMEM((1, LANES), F32)],
        compiler_params=_params(("arbitrary",)),
        name="out_proj",
    )(o_nsa, o_ret, w_out, xt, g2, w_router, b_router, tri)


def _row_copy(src, row, dst, slot, sem):
    return pltpu.make_async_copy(src.at[pl.ds(row, 1), :], dst.at[pl.ds(slot, 1), :], sem)


def _dispatch_kernel(pos_ref, pad_ref, h2_ref, xs_hbm, zrow, sem, zsem):
    i = pl.program_id(0)
    tm = h2_ref.shape[0]
    K = TOPK_IN_GROUP

    def issue(r, carry):
        a = (i * tm + r) * K
        for k in range(K):
            _row_copy(h2_ref, r, xs_hbm, pos_ref[a + k], sem).start()
        return carry

    lax.fori_loop(0, tm, issue, 0, unroll=8)

    @pl.when(i == pl.num_programs(0) - 1)
    def _():
        zrow[...] = jnp.zeros(zrow.shape, F32)
        for e in range(N_EXPERTS):
            lo, hi = pad_ref[e], pad_ref[N_EXPERTS + e]

            def fill(r, carry):
                _row_copy(zrow, 0, xs_hbm, r, zsem).start()
                return carry

            lax.fori_loop(lo, hi, fill, 0)

            def fill_wait(r, carry):
                _row_copy(zrow, 0, xs_hbm, r, zsem).wait()
                return carry

            lax.fori_loop(lo, hi, fill_wait, 0)

        M = zrow.shape[0]

        def tail_copy(t):
            return pltpu.make_async_copy(zrow, xs_hbm.at[pl.ds(pl.multiple_of(t * M, M), M), :], zsem)

        def tail(t, carry):
            tail_copy(t).start()
            return carry

        lax.fori_loop(pad_ref[2 * N_EXPERTS], xs_hbm.shape[0] // M, tail, 0)

        def tail_wait(t, carry):
            tail_copy(t).wait()
            return carry

        lax.fori_loop(pad_ref[2 * N_EXPERTS], xs_hbm.shape[0] // M, tail_wait, 0)

    for k in range(K):
        pltpu.make_async_copy(h2_ref, xs_hbm.at[pl.ds(0, tm), :], sem).wait()


def _dispatch(pos, pad_rows, h2, n_rows, tm):
    T, D = h2.shape
    grid_spec = pltpu.PrefetchScalarGridSpec(
        num_scalar_prefetch=2,
        grid=(T // tm,),
        in_specs=[pl.BlockSpec((tm, D), lambda i, pos, pad: (i, 0))],
        out_specs=pl.BlockSpec(memory_space=pl.ANY),
        scratch_shapes=[pltpu.VMEM((MOE_BLOCK, D), F32), pltpu.SemaphoreType.DMA, pltpu.SemaphoreType.DMA],
    )
    return pl.pallas_call(
        _dispatch_kernel,
        grid_spec=grid_spec,
        out_shape=jax.ShapeDtypeStruct((n_rows, D), F32),
        compiler_params=_params(("arbitrary",)),
        name="dispatch",
    )(pos, pad_rows, h2)


def _expert_kernel(ts_ref, xs_hbm, wg_ref, wu_ref, wd_ref, y_hbm,
                   xb0, xb1, yb0, yb1, wg_b, wu_b, wd_b, gsem, osem, *, n_blocks):
    e = pl.program_id(0)
    M = MOE_BLOCK
    t0 = ts_ref[e]
    t1 = ts_ref[e + 1]
    n_used = ts_ref[N_EXPERTS]
    xbufs = (xb0, xb1)
    ybufs = (yb0, yb1)

    def rows(t):
        return pl.ds(pl.multiple_of(t * M, M), M)

    def in_copy(t, p):
        return pltpu.make_async_copy(xs_hbm.at[rows(t), :], xbufs[p], gsem.at[p])

    def out_copy(t, p):
        return pltpu.make_async_copy(ybufs[p], y_hbm.at[rows(t), :], osem.at[p])

    @pl.when(e == 0)
    def _():
        in_copy(0, 0).start()

    @pl.when(t1 > t0)
    def _():
        wg_b[...] = wg_ref[0].astype(BF16)
        wu_b[...] = wu_ref[0].astype(BF16)
        wd_b[...] = wd_ref[0].astype(BF16)

    def tile_body(t, p):
        in_copy(t, p).wait()

        @pl.when(t + 1 < n_used)
        def _():
            in_copy(t + 1, 1 - p).start()

        @pl.when(t >= 2)
        def _():
            out_copy(t - 2, p).wait()

        xb = xbufs[p][...].astype(BF16)
        hg = _dot(xb, wg_b[...])
        hu = _dot(xb, wu_b[...])
        hb = (hg * jax.nn.sigmoid(hg) * hu).astype(BF16)
        ybufs[p][...] = _dot(hb, wd_b[...])
        out_copy(t, p).start()

    def tile(t, carry):
        parity = lax.rem(t, 2)
        for p in range(2):
            pl.when(parity == p)(functools.partial(tile_body, t, p))
        return carry

    lax.fori_loop(t0, t1, tile, 0)

    @pl.when(e == N_EXPERTS - 1)
    def _():
        parity = lax.rem(n_used, 2)
        for p in range(2):
            @pl.when(parity == p)
            def _(p=p):
                out_copy(n_used - 1, 1 - p).wait()

                @pl.when(n_used >= 2)
                def _():
                    out_copy(n_used - 2, p).wait()

        yb0[...] = jnp.zeros(yb0.shape, F32)

        def clear(t, carry):
            out_copy(t, 0).start()
            return carry

        lax.fori_loop(n_used, n_blocks, clear, 0)

        def clear_wait(t, carry):
            out_copy(t, 0).wait()
            return carry

        lax.fori_loop(n_used, n_blocks, clear_wait, 0)


def _experts(xs, tile_start, w_gate, w_up, w_down):
    n_rows, D = xs.shape
    n_blocks = n_rows // MOE_BLOCK
    grid_spec = pltpu.PrefetchScalarGridSpec(
        num_scalar_prefetch=1,
        grid=(N_EXPERTS,),
        in_specs=[
            pl.BlockSpec(memory_space=pl.ANY),
            pl.BlockSpec((1, D, D_EXPERT), lambda e, ts: (e, 0, 0)),
            pl.BlockSpec((1, D, D_EXPERT), lambda e, ts: (e, 0, 0)),
            pl.BlockSpec((1, D_EXPERT, D), lambda e, ts: (e, 0, 0)),
        ],
        out_specs=pl.BlockSpec(memory_space=pl.ANY),
        scratch_shapes=[pltpu.VMEM((MOE_BLOCK, D), F32)] * 4 + [
            pltpu.VMEM((D, D_EXPERT), BF16), pltpu.VMEM((D, D_EXPERT), BF16),
            pltpu.VMEM((D_EXPERT, D), BF16),
            pltpu.SemaphoreType.DMA((2,)), pltpu.SemaphoreType.DMA((2,)),
        ],
    )
    return pl.pallas_call(
        functools.partial(_expert_kernel, n_blocks=n_blocks),
        grid_spec=grid_spec,
        out_shape=jax.ShapeDtypeStruct((n_rows, D), F32),
        compiler_params=_params(("arbitrary",)),
        name="experts",
    )(tile_start, xs, w_gate, w_up, w_down)


def _combine_kernel(pos_ref, y_hbm, x1_ref, rt_ref, o_ref, ybuf, sems):
    i = pl.program_id(0)
    n = pl.num_programs(0)
    tm = x1_ref.shape[0]
    K = TOPK_IN_GROUP
    slot = lax.rem(i, 2)

    def issue_tile(t, s):
        def issue(r, carry):
            a = (t * tm + r) * K
            for k in range(K):
                _row_copy(y_hbm, pos_ref[a + k], ybuf.at[s, k], r, sems.at[s]).start(priority=k)
            return carry

        lax.fori_loop(0, tm, issue, 0, unroll=8)

    @pl.when(i == 0)
    def _():
        issue_tile(0, 0)

    @pl.when(i + 1 < n)
    def _():
        issue_tile(i + 1, 1 - slot)

    for k in range(K):
        pltpu.make_async_copy(y_hbm.at[pl.ds(0, tm), :], ybuf.at[slot, k], sems.at[slot]).wait()
    w = rt_ref[...]
    yb = ybuf[slot]
    o_ref[...] = x1_ref[...] + (yb[0] * w[:, K:K + 1] + yb[1] * w[:, K + 1:K + 2])


def _combine(pos, y_rows, x1, rt, tm):
    T, D = x1.shape
    grid_spec = pltpu.PrefetchScalarGridSpec(
        num_scalar_prefetch=1,
        grid=(T // tm,),
        in_specs=[
            pl.BlockSpec(memory_space=pl.ANY),
            pl.BlockSpec((tm, D), lambda i, pos: (i, 0)),
            pl.BlockSpec((tm, LANES), lambda i, pos: (i, 0)),
        ],
        out_specs=pl.BlockSpec((tm, D), lambda i, pos: (i, 0)),
        scratch_shapes=[pltpu.VMEM((2, TOPK_IN_GROUP, tm, D), F32), pltpu.SemaphoreType.DMA((2,))],
    )
    return pl.pallas_call(
        _combine_kernel,
        grid_spec=grid_spec,
        out_shape=jax.ShapeDtypeStruct((T, D), F32),
        compiler_params=_params(("arbitrary",)),
        name="combine",
    )(pos, y_rows, x1, rt)


def _block_layout(rt, counts, T):
    K = TOPK_IN_GROUP
    n_rows = (K * T + N_EXPERTS * (MOE_BLOCK - 1) + MOE_BLOCK - 1) // MOE_BLOCK * MOE_BLOCK
    n_blocks = n_rows // MOE_BLOCK
    cnt = counts[0, :N_EXPERTS].astype(jnp.int32)
    padded = (cnt + MOE_BLOCK - 1) // MOE_BLOCK * MOE_BLOCK
    pad_end = jnp.cumsum(padded)
    pad_start = pad_end - padded
    e = rt[:, 0:K].astype(jnp.int32)
    rank = rt[:, 2 * K:3 * K].astype(jnp.int32)
    pos = (pad_start[e] + rank).reshape(-1)
    tile_start = (jnp.concatenate([pad_start, pad_end[-1:]]) // MOE_BLOCK).astype(jnp.int32)
    pad_rows = jnp.concatenate([pad_start + cnt, pad_end, pad_end[-1:] // MOE_BLOCK]).astype(jnp.int32)
    return tile_start, pos, pad_rows, n_rows


def _layer(x, norm1_g, w_in, cmp_pos, cmp_w1, cmp_w2, q_norm_g, k_norm_g, ret_gn_g, ret_gn_b,
           w_out, norm2_g, w_rg, b_rg, w_re, b_re, w_eg, w_eu, w_ed, tiles):
    B, S, D = x.shape
    T = B * S
    xt = x.reshape(T, D)

    w_main = jnp.concatenate([w_in[:, GATE_COL0 + NSA_GATE_WIDTH:], w_in[:, :GATE_COL0]], axis=1).astype(BF16)
    gpg = NSA_GATE_WIDTH // NSA_KV_GROUPS
    w_gate = jnp.concatenate(
        [jnp.pad(w_in[:, GATE_COL0 + g * gpg:GATE_COL0 + (g + 1) * gpg], ((0, 0), (0, LANES - gpg)))
         for g in range(NSA_KV_GROUPS)], axis=1).astype(BF16)
    proj, gate = _in_proj(xt, norm1_g.reshape(1, D), w_main, w_gate, tiles["tm_in"], tiles["tn_in"])

    cmp_kv = _compress(proj, cmp_pos, cmp_w1, cmp_w2, k_norm_g, B, S)
    o_nsa = _nsa(proj, gate, cmp_kv, q_norm_g.reshape(1, HEAD_DIM), k_norm_g, B, S, tiles["tq"], tiles["tk"])

    o_ret = _retention(proj, ret_gn_g, ret_gn_b, B, S)

    n_r = N_GROUPS + N_EXPERTS
    w_router = jnp.pad(jnp.concatenate([w_rg, w_re], axis=1), ((0, 0), (0, LANES - n_r)))
    b_router = jnp.pad(jnp.concatenate([b_rg, b_re]), (0, LANES - n_r)).reshape(1, LANES)
    x1, h2, rt, counts = _out_proj(o_nsa, o_ret, w_out.astype(BF16), xt, norm2_g.reshape(1, D),
                                   w_router, b_router, tiles["tm_out"])

    tile_start, pos, pad_rows, n_rows = _block_layout(rt, counts, T)
    xs = _dispatch(pos, pad_rows, h2, n_rows, tiles["tm_dsp"])
    y_rows = _experts(xs, tile_start, w_eg, w_eu, w_ed)
    out = _combine(pos, y_rows, x1, rt, tiles["tm_cmb"])
    return out.reshape(B, S, D)


def _tiles(T, S):
    return {
        "tm_in": min(2048, T), "tn_in": 512,
        "tq": min(256, S), "tk": min(512, S),
        "tm_out": min(512, T),
        "tm_dsp": min(1024, T),
        "tm_cmb": min(256, T),
    }


def kernel(x, norm1_g, w_in, cmp_pos, cmp_w1, cmp_w2, q_norm_g, k_norm_g, ret_gn_g, ret_gn_b, w_out, norm2_g, w_router_group, b_router_group, w_router_expert, b_router_expert, w_exp_gate, w_exp_up, w_exp_down):
    B, S, _ = x.shape
    tiles = _tiles(B * S, S)
    for l in range(norm1_g.shape[0]):
        x = _layer(x, norm1_g[l], w_in[l], cmp_pos[l], cmp_w1[l], cmp_w2[l], q_norm_g[l], k_norm_g[l],
                   ret_gn_g[l], ret_gn_b[l], w_out[l], norm2_g[l], w_router_group[l], b_router_group[l],
                   w_router_expert[l], b_router_expert[l], w_exp_gate[l], w_exp_up[l], w_exp_down[l], tiles)
    return x
```

```python
import functools

import numpy as np
import jax
import jax.numpy as jnp
from jax import lax
from jax.experimental import pallas as pl
from jax.experimental.pallas import tpu as pltpu

F32 = jnp.float32
BF16 = jnp.bfloat16

D_MODEL = 2048
NSA_HEADS = 8
NSA_KV_GROUPS = 2
NSA_GROUP_HEADS = NSA_HEADS // NSA_KV_GROUPS
HEAD_DIM = 128
RET_HEADS = 8
CMP_BLOCK = 32
CMP_STRIDE = 16
SEL_BLOCK = 64
SEL_TOPK = 8
WINDOW = 512
RET_CHUNK = 128
ROPE_BASE = 10000.0
N_GROUPS = 4
EXPERTS_PER_GROUP = 8
N_EXPERTS = N_GROUPS * EXPERTS_PER_GROUP
TOPK_IN_GROUP = 2
D_EXPERT = 512
MOE_BLOCK = 256
RMS_EPS = 1e-6
GN_EPS = 1e-5
NEG_INF = -1e30
FORCED_SCORE = 1e6
EXP_ROW0 = 8
LOG2E = 1.4426950408889634

NSA_Q_WIDTH = NSA_HEADS * HEAD_DIM
NSA_KV_WIDTH = NSA_KV_GROUPS * HEAD_DIM
NSA_GATE_WIDTH = 3 * NSA_HEADS
RET_WIDTH = RET_HEADS * HEAD_DIM
GATE_COL0 = NSA_Q_WIDTH + 6 * NSA_KV_WIDTH
LANES = 128
VMEM_LIMIT = 56 * 1024 * 1024

Q_COL0 = 4 * RET_WIDTH
KV_COL0 = Q_COL0 + NSA_Q_WIDTH
KC_BLK, VC_BLK, KS_BLK, VS_BLK, KW_BLK, VW_BLK = [KV_COL0 // LANES + 2 * t for t in range(6)]


def _rms(xf, g):
    return xf * lax.rsqrt(jnp.mean(xf * xf, axis=-1, keepdims=True) + RMS_EPS) * g


def _dot(a, b):
    return jnp.dot(a, b, preferred_element_type=F32)


def _dot_nt(a, b):
    return lax.dot_general(a, b, (((1,), (1,)), ((), ())), preferred_element_type=F32)


def _dot_tn(a, b):
    return lax.dot_general(a, b, (((0,), (0,)), ((), ())), preferred_element_type=F32)


def _split3(p):
    p1 = p.astype(BF16)
    r1 = p - p1.astype(F32)
    p2 = r1.astype(BF16)
    p3 = (r1 - p2.astype(F32)).astype(BF16)
    return p1, p2, p3


def _params(sem):
    return pltpu.CompilerParams(dimension_semantics=sem, vmem_limit_bytes=VMEM_LIMIT)


IN_CHUNK = 256


def _in_proj_kernel(x_hbm, g_ref, w_ref, wg_ref, o_ref, og_ref, h0, h1, xc0, xc1, sem, *, tm):
    m = pl.program_id(0)
    n = pl.program_id(1)
    ck = IN_CHUNK
    nc = tm // ck
    hs = (h0, h1)
    xcs = (xc0, xc1)
    g = g_ref[...]

    def chunk_copy(tile, c, slot):
        row0 = pl.multiple_of(tile * tm + c * ck, ck)
        return pltpu.make_async_copy(x_hbm.at[pl.ds(row0, ck), :], xcs[slot], sem.at[slot])

    def normalise(slot, dst, c):
        dst[pl.ds(pl.multiple_of(c * ck, ck), ck), :] = _rms(xcs[slot][...], g).astype(BF16)

    @pl.when((m == 0) & (n == 0))
    def _():
        chunk_copy(0, 0, 0).start()
        for c in range(nc):
            if c + 1 < nc:
                chunk_copy(0, c + 1, (c + 1) % 2).start()
            chunk_copy(0, c, c % 2).wait()
            normalise(c % 2, h0, c)

    has_next = m + 1 < pl.num_programs(0)
    for slot in range(2):
        @pl.when(has_next & (n >= 1) & (n <= nc) & (lax.rem(n - 1, 2) == slot))
        def _(slot=slot):
            chunk_copy(m + 1, n - 1, slot).wait()

        @pl.when(has_next & (n < nc) & (lax.rem(n, 2) == slot))
        def _(slot=slot):
            chunk_copy(m + 1, n, slot).start()

    c_prev = jnp.where(n == 0, nc - 1, jnp.clip(n - 1, 0, nc - 1))
    even_chunk = lax.rem(c_prev, 2) == 0
    for cur in range(2):
        for slot in range(2):
            @pl.when((lax.rem(m, 2) == cur) & (even_chunk == (slot == 0)))
            def _(cur=cur, slot=slot):
                @pl.when(n == 0)
                def _():
                    og_ref[...] = _dot(hs[cur][...], wg_ref[...])

                normalise(slot, hs[1 - cur], c_prev)
                o_ref[...] = _dot(hs[cur][...], w_ref[...])


def _in_proj(xt, g1, w_main, w_gate, tm, tn):
    T, D = xt.shape
    n_main = w_main.shape[1]
    n_gate = w_gate.shape[1]
    assert tm % (2 * IN_CHUNK) == 0 and tm // IN_CHUNK < n_main // tn
    return pl.pallas_call(
        functools.partial(_in_proj_kernel, tm=tm),
        grid=(T // tm, n_main // tn),
        in_specs=[
            pl.BlockSpec(memory_space=pl.ANY),
            pl.BlockSpec((1, D), lambda m, n: (0, 0)),
            pl.BlockSpec((D, tn), lambda m, n: (0, n)),
            pl.BlockSpec((D, n_gate), lambda m, n: (0, 0)),
        ],
        out_specs=[
            pl.BlockSpec((tm, tn), lambda m, n: (m, n)),
            pl.BlockSpec((tm, n_gate), lambda m, n: (m, 0)),
        ],
        out_shape=[
            jax.ShapeDtypeStruct((T, n_main), F32),
            jax.ShapeDtypeStruct((T, n_gate), F32),
        ],
        scratch_shapes=[pltpu.VMEM((tm, D), BF16), pltpu.VMEM((tm, D), BF16),
                        pltpu.VMEM((IN_CHUNK, D), F32), pltpu.VMEM((IN_CHUNK, D), F32),
                        pltpu.SemaphoreType.DMA((2,))],
        compiler_params=_params(("arbitrary", "arbitrary")),
        name="in_proj",
    )(xt, g1, w_main, w_gate)


def _compress_kernel(x_ref, pos_ref, w1_ref, w2_ref, kg_ref, o_ref, xs_scr, *, S, n_pad):
    kv = pl.program_id(1)
    xs_scr[0:S, :] = x_ref[...]
    xs_scr[S:S + CMP_STRIDE, :] = jnp.zeros((CMP_STRIDE, HEAD_DIM), F32)
    acc = jnp.zeros((n_pad, HEAD_DIM), F32)
    for l in range(CMP_BLOCK):
        tb = xs_scr[pl.ds(l, n_pad, stride=CMP_STRIDE), :] + pos_ref[0, l:l + 1, :]
        acc = acc + _dot(tb.astype(BF16), w1_ref[0, l].astype(BF16))
    hid = acc * jax.nn.sigmoid(acc)
    out = _dot(hid.astype(BF16), w2_ref[0].astype(BF16))
    normed = _rms(out, kg_ref[0:1, :])
    o_ref[0, 0, 0] = jnp.where(kv == 0, normed, out).astype(BF16)


def _compress(proj, cmp_pos, cmp_w1, cmp_w2, k_norm_g, B, S):
    n_pad = S // CMP_STRIDE
    G = NSA_KV_GROUPS
    kern = functools.partial(_compress_kernel, S=S, n_pad=n_pad)
    return pl.pallas_call(
        kern,
        grid=(B, 2, G),
        in_specs=[
            pl.BlockSpec((S, HEAD_DIM), lambda b, kv, g: (b, KC_BLK + 2 * kv + g)),
            pl.BlockSpec((1, CMP_BLOCK, HEAD_DIM), lambda b, kv, g: (kv, 0, 0)),
            pl.BlockSpec((1, CMP_BLOCK, HEAD_DIM, HEAD_DIM), lambda b, kv, g: (kv, 0, 0, 0)),
            pl.BlockSpec((1, HEAD_DIM, HEAD_DIM), lambda b, kv, g: (kv, 0, 0)),
            pl.BlockSpec((3, HEAD_DIM), lambda b, kv, g: (0, 0)),
        ],
        out_specs=pl.BlockSpec((1, 1, 1, n_pad, HEAD_DIM), lambda b, kv, g: (b, kv, g, 0, 0)),
        out_shape=jax.ShapeDtypeStruct((B, 2, G, n_pad, HEAD_DIM), BF16),
        scratch_shapes=[pltpu.VMEM((S + CMP_STRIDE, HEAD_DIM), F32)],
        compiler_params=_params(("parallel", "parallel", "parallel")),
        name="compress",
    )(proj, cmp_pos, cmp_w1, cmp_w2, k_norm_g)


def _masked_softmax(s, msk):
    sm = jnp.where(msk, s, NEG_INF)
    m = jnp.max(sm, axis=-1, keepdims=True)
    e = jnp.where(msk, jnp.exp(sm - m), 0.0)
    den = jnp.sum(e, axis=-1, keepdims=True)
    return e / jnp.where(den > 0.0, den, 1.0)


def _nsa_kernel(q_ref, kc_ref, vc_ref, ks_ref, vs_ref, kw_ref, vw_ref, gate_ref, qg_ref, kg_ref,
                ovt_ref, ext_ref, o_ref, ksa, vsa, kwn, vwa, l_scr, acc_scr, *, S, tq, tk, wk):
    i = pl.program_id(2)
    H = NSA_GROUP_HEADS
    scale = HEAD_DIM ** -0.5
    c2 = scale * LOG2E
    n_cp = S // CMP_STRIDE
    n_sblk = S // SEL_BLOCK

    @pl.when(i == 0)
    def _():
        ksa[:, 0:HEAD_DIM] = _rms(ks_ref[...], kg_ref[1:2, :]).astype(BF16)
        ksa[:, HEAD_DIM:2 * HEAD_DIM] = ext_ref[...]
        kwn[...] = _rms(kw_ref[...], kg_ref[2:3, :]).astype(BF16)
        ones = jnp.ones((S, HEAD_DIM), BF16)
        vsa[:, 0:HEAD_DIM] = vs_ref[...].astype(BF16)
        vsa[:, HEAD_DIM:2 * HEAD_DIM] = ones
        vwa[:, 0:HEAD_DIM] = vw_ref[...].astype(BF16)
        vwa[:, HEAD_DIM:2 * HEAD_DIM] = ones

    pos = i * tq + lax.broadcasted_iota(jnp.int32, (tq, 1), 0)
    qg = qg_ref[...]
    q4 = jnp.concatenate([_rms(q_ref[:, h * HEAD_DIM:(h + 1) * HEAD_DIM], qg).astype(BF16)
                          for h in range(H)], axis=0)
    rows = [slice(h * tq, (h + 1) * tq) for h in range(H)]

    def softmax_pv(q_chains, k, v, tail_bias):
        head = k.shape[0] - tail_bias.shape[1]
        outs, sums = [], []
        for q2 in q_chains:
            s2 = _dot_nt(q2, k)
            es = []
            hpc = q2.shape[0] // tq
            for hh in range(hpc):
                t = s2[hh * tq:(hh + 1) * tq] * c2
                tb = t[:, head:] + tail_bias
                m = jnp.max(tb, axis=-1, keepdims=True)
                if head:
                    ta = t[:, :head]
                    m = jnp.maximum(m, jnp.max(ta, axis=-1, keepdims=True))
                    es.append(jnp.concatenate([jnp.exp2(ta - m).astype(BF16),
                                               jnp.exp2(tb - m).astype(BF16)], axis=1))
                else:
                    es.append(jnp.exp2(tb - m).astype(BF16))
            o2 = _dot(es[0] if hpc == 1 else jnp.concatenate(es, axis=0), v)
            outs += [o2[hh * tq:(hh + 1) * tq, 0:HEAD_DIM] for hh in range(hpc)]
            sums += [o2[hh * tq:(hh + 1) * tq, HEAD_DIM:2 * HEAD_DIM] for hh in range(hpc)]
        return outs, sums

    ncol = lax.broadcasted_iota(jnp.int32, (1, n_cp), 1)
    cmask = (ncol * CMP_STRIDE + (CMP_BLOCK - 1)) <= pos
    s_c4 = _dot_nt(q4, kc_ref[0, 0, 0]) * scale
    ps = [_masked_softmax(s_c4[r], cmask) for r in rows]
    psum = ps[0] + ps[1] + ps[2] + ps[3]
    o_c4 = _dot(jnp.concatenate([p.astype(BF16) for p in ps], axis=0), vc_ref[0, 0, 0])

    ovt = ovt_ref[...]
    p1, p2, p3 = _split3(psum)
    imp_t = (_dot_nt(ovt, p1) + _dot_nt(ovt, p2) + _dot_nt(ovt, p3))[0:n_sblk, :]
    pos_t = i * tq + lax.broadcasted_iota(jnp.int32, (1, tq), 1)
    jrow = lax.broadcasted_iota(jnp.int32, (n_sblk, 1), 0)
    cur_t = pos_t // SEL_BLOCK
    valid_t = jrow * SEL_BLOCK <= pos_t
    forced_t = (jrow == 0) | (jrow == cur_t) | (jrow == cur_t - 1)
    score = jnp.where(valid_t, jnp.where(forced_t, FORCED_SCORE, imp_t), -jnp.inf)
    rank = jnp.zeros((n_sblk, tq), jnp.int32)
    for i2 in range(n_sblk):
        si = score[i2:i2 + 1, :]
        ahead = (si > score) | ((si == score) & (jrow > i2))
        rank = rank + jnp.where(ahead, 1, 0)
    drop_t = jnp.where((rank < SEL_TOPK) & valid_t, 0.0, NEG_INF)
    drop_t = jnp.concatenate([drop_t, jnp.full((LANES - n_sblk, tq), NEG_INF, F32)], axis=0)
    drop = drop_t.T.astype(BF16)
    qa_pairs = [jnp.concatenate([q4[r], drop], axis=1) for r in rows]

    kst = pl.multiple_of(jnp.clip(i * tq - WINDOW, 0, S - wk), tq)
    wpos = kst + lax.broadcasted_iota(jnp.int32, (1, wk), 1)
    wbias = jnp.where((wpos <= pos) & (wpos > pos - WINDOW), 0.0, NEG_INF)
    o_ws, lws = softmax_pv([q4[r] for r in rows],
                           kwn[pl.ds(kst, wk), :], vwa[pl.ds(kst, wk), :], wbias)

    def slc_prefix(n):
        kpos = (n - tk) + lax.broadcasted_iota(jnp.int32, (1, tk), 1)
        causal = jnp.where(kpos <= pos, 0.0, NEG_INF)
        outs, sums = softmax_pv(qa_pairs, ksa[0:n, :], vsa[0:n, :], causal)
        for r, o, l in zip(rows, outs, sums):
            acc_scr[r] = o
            l_scr[r] = l

    n_cls = S // tk
    cls = ((i + 1) * tq - 1) // tk
    for c in range(n_cls):
        pl.when(cls == c)(functools.partial(slc_prefix, (c + 1) * tk))

    gates = jax.nn.sigmoid(gate_ref[...])
    for h, r in enumerate(rows):
        c = 3 * h
        o = (o_c4[r] * gates[:, c:c + 1]
             + acc_scr[r] * (gates[:, c + 1:c + 2] / l_scr[r])
             + o_ws[h] * (gates[:, c + 2:c + 3] / lws[h]))
        o_ref[:, h * HEAD_DIM:(h + 1) * HEAD_DIM] = o.astype(BF16)


def _nsa_tables(S):
    n_cp, n_cmp, n_sblk = S // CMP_STRIDE, (S - CMP_BLOCK) // CMP_STRIDE + 1, S // SEL_BLOCK
    n = np.arange(n_cp)[None, :] * CMP_STRIDE
    j = np.arange(LANES)[:, None]
    ovt = ((n < (j + 1) * SEL_BLOCK) & (n + CMP_BLOCK > j * SEL_BLOCK)
           & (np.arange(n_cp)[None, :] < n_cmp) & (j < n_sblk))
    key_block = (np.arange(S) // SEL_BLOCK)[:, None] == np.arange(LANES)[None, :]
    return jnp.asarray(ovt, BF16), jnp.asarray(key_block, BF16)


def _nsa(proj, gate, cmp_kv, q_norm_g, k_norm_g, B, S, tq, tk):
    G = NSA_KV_GROUPS
    H = NSA_GROUP_HEADS
    n_cp = S // CMP_STRIDE
    wk = min(S, WINDOW + tq)
    nq = S // tq
    gw = H * HEAD_DIM
    ovt, key_block = _nsa_tables(S)
    kern = functools.partial(_nsa_kernel, S=S, tq=tq, tk=tk, wk=wk)
    kv_spec = lambda blk: pl.BlockSpec((S, HEAD_DIM), lambda b, g, i: (b, blk + g))
    return pl.pallas_call(
        kern,
        grid=(B, G, nq),
        in_specs=[
            pl.BlockSpec((tq, gw), lambda b, g, i: (b * nq + i, Q_COL0 // gw + g)),
            pl.BlockSpec((1, 1, 1, n_cp, HEAD_DIM), lambda b, g, i: (b, 0, g, 0, 0)),
            pl.BlockSpec((1, 1, 1, n_cp, HEAD_DIM), lambda b, g, i: (b, 1, g, 0, 0)),
            kv_spec(KS_BLK), kv_spec(VS_BLK), kv_spec(KW_BLK), kv_spec(VW_BLK),
            pl.BlockSpec((tq, LANES), lambda b, g, i: (b * nq + i, g)),
            pl.BlockSpec((1, HEAD_DIM), lambda b, g, i: (0, 0)),
            pl.BlockSpec((3, HEAD_DIM), lambda b, g, i: (0, 0)),
            pl.BlockSpec((LANES, n_cp), lambda b, g, i: (0, 0)),
            pl.BlockSpec((S, LANES), lambda b, g, i: (0, 0)),
        ],
        out_specs=pl.BlockSpec((tq, gw), lambda b, g, i: (b * nq + i, g)),
        out_shape=jax.ShapeDtypeStruct((B * S, NSA_Q_WIDTH), BF16),
        scratch_shapes=[
            pltpu.VMEM((S, 2 * HEAD_DIM), BF16), pltpu.VMEM((S, 2 * HEAD_DIM), BF16),
            pltpu.VMEM((S, HEAD_DIM), BF16), pltpu.VMEM((S, 2 * HEAD_DIM), BF16),
            pltpu.VMEM((H * tq, HEAD_DIM), F32), pltpu.VMEM((H * tq, HEAD_DIM), F32)],
        compiler_params=_params(("parallel", "parallel", "arbitrary")),
        name="nsa",
    )(proj, cmp_kv, cmp_kv, proj, proj, proj, proj, gate, q_norm_g, k_norm_g, ovt, key_block)


def _ret_kernel(rq_ref, rk_ref, rv_ref, rg_ref, cos_ref, sin_ref, dec_ref, xi_ref, zeta_ref, cd_ref,
                gg_ref, gb_ref, o_ref, r_scr):
    scale = HEAD_DIM ** -0.5

    @pl.when(pl.program_id(1) == 0)
    def _():
        r_scr[...] = jnp.zeros(r_scr.shape, F32)

    cos = cos_ref[...]
    sin = sin_ref[...]
    for h in range(RET_HEADS):
        sl = slice(h * HEAD_DIM, (h + 1) * HEAD_DIM)
        q = rq_ref[:, sl]
        k = rk_ref[:, sl]
        qf = q * cos + pltpu.roll(q, HEAD_DIM // 2, 1) * sin
        kf = (k * cos + pltpu.roll(k, HEAD_DIM // 2, 1) * sin) * scale
        v = rv_ref[:, sl].astype(BF16)
        qb = qf.astype(BF16)
        r_old = r_scr[h]
        a = _dot_nt(qb, kf.astype(BF16)) * dec_ref[h]
        o = _dot(a.astype(BF16), v) + _dot(qb, r_old.astype(BF16)) * xi_ref[h]
        r_scr[h] = r_old * cd_ref[h:h + 1, :] + _dot_tn((kf * zeta_ref[h]).astype(BF16), v)
        mu = jnp.mean(o, axis=-1, keepdims=True)
        d = o - mu
        var = jnp.mean(d * d, axis=-1, keepdims=True)
        y = d * lax.rsqrt(var + GN_EPS) * gg_ref[h:h + 1, :] + gb_ref[h:h + 1, :]
        gt = rg_ref[:, sl]
        o_ref[:, sl] = (gt * jax.nn.sigmoid(gt) * y).astype(BF16)


def _ret_tables(S):
    C, H, half = RET_CHUNK, RET_HEADS, HEAD_DIM // 2
    inv_freq = ROPE_BASE ** (-jnp.arange(half, dtype=F32) / half)
    ang = jnp.arange(S, dtype=F32)[:, None] * inv_freq[None, :]
    cos2 = jnp.concatenate([jnp.cos(ang), jnp.cos(ang)], axis=1)
    sin2 = jnp.concatenate([-jnp.sin(ang), jnp.sin(ang)], axis=1)
    log_g = jnp.log1p(-jnp.exp2(-5.0 - jnp.arange(H, dtype=F32)))
    n = jnp.arange(C, dtype=F32)
    diff = n[:, None] - n[None, :]
    dec = jnp.where(diff >= 0, jnp.exp(log_g[:, None, None] * jnp.maximum(diff, 0.0)), 0.0)
    lanes = lambda t: jnp.broadcast_to(t[..., None], t.shape + (HEAD_DIM,))
    xi = lanes(jnp.exp(log_g[:, None] * (n + 1.0)))
    zeta = lanes(jnp.exp(log_g[:, None] * (C - 1.0 - n)))
    cd = lanes(jnp.exp(log_g * C))
    return cos2, sin2, dec, xi, zeta, cd


def _retention(proj, gn_g, gn_b, B, S):
    C, H = RET_CHUNK, RET_HEADS
    n_ch = S // C
    cos2, sin2, dec, xi, zeta, cd = _ret_tables(S)
    spec = lambda k: pl.BlockSpec((C, RET_WIDTH), lambda b, c: (b * n_ch + c, k))
    whole = lambda a: pl.BlockSpec(a.shape, lambda b, c: (0,) * a.ndim)
    return pl.pallas_call(
        _ret_kernel,
        grid=(B, n_ch),
        in_specs=[
            spec(0), spec(1), spec(2), spec(3),
            pl.BlockSpec((C, HEAD_DIM), lambda b, c: (c, 0)),
            pl.BlockSpec((C, HEAD_DIM), lambda b, c: (c, 0)),
            whole(dec), whole(xi), whole(zeta), whole(cd), whole(gn_g), whole(gn_b),
        ],
        out_specs=pl.BlockSpec((C, RET_WIDTH), lambda b, c: (b * n_ch + c, 0)),
        out_shape=jax.ShapeDtypeStruct((B * S, RET_WIDTH), BF16),
        scratch_shapes=[pltpu.VMEM((H, HEAD_DIM, HEAD_DIM), F32)],
        compiler_params=_params(("parallel", "arbitrary")),
        name="retention",
    )(proj, proj, proj, proj, cos2, sin2, dec, xi, zeta, cd, gn_g, gn_b)


def _lane_max(v):
    return jnp.max(v, axis=-1, keepdims=True)


def _lane_min(v):
    return jnp.min(v, axis=-1, keepdims=True)


def _out_kernel(on_ref, or_ref, w_ref, x_ref, g2_ref, wr_ref, br_ref, tri_ref,
                x1_ref, h2_ref, rt_ref, cnt_ref, cnt_scr):
    @pl.when(pl.program_id(0) == 0)
    def _():
        cnt_scr[...] = jnp.zeros(cnt_scr.shape, F32)

    half = on_ref.shape[1]
    acc = _dot(on_ref[...], w_ref[0:half, :]) + _dot(or_ref[...], w_ref[half:2 * half, :])
    x1 = x_ref[...] + acc
    x1_ref[...] = x1
    h2 = _rms(x1, g2_ref[...])
    h2_ref[...] = h2

    h_hi = h2.astype(BF16)
    h_lo = (h2 - h_hi.astype(F32)).astype(BF16)
    wr = wr_ref[...]
    w_hi = wr.astype(BF16)
    w_lo = (wr - w_hi.astype(F32)).astype(BF16)
    logits = _dot(h_hi, w_hi) + _dot(h_lo, w_hi) + _dot(h_hi, w_lo) + br_ref[...]

    tm = logits.shape[0]
    lt = logits.T
    G8 = EXPERTS_PER_GROUP
    row = lax.broadcasted_iota(jnp.int32, (G8, 1), 0).astype(F32)
    big = float(G8)
    col_max = lambda v: jnp.max(v, axis=0, keepdims=True)
    col_min = lambda v: jnp.min(v, axis=0, keepdims=True)
    col_sum = lambda v: jnp.sum(v, axis=0, keepdims=True)
    gm = row < N_GROUPS
    gl = jnp.where(gm, lt[0:G8], -jnp.inf)
    ge = jnp.where(gm, jnp.exp(gl - col_max(gl)), 0.0)
    pg = ge / col_sum(ge)
    g_top = col_max(pg)
    g_idx = col_min(jnp.where(gm & (pg == g_top), row, big))
    el = lt[EXP_ROW0:EXP_ROW0 + G8]
    for g in range(1, N_GROUPS):
        el = jnp.where(g_idx == float(g), lt[EXP_ROW0 + g * G8:EXP_ROW0 + (g + 1) * G8], el)
    ee = jnp.exp(el - col_max(el))
    pe = ee / col_sum(ee)
    t1 = col_max(pe)
    i1 = col_min(jnp.where(pe == t1, row, big))
    rest = row != i1
    t2 = col_max(jnp.where(rest, pe, -1.0))
    i2 = col_min(jnp.where(rest & (pe == t2), row, big))
    tsum = t1 + t2
    w1 = g_top * t1 / tsum
    w2 = g_top * t2 / tsum
    e1 = g_idx * float(G8) + i1
    e2 = g_idx * float(G8) + i2
    erow = lax.broadcasted_iota(jnp.int32, (N_EXPERTS, 1), 0).astype(F32)
    oh1 = jnp.where(erow == e1, 1.0, 0.0)
    oh2 = jnp.where(erow == e2, 1.0, 0.0)
    both = oh1 + oh2
    before = _dot(both.astype(BF16), tri_ref[...]) + cnt_scr[...]
    r1 = col_sum(before * oh1)
    r2 = col_sum(before * oh2)
    cnt_scr[...] = cnt_scr[...] + jnp.sum(both, axis=1, keepdims=True)
    cnt_ref[...] = jnp.broadcast_to(cnt_scr[...], cnt_ref.shape)
    rt_ref[...] = jnp.concatenate([e1, e2, w1, w2, r1, r2, jnp.zeros((2, tm), F32)], axis=0)


def _out_proj(o_nsa, o_ret, w_out, xt, g2, w_router, b_router, tm):
    T, D = xt.shape
    half = o_nsa.shape[1]
    tri = jnp.asarray(np.triu(np.ones((tm, tm), np.float32), 1), BF16)
    return pl.pallas_call(
        _out_kernel,
        grid=(T // tm,),
        in_specs=[
            pl.BlockSpec((tm, half), lambda m: (m, 0)),
            pl.BlockSpec((tm, half), lambda m: (m, 0)),
            pl.BlockSpec((2 * half, D), lambda m: (0, 0), pipeline_mode=pl.Buffered(1)),
            pl.BlockSpec((tm, D), lambda m: (m, 0)),
            pl.BlockSpec((1, D), lambda m: (0, 0)),
            pl.BlockSpec((D, LANES), lambda m: (0, 0)),
            pl.BlockSpec((1, LANES), lambda m: (0, 0)),
            pl.BlockSpec((tm, tm), lambda m: (0, 0)),
        ],
        out_specs=[
            pl.BlockSpec((tm, D), lambda m: (m, 0)),
            pl.BlockSpec((tm, D), lambda m: (m, 0)),
            pl.BlockSpec((8, tm), lambda m: (0, m)),
            pl.BlockSpec((N_EXPERTS, LANES), lambda m: (0, 0)),
        ],
        out_shape=[
            jax.ShapeDtypeStruct((T, D), F32),
            jax.ShapeDtypeStruct((T, D), F32),
            jax.ShapeDtypeStruct((8, T), F32),
            jax.ShapeDtypeStruct((N_EXPERTS, LANES), F32),
        ],
        scratch_shapes=[pltpu.VMEM((N_EXPERTS, 1), F32)],
        compiler_params=_params(("arbitrary",)),
        name="out_proj",
    )(o_nsa, o_ret, w_out, xt, g2, w_router, b_router, tri)


def _row_copy(src, row, dst, slot, sem):
    return pltpu.make_async_copy(src.at[pl.ds(row, 1), :], dst.at[pl.ds(slot, 1), :], sem)


def _dispatch_kernel(pos_ref, pad_ref, h2_ref, xs_hbm, zrow, sem, zsem, *, n_tok):
    i = pl.program_id(0)
    tm = h2_ref.shape[0]
    K = TOPK_IN_GROUP

    def issue(r, carry):
        for k in range(K):
            _row_copy(h2_ref, r, xs_hbm, pos_ref[k * n_tok + i * tm + r], sem).start()
        return carry

    lax.fori_loop(0, tm, issue, 0, unroll=8)

    @pl.when(i == pl.num_programs(0) - 1)
    def _():
        zrow[...] = jnp.zeros(zrow.shape, F32)
        for e in range(N_EXPERTS):
            lo, hi = pad_ref[e], pad_ref[N_EXPERTS + e]

            def fill(r, carry):
                _row_copy(zrow, 0, xs_hbm, r, zsem).start()
                return carry

            lax.fori_loop(lo, hi, fill, 0)

            def fill_wait(r, carry):
                _row_copy(zrow, 0, xs_hbm, r, zsem).wait()
                return carry

            lax.fori_loop(lo, hi, fill_wait, 0)

        M = zrow.shape[0]

        def tail_copy(t):
            return pltpu.make_async_copy(zrow, xs_hbm.at[pl.ds(pl.multiple_of(t * M, M), M), :], zsem)

        def tail(t, carry):
            tail_copy(t).start()
            return carry

        lax.fori_loop(pad_ref[2 * N_EXPERTS], xs_hbm.shape[0] // M, tail, 0)

        def tail_wait(t, carry):
            tail_copy(t).wait()
            return carry

        lax.fori_loop(pad_ref[2 * N_EXPERTS], xs_hbm.shape[0] // M, tail_wait, 0)

    for k in range(K):
        pltpu.make_async_copy(h2_ref, xs_hbm.at[pl.ds(0, tm), :], sem).wait()


def _dispatch(pos, pad_rows, h2, n_rows, tm):
    T, D = h2.shape
    grid_spec = pltpu.PrefetchScalarGridSpec(
        num_scalar_prefetch=2,
        grid=(T // tm,),
        in_specs=[pl.BlockSpec((tm, D), lambda i, pos, pad: (i, 0))],
        out_specs=pl.BlockSpec(memory_space=pl.ANY),
        scratch_shapes=[pltpu.VMEM((MOE_BLOCK, D), F32), pltpu.SemaphoreType.DMA, pltpu.SemaphoreType.DMA],
    )
    return pl.pallas_call(
        functools.partial(_dispatch_kernel, n_tok=T),
        grid_spec=grid_spec,
        out_shape=jax.ShapeDtypeStruct((n_rows, D), F32),
        compiler_params=_params(("arbitrary",)),
        name="dispatch",
    )(pos, pad_rows, h2)


def _expert_kernel(ts_ref, xs_hbm, wg_ref, wu_ref, wd_ref, y_hbm,
                   xb0, xb1, yb0, yb1, wg_b, wu_b, wd_b, gsem, osem, *, n_blocks):
    e = pl.program_id(0)
    M = MOE_BLOCK
    t0 = ts_ref[e]
    t1 = ts_ref[e + 1]
    n_used = ts_ref[N_EXPERTS]
    xbufs = (xb0, xb1)
    ybufs = (yb0, yb1)

    def rows(t):
        return pl.ds(pl.multiple_of(t * M, M), M)

    def in_copy(t, p):
        return pltpu.make_async_copy(xs_hbm.at[rows(t), :], xbufs[p], gsem.at[p])

    def out_copy(t, p):
        return pltpu.make_async_copy(ybufs[p], y_hbm.at[rows(t), :], osem.at[p])

    @pl.when(e == 0)
    def _():
        in_copy(0, 0).start()

    @pl.when(t1 > t0)
    def _():
        wg_b[...] = wg_ref[0].astype(BF16)
        wu_b[...] = wu_ref[0].astype(BF16)
        wd_b[...] = wd_ref[0].astype(BF16)

    def tile_body(t, p):
        in_copy(t, p).wait()

        @pl.when(t + 1 < n_used)
        def _():
            in_copy(t + 1, 1 - p).start()

        @pl.when(t >= 2)
        def _():
            out_copy(t - 2, p).wait()

        xb = xbufs[p][...].astype(BF16)
        hg = _dot(xb, wg_b[...])
        hu = _dot(xb, wu_b[...])
        hb = (hg * jax.nn.sigmoid(hg) * hu).astype(BF16)
        ybufs[p][...] = _dot(hb, wd_b[...])
        out_copy(t, p).start()

    def tile(t, carry):
        parity = lax.rem(t, 2)
        for p in range(2):
            pl.when(parity == p)(functools.partial(tile_body, t, p))
        return carry

    lax.fori_loop(t0, t1, tile, 0)

    @pl.when(e == N_EXPERTS - 1)
    def _():
        parity = lax.rem(n_used, 2)
        for p in range(2):
            @pl.when(parity == p)
            def _(p=p):
                out_copy(n_used - 1, 1 - p).wait()

                @pl.when(n_used >= 2)
                def _():
                    out_copy(n_used - 2, p).wait()

        yb0[...] = jnp.zeros(yb0.shape, F32)

        def clear(t, carry):
            out_copy(t, 0).start()
            return carry

        lax.fori_loop(n_used, n_blocks, clear, 0)

        def clear_wait(t, carry):
            out_copy(t, 0).wait()
            return carry

        lax.fori_loop(n_used, n_blocks, clear_wait, 0)


def _experts(xs, tile_start, w_gate, w_up, w_down):
    n_rows, D = xs.shape
    n_blocks = n_rows // MOE_BLOCK
    grid_spec = pltpu.PrefetchScalarGridSpec(
        num_scalar_prefetch=1,
        grid=(N_EXPERTS,),
        in_specs=[
            pl.BlockSpec(memory_space=pl.ANY),
            pl.BlockSpec((1, D, D_EXPERT), lambda e, ts: (e, 0, 0)),
            pl.BlockSpec((1, D, D_EXPERT), lambda e, ts: (e, 0, 0)),
            pl.BlockSpec((1, D_EXPERT, D), lambda e, ts: (e, 0, 0)),
        ],
        out_specs=pl.BlockSpec(memory_space=pl.ANY),
        scratch_shapes=[pltpu.VMEM((MOE_BLOCK, D), F32)] * 4 + [
            pltpu.VMEM((D, D_EXPERT), BF16), pltpu.VMEM((D, D_EXPERT), BF16),
            pltpu.VMEM((D_EXPERT, D), BF16),
            pltpu.SemaphoreType.DMA((2,)), pltpu.SemaphoreType.DMA((2,)),
        ],
    )
    return pl.pallas_call(
        functools.partial(_expert_kernel, n_blocks=n_blocks),
        grid_spec=grid_spec,
        out_shape=jax.ShapeDtypeStruct((n_rows, D), F32),
        compiler_params=_params(("arbitrary",)),
        name="experts",
    )(tile_start, xs, w_gate, w_up, w_down)


def _combine_kernel(pos_ref, y_hbm, x1_ref, rt_ref, o_ref, ybuf, sems, *, n_tok):
    i = pl.program_id(0)
    n = pl.num_programs(0)
    tm = x1_ref.shape[0]
    K = TOPK_IN_GROUP
    slot = lax.rem(i, 2)

    def issue_tile(t, s):
        def issue(r, carry):
            for k in range(K):
                _row_copy(y_hbm, pos_ref[k * n_tok + t * tm + r], ybuf.at[s, k], r,
                          sems.at[s]).start(priority=k)
            return carry

        lax.fori_loop(0, tm, issue, 0, unroll=8)

    @pl.when(i == 0)
    def _():
        issue_tile(0, 0)

    @pl.when(i + 1 < n)
    def _():
        issue_tile(i + 1, 1 - slot)

    for k in range(K):
        pltpu.make_async_copy(y_hbm.at[pl.ds(0, tm), :], ybuf.at[slot, k], sems.at[slot]).wait()
    w = rt_ref[...].T
    yb = ybuf[slot]
    o_ref[...] = x1_ref[...] + (yb[0] * w[:, K:K + 1] + yb[1] * w[:, K + 1:K + 2])


def _combine(pos, y_rows, x1, rt, tm):
    T, D = x1.shape
    grid_spec = pltpu.PrefetchScalarGridSpec(
        num_scalar_prefetch=1,
        grid=(T // tm,),
        in_specs=[
            pl.BlockSpec(memory_space=pl.ANY),
            pl.BlockSpec((tm, D), lambda i, pos: (i, 0)),
            pl.BlockSpec((8, tm), lambda i, pos: (0, i)),
        ],
        out_specs=pl.BlockSpec((tm, D), lambda i, pos: (i, 0)),
        scratch_shapes=[pltpu.VMEM((2, TOPK_IN_GROUP, tm, D), F32), pltpu.SemaphoreType.DMA((2,))],
    )
    return pl.pallas_call(
        functools.partial(_combine_kernel, n_tok=T),
        grid_spec=grid_spec,
        out_shape=jax.ShapeDtypeStruct((T, D), F32),
        compiler_params=_params(("arbitrary",)),
        name="combine",
    )(pos, y_rows, x1, rt)


def _block_layout(rt, counts, T):
    K = TOPK_IN_GROUP
    n_rows = (K * T + N_EXPERTS * (MOE_BLOCK - 1) + MOE_BLOCK - 1) // MOE_BLOCK * MOE_BLOCK
    n_blocks = n_rows // MOE_BLOCK
    cnt = counts[:, 0].astype(jnp.int32)
    padded = (cnt + MOE_BLOCK - 1) // MOE_BLOCK * MOE_BLOCK
    pad_end = jnp.cumsum(padded)
    pad_start = pad_end - padded
    e = rt[0:K].astype(jnp.int32).reshape(-1)
    rank = rt[2 * K:3 * K].astype(jnp.int32).reshape(-1)
    hit = e[:, None] == jnp.arange(N_EXPERTS, dtype=jnp.int32)[None, :]
    pos = jnp.sum(jnp.where(hit, pad_start[None, :].astype(jnp.int32), 0), axis=1) + rank
    tile_start = (jnp.concatenate([pad_start, pad_end[-1:]]) // MOE_BLOCK).astype(jnp.int32)
    pad_rows = jnp.concatenate([pad_start + cnt, pad_end, pad_end[-1:] // MOE_BLOCK]).astype(jnp.int32)
    return tile_start, pos, pad_rows, n_rows


def _layer(x, norm1_g, w_in, cmp_pos, cmp_w1, cmp_w2, q_norm_g, k_norm_g, ret_gn_g, ret_gn_b,
           w_out, norm2_g, w_rg, b_rg, w_re, b_re, w_eg, w_eu, w_ed, tiles):
    B, S, D = x.shape
    T = B * S
    xt = x.reshape(T, D)

    w_main = jnp.concatenate([w_in[:, GATE_COL0 + NSA_GATE_WIDTH:], w_in[:, :GATE_COL0]], axis=1).astype(BF16)
    gpg = NSA_GATE_WIDTH // NSA_KV_GROUPS
    w_gate = jnp.concatenate(
        [jnp.pad(w_in[:, GATE_COL0 + g * gpg:GATE_COL0 + (g + 1) * gpg], ((0, 0), (0, LANES - gpg)))
         for g in range(NSA_KV_GROUPS)], axis=1).astype(BF16)
    proj, gate = _in_proj(xt, norm1_g.reshape(1, D), w_main, w_gate, tiles["tm_in"], tiles["tn_in"])

    cmp_kv = _compress(proj, cmp_pos, cmp_w1, cmp_w2, k_norm_g, B, S)
    o_nsa = _nsa(proj, gate, cmp_kv, q_norm_g.reshape(1, HEAD_DIM), k_norm_g, B, S, tiles["tq"], tiles["tk"])

    o_ret = _retention(proj, ret_gn_g, ret_gn_b, B, S)

    gap, tail = EXP_ROW0 - N_GROUPS, LANES - EXP_ROW0 - N_EXPERTS
    w_router = jnp.concatenate([w_rg, jnp.zeros((D, gap), F32), w_re, jnp.zeros((D, tail), F32)], axis=1)
    b_router = jnp.concatenate([b_rg, jnp.zeros((gap,), F32), b_re, jnp.zeros((tail,), F32)]).reshape(1, LANES)
    x1, h2, rt, counts = _out_proj(o_nsa, o_ret, w_out.astype(BF16), xt, norm2_g.reshape(1, D),
                                   w_router, b_router, tiles["tm_out"])

    tile_start, pos, pad_rows, n_rows = _block_layout(rt, counts, T)
    xs = _dispatch(pos, pad_rows, h2, n_rows, tiles["tm_dsp"])
    y_rows = _experts(xs, tile_start, w_eg, w_eu, w_ed)
    out = _combine(pos, y_rows, x1, rt, tiles["tm_cmb"])
    return out.reshape(B, S, D)


def _tiles(T, S):
    return {
        "tm_in": min(2048, T), "tn_in": 512,
        "tq": min(256, S), "tk": min(512, S),
        "tm_out": min(512, T),
        "tm_dsp": min(1024, T),
        "tm_cmb": min(256, T),
    }


def kernel(x, norm1_g, w_in, cmp_pos, cmp_w1, cmp_w2, q_norm_g, k_norm_g, ret_gn_g, ret_gn_b, w_out, norm2_g, w_router_group, b_router_group, w_router_expert, b_router_expert, w_exp_gate, w_exp_up, w_exp_down):
    B, S, _ = x.shape
    tiles = _tiles(B * S, S)
    for l in range(norm1_g.shape[0]):
        x = _layer(x, norm1_g[l], w_in[l], cmp_pos[l], cmp_w1[l], cmp_w2[l], q_norm_g[l], k_norm_g[l],
                   ret_gn_g[l], ret_gn_b[l], w_out[l], norm2_g[l], w_router_group[l], b_router_group[l],
                   w_router_expert[l], b_router_expert[l], w_exp_gate[l], w_exp_up[l], w_exp_down[l], tiles)
    return x
```

```python
import functools

import numpy as np
import jax
import jax.numpy as jnp
from jax import lax
from jax.experimental import pallas as pl
from jax.experimental.pallas import tpu as pltpu

F32 = jnp.float32
BF16 = jnp.bfloat16

D_MODEL = 2048
NSA_HEADS = 8
NSA_KV_GROUPS = 2
NSA_GROUP_HEADS = NSA_HEADS // NSA_KV_GROUPS
HEAD_DIM = 128
RET_HEADS = 8
CMP_BLOCK = 32
CMP_STRIDE = 16
SEL_BLOCK = 64
SEL_TOPK = 8
WINDOW = 512
RET_CHUNK = 128
ROPE_BASE = 10000.0
N_GROUPS = 4
EXPERTS_PER_GROUP = 8
N_EXPERTS = N_GROUPS * EXPERTS_PER_GROUP
TOPK_IN_GROUP = 2
D_EXPERT = 512
MOE_BLOCK = 256
RMS_EPS = 1e-6
GN_EPS = 1e-5
NEG_INF = -1e30
FORCED_SCORE = 1e6
EXP_ROW0 = 8
LOG2E = 1.4426950408889634

NSA_Q_WIDTH = NSA_HEADS * HEAD_DIM
NSA_KV_WIDTH = NSA_KV_GROUPS * HEAD_DIM
NSA_GATE_WIDTH = 3 * NSA_HEADS
RET_WIDTH = RET_HEADS * HEAD_DIM
GATE_COL0 = NSA_Q_WIDTH + 6 * NSA_KV_WIDTH
LANES = 128
VMEM_LIMIT = 56 * 1024 * 1024

Q_COL0 = 4 * RET_WIDTH
KV_COL0 = Q_COL0 + NSA_Q_WIDTH
KC_BLK, VC_BLK, KS_BLK, VS_BLK, KW_BLK, VW_BLK = [KV_COL0 // LANES + 2 * t for t in range(6)]


def _rms(xf, g):
    return xf * lax.rsqrt(jnp.mean(xf * xf, axis=-1, keepdims=True) + RMS_EPS) * g


def _dot(a, b):
    return jnp.dot(a, b, preferred_element_type=F32)


def _dot_nt(a, b):
    return lax.dot_general(a, b, (((1,), (1,)), ((), ())), preferred_element_type=F32)


def _dot_tn(a, b):
    return lax.dot_general(a, b, (((0,), (0,)), ((), ())), preferred_element_type=F32)


def _split3(p):
    p1 = p.astype(BF16)
    r1 = p - p1.astype(F32)
    p2 = r1.astype(BF16)
    p3 = (r1 - p2.astype(F32)).astype(BF16)
    return p1, p2, p3


def _params(sem):
    return pltpu.CompilerParams(dimension_semantics=sem, vmem_limit_bytes=VMEM_LIMIT)


IN_CHUNK = 256


def _in_proj_kernel(x_hbm, g_ref, w_ref, wg_ref, o_ref, og_ref, h0, h1, xc0, xc1, sem, *, tm):
    m = pl.program_id(0)
    n = pl.program_id(1)
    ck = IN_CHUNK
    nc = tm // ck
    hs = (h0, h1)
    xcs = (xc0, xc1)
    g = g_ref[...]

    def chunk_copy(tile, c, slot):
        row0 = pl.multiple_of(tile * tm + c * ck, ck)
        return pltpu.make_async_copy(x_hbm.at[pl.ds(row0, ck), :], xcs[slot], sem.at[slot])

    def normalise(slot, dst, c):
        dst[pl.ds(pl.multiple_of(c * ck, ck), ck), :] = _rms(xcs[slot][...], g).astype(BF16)

    @pl.when((m == 0) & (n == 0))
    def _():
        chunk_copy(0, 0, 0).start()
        for c in range(nc):
            if c + 1 < nc:
                chunk_copy(0, c + 1, (c + 1) % 2).start()
            chunk_copy(0, c, c % 2).wait()
            normalise(c % 2, h0, c)

    has_next = m + 1 < pl.num_programs(0)
    for slot in range(2):
        @pl.when(has_next & (n >= 1) & (n <= nc) & (lax.rem(n - 1, 2) == slot))
        def _(slot=slot):
            chunk_copy(m + 1, n - 1, slot).wait()

        @pl.when(has_next & (n < nc) & (lax.rem(n, 2) == slot))
        def _(slot=slot):
            chunk_copy(m + 1, n, slot).start()

    c_prev = jnp.where(n == 0, nc - 1, jnp.clip(n - 1, 0, nc - 1))
    even_chunk = lax.rem(c_prev, 2) == 0
    for cur in range(2):
        for slot in range(2):
            @pl.when((lax.rem(m, 2) == cur) & (even_chunk == (slot == 0)))
            def _(cur=cur, slot=slot):
                @pl.when(n == 0)
                def _():
                    og_ref[...] = _dot(hs[cur][...], wg_ref[...])

                normalise(slot, hs[1 - cur], c_prev)
                o_ref[...] = _dot(hs[cur][...], w_ref[...])


def _in_proj(xt, g1, w_main, w_gate, tm, tn):
    T, D = xt.shape
    n_main = w_main.shape[1]
    n_gate = w_gate.shape[1]
    assert tm % (2 * IN_CHUNK) == 0 and tm // IN_CHUNK < n_main // tn
    return pl.pallas_call(
        functools.partial(_in_proj_kernel, tm=tm),
        grid=(T // tm, n_main // tn),
        in_specs=[
            pl.BlockSpec(memory_space=pl.ANY),
            pl.BlockSpec((1, D), lambda m, n: (0, 0)),
            pl.BlockSpec((D, tn), lambda m, n: (0, n)),
            pl.BlockSpec((D, n_gate), lambda m, n: (0, 0)),
        ],
        out_specs=[
            pl.BlockSpec((tm, tn), lambda m, n: (m, n)),
            pl.BlockSpec((tm, n_gate), lambda m, n: (m, 0)),
        ],
        out_shape=[
            jax.ShapeDtypeStruct((T, n_main), F32),
            jax.ShapeDtypeStruct((T, n_gate), F32),
        ],
        scratch_shapes=[pltpu.VMEM((tm, D), BF16), pltpu.VMEM((tm, D), BF16),
                        pltpu.VMEM((IN_CHUNK, D), F32), pltpu.VMEM((IN_CHUNK, D), F32),
                        pltpu.SemaphoreType.DMA((2,))],
        compiler_params=_params(("arbitrary", "arbitrary")),
        name="in_proj",
    )(xt, g1, w_main, w_gate)


def _compress_kernel(x_ref, pos_ref, w1_ref, w2_ref, kg_ref, o_ref, xs_scr, *, S, n_pad):
    kv = pl.program_id(1)
    xs_scr[0:S, :] = x_ref[...]
    xs_scr[S:S + CMP_STRIDE, :] = jnp.zeros((CMP_STRIDE, HEAD_DIM), F32)
    acc = jnp.zeros((n_pad, HEAD_DIM), F32)
    for l in range(CMP_BLOCK):
        tb = xs_scr[pl.ds(l, n_pad, stride=CMP_STRIDE), :] + pos_ref[0, l:l + 1, :]
        acc = acc + _dot(tb.astype(BF16), w1_ref[0, l].astype(BF16))
    hid = acc * jax.nn.sigmoid(acc)
    out = _dot(hid.astype(BF16), w2_ref[0].astype(BF16))
    normed = _rms(out, kg_ref[0:1, :])
    o_ref[0, 0, 0] = jnp.where(kv == 0, normed, out).astype(BF16)


def _compress(proj, cmp_pos, cmp_w1, cmp_w2, k_norm_g, B, S):
    n_pad = S // CMP_STRIDE
    G = NSA_KV_GROUPS
    kern = functools.partial(_compress_kernel, S=S, n_pad=n_pad)
    return pl.pallas_call(
        kern,
        grid=(B, 2, G),
        in_specs=[
            pl.BlockSpec((S, HEAD_DIM), lambda b, kv, g: (b, KC_BLK + 2 * kv + g)),
            pl.BlockSpec((1, CMP_BLOCK, HEAD_DIM), lambda b, kv, g: (kv, 0, 0)),
            pl.BlockSpec((1, CMP_BLOCK, HEAD_DIM, HEAD_DIM), lambda b, kv, g: (kv, 0, 0, 0)),
            pl.BlockSpec((1, HEAD_DIM, HEAD_DIM), lambda b, kv, g: (kv, 0, 0)),
            pl.BlockSpec((3, HEAD_DIM), lambda b, kv, g: (0, 0)),
        ],
        out_specs=pl.BlockSpec((1, 1, 1, n_pad, HEAD_DIM), lambda b, kv, g: (b, kv, g, 0, 0)),
        out_shape=jax.ShapeDtypeStruct((B, 2, G, n_pad, HEAD_DIM), BF16),
        scratch_shapes=[pltpu.VMEM((S + CMP_STRIDE, HEAD_DIM), F32)],
        compiler_params=_params(("parallel", "parallel", "parallel")),
        name="compress",
    )(proj, cmp_pos, cmp_w1, cmp_w2, k_norm_g)


def _nsa_kernel(q_ref, kc_ref, vc_ref, ks_ref, vs_ref, kw_ref, vw_ref, gate_ref, qg_ref, kg_ref,
                ovt_ref, ext_ref, o_ref, ksa, vsa, kwn, vwa, l_scr, acc_scr, *, S, tq, tk, wk):
    i = pl.program_id(2)
    H = NSA_GROUP_HEADS
    scale = HEAD_DIM ** -0.5
    c2 = scale * LOG2E
    n_cp = S // CMP_STRIDE
    n_sblk = S // SEL_BLOCK

    @pl.when(i == 0)
    def _():
        ksa[:, 0:HEAD_DIM] = _rms(ks_ref[...], kg_ref[1:2, :]).astype(BF16)
        ksa[:, HEAD_DIM:2 * HEAD_DIM] = ext_ref[...]
        kwn[...] = _rms(kw_ref[...], kg_ref[2:3, :]).astype(BF16)
        ones = jnp.ones((S, HEAD_DIM), BF16)
        vsa[:, 0:HEAD_DIM] = vs_ref[...].astype(BF16)
        vsa[:, HEAD_DIM:2 * HEAD_DIM] = ones
        vwa[:, 0:HEAD_DIM] = vw_ref[...].astype(BF16)
        vwa[:, HEAD_DIM:2 * HEAD_DIM] = ones

    pos = i * tq + lax.broadcasted_iota(jnp.int32, (tq, 1), 0)
    qg = qg_ref[...]
    q4 = jnp.concatenate([_rms(q_ref[:, h * HEAD_DIM:(h + 1) * HEAD_DIM], qg).astype(BF16)
                          for h in range(H)], axis=0)
    rows = [slice(h * tq, (h + 1) * tq) for h in range(H)]

    def softmax_pv(q_chains, k, v, tail_bias):
        head = k.shape[0] - tail_bias.shape[1]
        outs, sums = [], []
        for q2 in q_chains:
            s2 = _dot_nt(q2, k)
            es = []
            hpc = q2.shape[0] // tq
            for hh in range(hpc):
                t = s2[hh * tq:(hh + 1) * tq] * c2
                tb = t[:, head:] + tail_bias
                m = jnp.max(tb, axis=-1, keepdims=True)
                if head:
                    ta = t[:, :head]
                    m = jnp.maximum(m, jnp.max(ta, axis=-1, keepdims=True))
                    es.append(jnp.concatenate([jnp.exp2(ta - m).astype(BF16),
                                               jnp.exp2(tb - m).astype(BF16)], axis=1))
                else:
                    es.append(jnp.exp2(tb - m).astype(BF16))
            o2 = _dot(es[0] if hpc == 1 else jnp.concatenate(es, axis=0), v)
            outs += [o2[hh * tq:(hh + 1) * tq, 0:HEAD_DIM] for hh in range(hpc)]
            sums += [o2[hh * tq:(hh + 1) * tq, HEAD_DIM:2 * HEAD_DIM] for hh in range(hpc)]
        return outs, sums

    pos_t = i * tq + lax.broadcasted_iota(jnp.int32, (1, tq), 1)
    nrow = lax.broadcasted_iota(jnp.int32, (n_cp, 1), 0)
    cmask_t = (nrow * CMP_STRIDE + (CMP_BLOCK - 1)) <= pos_t
    s_ct = _dot_nt(kc_ref[0, 0, 0], q4) * scale
    vc = vc_ref[0, 0, 0]
    ps_t, o_cs = [], []
    for r in rows:
        sm = jnp.where(cmask_t, s_ct[:, r], NEG_INF)
        e = jnp.where(cmask_t, jnp.exp(sm - jnp.max(sm, axis=0, keepdims=True)), 0.0)
        den = jnp.sum(e, axis=0, keepdims=True)
        p = e / jnp.where(den > 0.0, den, 1.0)
        ps_t.append(p)
        o_cs.append(_dot_tn(p.astype(BF16), vc))
    psum_t = ps_t[0] + ps_t[1] + ps_t[2] + ps_t[3]

    ovt = ovt_ref[...]
    p1, p2, p3 = _split3(psum_t)
    imp_t = (_dot(ovt, p1) + _dot(ovt, p2) + _dot(ovt, p3))[0:n_sblk, :]
    jrow = lax.broadcasted_iota(jnp.int32, (n_sblk, 1), 0)
    cur_t = pos_t // SEL_BLOCK
    valid_t = jrow * SEL_BLOCK <= pos_t
    forced_t = (jrow == 0) | (jrow == cur_t) | (jrow == cur_t - 1)
    score = jnp.where(valid_t, jnp.where(forced_t, FORCED_SCORE, imp_t), -jnp.inf)
    rank = jnp.zeros((n_sblk, tq), jnp.int32)
    for i2 in range(n_sblk):
        si = score[i2:i2 + 1, :]
        ahead = (si > score) | ((si == score) & (jrow > i2))
        rank = rank + jnp.where(ahead, 1, 0)
    drop_t = jnp.where((rank < SEL_TOPK) & valid_t, 0.0, NEG_INF)
    drop_t = jnp.concatenate([drop_t, jnp.full((LANES - n_sblk, tq), NEG_INF, F32)], axis=0)
    drop = drop_t.T.astype(BF16)
    qa_pairs = [jnp.concatenate([q4[r], drop], axis=1) for r in rows]

    kst = pl.multiple_of(jnp.clip(i * tq - WINDOW, 0, S - wk), tq)
    wpos = kst + lax.broadcasted_iota(jnp.int32, (1, wk), 1)
    wbias = jnp.where((wpos <= pos) & (wpos > pos - WINDOW), 0.0, NEG_INF)
    o_ws, lws = softmax_pv([q4[r] for r in rows],
                           kwn[pl.ds(kst, wk), :], vwa[pl.ds(kst, wk), :], wbias)

    def slc_prefix(n):
        kpos = (n - tk) + lax.broadcasted_iota(jnp.int32, (1, tk), 1)
        causal = jnp.where(kpos <= pos, 0.0, NEG_INF)
        outs, sums = softmax_pv(qa_pairs, ksa[0:n, :], vsa[0:n, :], causal)
        for r, o, l in zip(rows, outs, sums):
            acc_scr[r] = o
            l_scr[r] = l

    n_cls = S // tk
    cls = ((i + 1) * tq - 1) // tk
    for c in range(n_cls):
        pl.when(cls == c)(functools.partial(slc_prefix, (c + 1) * tk))

    gates = jax.nn.sigmoid(gate_ref[...])
    for h, r in enumerate(rows):
        c = 3 * h
        o = (o_cs[h] * gates[:, c:c + 1]
             + acc_scr[r] * (gates[:, c + 1:c + 2] / l_scr[r])
             + o_ws[h] * (gates[:, c + 2:c + 3] / lws[h]))
        o_ref[:, h * HEAD_DIM:(h + 1) * HEAD_DIM] = o.astype(BF16)


def _nsa_tables(S):
    n_cp, n_cmp, n_sblk = S // CMP_STRIDE, (S - CMP_BLOCK) // CMP_STRIDE + 1, S // SEL_BLOCK
    n = np.arange(n_cp)[None, :] * CMP_STRIDE
    j = np.arange(LANES)[:, None]
    ovt = ((n < (j + 1) * SEL_BLOCK) & (n + CMP_BLOCK > j * SEL_BLOCK)
           & (np.arange(n_cp)[None, :] < n_cmp) & (j < n_sblk))
    key_block = (np.arange(S) // SEL_BLOCK)[:, None] == np.arange(LANES)[None, :]
    return jnp.asarray(ovt, BF16), jnp.asarray(key_block, BF16)


def _nsa(proj, gate, cmp_kv, q_norm_g, k_norm_g, B, S, tq, tk):
    G = NSA_KV_GROUPS
    H = NSA_GROUP_HEADS
    n_cp = S // CMP_STRIDE
    wk = min(S, WINDOW + tq)
    nq = S // tq
    gw = H * HEAD_DIM
    ovt, key_block = _nsa_tables(S)
    kern = functools.partial(_nsa_kernel, S=S, tq=tq, tk=tk, wk=wk)
    kv_spec = lambda blk: pl.BlockSpec((S, HEAD_DIM), lambda b, g, i: (b, blk + g))
    return pl.pallas_call(
        kern,
        grid=(B, G, nq),
        in_specs=[
            pl.BlockSpec((tq, gw), lambda b, g, i: (b * nq + i, Q_COL0 // gw + g)),
            pl.BlockSpec((1, 1, 1, n_cp, HEAD_DIM), lambda b, g, i: (b, 0, g, 0, 0)),
            pl.BlockSpec((1, 1, 1, n_cp, HEAD_DIM), lambda b, g, i: (b, 1, g, 0, 0)),
            kv_spec(KS_BLK), kv_spec(VS_BLK), kv_spec(KW_BLK), kv_spec(VW_BLK),
            pl.BlockSpec((tq, LANES), lambda b, g, i: (b * nq + i, g)),
            pl.BlockSpec((1, HEAD_DIM), lambda b, g, i: (0, 0)),
            pl.BlockSpec((3, HEAD_DIM), lambda b, g, i: (0, 0)),
            pl.BlockSpec((LANES, n_cp), lambda b, g, i: (0, 0)),
            pl.BlockSpec((S, LANES), lambda b, g, i: (0, 0)),
        ],
        out_specs=pl.BlockSpec((tq, gw), lambda b, g, i: (b * nq + i, g)),
        out_shape=jax.ShapeDtypeStruct((B * S, NSA_Q_WIDTH), BF16),
        scratch_shapes=[
            pltpu.VMEM((S, 2 * HEAD_DIM), BF16), pltpu.VMEM((S, 2 * HEAD_DIM), BF16),
            pltpu.VMEM((S, HEAD_DIM), BF16), pltpu.VMEM((S, 2 * HEAD_DIM), BF16),
            pltpu.VMEM((H * tq, HEAD_DIM), F32), pltpu.VMEM((H * tq, HEAD_DIM), F32)],
        compiler_params=_params(("parallel", "parallel", "arbitrary")),
        name="nsa",
    )(proj, cmp_kv, cmp_kv, proj, proj, proj, proj, gate, q_norm_g, k_norm_g, ovt, key_block)


def _ret_kernel(rq_ref, rk_ref, rv_ref, rg_ref, cos_ref, sin_ref, dec_ref, xi_ref, zeta_ref, cd_ref,
                gg_ref, gb_ref, o_ref, r_scr):
    scale = HEAD_DIM ** -0.5

    @pl.when(pl.program_id(1) == 0)
    def _():
        r_scr[...] = jnp.zeros(r_scr.shape, F32)

    cos = cos_ref[...]
    sin = sin_ref[...]
    for h in range(RET_HEADS):
        sl = slice(h * HEAD_DIM, (h + 1) * HEAD_DIM)
        q = rq_ref[:, sl]
        k = rk_ref[:, sl]
        qf = q * cos + pltpu.roll(q, HEAD_DIM // 2, 1) * sin
        kf = (k * cos + pltpu.roll(k, HEAD_DIM // 2, 1) * sin) * scale
        v = rv_ref[:, sl].astype(BF16)
        qb = qf.astype(BF16)
        r_old = r_scr[h]
        a = _dot_nt(qb, kf.astype(BF16)) * dec_ref[h]
        o = _dot(a.astype(BF16), v) + _dot(qb, r_old.astype(BF16)) * xi_ref[h]
        r_scr[h] = r_old * cd_ref[h:h + 1, :] + _dot_tn((kf * zeta_ref[h]).astype(BF16), v)
        mu = jnp.mean(o, axis=-1, keepdims=True)
        d = o - mu
        var = jnp.mean(d * d, axis=-1, keepdims=True)
        y = d * lax.rsqrt(var + GN_EPS) * gg_ref[h:h + 1, :] + gb_ref[h:h + 1, :]
        gt = rg_ref[:, sl]
        o_ref[:, sl] = (gt * jax.nn.sigmoid(gt) * y).astype(BF16)


def _ret_tables(S):
    C, H, half = RET_CHUNK, RET_HEADS, HEAD_DIM // 2
    inv_freq = ROPE_BASE ** (-jnp.arange(half, dtype=F32) / half)
    ang = jnp.arange(S, dtype=F32)[:, None] * inv_freq[None, :]
    cos2 = jnp.concatenate([jnp.cos(ang), jnp.cos(ang)], axis=1)
    sin2 = jnp.concatenate([-jnp.sin(ang), jnp.sin(ang)], axis=1)
    log_g = jnp.log1p(-jnp.exp2(-5.0 - jnp.arange(H, dtype=F32)))
    n = jnp.arange(C, dtype=F32)
    diff = n[:, None] - n[None, :]
    dec = jnp.where(diff >= 0, jnp.exp(log_g[:, None, None] * jnp.maximum(diff, 0.0)), 0.0)
    lanes = lambda t: jnp.broadcast_to(t[..., None], t.shape + (HEAD_DIM,))
    xi = lanes(jnp.exp(log_g[:, None] * (n + 1.0)))
    zeta = lanes(jnp.exp(log_g[:, None] * (C - 1.0 - n)))
    cd = lanes(jnp.exp(log_g * C))
    return cos2, sin2, dec, xi, zeta, cd


def _retention(proj, gn_g, gn_b, B, S):
    C, H = RET_CHUNK, RET_HEADS
    n_ch = S // C
    cos2, sin2, dec, xi, zeta, cd = _ret_tables(S)
    spec = lambda k: pl.BlockSpec((C, RET_WIDTH), lambda b, c: (b * n_ch + c, k))
    whole = lambda a: pl.BlockSpec(a.shape, lambda b, c: (0,) * a.ndim)
    return pl.pallas_call(
        _ret_kernel,
        grid=(B, n_ch),
        in_specs=[
            spec(0), spec(1), spec(2), spec(3),
            pl.BlockSpec((C, HEAD_DIM), lambda b, c: (c, 0)),
            pl.BlockSpec((C, HEAD_DIM), lambda b, c: (c, 0)),
            whole(dec), whole(xi), whole(zeta), whole(cd), whole(gn_g), whole(gn_b),
        ],
        out_specs=pl.BlockSpec((C, RET_WIDTH), lambda b, c: (b * n_ch + c, 0)),
        out_shape=jax.ShapeDtypeStruct((B * S, RET_WIDTH), BF16),
        scratch_shapes=[pltpu.VMEM((H, HEAD_DIM, HEAD_DIM), F32)],
        compiler_params=_params(("parallel", "arbitrary")),
        name="retention",
    )(proj, proj, proj, proj, cos2, sin2, dec, xi, zeta, cd, gn_g, gn_b)


def _out_kernel(on_ref, or_ref, w_ref, x_ref, g2_ref, wr_ref, br_ref, tri_ref,
                x1_ref, h2_ref, rt_ref, cnt_ref, cnt_scr):
    @pl.when(pl.program_id(0) == 0)
    def _():
        cnt_scr[...] = jnp.zeros(cnt_scr.shape, F32)

    half = on_ref.shape[1]
    acc = _dot(on_ref[...], w_ref[0:half, :]) + _dot(or_ref[...], w_ref[half:2 * half, :])
    x1 = x_ref[...] + acc
    x1_ref[...] = x1
    h2 = _rms(x1, g2_ref[...])
    h2_ref[...] = h2

    h_hi = h2.astype(BF16)
    h_lo = (h2 - h_hi.astype(F32)).astype(BF16)
    wr = wr_ref[...]
    w_hi = wr.astype(BF16)
    w_lo = (wr - w_hi.astype(F32)).astype(BF16)
    logits = _dot(h_hi, w_hi) + _dot(h_lo, w_hi) + _dot(h_hi, w_lo) + br_ref[...]

    tm = logits.shape[0]
    lt = logits.T
    G8 = EXPERTS_PER_GROUP
    row = lax.broadcasted_iota(jnp.int32, (G8, 1), 0).astype(F32)
    big = float(G8)
    col_max = lambda v: jnp.max(v, axis=0, keepdims=True)
    col_min = lambda v: jnp.min(v, axis=0, keepdims=True)
    col_sum = lambda v: jnp.sum(v, axis=0, keepdims=True)
    gm = row < N_GROUPS
    gl = jnp.where(gm, lt[0:G8], -jnp.inf)
    ge = jnp.where(gm, jnp.exp(gl - col_max(gl)), 0.0)
    pg = ge / col_sum(ge)
    g_top = col_max(pg)
    g_idx = col_min(jnp.where(gm & (pg == g_top), row, big))
    el = lt[EXP_ROW0:EXP_ROW0 + G8]
    for g in range(1, N_GROUPS):
        el = jnp.where(g_idx == float(g), lt[EXP_ROW0 + g * G8:EXP_ROW0 + (g + 1) * G8], el)
    ee = jnp.exp(el - col_max(el))
    pe = ee / col_sum(ee)
    t1 = col_max(pe)
    i1 = col_min(jnp.where(pe == t1, row, big))
    rest = row != i1
    t2 = col_max(jnp.where(rest, pe, -1.0))
    i2 = col_min(jnp.where(rest & (pe == t2), row, big))
    tsum = t1 + t2
    w1 = g_top * t1 / tsum
    w2 = g_top * t2 / tsum
    e1 = g_idx * float(G8) + i1
    e2 = g_idx * float(G8) + i2
    erow = lax.broadcasted_iota(jnp.int32, (N_EXPERTS, 1), 0).astype(F32)
    oh1 = jnp.where(erow == e1, 1.0, 0.0)
    oh2 = jnp.where(erow == e2, 1.0, 0.0)
    both = oh1 + oh2
    before = _dot(both.astype(BF16), tri_ref[...]) + cnt_scr[...]
    r1 = col_sum(before * oh1)
    r2 = col_sum(before * oh2)
    cnt_scr[...] = cnt_scr[...] + jnp.sum(both, axis=1, keepdims=True)
    cnt_ref[...] = jnp.broadcast_to(cnt_scr[...], cnt_ref.shape)
    rt_ref[...] = jnp.concatenate([e1, e2, w1, w2, r1, r2, jnp.zeros((2, tm), F32)], axis=0)


def _out_proj(o_nsa, o_ret, w_out, xt, g2, w_router, b_router, tm):
    T, D = xt.shape
    half = o_nsa.shape[1]
    tri = jnp.asarray(np.triu(np.ones((tm, tm), np.float32), 1), BF16)
    return pl.pallas_call(
        _out_kernel,
        grid=(T // tm,),
        in_specs=[
            pl.BlockSpec((tm, half), lambda m: (m, 0)),
            pl.BlockSpec((tm, half), lambda m: (m, 0)),
            pl.BlockSpec((2 * half, D), lambda m: (0, 0), pipeline_mode=pl.Buffered(1)),
            pl.BlockSpec((tm, D), lambda m: (m, 0)),
            pl.BlockSpec((1, D), lambda m: (0, 0)),
            pl.BlockSpec((D, LANES), lambda m: (0, 0)),
            pl.BlockSpec((1, LANES), lambda m: (0, 0)),
            pl.BlockSpec((tm, tm), lambda m: (0, 0)),
        ],
        out_specs=[
            pl.BlockSpec((tm, D), lambda m: (m, 0)),
            pl.BlockSpec((tm, D), lambda m: (m, 0)),
            pl.BlockSpec((8, tm), lambda m: (0, m)),
            pl.BlockSpec((N_EXPERTS, LANES), lambda m: (0, 0)),
        ],
        out_shape=[
            jax.ShapeDtypeStruct((T, D), F32),
            jax.ShapeDtypeStruct((T, D), F32),
            jax.ShapeDtypeStruct((8, T), F32),
            jax.ShapeDtypeStruct((N_EXPERTS, LANES), F32),
        ],
        scratch_shapes=[pltpu.VMEM((N_EXPERTS, 1), F32)],
        compiler_params=_params(("arbitrary",)),
        name="out_proj",
    )(o_nsa, o_ret, w_out, xt, g2, w_router, b_router, tri)


def _row_copy(src, row, dst, slot, sem):
    return pltpu.make_async_copy(src.at[pl.ds(row, 1), :], dst.at[pl.ds(slot, 1), :], sem)


def _dispatch_kernel(pos_ref, pad_ref, h2_ref, xs_hbm, zrow, sem, zsem, *, n_tok):
    i = pl.program_id(0)
    tm = h2_ref.shape[0]
    K = TOPK_IN_GROUP

    def issue(r, carry):
        for k in range(K):
            _row_copy(h2_ref, r, xs_hbm, pos_ref[k * n_tok + i * tm + r], sem).start()
        return carry

    lax.fori_loop(0, tm, issue, 0, unroll=8)

    @pl.when(i == pl.num_programs(0) - 1)
    def _():
        zrow[...] = jnp.zeros(zrow.shape, F32)
        for e in range(N_EXPERTS):
            lo, hi = pad_ref[e], pad_ref[N_EXPERTS + e]

            def fill(r, carry):
                _row_copy(zrow, 0, xs_hbm, r, zsem).start()
                return carry

            lax.fori_loop(lo, hi, fill, 0)

            def fill_wait(r, carry):
                _row_copy(zrow, 0, xs_hbm, r, zsem).wait()
                return carry

            lax.fori_loop(lo, hi, fill_wait, 0)

        M = zrow.shape[0]

        def tail_copy(t):
            return pltpu.make_async_copy(zrow, xs_hbm.at[pl.ds(pl.multiple_of(t * M, M), M), :], zsem)

        def tail(t, carry):
            tail_copy(t).start()
            return carry

        lax.fori_loop(pad_ref[2 * N_EXPERTS], xs_hbm.shape[0] // M, tail, 0)

        def tail_wait(t, carry):
            tail_copy(t).wait()
            return carry

        lax.fori_loop(pad_ref[2 * N_EXPERTS], xs_hbm.shape[0] // M, tail_wait, 0)

    for k in range(K):
        pltpu.make_async_copy(h2_ref, xs_hbm.at[pl.ds(0, tm), :], sem).wait()


def _dispatch(pos, pad_rows, h2, n_rows, tm):
    T, D = h2.shape
    grid_spec = pltpu.PrefetchScalarGridSpec(
        num_scalar_prefetch=2,
        grid=(T // tm,),
        in_specs=[pl.BlockSpec((tm, D), lambda i, pos, pad: (i, 0))],
        out_specs=pl.BlockSpec(memory_space=pl.ANY),
        scratch_shapes=[pltpu.VMEM((MOE_BLOCK, D), F32), pltpu.SemaphoreType.DMA, pltpu.SemaphoreType.DMA],
    )
    return pl.pallas_call(
        functools.partial(_dispatch_kernel, n_tok=T),
        grid_spec=grid_spec,
        out_shape=jax.ShapeDtypeStruct((n_rows, D), F32),
        compiler_params=_params(("arbitrary",)),
        name="dispatch",
    )(pos, pad_rows, h2)


def _expert_kernel(ts_ref, xs_hbm, wg_ref, wu_ref, wd_ref, y_hbm,
                   xb0, xb1, yb0, yb1, wg_b, wu_b, wd_b, gsem, osem, *, n_blocks):
    e = pl.program_id(0)
    M = MOE_BLOCK
    t0 = ts_ref[e]
    t1 = ts_ref[e + 1]
    n_used = ts_ref[N_EXPERTS]
    xbufs = (xb0, xb1)
    ybufs = (yb0, yb1)

    def rows(t):
        return pl.ds(pl.multiple_of(t * M, M), M)

    def in_copy(t, p):
        return pltpu.make_async_copy(xs_hbm.at[rows(t), :], xbufs[p], gsem.at[p])

    def out_copy(t, p):
        return pltpu.make_async_copy(ybufs[p], y_hbm.at[rows(t), :], osem.at[p])

    @pl.when(e == 0)
    def _():
        in_copy(0, 0).start()

    @pl.when(t1 > t0)
    def _():
        wg_b[...] = wg_ref[0].astype(BF16)
        wu_b[...] = wu_ref[0].astype(BF16)
        wd_b[...] = wd_ref[0].astype(BF16)

    def tile_body(t, p):
        in_copy(t, p).wait()

        @pl.when(t + 1 < n_used)
        def _():
            in_copy(t + 1, 1 - p).start()

        @pl.when(t >= 2)
        def _():
            out_copy(t - 2, p).wait()

        xb = xbufs[p][...].astype(BF16)
        hg = _dot(xb, wg_b[...])
        hu = _dot(xb, wu_b[...])
        hb = (hg * jax.nn.sigmoid(hg) * hu).astype(BF16)
        ybufs[p][...] = _dot(hb, wd_b[...])
        out_copy(t, p).start()

    def tile(t, carry):
        parity = lax.rem(t, 2)
        for p in range(2):
            pl.when(parity == p)(functools.partial(tile_body, t, p))
        return carry

    lax.fori_loop(t0, t1, tile, 0)

    @pl.when(e == N_EXPERTS - 1)
    def _():
        parity = lax.rem(n_used, 2)
        for p in range(2):
            @pl.when(parity == p)
            def _(p=p):
                out_copy(n_used - 1, 1 - p).wait()

                @pl.when(n_used >= 2)
                def _():
                    out_copy(n_used - 2, p).wait()

        yb0[...] = jnp.zeros(yb0.shape, F32)

        def clear(t, carry):
            out_copy(t, 0).start()
            return carry

        lax.fori_loop(n_used, n_blocks, clear, 0)

        def clear_wait(t, carry):
            out_copy(t, 0).wait()
            return carry

        lax.fori_loop(n_used, n_blocks, clear_wait, 0)


def _experts(xs, tile_start, w_gate, w_up, w_down):
    n_rows, D = xs.shape
    n_blocks = n_rows // MOE_BLOCK
    grid_spec = pltpu.PrefetchScalarGridSpec(
        num_scalar_prefetch=1,
        grid=(N_EXPERTS,),
        in_specs=[
            pl.BlockSpec(memory_space=pl.ANY),
            pl.BlockSpec((1, D, D_EXPERT), lambda e, ts: (e, 0, 0)),
            pl.BlockSpec((1, D, D_EXPERT), lambda e, ts: (e, 0, 0)),
            pl.BlockSpec((1, D_EXPERT, D), lambda e, ts: (e, 0, 0)),
        ],
        out_specs=pl.BlockSpec(memory_space=pl.ANY),
        scratch_shapes=[pltpu.VMEM((MOE_BLOCK, D), F32)] * 4 + [
            pltpu.VMEM((D, D_EXPERT), BF16), pltpu.VMEM((D, D_EXPERT), BF16),
            pltpu.VMEM((D_EXPERT, D), BF16),
            pltpu.SemaphoreType.DMA((2,)), pltpu.SemaphoreType.DMA((2,)),
        ],
    )
    return pl.pallas_call(
        functools.partial(_expert_kernel, n_blocks=n_blocks),
        grid_spec=grid_spec,
        out_shape=jax.ShapeDtypeStruct((n_rows, D), F32),
        compiler_params=_params(("arbitrary",)),
        name="experts",
    )(tile_start, xs, w_gate, w_up, w_down)


def _combine_kernel(pos_ref, y_hbm, x1_ref, rt_ref, o_ref, ybuf, sems, *, n_tok):
    i = pl.program_id(0)
    n = pl.num_programs(0)
    tm = x1_ref.shape[0]
    K = TOPK_IN_GROUP
    slot = lax.rem(i, 2)

    def issue_tile(t, s):
        def issue(r, carry):
            for k in range(K):
                _row_copy(y_hbm, pos_ref[k * n_tok + t * tm + r], ybuf.at[s, k], r,
                          sems.at[s]).start(priority=k)
            return carry

        lax.fori_loop(0, tm, issue, 0, unroll=8)

    @pl.when(i == 0)
    def _():
        issue_tile(0, 0)

    @pl.when(i + 1 < n)
    def _():
        issue_tile(i + 1, 1 - slot)

    for k in range(K):
        pltpu.make_async_copy(y_hbm.at[pl.ds(0, tm), :], ybuf.at[slot, k], sems.at[slot]).wait()
    w = rt_ref[...].T
    yb = ybuf[slot]
    o_ref[...] = x1_ref[...] + (yb[0] * w[:, K:K + 1] + yb[1] * w[:, K + 1:K + 2])


def _combine(pos, y_rows, x1, rt, tm):
    T, D = x1.shape
    grid_spec = pltpu.PrefetchScalarGridSpec(
        num_scalar_prefetch=1,
        grid=(T // tm,),
        in_specs=[
            pl.BlockSpec(memory_space=pl.ANY),
            pl.BlockSpec((tm, D), lambda i, pos: (i, 0)),
            pl.BlockSpec((8, tm), lambda i, pos: (0, i)),
        ],
        out_specs=pl.BlockSpec((tm, D), lambda i, pos: (i, 0)),
        scratch_shapes=[pltpu.VMEM((2, TOPK_IN_GROUP, tm, D), F32), pltpu.SemaphoreType.DMA((2,))],
    )
    return pl.pallas_call(
        functools.partial(_combine_kernel, n_tok=T),
        grid_spec=grid_spec,
        out_shape=jax.ShapeDtypeStruct((T, D), F32),
        compiler_params=_params(("arbitrary",)),
        name="combine",
    )(pos, y_rows, x1, rt)


def _block_layout(rt, counts, T):
    K = TOPK_IN_GROUP
    n_rows = (K * T + N_EXPERTS * (MOE_BLOCK - 1) + MOE_BLOCK - 1) // MOE_BLOCK * MOE_BLOCK
    n_blocks = n_rows // MOE_BLOCK
    cnt = counts[:, 0].astype(jnp.int32)
    padded = (cnt + MOE_BLOCK - 1) // MOE_BLOCK * MOE_BLOCK
    pad_end = jnp.cumsum(padded)
    pad_start = pad_end - padded
    e = rt[0:K].astype(jnp.int32).reshape(-1)
    rank = rt[2 * K:3 * K].astype(jnp.int32).reshape(-1)
    hit = e[:, None] == jnp.arange(N_EXPERTS, dtype=jnp.int32)[None, :]
    pos = jnp.sum(jnp.where(hit, pad_start[None, :].astype(jnp.int32), 0), axis=1) + rank
    tile_start = (jnp.concatenate([pad_start, pad_end[-1:]]) // MOE_BLOCK).astype(jnp.int32)
    pad_rows = jnp.concatenate([pad_start + cnt, pad_end, pad_end[-1:] // MOE_BLOCK]).astype(jnp.int32)
    return tile_start, pos, pad_rows, n_rows


def _layer(x, norm1_g, w_in, cmp_pos, cmp_w1, cmp_w2, q_norm_g, k_norm_g, ret_gn_g, ret_gn_b,
           w_out, norm2_g, w_rg, b_rg, w_re, b_re, w_eg, w_eu, w_ed, tiles):
    B, S, D = x.shape
    T = B * S
    xt = x.reshape(T, D)

    w_main = jnp.concatenate([w_in[:, GATE_COL0 + NSA_GATE_WIDTH:], w_in[:, :GATE_COL0]], axis=1).astype(BF16)
    gpg = NSA_GATE_WIDTH // NSA_KV_GROUPS
    w_gate = jnp.concatenate(
        [jnp.pad(w_in[:, GATE_COL0 + g * gpg:GATE_COL0 + (g + 1) * gpg], ((0, 0), (0, LANES - gpg)))
         for g in range(NSA_KV_GROUPS)], axis=1).astype(BF16)
    proj, gate = _in_proj(xt, norm1_g.reshape(1, D), w_main, w_gate, tiles["tm_in"], tiles["tn_in"])

    cmp_kv = _compress(proj, cmp_pos, cmp_w1, cmp_w2, k_norm_g, B, S)
    o_nsa = _nsa(proj, gate, cmp_kv, q_norm_g.reshape(1, HEAD_DIM), k_norm_g, B, S, tiles["tq"], tiles["tk"])

    o_ret = _retention(proj, ret_gn_g, ret_gn_b, B, S)

    gap, tail = EXP_ROW0 - N_GROUPS, LANES - EXP_ROW0 - N_EXPERTS
    w_router = jnp.concatenate([w_rg, jnp.zeros((D, gap), F32), w_re, jnp.zeros((D, tail), F32)], axis=1)
    b_router = jnp.concatenate([b_rg, jnp.zeros((gap,), F32), b_re, jnp.zeros((tail,), F32)]).reshape(1, LANES)
    x1, h2, rt, counts = _out_proj(o_nsa, o_ret, w_out.astype(BF16), xt, norm2_g.reshape(1, D),
                                   w_router, b_router, tiles["tm_out"])

    tile_start, pos, pad_rows, n_rows = _block_layout(rt, counts, T)
    xs = _dispatch(pos, pad_rows, h2, n_rows, tiles["tm_dsp"])
    y_rows = _experts(xs, tile_start, w_eg, w_eu, w_ed)
    out = _combine(pos, y_rows, x1, rt, tiles["tm_cmb"])
    return out.reshape(B, S, D)


def _tiles(T, S):
    return {
        "tm_in": min(2048, T), "tn_in": 512,
        "tq": min(256, S), "tk": min(512, S),
        "tm_out": min(512, T),
        "tm_dsp": min(1024, T),
        "tm_cmb": min(256, T),
    }


def kernel(x, norm1_g, w_in, cmp_pos, cmp_w1, cmp_w2, q_norm_g, k_norm_g, ret_gn_g, ret_gn_b, w_out, norm2_g, w_router_group, b_router_group, w_router_expert, b_router_expert, w_exp_gate, w_exp_up, w_exp_down):
    B, S, _ = x.shape
    tiles = _tiles(B * S, S)
    for l in range(norm1_g.shape[0]):
        x = _layer(x, norm1_g[l], w_in[l], cmp_pos[l], cmp_w1[l], cmp_w2[l], q_norm_g[l], k_norm_g[l],
                   ret_gn_g[l], ret_gn_b[l], w_out[l], norm2_g[l], w_router_group[l], b_router_group[l],
                   w_router_expert[l], b_router_expert[l], w_exp_gate[l], w_exp_up[l], w_exp_down[l], tiles)
    return x
```

```python
import functools

import numpy as np
import jax
import jax.numpy as jnp
from jax import lax
from jax.experimental import pallas as pl
from jax.experimental.pallas import tpu as pltpu

F32 = jnp.float32
BF16 = jnp.bfloat16

D_MODEL = 2048
NSA_HEADS = 8
NSA_KV_GROUPS = 2
NSA_GROUP_HEADS = NSA_HEADS // NSA_KV_GROUPS
HEAD_DIM = 128
RET_HEADS = 8
CMP_BLOCK = 32
CMP_STRIDE = 16
SEL_BLOCK = 64
SEL_TOPK = 8
WINDOW = 512
RET_CHUNK = 128
ROPE_BASE = 10000.0
N_GROUPS = 4
EXPERTS_PER_GROUP = 8
N_EXPERTS = N_GROUPS * EXPERTS_PER_GROUP
TOPK_IN_GROUP = 2
D_EXPERT = 512
MOE_BLOCK = 256
RMS_EPS = 1e-6
GN_EPS = 1e-5
NEG_INF = -1e30
FORCED_SCORE = 1e6
EXP_ROW0 = 8
LOG2E = 1.4426950408889634

NSA_Q_WIDTH = NSA_HEADS * HEAD_DIM
NSA_KV_WIDTH = NSA_KV_GROUPS * HEAD_DIM
NSA_GATE_WIDTH = 3 * NSA_HEADS
RET_WIDTH = RET_HEADS * HEAD_DIM
GATE_COL0 = NSA_Q_WIDTH + 6 * NSA_KV_WIDTH
LANES = 128
VMEM_LIMIT = 56 * 1024 * 1024

Q_COL0 = 4 * RET_WIDTH
KV_COL0 = Q_COL0 + NSA_Q_WIDTH
KC_BLK, VC_BLK, KS_BLK, VS_BLK, KW_BLK, VW_BLK = [KV_COL0 // LANES + 2 * t for t in range(6)]


def _rms(xf, g):
    return xf * lax.rsqrt(jnp.mean(xf * xf, axis=-1, keepdims=True) + RMS_EPS) * g


def _dot(a, b):
    return jnp.dot(a, b, preferred_element_type=F32)


def _dot_nt(a, b):
    return lax.dot_general(a, b, (((1,), (1,)), ((), ())), preferred_element_type=F32)


def _dot_tn(a, b):
    return lax.dot_general(a, b, (((0,), (0,)), ((), ())), preferred_element_type=F32)


def _split3(p):
    p1 = p.astype(BF16)
    r1 = p - p1.astype(F32)
    p2 = r1.astype(BF16)
    p3 = (r1 - p2.astype(F32)).astype(BF16)
    return p1, p2, p3


def _params(sem):
    return pltpu.CompilerParams(dimension_semantics=sem, vmem_limit_bytes=VMEM_LIMIT)


IN_CHUNK = 256


def _in_proj_kernel(x_hbm, g_ref, w_ref, wg_ref, o_ref, og_ref, h0, h1, xc0, xc1, sem, *, tm):
    m = pl.program_id(0)
    n = pl.program_id(1)
    ck = IN_CHUNK
    nc = tm // ck
    hs = (h0, h1)
    xcs = (xc0, xc1)
    g = g_ref[...]

    def chunk_copy(tile, c, slot):
        row0 = pl.multiple_of(tile * tm + c * ck, ck)
        return pltpu.make_async_copy(x_hbm.at[pl.ds(row0, ck), :], xcs[slot], sem.at[slot])

    def normalise(slot, dst, c):
        dst[pl.ds(pl.multiple_of(c * ck, ck), ck), :] = _rms(xcs[slot][...], g).astype(BF16)

    @pl.when((m == 0) & (n == 0))
    def _():
        chunk_copy(0, 0, 0).start()
        for c in range(nc):
            if c + 1 < nc:
                chunk_copy(0, c + 1, (c + 1) % 2).start()
            chunk_copy(0, c, c % 2).wait()
            normalise(c % 2, h0, c)

    has_next = m + 1 < pl.num_programs(0)
    for slot in range(2):
        @pl.when(has_next & (n >= 1) & (n <= nc) & (lax.rem(n - 1, 2) == slot))
        def _(slot=slot):
            chunk_copy(m + 1, n - 1, slot).wait()

        @pl.when(has_next & (n < nc) & (lax.rem(n, 2) == slot))
        def _(slot=slot):
            chunk_copy(m + 1, n, slot).start()

    c_prev = jnp.where(n == 0, nc - 1, jnp.clip(n - 1, 0, nc - 1))
    even_chunk = lax.rem(c_prev, 2) == 0
    for cur in range(2):
        for slot in range(2):
            @pl.when((lax.rem(m, 2) == cur) & (even_chunk == (slot == 0)))
            def _(cur=cur, slot=slot):
                @pl.when(n == 0)
                def _():
                    og_ref[...] = _dot(hs[cur][...], wg_ref[...])

                normalise(slot, hs[1 - cur], c_prev)
                o_ref[...] = _dot(hs[cur][...], w_ref[...])


def _in_proj(xt, g1, w_main, w_gate, tm, tn):
    T, D = xt.shape
    n_main = w_main.shape[1]
    n_gate = w_gate.shape[1]
    assert tm % (2 * IN_CHUNK) == 0 and tm // IN_CHUNK < n_main // tn
    return pl.pallas_call(
        functools.partial(_in_proj_kernel, tm=tm),
        grid=(T // tm, n_main // tn),
        in_specs=[
            pl.BlockSpec(memory_space=pl.ANY),
            pl.BlockSpec((1, D), lambda m, n: (0, 0)),
            pl.BlockSpec((D, tn), lambda m, n: (0, n)),
            pl.BlockSpec((D, n_gate), lambda m, n: (0, 0)),
        ],
        out_specs=[
            pl.BlockSpec((tm, tn), lambda m, n: (m, n)),
            pl.BlockSpec((tm, n_gate), lambda m, n: (m, 0)),
        ],
        out_shape=[
            jax.ShapeDtypeStruct((T, n_main), F32),
            jax.ShapeDtypeStruct((T, n_gate), F32),
        ],
        scratch_shapes=[pltpu.VMEM((tm, D), BF16), pltpu.VMEM((tm, D), BF16),
                        pltpu.VMEM((IN_CHUNK, D), F32), pltpu.VMEM((IN_CHUNK, D), F32),
                        pltpu.SemaphoreType.DMA((2,))],
        compiler_params=_params(("arbitrary", "arbitrary")),
        name="in_proj",
    )(xt, g1, w_main, w_gate)


def _compress_kernel(x_ref, pos_ref, w1_ref, w2_ref, kg_ref, o_ref, xs_scr, *, S, n_pad):
    kv = pl.program_id(1)
    xs_scr[0:S, :] = x_ref[...]
    xs_scr[S:S + CMP_STRIDE, :] = jnp.zeros((CMP_STRIDE, HEAD_DIM), F32)
    acc = jnp.zeros((n_pad, HEAD_DIM), F32)
    for l in range(CMP_BLOCK):
        tb = xs_scr[pl.ds(l, n_pad, stride=CMP_STRIDE), :] + pos_ref[0, l:l + 1, :]
        acc = acc + _dot(tb.astype(BF16), w1_ref[0, l].astype(BF16))
    hid = acc * jax.nn.sigmoid(acc)
    out = _dot(hid.astype(BF16), w2_ref[0].astype(BF16))
    normed = _rms(out, kg_ref[0:1, :])
    o_ref[0, 0, 0] = jnp.where(kv == 0, normed, out).astype(BF16)


def _compress(proj, cmp_pos, cmp_w1, cmp_w2, k_norm_g, B, S):
    n_pad = S // CMP_STRIDE
    G = NSA_KV_GROUPS
    kern = functools.partial(_compress_kernel, S=S, n_pad=n_pad)
    return pl.pallas_call(
        kern,
        grid=(B, 2, G),
        in_specs=[
            pl.BlockSpec((S, HEAD_DIM), lambda b, kv, g: (b, KC_BLK + 2 * kv + g)),
            pl.BlockSpec((1, CMP_BLOCK, HEAD_DIM), lambda b, kv, g: (kv, 0, 0)),
            pl.BlockSpec((1, CMP_BLOCK, HEAD_DIM, HEAD_DIM), lambda b, kv, g: (kv, 0, 0, 0)),
            pl.BlockSpec((1, HEAD_DIM, HEAD_DIM), lambda b, kv, g: (kv, 0, 0)),
            pl.BlockSpec((3, HEAD_DIM), lambda b, kv, g: (0, 0)),
        ],
        out_specs=pl.BlockSpec((1, 1, 1, n_pad, HEAD_DIM), lambda b, kv, g: (b, kv, g, 0, 0)),
        out_shape=jax.ShapeDtypeStruct((B, 2, G, n_pad, HEAD_DIM), BF16),
        scratch_shapes=[pltpu.VMEM((S + CMP_STRIDE, HEAD_DIM), F32)],
        compiler_params=_params(("parallel", "parallel", "parallel")),
        name="compress",
    )(proj, cmp_pos, cmp_w1, cmp_w2, k_norm_g)


def _nsa_kernel(q_ref, kc_ref, vc_ref, ks_ref, vs_ref, kw_ref, vw_ref, gate_ref, qg_ref, kg_ref,
                ovt_ref, ext_ref, o_ref, ksa, vsa, kwn, vwa, l_scr, acc_scr, *, S, tq, tk, wk):
    i = pl.program_id(2)
    H = NSA_GROUP_HEADS
    scale = HEAD_DIM ** -0.5
    c2 = scale * LOG2E
    n_cp = S // CMP_STRIDE
    n_sblk = S // SEL_BLOCK

    @pl.when(i == 0)
    def _():
        ksa[:, 0:HEAD_DIM] = _rms(ks_ref[...], kg_ref[1:2, :]).astype(BF16)
        ksa[:, HEAD_DIM:2 * HEAD_DIM] = ext_ref[...]
        kwn[...] = _rms(kw_ref[...], kg_ref[2:3, :]).astype(BF16)
        ones = jnp.ones((S, HEAD_DIM), BF16)
        vsa[:, 0:HEAD_DIM] = vs_ref[...].astype(BF16)
        vsa[:, HEAD_DIM:2 * HEAD_DIM] = ones
        vwa[:, 0:HEAD_DIM] = vw_ref[...].astype(BF16)
        vwa[:, HEAD_DIM:2 * HEAD_DIM] = ones

    pos = i * tq + lax.broadcasted_iota(jnp.int32, (tq, 1), 0)
    qg = qg_ref[...]
    q4 = jnp.concatenate([_rms(q_ref[:, h * HEAD_DIM:(h + 1) * HEAD_DIM], qg).astype(BF16)
                          for h in range(H)], axis=0)
    rows = [slice(h * tq, (h + 1) * tq) for h in range(H)]

    def softmax_pv(q_chains, k, v, tail_bias):
        head = k.shape[0] - tail_bias.shape[1]
        outs, sums = [], []
        for q2 in q_chains:
            s2 = _dot_nt(q2, k)
            es = []
            hpc = q2.shape[0] // tq
            for hh in range(hpc):
                t = s2[hh * tq:(hh + 1) * tq] * c2
                tb = t[:, head:] + tail_bias
                m = jnp.max(tb, axis=-1, keepdims=True)
                if head:
                    ta = t[:, :head]
                    m = jnp.maximum(m, jnp.max(ta, axis=-1, keepdims=True))
                    es.append(jnp.concatenate([jnp.exp2(ta - m).astype(BF16),
                                               jnp.exp2(tb - m).astype(BF16)], axis=1))
                else:
                    es.append(jnp.exp2(tb - m).astype(BF16))
            o2 = _dot(es[0] if hpc == 1 else jnp.concatenate(es, axis=0), v)
            outs += [o2[hh * tq:(hh + 1) * tq, 0:HEAD_DIM] for hh in range(hpc)]
            sums += [o2[hh * tq:(hh + 1) * tq, HEAD_DIM:2 * HEAD_DIM] for hh in range(hpc)]
        return outs, sums

    pos_t = i * tq + lax.broadcasted_iota(jnp.int32, (1, tq), 1)
    nrow = lax.broadcasted_iota(jnp.int32, (n_cp, 1), 0)
    cmask_t = (nrow * CMP_STRIDE + (CMP_BLOCK - 1)) <= pos_t
    s_ct = _dot_nt(kc_ref[0, 0, 0], q4) * scale
    vc = vc_ref[0, 0, 0]
    ps_t, o_cs = [], []
    for r in rows:
        sm = jnp.where(cmask_t, s_ct[:, r], NEG_INF)
        e = jnp.where(cmask_t, jnp.exp(sm - jnp.max(sm, axis=0, keepdims=True)), 0.0)
        den = jnp.sum(e, axis=0, keepdims=True)
        p = e / jnp.where(den > 0.0, den, 1.0)
        ps_t.append(p)
        o_cs.append(_dot_tn(p.astype(BF16), vc))
    psum_t = ps_t[0] + ps_t[1] + ps_t[2] + ps_t[3]

    ovt = ovt_ref[...]
    p1, p2, p3 = _split3(psum_t)
    imp_t = (_dot(ovt, p1) + _dot(ovt, p2) + _dot(ovt, p3))[0:n_sblk, :]
    jrow = lax.broadcasted_iota(jnp.int32, (n_sblk, 1), 0)
    cur_t = pos_t // SEL_BLOCK
    valid_t = jrow * SEL_BLOCK <= pos_t
    forced_t = (jrow == 0) | (jrow == cur_t) | (jrow == cur_t - 1)
    score = jnp.where(valid_t, jnp.where(forced_t, FORCED_SCORE, imp_t), -jnp.inf)
    rank = jnp.zeros((n_sblk, tq), jnp.int32)
    for i2 in range(n_sblk):
        si = score[i2:i2 + 1, :]
        ahead = (si > score) | ((si == score) & (jrow > i2))
        rank = rank + jnp.where(ahead, 1, 0)
    drop_t = jnp.where((rank < SEL_TOPK) & valid_t, 0.0, NEG_INF)
    drop_t = jnp.concatenate([drop_t, jnp.full((LANES - n_sblk, tq), NEG_INF, F32)], axis=0)
    drop = drop_t.T.astype(BF16)
    qa_pairs = [jnp.concatenate([q4[r], drop], axis=1) for r in rows]

    kst = pl.multiple_of(jnp.clip(i * tq - WINDOW, 0, S - wk), tq)
    wpos = kst + lax.broadcasted_iota(jnp.int32, (1, wk), 1)
    wbias = jnp.where((wpos <= pos) & (wpos > pos - WINDOW), 0.0, NEG_INF)
    o_ws, lws = softmax_pv([q4[r] for r in rows],
                           kwn[pl.ds(kst, wk), :], vwa[pl.ds(kst, wk), :], wbias)

    def slc_prefix(n):
        kpos = (n - tk) + lax.broadcasted_iota(jnp.int32, (1, tk), 1)
        causal = jnp.where(kpos <= pos, 0.0, NEG_INF)
        outs, sums = softmax_pv(qa_pairs, ksa[0:n, :], vsa[0:n, :], causal)
        for r, o, l in zip(rows, outs, sums):
            acc_scr[r] = o
            l_scr[r] = l

    n_cls = S // tk
    cls = ((i + 1) * tq - 1) // tk
    for c in range(n_cls):
        pl.when(cls == c)(functools.partial(slc_prefix, (c + 1) * tk))

    gates = jax.nn.sigmoid(gate_ref[...])
    for h, r in enumerate(rows):
        c = 3 * h
        o = (o_cs[h] * gates[:, c:c + 1]
             + acc_scr[r] * (gates[:, c + 1:c + 2] / l_scr[r])
             + o_ws[h] * (gates[:, c + 2:c + 3] / lws[h]))
        o_ref[:, h * HEAD_DIM:(h + 1) * HEAD_DIM] = o.astype(BF16)


def _nsa_tables(S):
    n_cp, n_cmp, n_sblk = S // CMP_STRIDE, (S - CMP_BLOCK) // CMP_STRIDE + 1, S // SEL_BLOCK
    n = np.arange(n_cp)[None, :] * CMP_STRIDE
    j = np.arange(LANES)[:, None]
    ovt = ((n < (j + 1) * SEL_BLOCK) & (n + CMP_BLOCK > j * SEL_BLOCK)
           & (np.arange(n_cp)[None, :] < n_cmp) & (j < n_sblk))
    key_block = (np.arange(S) // SEL_BLOCK)[:, None] == np.arange(LANES)[None, :]
    return jnp.asarray(ovt, BF16), jnp.asarray(key_block, BF16)


def _nsa(proj, gate, cmp_kv, q_norm_g, k_norm_g, B, S, tq, tk):
    G = NSA_KV_GROUPS
    H = NSA_GROUP_HEADS
    n_cp = S // CMP_STRIDE
    wk = min(S, WINDOW + tq)
    nq = S // tq
    gw = H * HEAD_DIM
    ovt, key_block = _nsa_tables(S)
    kern = functools.partial(_nsa_kernel, S=S, tq=tq, tk=tk, wk=wk)
    kv_spec = lambda blk: pl.BlockSpec((S, HEAD_DIM), lambda b, g, i: (b, blk + g))
    return pl.pallas_call(
        kern,
        grid=(B, G, nq),
        in_specs=[
            pl.BlockSpec((tq, gw), lambda b, g, i: (b * nq + i, Q_COL0 // gw + g)),
            pl.BlockSpec((1, 1, 1, n_cp, HEAD_DIM), lambda b, g, i: (b, 0, g, 0, 0)),
            pl.BlockSpec((1, 1, 1, n_cp, HEAD_DIM), lambda b, g, i: (b, 1, g, 0, 0)),
            kv_spec(KS_BLK), kv_spec(VS_BLK), kv_spec(KW_BLK), kv_spec(VW_BLK),
            pl.BlockSpec((tq, LANES), lambda b, g, i: (b * nq + i, g)),
            pl.BlockSpec((1, HEAD_DIM), lambda b, g, i: (0, 0)),
            pl.BlockSpec((3, HEAD_DIM), lambda b, g, i: (0, 0)),
            pl.BlockSpec((LANES, n_cp), lambda b, g, i: (0, 0)),
            pl.BlockSpec((S, LANES), lambda b, g, i: (0, 0)),
        ],
        out_specs=pl.BlockSpec((tq, gw), lambda b, g, i: (b * nq + i, g)),
        out_shape=jax.ShapeDtypeStruct((B * S, NSA_Q_WIDTH), BF16),
        scratch_shapes=[
            pltpu.VMEM((S, 2 * HEAD_DIM), BF16), pltpu.VMEM((S, 2 * HEAD_DIM), BF16),
            pltpu.VMEM((S, HEAD_DIM), BF16), pltpu.VMEM((S, 2 * HEAD_DIM), BF16),
            pltpu.VMEM((H * tq, HEAD_DIM), F32), pltpu.VMEM((H * tq, HEAD_DIM), F32)],
        compiler_params=_params(("parallel", "parallel", "arbitrary")),
        name="nsa",
    )(proj, cmp_kv, cmp_kv, proj, proj, proj, proj, gate, q_norm_g, k_norm_g, ovt, key_block)


def _ret_kernel(rq_ref, rk_ref, rv_ref, rg_ref, cos_ref, sin_ref, dec_ref, xi_ref, zeta_ref, cd_ref,
                gg_ref, gb_ref, o_ref, r_scr):
    scale = HEAD_DIM ** -0.5

    @pl.when(pl.program_id(1) == 0)
    def _():
        r_scr[...] = jnp.zeros(r_scr.shape, F32)

    cos = cos_ref[...]
    sin = sin_ref[...]
    for h in range(RET_HEADS):
        sl = slice(h * HEAD_DIM, (h + 1) * HEAD_DIM)
        q = rq_ref[:, sl]
        k = rk_ref[:, sl]
        qf = q * cos + pltpu.roll(q, HEAD_DIM // 2, 1) * sin
        kf = (k * cos + pltpu.roll(k, HEAD_DIM // 2, 1) * sin) * scale
        v = rv_ref[:, sl].astype(BF16)
        qb = qf.astype(BF16)
        r_old = r_scr[h]
        a = _dot_nt(qb, kf.astype(BF16)) * dec_ref[h]
        o = _dot(a.astype(BF16), v) + _dot(qb, r_old.astype(BF16)) * xi_ref[h]
        r_scr[h] = r_old * cd_ref[h:h + 1, :] + _dot_tn((kf * zeta_ref[h]).astype(BF16), v)
        mu = jnp.mean(o, axis=-1, keepdims=True)
        d = o - mu
        var = jnp.mean(d * d, axis=-1, keepdims=True)
        y = d * lax.rsqrt(var + GN_EPS) * gg_ref[h:h + 1, :] + gb_ref[h:h + 1, :]
        gt = rg_ref[:, sl]
        o_ref[:, sl] = (gt * jax.nn.sigmoid(gt) * y).astype(BF16)


def _ret_tables(S):
    C, H, half = RET_CHUNK, RET_HEADS, HEAD_DIM // 2
    inv_freq = ROPE_BASE ** (-jnp.arange(half, dtype=F32) / half)
    ang = jnp.arange(S, dtype=F32)[:, None] * inv_freq[None, :]
    cos2 = jnp.concatenate([jnp.cos(ang), jnp.cos(ang)], axis=1)
    sin2 = jnp.concatenate([-jnp.sin(ang), jnp.sin(ang)], axis=1)
    log_g = jnp.log1p(-jnp.exp2(-5.0 - jnp.arange(H, dtype=F32)))
    n = jnp.arange(C, dtype=F32)
    diff = n[:, None] - n[None, :]
    dec = jnp.where(diff >= 0, jnp.exp(log_g[:, None, None] * jnp.maximum(diff, 0.0)), 0.0)
    lanes = lambda t: jnp.broadcast_to(t[..., None], t.shape + (HEAD_DIM,))
    xi = lanes(jnp.exp(log_g[:, None] * (n + 1.0)))
    zeta = lanes(jnp.exp(log_g[:, None] * (C - 1.0 - n)))
    cd = lanes(jnp.exp(log_g * C))
    return cos2, sin2, dec, xi, zeta, cd


def _retention(proj, gn_g, gn_b, B, S):
    C, H = RET_CHUNK, RET_HEADS
    n_ch = S // C
    cos2, sin2, dec, xi, zeta, cd = _ret_tables(S)
    spec = lambda k: pl.BlockSpec((C, RET_WIDTH), lambda b, c: (b * n_ch + c, k))
    whole = lambda a: pl.BlockSpec(a.shape, lambda b, c: (0,) * a.ndim)
    return pl.pallas_call(
        _ret_kernel,
        grid=(B, n_ch),
        in_specs=[
            spec(0), spec(1), spec(2), spec(3),
            pl.BlockSpec((C, HEAD_DIM), lambda b, c: (c, 0)),
            pl.BlockSpec((C, HEAD_DIM), lambda b, c: (c, 0)),
            whole(dec), whole(xi), whole(zeta), whole(cd), whole(gn_g), whole(gn_b),
        ],
        out_specs=pl.BlockSpec((C, RET_WIDTH), lambda b, c: (b * n_ch + c, 0)),
        out_shape=jax.ShapeDtypeStruct((B * S, RET_WIDTH), BF16),
        scratch_shapes=[pltpu.VMEM((H, HEAD_DIM, HEAD_DIM), F32)],
        compiler_params=_params(("parallel", "arbitrary")),
        name="retention",
    )(proj, proj, proj, proj, cos2, sin2, dec, xi, zeta, cd, gn_g, gn_b)


def _out_kernel(on_ref, or_ref, w_ref, x_ref, g2_ref, wr_ref, br_ref, tri_ref,
                x1_ref, h2_ref, rt_ref, cnt_ref, cnt_scr):
    @pl.when(pl.program_id(0) == 0)
    def _():
        cnt_scr[...] = jnp.zeros(cnt_scr.shape, F32)

    half = on_ref.shape[1]
    acc = _dot(on_ref[...], w_ref[0:half, :]) + _dot(or_ref[...], w_ref[half:2 * half, :])
    x1 = x_ref[...] + acc
    x1_ref[...] = x1
    h2 = _rms(x1, g2_ref[...])
    h2_ref[...] = h2

    h_hi = h2.astype(BF16)
    h_lo = (h2 - h_hi.astype(F32)).astype(BF16)
    wr = wr_ref[...]
    w_hi = wr.astype(BF16)
    w_lo = (wr - w_hi.astype(F32)).astype(BF16)
    logits = _dot(h_hi, w_hi) + _dot(h_lo, w_hi) + _dot(h_hi, w_lo) + br_ref[...]

    tm = logits.shape[0]
    lt = logits.T
    G8 = EXPERTS_PER_GROUP
    row = lax.broadcasted_iota(jnp.int32, (G8, 1), 0).astype(F32)
    big = float(G8)
    col_max = lambda v: jnp.max(v, axis=0, keepdims=True)
    col_min = lambda v: jnp.min(v, axis=0, keepdims=True)
    col_sum = lambda v: jnp.sum(v, axis=0, keepdims=True)
    gm = row < N_GROUPS
    gl = jnp.where(gm, lt[0:G8], -jnp.inf)
    ge = jnp.where(gm, jnp.exp(gl - col_max(gl)), 0.0)
    pg = ge / col_sum(ge)
    g_top = col_max(pg)
    g_idx = col_min(jnp.where(gm & (pg == g_top), row, big))
    el = lt[EXP_ROW0:EXP_ROW0 + G8]
    for g in range(1, N_GROUPS):
        el = jnp.where(g_idx == float(g), lt[EXP_ROW0 + g * G8:EXP_ROW0 + (g + 1) * G8], el)
    ee = jnp.exp(el - col_max(el))
    pe = ee / col_sum(ee)
    t1 = col_max(pe)
    i1 = col_min(jnp.where(pe == t1, row, big))
    rest = row != i1
    t2 = col_max(jnp.where(rest, pe, -1.0))
    i2 = col_min(jnp.where(rest & (pe == t2), row, big))
    tsum = t1 + t2
    w1 = g_top * t1 / tsum
    w2 = g_top * t2 / tsum
    e1 = g_idx * float(G8) + i1
    e2 = g_idx * float(G8) + i2
    erow = lax.broadcasted_iota(jnp.int32, (N_EXPERTS, 1), 0).astype(F32)
    oh1 = jnp.where(erow == e1, 1.0, 0.0)
    oh2 = jnp.where(erow == e2, 1.0, 0.0)
    both = oh1 + oh2
    before = _dot(both.astype(BF16), tri_ref[...]) + cnt_scr[...]
    r1 = col_sum(before * oh1)
    r2 = col_sum(before * oh2)
    cnt_scr[...] = cnt_scr[...] + jnp.sum(both, axis=1, keepdims=True)
    cnt_ref[...] = jnp.broadcast_to(cnt_scr[...], cnt_ref.shape)
    rt_ref[...] = jnp.concatenate([e1, e2, w1, w2, r1, r2, jnp.zeros((2, tm), F32)], axis=0)


def _out_proj(o_nsa, o_ret, w_out, xt, g2, w_router, b_router, tm):
    T, D = xt.shape
    half = o_nsa.shape[1]
    tri = jnp.asarray(np.triu(np.ones((tm, tm), np.float32), 1), BF16)
    return pl.pallas_call(
        _out_kernel,
        grid=(T // tm,),
        in_specs=[
            pl.BlockSpec((tm, half), lambda m: (m, 0)),
            pl.BlockSpec((tm, half), lambda m: (m, 0)),
            pl.BlockSpec((2 * half, D), lambda m: (0, 0), pipeline_mode=pl.Buffered(1)),
            pl.BlockSpec((tm, D), lambda m: (m, 0)),
            pl.BlockSpec((1, D), lambda m: (0, 0)),
            pl.BlockSpec((D, LANES), lambda m: (0, 0)),
            pl.BlockSpec((1, LANES), lambda m: (0, 0)),
            pl.BlockSpec((tm, tm), lambda m: (0, 0)),
        ],
        out_specs=[
            pl.BlockSpec((tm, D), lambda m: (m, 0)),
            pl.BlockSpec((tm, D), lambda m: (m, 0)),
            pl.BlockSpec((8, tm), lambda m: (0, m)),
            pl.BlockSpec((N_EXPERTS, LANES), lambda m: (0, 0)),
        ],
        out_shape=[
            jax.ShapeDtypeStruct((T, D), F32),
            jax.ShapeDtypeStruct((T, D), F32),
            jax.ShapeDtypeStruct((8, T), F32),
            jax.ShapeDtypeStruct((N_EXPERTS, LANES), F32),
        ],
        scratch_shapes=[pltpu.VMEM((N_EXPERTS, 1), F32)],
        compiler_params=_params(("arbitrary",)),
        name="out_proj",
    )(o_nsa, o_ret, w_out, xt, g2, w_router, b_router, tri)


def _row_copy(src, row, dst, slot, sem):
    return pltpu.make_async_copy(src.at[pl.ds(row, 1), :], dst.at[pl.ds(slot, 1), :], sem)


def _dispatch_kernel(pos_ref, pad_ref, h2_ref, xs_hbm, zrow, sem, zsem, *, n_tok):
    i = pl.program_id(0)
    tm = h2_ref.shape[0]
    K = TOPK_IN_GROUP

    def issue(r, carry):
        for k in range(K):
            _row_copy(h2_ref, r, xs_hbm, pos_ref[k * n_tok + i * tm + r], sem).start()
        return carry

    lax.fori_loop(0, tm, issue, 0, unroll=8)

    @pl.when(i == pl.num_programs(0) - 1)
    def _():
        zrow[...] = jnp.zeros(zrow.shape, F32)
        for e in range(N_EXPERTS):
            lo, hi = pad_ref[e], pad_ref[N_EXPERTS + e]
            mid = jnp.minimum(lax.shift_left(lax.shift_right_logical(lo + 7, 3), 3), hi)

            def fill(r, carry):
                _row_copy(zrow, 0, xs_hbm, r, zsem).start()
                return carry

            lax.fori_loop(lo, mid, fill, 0)

            def fill_wait(r, carry):
                _row_copy(zrow, 0, xs_hbm, r, zsem).wait()
                return carry

            lax.fori_loop(lo, mid, fill_wait, 0)

            def piece(j):
                r8 = pl.multiple_of(mid + j * 8, 8)
                return pltpu.make_async_copy(zrow.at[pl.ds(0, 8), :], xs_hbm.at[pl.ds(r8, 8), :], zsem)

            n8 = lax.shift_right_logical(hi - mid, 3)

            def fill8(j, carry):
                piece(j).start()
                return carry

            lax.fori_loop(0, n8, fill8, 0)

            def fill8_wait(j, carry):
                piece(j).wait()
                return carry

            lax.fori_loop(0, n8, fill8_wait, 0)

        M = zrow.shape[0]

        def tail_copy(t):
            return pltpu.make_async_copy(zrow, xs_hbm.at[pl.ds(pl.multiple_of(t * M, M), M), :], zsem)

        def tail(t, carry):
            tail_copy(t).start()
            return carry

        lax.fori_loop(pad_ref[2 * N_EXPERTS], xs_hbm.shape[0] // M, tail, 0)

        def tail_wait(t, carry):
            tail_copy(t).wait()
            return carry

        lax.fori_loop(pad_ref[2 * N_EXPERTS], xs_hbm.shape[0] // M, tail_wait, 0)

    for k in range(K):
        pltpu.make_async_copy(h2_ref, xs_hbm.at[pl.ds(0, tm), :], sem).wait()


def _dispatch(pos, pad_rows, h2, n_rows, tm):
    T, D = h2.shape
    grid_spec = pltpu.PrefetchScalarGridSpec(
        num_scalar_prefetch=2,
        grid=(T // tm,),
        in_specs=[pl.BlockSpec((tm, D), lambda i, pos, pad: (i, 0))],
        out_specs=pl.BlockSpec(memory_space=pl.ANY),
        scratch_shapes=[pltpu.VMEM((MOE_BLOCK, D), F32), pltpu.SemaphoreType.DMA, pltpu.SemaphoreType.DMA],
    )
    return pl.pallas_call(
        functools.partial(_dispatch_kernel, n_tok=T),
        grid_spec=grid_spec,
        out_shape=jax.ShapeDtypeStruct((n_rows, D), F32),
        compiler_params=_params(("arbitrary",)),
        name="dispatch",
    )(pos, pad_rows, h2)


def _expert_kernel(ts_ref, xs_hbm, wg_ref, wu_ref, wd_ref, y_hbm,
                   xb0, xb1, yb0, yb1, wg_b, wu_b, wd_b, gsem, osem, *, n_blocks):
    e = pl.program_id(0)
    M = MOE_BLOCK
    t0 = ts_ref[e]
    t1 = ts_ref[e + 1]
    n_used = ts_ref[N_EXPERTS]
    xbufs = (xb0, xb1)
    ybufs = (yb0, yb1)

    def rows(t):
        return pl.ds(pl.multiple_of(t * M, M), M)

    def in_copy(t, p):
        return pltpu.make_async_copy(xs_hbm.at[rows(t), :], xbufs[p], gsem.at[p])

    def out_copy(t, p):
        return pltpu.make_async_copy(ybufs[p], y_hbm.at[rows(t), :], osem.at[p])

    @pl.when(e == 0)
    def _():
        in_copy(0, 0).start()

    @pl.when(t1 > t0)
    def _():
        wg_b[...] = wg_ref[0].astype(BF16)
        wu_b[...] = wu_ref[0].astype(BF16)
        wd_b[...] = wd_ref[0].astype(BF16)

    def tile_body(t, p):
        in_copy(t, p).wait()

        @pl.when(t + 1 < n_used)
        def _():
            in_copy(t + 1, 1 - p).start()

        @pl.when(t >= 2)
        def _():
            out_copy(t - 2, p).wait()

        xb = xbufs[p][...].astype(BF16)
        hg = _dot(xb, wg_b[...])
        hu = _dot(xb, wu_b[...])
        hb = (hg * jax.nn.sigmoid(hg) * hu).astype(BF16)
        ybufs[p][...] = _dot(hb, wd_b[...])
        out_copy(t, p).start()

    def tile(t, carry):
        parity = lax.rem(t, 2)
        for p in range(2):
            pl.when(parity == p)(functools.partial(tile_body, t, p))
        return carry

    lax.fori_loop(t0, t1, tile, 0)

    @pl.when(e == N_EXPERTS - 1)
    def _():
        parity = lax.rem(n_used, 2)
        for p in range(2):
            @pl.when(parity == p)
            def _(p=p):
                out_copy(n_used - 1, 1 - p).wait()

                @pl.when(n_used >= 2)
                def _():
                    out_copy(n_used - 2, p).wait()

        yb0[...] = jnp.zeros(yb0.shape, F32)

        def clear(t, carry):
            out_copy(t, 0).start()
            return carry

        lax.fori_loop(n_used, n_blocks, clear, 0)

        def clear_wait(t, carry):
            out_copy(t, 0).wait()
            return carry

        lax.fori_loop(n_used, n_blocks, clear_wait, 0)


def _experts(xs, tile_start, w_gate, w_up, w_down):
    n_rows, D = xs.shape
    n_blocks = n_rows // MOE_BLOCK
    grid_spec = pltpu.PrefetchScalarGridSpec(
        num_scalar_prefetch=1,
        grid=(N_EXPERTS,),
        in_specs=[
            pl.BlockSpec(memory_space=pl.ANY),
            pl.BlockSpec((1, D, D_EXPERT), lambda e, ts: (e, 0, 0)),
            pl.BlockSpec((1, D, D_EXPERT), lambda e, ts: (e, 0, 0)),
            pl.BlockSpec((1, D_EXPERT, D), lambda e, ts: (e, 0, 0)),
        ],
        out_specs=pl.BlockSpec(memory_space=pl.ANY),
        scratch_shapes=[pltpu.VMEM((MOE_BLOCK, D), F32)] * 4 + [
            pltpu.VMEM((D, D_EXPERT), BF16), pltpu.VMEM((D, D_EXPERT), BF16),
            pltpu.VMEM((D_EXPERT, D), BF16),
            pltpu.SemaphoreType.DMA((2,)), pltpu.SemaphoreType.DMA((2,)),
        ],
    )
    return pl.pallas_call(
        functools.partial(_expert_kernel, n_blocks=n_blocks),
        grid_spec=grid_spec,
        out_shape=jax.ShapeDtypeStruct((n_rows, D), F32),
        compiler_params=_params(("arbitrary",)),
        name="experts",
    )(tile_start, xs, w_gate, w_up, w_down)


def _combine_kernel(pos_ref, y_hbm, x1_ref, rt_ref, o_ref, ybuf, sems, *, n_tok):
    i = pl.program_id(0)
    n = pl.num_programs(0)
    tm = x1_ref.shape[0]
    K = TOPK_IN_GROUP
    slot = lax.rem(i, 2)

    def issue_tile(t, s):
        def issue(r, carry):
            for k in range(K):
                _row_copy(y_hbm, pos_ref[k * n_tok + t * tm + r], ybuf.at[s, k], r,
                          sems.at[s]).start(priority=k)
            return carry

        lax.fori_loop(0, tm, issue, 0, unroll=8)

    @pl.when(i == 0)
    def _():
        issue_tile(0, 0)

    @pl.when(i + 1 < n)
    def _():
        issue_tile(i + 1, 1 - slot)

    for k in range(K):
        pltpu.make_async_copy(y_hbm.at[pl.ds(0, tm), :], ybuf.at[slot, k], sems.at[slot]).wait()
    w = rt_ref[...].T
    yb = ybuf[slot]
    o_ref[...] = x1_ref[...] + (yb[0] * w[:, K:K + 1] + yb[1] * w[:, K + 1:K + 2])


def _combine(pos, y_rows, x1, rt, tm):
    T, D = x1.shape
    grid_spec = pltpu.PrefetchScalarGridSpec(
        num_scalar_prefetch=1,
        grid=(T // tm,),
        in_specs=[
            pl.BlockSpec(memory_space=pl.ANY),
            pl.BlockSpec((tm, D), lambda i, pos: (i, 0)),
            pl.BlockSpec((8, tm), lambda i, pos: (0, i)),
        ],
        out_specs=pl.BlockSpec((tm, D), lambda i, pos: (i, 0)),
        scratch_shapes=[pltpu.VMEM((2, TOPK_IN_GROUP, tm, D), F32), pltpu.SemaphoreType.DMA((2,))],
    )
    return pl.pallas_call(
        functools.partial(_combine_kernel, n_tok=T),
        grid_spec=grid_spec,
        out_shape=jax.ShapeDtypeStruct((T, D), F32),
        compiler_params=_params(("arbitrary",)),
        name="combine",
    )(pos, y_rows, x1, rt)


def _block_layout(rt, counts, T):
    K = TOPK_IN_GROUP
    n_rows = (K * T + N_EXPERTS * (MOE_BLOCK - 1) + MOE_BLOCK - 1) // MOE_BLOCK * MOE_BLOCK
    n_blocks = n_rows // MOE_BLOCK
    cnt = counts[:, 0].astype(jnp.int32)
    padded = (cnt + MOE_BLOCK - 1) // MOE_BLOCK * MOE_BLOCK
    pad_end = jnp.cumsum(padded)
    pad_start = pad_end - padded
    e = rt[0:K].astype(jnp.int32).reshape(-1)
    rank = rt[2 * K:3 * K].astype(jnp.int32).reshape(-1)
    hit = e[:, None] == jnp.arange(N_EXPERTS, dtype=jnp.int32)[None, :]
    pos = jnp.sum(jnp.where(hit, pad_start[None, :].astype(jnp.int32), 0), axis=1) + rank
    tile_start = (jnp.concatenate([pad_start, pad_end[-1:]]) // MOE_BLOCK).astype(jnp.int32)
    pad_rows = jnp.concatenate([pad_start + cnt, pad_end, pad_end[-1:] // MOE_BLOCK]).astype(jnp.int32)
    return tile_start, pos, pad_rows, n_rows


def _layer(x, norm1_g, w_in, cmp_pos, cmp_w1, cmp_w2, q_norm_g, k_norm_g, ret_gn_g, ret_gn_b,
           w_out, norm2_g, w_rg, b_rg, w_re, b_re, w_eg, w_eu, w_ed, tiles):
    B, S, D = x.shape
    T = B * S
    xt = x.reshape(T, D)

    w_main = jnp.concatenate([w_in[:, GATE_COL0 + NSA_GATE_WIDTH:], w_in[:, :GATE_COL0]], axis=1).astype(BF16)
    gpg = NSA_GATE_WIDTH // NSA_KV_GROUPS
    w_gate = jnp.concatenate(
        [jnp.pad(w_in[:, GATE_COL0 + g * gpg:GATE_COL0 + (g + 1) * gpg], ((0, 0), (0, LANES - gpg)))
         for g in range(NSA_KV_GROUPS)], axis=1).astype(BF16)
    proj, gate = _in_proj(xt, norm1_g.reshape(1, D), w_main, w_gate, tiles["tm_in"], tiles["tn_in"])

    cmp_kv = _compress(proj, cmp_pos, cmp_w1, cmp_w2, k_norm_g, B, S)
    o_nsa = _nsa(proj, gate, cmp_kv, q_norm_g.reshape(1, HEAD_DIM), k_norm_g, B, S, tiles["tq"], tiles["tk"])

    o_ret = _retention(proj, ret_gn_g, ret_gn_b, B, S)

    gap, tail = EXP_ROW0 - N_GROUPS, LANES - EXP_ROW0 - N_EXPERTS
    w_router = jnp.concatenate([w_rg, jnp.zeros((D, gap), F32), w_re, jnp.zeros((D, tail), F32)], axis=1)
    b_router = jnp.concatenate([b_rg, jnp.zeros((gap,), F32), b_re, jnp.zeros((tail,), F32)]).reshape(1, LANES)
    x1, h2, rt, counts = _out_proj(o_nsa, o_ret, w_out.astype(BF16), xt, norm2_g.reshape(1, D),
                                   w_router, b_router, tiles["tm_out"])

    tile_start, pos, pad_rows, n_rows = _block_layout(rt, counts, T)
    xs = _dispatch(pos, pad_rows, h2, n_rows, tiles["tm_dsp"])
    y_rows = _experts(xs, tile_start, w_eg, w_eu, w_ed)
    out = _combine(pos, y_rows, x1, rt, tiles["tm_cmb"])
    return out.reshape(B, S, D)


def _tiles(T, S):
    return {
        "tm_in": min(2048, T), "tn_in": 512,
        "tq": min(256, S), "tk": min(256, S),
        "tm_out": min(512, T),
        "tm_dsp": min(1024, T),
        "tm_cmb": min(256, T),
    }


def kernel(x, norm1_g, w_in, cmp_pos, cmp_w1, cmp_w2, q_norm_g, k_norm_g, ret_gn_g, ret_gn_b, w_out, norm2_g, w_router_group, b_router_group, w_router_expert, b_router_expert, w_exp_gate, w_exp_up, w_exp_down):
    B, S, _ = x.shape
    tiles = _tiles(B * S, S)
    for l in range(norm1_g.shape[0]):
        x = _layer(x, norm1_g[l], w_in[l], cmp_pos[l], cmp_w1[l], cmp_w2[l], q_norm_g[l], k_norm_g[l],
                   ret_gn_g[l], ret_gn_b[l], w_out[l], norm2_g[l], w_router_group[l], b_router_group[l],
                   w_router_expert[l], b_router_expert[l], w_exp_gate[l], w_exp_up[l], w_exp_down[l], tiles)
    return x
```

```python
import functools

import numpy as np
import jax
import jax.numpy as jnp
from jax import lax
from jax.experimental import pallas as pl
from jax.experimental.pallas import tpu as pltpu

F32 = jnp.float32
BF16 = jnp.bfloat16

D_MODEL = 2048
NSA_HEADS = 8
NSA_KV_GROUPS = 2
NSA_GROUP_HEADS = NSA_HEADS // NSA_KV_GROUPS
HEAD_DIM = 128
RET_HEADS = 8
CMP_BLOCK = 32
CMP_STRIDE = 16
SEL_BLOCK = 64
SEL_TOPK = 8
WINDOW = 512
RET_CHUNK = 128
ROPE_BASE = 10000.0
N_GROUPS = 4
EXPERTS_PER_GROUP = 8
N_EXPERTS = N_GROUPS * EXPERTS_PER_GROUP
TOPK_IN_GROUP = 2
D_EXPERT = 512
MOE_BLOCK = 256
RMS_EPS = 1e-6
GN_EPS = 1e-5
NEG_INF = -1e30
FORCED_SCORE = 1e6
EXP_ROW0 = 8
LOG2E = 1.4426950408889634

NSA_Q_WIDTH = NSA_HEADS * HEAD_DIM
NSA_KV_WIDTH = NSA_KV_GROUPS * HEAD_DIM
NSA_GATE_WIDTH = 3 * NSA_HEADS
RET_WIDTH = RET_HEADS * HEAD_DIM
GATE_COL0 = NSA_Q_WIDTH + 6 * NSA_KV_WIDTH
LANES = 128
VMEM_LIMIT = 56 * 1024 * 1024

Q_COL0 = 4 * RET_WIDTH
KV_COL0 = Q_COL0 + NSA_Q_WIDTH
KC_BLK, VC_BLK, KS_BLK, VS_BLK, KW_BLK, VW_BLK = [KV_COL0 // LANES + 2 * t for t in range(6)]


def _rms(xf, g):
    return xf * lax.rsqrt(jnp.mean(xf * xf, axis=-1, keepdims=True) + RMS_EPS) * g


def _dot(a, b):
    return jnp.dot(a, b, preferred_element_type=F32)


def _dot_nt(a, b):
    return lax.dot_general(a, b, (((1,), (1,)), ((), ())), preferred_element_type=F32)


def _dot_tn(a, b):
    return lax.dot_general(a, b, (((0,), (0,)), ((), ())), preferred_element_type=F32)


def _split3(p):
    p1 = p.astype(BF16)
    r1 = p - p1.astype(F32)
    p2 = r1.astype(BF16)
    p3 = (r1 - p2.astype(F32)).astype(BF16)
    return p1, p2, p3


def _params(sem):
    return pltpu.CompilerParams(dimension_semantics=sem, vmem_limit_bytes=VMEM_LIMIT)


IN_CHUNK = 256


def _in_proj_kernel(x_hbm, g_ref, w_ref, wg_ref, o_ref, og_ref, h0, h1, xc0, xc1, sem, *, tm):
    m = pl.program_id(0)
    n = pl.program_id(1)
    ck = IN_CHUNK
    nc = tm // ck
    hs = (h0, h1)
    xcs = (xc0, xc1)
    g = g_ref[...]

    def chunk_copy(tile, c, slot):
        row0 = pl.multiple_of(tile * tm + c * ck, ck)
        return pltpu.make_async_copy(x_hbm.at[pl.ds(row0, ck), :], xcs[slot], sem.at[slot])

    def normalise(slot, dst, c):
        dst[pl.ds(pl.multiple_of(c * ck, ck), ck), :] = _rms(xcs[slot][...], g).astype(BF16)

    @pl.when((m == 0) & (n == 0))
    def _():
        chunk_copy(0, 0, 0).start()
        for c in range(nc):
            if c + 1 < nc:
                chunk_copy(0, c + 1, (c + 1) % 2).start()
            chunk_copy(0, c, c % 2).wait()
            normalise(c % 2, h0, c)

    has_next = m + 1 < pl.num_programs(0)
    for slot in range(2):
        @pl.when(has_next & (n >= 1) & (n <= nc) & (lax.rem(n - 1, 2) == slot))
        def _(slot=slot):
            chunk_copy(m + 1, n - 1, slot).wait()

        @pl.when(has_next & (n < nc) & (lax.rem(n, 2) == slot))
        def _(slot=slot):
            chunk_copy(m + 1, n, slot).start()

    c_prev = jnp.where(n == 0, nc - 1, jnp.clip(n - 1, 0, nc - 1))
    even_chunk = lax.rem(c_prev, 2) == 0
    for cur in range(2):
        for slot in range(2):
            @pl.when((lax.rem(m, 2) == cur) & (even_chunk == (slot == 0)))
            def _(cur=cur, slot=slot):
                @pl.when(n == 0)
                def _():
                    og_ref[...] = _dot(hs[cur][...], wg_ref[...])

                normalise(slot, hs[1 - cur], c_prev)
                o_ref[...] = _dot(hs[cur][...], w_ref[...])


def _in_proj(xt, g1, w_main, w_gate, tm, tn):
    T, D = xt.shape
    n_main = w_main.shape[1]
    n_gate = w_gate.shape[1]
    assert tm % (2 * IN_CHUNK) == 0 and tm // IN_CHUNK < n_main // tn
    return pl.pallas_call(
        functools.partial(_in_proj_kernel, tm=tm),
        grid=(T // tm, n_main // tn),
        in_specs=[
            pl.BlockSpec(memory_space=pl.ANY),
            pl.BlockSpec((1, D), lambda m, n: (0, 0)),
            pl.BlockSpec((D, tn), lambda m, n: (0, n)),
            pl.BlockSpec((D, n_gate), lambda m, n: (0, 0)),
        ],
        out_specs=[
            pl.BlockSpec((tm, tn), lambda m, n: (m, n)),
            pl.BlockSpec((tm, n_gate), lambda m, n: (m, 0)),
        ],
        out_shape=[
            jax.ShapeDtypeStruct((T, n_main), F32),
            jax.ShapeDtypeStruct((T, n_gate), F32),
        ],
        scratch_shapes=[pltpu.VMEM((tm, D), BF16), pltpu.VMEM((tm, D), BF16),
                        pltpu.VMEM((IN_CHUNK, D), F32), pltpu.VMEM((IN_CHUNK, D), F32),
                        pltpu.SemaphoreType.DMA((2,))],
        compiler_params=_params(("arbitrary", "arbitrary")),
        name="in_proj",
    )(xt, g1, w_main, w_gate)


def _compress_kernel(x_ref, pos_ref, w1_ref, w2_ref, kg_ref, o_ref, xs_scr, *, S, n_pad):
    kv = pl.program_id(1)
    xs_scr[0:S, :] = x_ref[...]
    xs_scr[S:S + CMP_STRIDE, :] = jnp.zeros((CMP_STRIDE, HEAD_DIM), F32)
    acc = jnp.zeros((n_pad, HEAD_DIM), F32)
    for l in range(CMP_BLOCK):
        tb = xs_scr[pl.ds(l, n_pad, stride=CMP_STRIDE), :] + pos_ref[0, l:l + 1, :]
        acc = acc + _dot(tb.astype(BF16), w1_ref[0, l].astype(BF16))
    hid = acc * jax.nn.sigmoid(acc)
    out = _dot(hid.astype(BF16), w2_ref[0].astype(BF16))
    normed = _rms(out, kg_ref[0:1, :])
    o_ref[0, 0, 0] = jnp.where(kv == 0, normed, out).astype(BF16)


def _compress(proj, cmp_pos, cmp_w1, cmp_w2, k_norm_g, B, S):
    n_pad = S // CMP_STRIDE
    G = NSA_KV_GROUPS
    kern = functools.partial(_compress_kernel, S=S, n_pad=n_pad)
    return pl.pallas_call(
        kern,
        grid=(B, 2, G),
        in_specs=[
            pl.BlockSpec((S, HEAD_DIM), lambda b, kv, g: (b, KC_BLK + 2 * kv + g)),
            pl.BlockSpec((1, CMP_BLOCK, HEAD_DIM), lambda b, kv, g: (kv, 0, 0)),
            pl.BlockSpec((1, CMP_BLOCK, HEAD_DIM, HEAD_DIM), lambda b, kv, g: (kv, 0, 0, 0)),
            pl.BlockSpec((1, HEAD_DIM, HEAD_DIM), lambda b, kv, g: (kv, 0, 0)),
            pl.BlockSpec((3, HEAD_DIM), lambda b, kv, g: (0, 0)),
        ],
        out_specs=pl.BlockSpec((1, 1, 1, n_pad, HEAD_DIM), lambda b, kv, g: (b, kv, g, 0, 0)),
        out_shape=jax.ShapeDtypeStruct((B, 2, G, n_pad, HEAD_DIM), BF16),
        scratch_shapes=[pltpu.VMEM((S + CMP_STRIDE, HEAD_DIM), F32)],
        compiler_params=_params(("parallel", "parallel", "parallel")),
        name="compress",
    )(proj, cmp_pos, cmp_w1, cmp_w2, k_norm_g)


def _nsa_kernel(q_ref, kc_ref, vc_ref, ks_ref, vs_ref, kw_ref, vw_ref, gate_ref, qg_ref, kg_ref,
                ovt_ref, ext_ref, o_ref, ksa, vsa, kwn, vwa, l_scr, acc_scr, *, S, tq, tk, wk):
    i = pl.program_id(2)
    H = NSA_GROUP_HEADS
    scale = HEAD_DIM ** -0.5
    c2 = scale * LOG2E
    n_cp = S // CMP_STRIDE
    n_sblk = S // SEL_BLOCK

    @pl.when(i == 0)
    def _():
        ksa[:, 0:HEAD_DIM] = _rms(ks_ref[...], kg_ref[1:2, :]).astype(BF16)
        ksa[:, HEAD_DIM:2 * HEAD_DIM] = ext_ref[...]
        kwn[...] = _rms(kw_ref[...], kg_ref[2:3, :]).astype(BF16)
        ones = jnp.ones((S, HEAD_DIM), BF16)
        vsa[:, 0:HEAD_DIM] = vs_ref[...].astype(BF16)
        vsa[:, HEAD_DIM:2 * HEAD_DIM] = ones
        vwa[:, 0:HEAD_DIM] = vw_ref[...].astype(BF16)
        vwa[:, HEAD_DIM:2 * HEAD_DIM] = ones

    pos = i * tq + lax.broadcasted_iota(jnp.int32, (tq, 1), 0)
    qg = qg_ref[...]
    q4 = jnp.concatenate([_rms(q_ref[:, h * HEAD_DIM:(h + 1) * HEAD_DIM], qg).astype(BF16)
                          for h in range(H)], axis=0)
    rows = [slice(h * tq, (h + 1) * tq) for h in range(H)]

    def softmax_pv(q_chains, k, v, tail_bias):
        head = k.shape[0] - tail_bias.shape[1]
        outs, sums = [], []
        for q2 in q_chains:
            s2 = _dot_nt(q2, k)
            es = []
            hpc = q2.shape[0] // tq
            for hh in range(hpc):
                t = s2[hh * tq:(hh + 1) * tq] * c2
                tb = t[:, head:] + tail_bias
                m = jnp.max(tb, axis=-1, keepdims=True)
                if head:
                    ta = t[:, :head]
                    m = jnp.maximum(m, jnp.max(ta, axis=-1, keepdims=True))
                    es.append(jnp.concatenate([jnp.exp2(ta - m).astype(BF16),
                                               jnp.exp2(tb - m).astype(BF16)], axis=1))
                else:
                    es.append(jnp.exp2(tb - m).astype(BF16))
            o2 = _dot(es[0] if hpc == 1 else jnp.concatenate(es, axis=0), v)
            outs += [o2[hh * tq:(hh + 1) * tq, 0:HEAD_DIM] for hh in range(hpc)]
            sums += [o2[hh * tq:(hh + 1) * tq, HEAD_DIM:2 * HEAD_DIM] for hh in range(hpc)]
        return outs, sums

    pos_t = i * tq + lax.broadcasted_iota(jnp.int32, (1, tq), 1)
    nrow = lax.broadcasted_iota(jnp.int32, (n_cp, 1), 0)
    cmask_t = (nrow * CMP_STRIDE + (CMP_BLOCK - 1)) <= pos_t
    s_ct = _dot_nt(kc_ref[0, 0, 0], q4) * scale
    vc = vc_ref[0, 0, 0]
    ps_t, o_cs = [], []
    for r in rows:
        sm = jnp.where(cmask_t, s_ct[:, r], NEG_INF)
        e = jnp.where(cmask_t, jnp.exp(sm - jnp.max(sm, axis=0, keepdims=True)), 0.0)
        den = jnp.sum(e, axis=0, keepdims=True)
        p = e / jnp.where(den > 0.0, den, 1.0)
        ps_t.append(p)
        o_cs.append(_dot_tn(p.astype(BF16), vc))
    psum_t = ps_t[0] + ps_t[1] + ps_t[2] + ps_t[3]

    ovt = ovt_ref[...]
    p1, p2, p3 = _split3(psum_t)
    imp_t = (_dot(ovt, p1) + _dot(ovt, p2) + _dot(ovt, p3))[0:n_sblk, :]
    jrow = lax.broadcasted_iota(jnp.int32, (n_sblk, 1), 0)
    cur_t = pos_t // SEL_BLOCK
    valid_t = jrow * SEL_BLOCK <= pos_t
    forced_t = (jrow == 0) | (jrow == cur_t) | (jrow == cur_t - 1)
    score = jnp.where(valid_t, jnp.where(forced_t, FORCED_SCORE, imp_t), -jnp.inf)
    rank = jnp.zeros((n_sblk, tq), jnp.int32)
    for i2 in range(n_sblk):
        si = score[i2:i2 + 1, :]
        ahead = (si > score) | ((si == score) & (jrow > i2))
        rank = rank + jnp.where(ahead, 1, 0)
    drop_t = jnp.where((rank < SEL_TOPK) & valid_t, 0.0, NEG_INF)
    drop_t = jnp.concatenate([drop_t, jnp.full((LANES - n_sblk, tq), NEG_INF, F32)], axis=0)
    drop = drop_t.T.astype(BF16)
    qa_pairs = [jnp.concatenate([q4[r], drop], axis=1) for r in rows]

    kst = pl.multiple_of(jnp.clip(i * tq - WINDOW, 0, S - wk), tq)
    wpos = kst + lax.broadcasted_iota(jnp.int32, (1, wk), 1)
    wbias = jnp.where((wpos <= pos) & (wpos > pos - WINDOW), 0.0, NEG_INF)
    o_ws, lws = softmax_pv([q4[r] for r in rows],
                           kwn[pl.ds(kst, wk), :], vwa[pl.ds(kst, wk), :], wbias)

    def slc_prefix(n):
        kpos = (n - tk) + lax.broadcasted_iota(jnp.int32, (1, tk), 1)
        causal = jnp.where(kpos <= pos, 0.0, NEG_INF)
        outs, sums = softmax_pv(qa_pairs, ksa[0:n, :], vsa[0:n, :], causal)
        for r, o, l in zip(rows, outs, sums):
            acc_scr[r] = o
            l_scr[r] = l

    n_cls = S // tk
    cls = ((i + 1) * tq - 1) // tk
    for c in range(n_cls):
        pl.when(cls == c)(functools.partial(slc_prefix, (c + 1) * tk))

    gates = jax.nn.sigmoid(gate_ref[...])
    for h, r in enumerate(rows):
        c = 3 * h
        o = (o_cs[h] * gates[:, c:c + 1]
             + acc_scr[r] * (gates[:, c + 1:c + 2] / l_scr[r])
             + o_ws[h] * (gates[:, c + 2:c + 3] / lws[h]))
        o_ref[:, h * HEAD_DIM:(h + 1) * HEAD_DIM] = o.astype(BF16)


def _nsa_tables(S):
    n_cp, n_cmp, n_sblk = S // CMP_STRIDE, (S - CMP_BLOCK) // CMP_STRIDE + 1, S // SEL_BLOCK
    n = np.arange(n_cp)[None, :] * CMP_STRIDE
    j = np.arange(LANES)[:, None]
    ovt = ((n < (j + 1) * SEL_BLOCK) & (n + CMP_BLOCK > j * SEL_BLOCK)
           & (np.arange(n_cp)[None, :] < n_cmp) & (j < n_sblk))
    key_block = (np.arange(S) // SEL_BLOCK)[:, None] == np.arange(LANES)[None, :]
    return jnp.asarray(ovt, BF16), jnp.asarray(key_block, BF16)


def _nsa(proj, gate, cmp_kv, q_norm_g, k_norm_g, B, S, tq, tk):
    G = NSA_KV_GROUPS
    H = NSA_GROUP_HEADS
    n_cp = S // CMP_STRIDE
    wk = min(S, WINDOW + tq)
    nq = S // tq
    gw = H * HEAD_DIM
    ovt, key_block = _nsa_tables(S)
    kern = functools.partial(_nsa_kernel, S=S, tq=tq, tk=tk, wk=wk)
    kv_spec = lambda blk: pl.BlockSpec((S, HEAD_DIM), lambda b, g, i: (b, blk + g))
    return pl.pallas_call(
        kern,
        grid=(B, G, nq),
        in_specs=[
            pl.BlockSpec((tq, gw), lambda b, g, i: (b * nq + i, Q_COL0 // gw + g)),
            pl.BlockSpec((1, 1, 1, n_cp, HEAD_DIM), lambda b, g, i: (b, 0, g, 0, 0)),
            pl.BlockSpec((1, 1, 1, n_cp, HEAD_DIM), lambda b, g, i: (b, 1, g, 0, 0)),
            kv_spec(KS_BLK), kv_spec(VS_BLK), kv_spec(KW_BLK), kv_spec(VW_BLK),
            pl.BlockSpec((tq, LANES), lambda b, g, i: (b * nq + i, g)),
            pl.BlockSpec((1, HEAD_DIM), lambda b, g, i: (0, 0)),
            pl.BlockSpec((3, HEAD_DIM), lambda b, g, i: (0, 0)),
            pl.BlockSpec((LANES, n_cp), lambda b, g, i: (0, 0)),
            pl.BlockSpec((S, LANES), lambda b, g, i: (0, 0)),
        ],
        out_specs=pl.BlockSpec((tq, gw), lambda b, g, i: (b * nq + i, g)),
        out_shape=jax.ShapeDtypeStruct((B * S, NSA_Q_WIDTH), BF16),
        scratch_shapes=[
            pltpu.VMEM((S, 2 * HEAD_DIM), BF16), pltpu.VMEM((S, 2 * HEAD_DIM), BF16),
            pltpu.VMEM((S, HEAD_DIM), BF16), pltpu.VMEM((S, 2 * HEAD_DIM), BF16),
            pltpu.VMEM((H * tq, HEAD_DIM), F32), pltpu.VMEM((H * tq, HEAD_DIM), F32)],
        compiler_params=_params(("parallel", "parallel", "arbitrary")),
        name="nsa",
    )(proj, cmp_kv, cmp_kv, proj, proj, proj, proj, gate, q_norm_g, k_norm_g, ovt, key_block)


def _ret_kernel(rq_ref, rk_ref, rv_ref, rg_ref, cos_ref, sin_ref, dec_ref, xi_ref, zeta_ref, cd_ref,
                gg_ref, gb_ref, o_ref, r_scr):
    scale = HEAD_DIM ** -0.5

    @pl.when(pl.program_id(1) == 0)
    def _():
        r_scr[...] = jnp.zeros(r_scr.shape, F32)

    cos = cos_ref[...]
    sin = sin_ref[...]
    for h in range(RET_HEADS):
        sl = slice(h * HEAD_DIM, (h + 1) * HEAD_DIM)
        q = rq_ref[:, sl]
        k = rk_ref[:, sl]
        qf = q * cos + pltpu.roll(q, HEAD_DIM // 2, 1) * sin
        kf = (k * cos + pltpu.roll(k, HEAD_DIM // 2, 1) * sin) * scale
        v = rv_ref[:, sl].astype(BF16)
        qb = qf.astype(BF16)
        r_old = r_scr[h]
        a = _dot_nt(qb, kf.astype(BF16)) * dec_ref[h]
        o = _dot(a.astype(BF16), v) + _dot(qb, r_old.astype(BF16)) * xi_ref[h]
        r_scr[h] = r_old * cd_ref[h:h + 1, :] + _dot_tn((kf * zeta_ref[h]).astype(BF16), v)
        mu = jnp.mean(o, axis=-1, keepdims=True)
        d = o - mu
        var = jnp.mean(d * d, axis=-1, keepdims=True)
        y = d * lax.rsqrt(var + GN_EPS) * gg_ref[h:h + 1, :] + gb_ref[h:h + 1, :]
        gt = rg_ref[:, sl]
        o_ref[:, sl] = (gt * jax.nn.sigmoid(gt) * y).astype(BF16)


def _ret_tables(S):
    C, H, half = RET_CHUNK, RET_HEADS, HEAD_DIM // 2
    inv_freq = ROPE_BASE ** (-jnp.arange(half, dtype=F32) / half)
    ang = jnp.arange(S, dtype=F32)[:, None] * inv_freq[None, :]
    cos2 = jnp.concatenate([jnp.cos(ang), jnp.cos(ang)], axis=1)
    sin2 = jnp.concatenate([-jnp.sin(ang), jnp.sin(ang)], axis=1)
    log_g = jnp.log1p(-jnp.exp2(-5.0 - jnp.arange(H, dtype=F32)))
    n = jnp.arange(C, dtype=F32)
    diff = n[:, None] - n[None, :]
    dec = jnp.where(diff >= 0, jnp.exp(log_g[:, None, None] * jnp.maximum(diff, 0.0)), 0.0)
    lanes = lambda t: jnp.broadcast_to(t[..., None], t.shape + (HEAD_DIM,))
    xi = lanes(jnp.exp(log_g[:, None] * (n + 1.0)))
    zeta = lanes(jnp.exp(log_g[:, None] * (C - 1.0 - n)))
    cd = lanes(jnp.exp(log_g * C))
    return cos2, sin2, dec, xi, zeta, cd


def _retention(proj, gn_g, gn_b, B, S):
    C, H = RET_CHUNK, RET_HEADS
    n_ch = S // C
    cos2, sin2, dec, xi, zeta, cd = _ret_tables(S)
    spec = lambda k: pl.BlockSpec((C, RET_WIDTH), lambda b, c: (b * n_ch + c, k))
    whole = lambda a: pl.BlockSpec(a.shape, lambda b, c: (0,) * a.ndim)
    return pl.pallas_call(
        _ret_kernel,
        grid=(B, n_ch),
        in_specs=[
            spec(0), spec(1), spec(2), spec(3),
            pl.BlockSpec((C, HEAD_DIM), lambda b, c: (c, 0)),
            pl.BlockSpec((C, HEAD_DIM), lambda b, c: (c, 0)),
            whole(dec), whole(xi), whole(zeta), whole(cd), whole(gn_g), whole(gn_b),
        ],
        out_specs=pl.BlockSpec((C, RET_WIDTH), lambda b, c: (b * n_ch + c, 0)),
        out_shape=jax.ShapeDtypeStruct((B * S, RET_WIDTH), BF16),
        scratch_shapes=[pltpu.VMEM((H, HEAD_DIM, HEAD_DIM), F32)],
        compiler_params=_params(("parallel", "arbitrary")),
        name="retention",
    )(proj, proj, proj, proj, cos2, sin2, dec, xi, zeta, cd, gn_g, gn_b)


def _out_kernel(on_ref, or_ref, w_ref, x_ref, g2_ref, wr_ref, br_ref, tri_ref,
                x1_ref, h2_ref, rt_ref, cnt_ref, cnt_scr, ws_scr):
    @pl.when(pl.program_id(0) == 0)
    def _():
        cnt_scr[...] = jnp.zeros(cnt_scr.shape, F32)
        wr = wr_ref[...]
        w_hi = wr.astype(BF16)
        ws_scr[0] = w_hi
        ws_scr[1] = (wr - w_hi.astype(F32)).astype(BF16)

    half = on_ref.shape[1]
    acc = _dot(on_ref[...], w_ref[0:half, :]) + _dot(or_ref[...], w_ref[half:2 * half, :])
    x1 = x_ref[...] + acc
    x1_ref[...] = x1
    h2 = _rms(x1, g2_ref[...])
    h2_ref[...] = h2

    h_hi = h2.astype(BF16)
    h_lo = (h2 - h_hi.astype(F32)).astype(BF16)
    logits = _dot(h_hi, ws_scr[0]) + _dot(h_lo, ws_scr[0]) + _dot(h_hi, ws_scr[1]) + br_ref[...]

    tm = logits.shape[0]
    lt = logits.T
    G8 = EXPERTS_PER_GROUP
    row = lax.broadcasted_iota(jnp.int32, (G8, 1), 0).astype(F32)
    big = float(G8)
    col_max = lambda v: jnp.max(v, axis=0, keepdims=True)
    col_min = lambda v: jnp.min(v, axis=0, keepdims=True)
    col_sum = lambda v: jnp.sum(v, axis=0, keepdims=True)
    gm = row < N_GROUPS
    gl = jnp.where(gm, lt[0:G8], -jnp.inf)
    ge = jnp.where(gm, jnp.exp(gl - col_max(gl)), 0.0)
    pg = ge / col_sum(ge)
    g_top = col_max(pg)
    g_idx = col_min(jnp.where(gm & (pg == g_top), row, big))
    el = lt[EXP_ROW0:EXP_ROW0 + G8]
    for g in range(1, N_GROUPS):
        el = jnp.where(g_idx == float(g), lt[EXP_ROW0 + g * G8:EXP_ROW0 + (g + 1) * G8], el)
    ee = jnp.exp(el - col_max(el))
    pe = ee / col_sum(ee)
    t1 = col_max(pe)
    i1 = col_min(jnp.where(pe == t1, row, big))
    rest = row != i1
    t2 = col_max(jnp.where(rest, pe, -1.0))
    i2 = col_min(jnp.where(rest & (pe == t2), row, big))
    tsum = t1 + t2
    w1 = g_top * t1 / tsum
    w2 = g_top * t2 / tsum
    e1 = g_idx * float(G8) + i1
    e2 = g_idx * float(G8) + i2
    erow = lax.broadcasted_iota(jnp.int32, (N_EXPERTS, 1), 0).astype(F32)
    oh1 = jnp.where(erow == e1, 1.0, 0.0)
    oh2 = jnp.where(erow == e2, 1.0, 0.0)
    both = oh1 + oh2
    before = _dot(both.astype(BF16), tri_ref[...]) + cnt_scr[...]
    r1 = col_sum(before * oh1)
    r2 = col_sum(before * oh2)
    cnt_scr[...] = cnt_scr[...] + jnp.sum(both, axis=1, keepdims=True)
    cnt_ref[...] = jnp.broadcast_to(cnt_scr[...], cnt_ref.shape)
    rt_ref[...] = jnp.concatenate([e1, e2, w1, w2, r1, r2, jnp.zeros((2, tm), F32)], axis=0)


def _out_proj(o_nsa, o_ret, w_out, xt, g2, w_router, b_router, tm):
    T, D = xt.shape
    half = o_nsa.shape[1]
    tri = jnp.asarray(np.triu(np.ones((tm, tm), np.float32), 1), BF16)
    return pl.pallas_call(
        _out_kernel,
        grid=(T // tm,),
        in_specs=[
            pl.BlockSpec((tm, half), lambda m: (m, 0)),
            pl.BlockSpec((tm, half), lambda m: (m, 0)),
            pl.BlockSpec((2 * half, D), lambda m: (0, 0), pipeline_mode=pl.Buffered(1)),
            pl.BlockSpec((tm, D), lambda m: (m, 0)),
            pl.BlockSpec((1, D), lambda m: (0, 0)),
            pl.BlockSpec((D, LANES), lambda m: (0, 0)),
            pl.BlockSpec((1, LANES), lambda m: (0, 0)),
            pl.BlockSpec((tm, tm), lambda m: (0, 0)),
        ],
        out_specs=[
            pl.BlockSpec((tm, D), lambda m: (m, 0)),
            pl.BlockSpec((tm, D), lambda m: (m, 0)),
            pl.BlockSpec((8, tm), lambda m: (0, m)),
            pl.BlockSpec((N_EXPERTS, LANES), lambda m: (0, 0)),
        ],
        out_shape=[
            jax.ShapeDtypeStruct((T, D), F32),
            jax.ShapeDtypeStruct((T, D), F32),
            jax.ShapeDtypeStruct((8, T), F32),
            jax.ShapeDtypeStruct((N_EXPERTS, LANES), F32),
        ],
        scratch_shapes=[pltpu.VMEM((N_EXPERTS, 1), F32), pltpu.VMEM((2, D, LANES), BF16)],
        compiler_params=_params(("arbitrary",)),
        name="out_proj",
    )(o_nsa, o_ret, w_out, xt, g2, w_router, b_router, tri)


def _row_copy(src, row, dst, slot, sem):
    return pltpu.make_async_copy(src.at[pl.ds(row, 1), :], dst.at[pl.ds(slot, 1), :], sem)


def _dispatch_kernel(pos_ref, pad_ref, h2_ref, xs_hbm, zrow, sem, zsem, *, n_tok):
    i = pl.program_id(0)
    tm = h2_ref.shape[0]
    K = TOPK_IN_GROUP

    def issue(r, carry):
        for k in range(K):
            _row_copy(h2_ref, r, xs_hbm, pos_ref[k * n_tok + i * tm + r], sem).start()
        return carry

    lax.fori_loop(0, tm, issue, 0, unroll=8)

    @pl.when(i == pl.num_programs(0) - 1)
    def _():
        zrow[...] = jnp.zeros(zrow.shape, F32)
        for e in range(N_EXPERTS):
            lo, hi = pad_ref[e], pad_ref[N_EXPERTS + e]
            mid = jnp.minimum(lax.shift_left(lax.shift_right_logical(lo + 7, 3), 3), hi)

            def fill(r, carry):
                _row_copy(zrow, 0, xs_hbm, r, zsem).start()
                return carry

            lax.fori_loop(lo, mid, fill, 0)

            def fill_wait(r, carry):
                _row_copy(zrow, 0, xs_hbm, r, zsem).wait()
                return carry

            lax.fori_loop(lo, mid, fill_wait, 0)

            def piece(j):
                r8 = pl.multiple_of(mid + j * 8, 8)
                return pltpu.make_async_copy(zrow.at[pl.ds(0, 8), :], xs_hbm.at[pl.ds(r8, 8), :], zsem)

            n8 = lax.shift_right_logical(hi - mid, 3)

            def fill8(j, carry):
                piece(j).start()
                return carry

            lax.fori_loop(0, n8, fill8, 0)

            def fill8_wait(j, carry):
                piece(j).wait()
                return carry

            lax.fori_loop(0, n8, fill8_wait, 0)

        M = zrow.shape[0]

        def tail_copy(t):
            return pltpu.make_async_copy(zrow, xs_hbm.at[pl.ds(pl.multiple_of(t * M, M), M), :], zsem)

        def tail(t, carry):
            tail_copy(t).start()
            return carry

        lax.fori_loop(pad_ref[2 * N_EXPERTS], xs_hbm.shape[0] // M, tail, 0)

        def tail_wait(t, carry):
            tail_copy(t).wait()
            return carry

        lax.fori_loop(pad_ref[2 * N_EXPERTS], xs_hbm.shape[0] // M, tail_wait, 0)

    for k in range(K):
        pltpu.make_async_copy(h2_ref, xs_hbm.at[pl.ds(0, tm), :], sem).wait()


def _dispatch(pos, pad_rows, h2, n_rows, tm):
    T, D = h2.shape
    grid_spec = pltpu.PrefetchScalarGridSpec(
        num_scalar_prefetch=2,
        grid=(T // tm,),
        in_specs=[pl.BlockSpec((tm, D), lambda i, pos, pad: (i, 0))],
        out_specs=pl.BlockSpec(memory_space=pl.ANY),
        scratch_shapes=[pltpu.VMEM((MOE_BLOCK, D), F32), pltpu.SemaphoreType.DMA, pltpu.SemaphoreType.DMA],
    )
    return pl.pallas_call(
        functools.partial(_dispatch_kernel, n_tok=T),
        grid_spec=grid_spec,
        out_shape=jax.ShapeDtypeStruct((n_rows, D), F32),
        compiler_params=_params(("arbitrary",)),
        name="dispatch",
    )(pos, pad_rows, h2)


def _expert_kernel(ts_ref, xs_hbm, wg_ref, wu_ref, wd_ref, y_hbm,
                   xb0, xb1, yb0, yb1, wg_b, wu_b, wd_b, gsem, osem, *, n_blocks):
    e = pl.program_id(0)
    M = MOE_BLOCK
    t0 = ts_ref[e]
    t1 = ts_ref[e + 1]
    n_used = ts_ref[N_EXPERTS]
    xbufs = (xb0, xb1)
    ybufs = (yb0, yb1)

    def rows(t):
        return pl.ds(pl.multiple_of(t * M, M), M)

    def in_copy(t, p):
        return pltpu.make_async_copy(xs_hbm.at[rows(t), :], xbufs[p], gsem.at[p])

    def out_copy(t, p):
        return pltpu.make_async_copy(ybufs[p], y_hbm.at[rows(t), :], osem.at[p])

    @pl.when(e == 0)
    def _():
        in_copy(0, 0).start()

    @pl.when(t1 > t0)
    def _():
        wg_b[...] = wg_ref[0].astype(BF16)
        wu_b[...] = wu_ref[0].astype(BF16)
        wd_b[...] = wd_ref[0].astype(BF16)

    def tile_body(t, p):
        in_copy(t, p).wait()

        @pl.when(t + 1 < n_used)
        def _():
            in_copy(t + 1, 1 - p).start()

        @pl.when(t >= 2)
        def _():
            out_copy(t - 2, p).wait()

        xb = xbufs[p][...].astype(BF16)
        hg = _dot(xb, wg_b[...])
        hu = _dot(xb, wu_b[...])
        hb = (hg * jax.nn.sigmoid(hg) * hu).astype(BF16)
        ybufs[p][...] = _dot(hb, wd_b[...])
        out_copy(t, p).start()

    def tile(t, carry):
        parity = lax.rem(t, 2)
        for p in range(2):
            pl.when(parity == p)(functools.partial(tile_body, t, p))
        return carry

    lax.fori_loop(t0, t1, tile, 0)

    @pl.when(e == N_EXPERTS - 1)
    def _():
        parity = lax.rem(n_used, 2)
        for p in range(2):
            @pl.when(parity == p)
            def _(p=p):
                out_copy(n_used - 1, 1 - p).wait()

                @pl.when(n_used >= 2)
                def _():
                    out_copy(n_used - 2, p).wait()

        yb0[...] = jnp.zeros(yb0.shape, F32)

        def clear(t, carry):
            out_copy(t, 0).start()
            return carry

        lax.fori_loop(n_used, n_blocks, clear, 0)

        def clear_wait(t, carry):
            out_copy(t, 0).wait()
            return carry

        lax.fori_loop(n_used, n_blocks, clear_wait, 0)


def _experts(xs, tile_start, w_gate, w_up, w_down):
    n_rows, D = xs.shape
    n_blocks = n_rows // MOE_BLOCK
    grid_spec = pltpu.PrefetchScalarGridSpec(
        num_scalar_prefetch=1,
        grid=(N_EXPERTS,),
        in_specs=[
            pl.BlockSpec(memory_space=pl.ANY),
            pl.BlockSpec((1, D, D_EXPERT), lambda e, ts: (e, 0, 0)),
            pl.BlockSpec((1, D, D_EXPERT), lambda e, ts: (e, 0, 0)),
            pl.BlockSpec((1, D_EXPERT, D), lambda e, ts: (e, 0, 0)),
        ],
        out_specs=pl.BlockSpec(memory_space=pl.ANY),
        scratch_shapes=[pltpu.VMEM((MOE_BLOCK, D), F32)] * 4 + [
            pltpu.VMEM((D, D_EXPERT), BF16), pltpu.VMEM((D, D_EXPERT), BF16),
            pltpu.VMEM((D_EXPERT, D), BF16),
            pltpu.SemaphoreType.DMA((2,)), pltpu.SemaphoreType.DMA((2,)),
        ],
    )
    return pl.pallas_call(
        functools.partial(_expert_kernel, n_blocks=n_blocks),
        grid_spec=grid_spec,
        out_shape=jax.ShapeDtypeStruct((n_rows, D), F32),
        compiler_params=_params(("arbitrary",)),
        name="experts",
    )(tile_start, xs, w_gate, w_up, w_down)


def _combine_kernel(pos_ref, y_hbm, x1_ref, rt_ref, o_ref, ybuf, sems, *, n_tok):
    i = pl.program_id(0)
    n = pl.num_programs(0)
    tm = x1_ref.shape[0]
    K = TOPK_IN_GROUP
    slot = lax.rem(i, 2)

    def issue_tile(t, s):
        def issue(r, carry):
            for k in range(K):
                _row_copy(y_hbm, pos_ref[k * n_tok + t * tm + r], ybuf.at[s, k], r,
                          sems.at[s]).start(priority=k)
            return carry

        lax.fori_loop(0, tm, issue, 0, unroll=8)

    @pl.when(i == 0)
    def _():
        issue_tile(0, 0)

    @pl.when(i + 1 < n)
    def _():
        issue_tile(i + 1, 1 - slot)

    for k in range(K):
        pltpu.make_async_copy(y_hbm.at[pl.ds(0, tm), :], ybuf.at[slot, k], sems.at[slot]).wait()
    w = rt_ref[...].T
    yb = ybuf[slot]
    o_ref[...] = x1_ref[...] + (yb[0] * w[:, K:K + 1] + yb[1] * w[:, K + 1:K + 2])


def _combine(pos, y_rows, x1, rt, tm):
    T, D = x1.shape
    grid_spec = pltpu.PrefetchScalarGridSpec(
        num_scalar_prefetch=1,
        grid=(T // tm,),
        in_specs=[
            pl.BlockSpec(memory_space=pl.ANY),
            pl.BlockSpec((tm, D), lambda i, pos: (i, 0)),
            pl.BlockSpec((8, tm), lambda i, pos: (0, i)),
        ],
        out_specs=pl.BlockSpec((tm, D), lambda i, pos: (i, 0)),
        scratch_shapes=[pltpu.VMEM((2, TOPK_IN_GROUP, tm, D), F32), pltpu.SemaphoreType.DMA((2,))],
    )
    return pl.pallas_call(
        functools.partial(_combine_kernel, n_tok=T),
        grid_spec=grid_spec,
        out_shape=jax.ShapeDtypeStruct((T, D), F32),
        compiler_params=_params(("arbitrary",)),
        name="combine",
    )(pos, y_rows, x1, rt)


def _block_layout(rt, counts, T):
    K = TOPK_IN_GROUP
    n_rows = (K * T + N_EXPERTS * (MOE_BLOCK - 1) + MOE_BLOCK - 1) // MOE_BLOCK * MOE_BLOCK
    n_blocks = n_rows // MOE_BLOCK
    cnt = counts[:, 0].astype(jnp.int32)
    padded = (cnt + MOE_BLOCK - 1) // MOE_BLOCK * MOE_BLOCK
    pad_end = jnp.cumsum(padded)
    pad_start = pad_end - padded
    e = rt[0:K].astype(jnp.int32).reshape(-1)
    rank = rt[2 * K:3 * K].astype(jnp.int32).reshape(-1)
    hit = e[:, None] == jnp.arange(N_EXPERTS, dtype=jnp.int32)[None, :]
    pos = jnp.sum(jnp.where(hit, pad_start[None, :].astype(jnp.int32), 0), axis=1) + rank
    tile_start = (jnp.concatenate([pad_start, pad_end[-1:]]) // MOE_BLOCK).astype(jnp.int32)
    pad_rows = jnp.concatenate([pad_start + cnt, pad_end, pad_end[-1:] // MOE_BLOCK]).astype(jnp.int32)
    return tile_start, pos, pad_rows, n_rows


def _layer(x, norm1_g, w_in, cmp_pos, cmp_w1, cmp_w2, q_norm_g, k_norm_g, ret_gn_g, ret_gn_b,
           w_out, norm2_g, w_rg, b_rg, w_re, b_re, w_eg, w_eu, w_ed, tiles):
    B, S, D = x.shape
    T = B * S
    xt = x.reshape(T, D)

    w_main = jnp.concatenate([w_in[:, GATE_COL0 + NSA_GATE_WIDTH:], w_in[:, :GATE_COL0]], axis=1).astype(BF16)
    gpg = NSA_GATE_WIDTH // NSA_KV_GROUPS
    w_gate = jnp.concatenate(
        [jnp.pad(w_in[:, GATE_COL0 + g * gpg:GATE_COL0 + (g + 1) * gpg], ((0, 0), (0, LANES - gpg)))
         for g in range(NSA_KV_GROUPS)], axis=1).astype(BF16)
    proj, gate = _in_proj(xt, norm1_g.reshape(1, D), w_main, w_gate, tiles["tm_in"], tiles["tn_in"])

    cmp_kv = _compress(proj, cmp_pos, cmp_w1, cmp_w2, k_norm_g, B, S)
    o_nsa = _nsa(proj, gate, cmp_kv, q_norm_g.reshape(1, HEAD_DIM), k_norm_g, B, S, tiles["tq"], tiles["tk"])

    o_ret = _retention(proj, ret_gn_g, ret_gn_b, B, S)

    gap, tail = EXP_ROW0 - N_GROUPS, LANES - EXP_ROW0 - N_EXPERTS
    w_router = jnp.concatenate([w_rg, jnp.zeros((D, gap), F32), w_re, jnp.zeros((D, tail), F32)], axis=1)
    b_router = jnp.concatenate([b_rg, jnp.zeros((gap,), F32), b_re, jnp.zeros((tail,), F32)]).reshape(1, LANES)
    x1, h2, rt, counts = _out_proj(o_nsa, o_ret, w_out.astype(BF16), xt, norm2_g.reshape(1, D),
                                   w_router, b_router, tiles["tm_out"])

    tile_start, pos, pad_rows, n_rows = _block_layout(rt, counts, T)
    xs = _dispatch(pos, pad_rows, h2, n_rows, tiles["tm_dsp"])
    y_rows = _experts(xs, tile_start, w_eg, w_eu, w_ed)
    out = _combine(pos, y_rows, x1, rt, tiles["tm_cmb"])
    return out.reshape(B, S, D)


def _tiles(T, S):
    return {
        "tm_in": min(2048, T), "tn_in": 512,
        "tq": min(256, S), "tk": min(256, S),
        "tm_out": min(512, T),
        "tm_dsp": min(1024, T),
        "tm_cmb": min(256, T),
    }


def kernel(x, norm1_g, w_in, cmp_pos, cmp_w1, cmp_w2, q_norm_g, k_norm_g, ret_gn_g, ret_gn_b, w_out, norm2_g, w_router_group, b_router_group, w_router_expert, b_router_expert, w_exp_gate, w_exp_up, w_exp_down):
    B, S, _ = x.shape
    tiles = _tiles(B * S, S)
    for l in range(norm1_g.shape[0]):
        x = _layer(x, norm1_g[l], w_in[l], cmp_pos[l], cmp_w1[l], cmp_w2[l], q_norm_g[l], k_norm_g[l],
                   ret_gn_g[l], ret_gn_b[l], w_out[l], norm2_g[l], w_router_group[l], b_router_group[l],
                   w_router_expert[l], b_router_expert[l], w_exp_gate[l], w_exp_up[l], w_exp_down[l], tiles)
    return x
```

```python
import functools

import numpy as np
import jax
import jax.numpy as jnp
from jax import lax
from jax.experimental import pallas as pl
from jax.experimental.pallas import tpu as pltpu

F32 = jnp.float32
BF16 = jnp.bfloat16

D_MODEL = 2048
NSA_HEADS = 8
NSA_KV_GROUPS = 2
NSA_GROUP_HEADS = NSA_HEADS // NSA_KV_GROUPS
HEAD_DIM = 128
RET_HEADS = 8
CMP_BLOCK = 32
CMP_STRIDE = 16
SEL_BLOCK = 64
SEL_TOPK = 8
WINDOW = 512
RET_CHUNK = 128
ROPE_BASE = 10000.0
N_GROUPS = 4
EXPERTS_PER_GROUP = 8
N_EXPERTS = N_GROUPS * EXPERTS_PER_GROUP
TOPK_IN_GROUP = 2
D_EXPERT = 512
MOE_BLOCK = 256
RMS_EPS = 1e-6
GN_EPS = 1e-5
NEG_INF = -1e30
FORCED_SCORE = 1e6
EXP_ROW0 = 8
LOG2E = 1.4426950408889634

NSA_Q_WIDTH = NSA_HEADS * HEAD_DIM
NSA_KV_WIDTH = NSA_KV_GROUPS * HEAD_DIM
NSA_GATE_WIDTH = 3 * NSA_HEADS
RET_WIDTH = RET_HEADS * HEAD_DIM
GATE_COL0 = NSA_Q_WIDTH + 6 * NSA_KV_WIDTH
LANES = 128
VMEM_LIMIT = 56 * 1024 * 1024

Q_COL0 = 4 * RET_WIDTH
KV_COL0 = Q_COL0 + NSA_Q_WIDTH
KC_BLK, VC_BLK, KS_BLK, VS_BLK, KW_BLK, VW_BLK = [KV_COL0 // LANES + 2 * t for t in range(6)]


def _rms(xf, g):
    return xf * lax.rsqrt(jnp.mean(xf * xf, axis=-1, keepdims=True) + RMS_EPS) * g


def _dot(a, b):
    return jnp.dot(a, b, preferred_element_type=F32)


def _dot_nt(a, b):
    return lax.dot_general(a, b, (((1,), (1,)), ((), ())), preferred_element_type=F32)


def _dot_tn(a, b):
    return lax.dot_general(a, b, (((0,), (0,)), ((), ())), preferred_element_type=F32)


def _split3(p):
    p1 = p.astype(BF16)
    r1 = p - p1.astype(F32)
    p2 = r1.astype(BF16)
    p3 = (r1 - p2.astype(F32)).astype(BF16)
    return p1, p2, p3


def _params(sem):
    return pltpu.CompilerParams(dimension_semantics=sem, vmem_limit_bytes=VMEM_LIMIT)


IN_CHUNK = 256


def _in_proj_kernel(x_hbm, g_ref, w_ref, wg_ref, o_ref, og_ref, h0, h1, xc0, xc1, sem, *, tm):
    m = pl.program_id(0)
    n = pl.program_id(1)
    ck = IN_CHUNK
    nc = tm // ck
    hs = (h0, h1)
    xcs = (xc0, xc1)
    g = g_ref[...]

    def chunk_copy(tile, c, slot):
        row0 = pl.multiple_of(tile * tm + c * ck, ck)
        return pltpu.make_async_copy(x_hbm.at[pl.ds(row0, ck), :], xcs[slot], sem.at[slot])

    def normalise(slot, dst, c):
        dst[pl.ds(pl.multiple_of(c * ck, ck), ck), :] = _rms(xcs[slot][...], g).astype(BF16)

    @pl.when((m == 0) & (n == 0))
    def _():
        chunk_copy(0, 0, 0).start()
        for c in range(nc):
            if c + 1 < nc:
                chunk_copy(0, c + 1, (c + 1) % 2).start()
            chunk_copy(0, c, c % 2).wait()
            normalise(c % 2, h0, c)

    has_next = m + 1 < pl.num_programs(0)
    for slot in range(2):
        @pl.when(has_next & (n >= 1) & (n <= nc) & (lax.rem(n - 1, 2) == slot))
        def _(slot=slot):
            chunk_copy(m + 1, n - 1, slot).wait()

        @pl.when(has_next & (n < nc) & (lax.rem(n, 2) == slot))
        def _(slot=slot):
            chunk_copy(m + 1, n, slot).start()

    c_prev = jnp.where(n == 0, nc - 1, jnp.clip(n - 1, 0, nc - 1))
    even_chunk = lax.rem(c_prev, 2) == 0
    for cur in range(2):
        for slot in range(2):
            @pl.when((lax.rem(m, 2) == cur) & (even_chunk == (slot == 0)))
            def _(cur=cur, slot=slot):
                @pl.when(n == 0)
                def _():
                    og_ref[...] = _dot(hs[cur][...], wg_ref[...])

                normalise(slot, hs[1 - cur], c_prev)
                o_ref[...] = _dot(hs[cur][...], w_ref[...])


def _in_proj(xt, g1, w_main, w_gate, tm, tn):
    T, D = xt.shape
    n_main = w_main.shape[1]
    n_gate = w_gate.shape[1]
    assert tm % (2 * IN_CHUNK) == 0 and tm // IN_CHUNK < n_main // tn
    return pl.pallas_call(
        functools.partial(_in_proj_kernel, tm=tm),
        grid=(T // tm, n_main // tn),
        in_specs=[
            pl.BlockSpec(memory_space=pl.ANY),
            pl.BlockSpec((1, D), lambda m, n: (0, 0)),
            pl.BlockSpec((D, tn), lambda m, n: (0, n)),
            pl.BlockSpec((D, n_gate), lambda m, n: (0, 0)),
        ],
        out_specs=[
            pl.BlockSpec((tm, tn), lambda m, n: (m, n)),
            pl.BlockSpec((tm, n_gate), lambda m, n: (m, 0)),
        ],
        out_shape=[
            jax.ShapeDtypeStruct((T, n_main), F32),
            jax.ShapeDtypeStruct((T, n_gate), F32),
        ],
        scratch_shapes=[pltpu.VMEM((tm, D), BF16), pltpu.VMEM((tm, D), BF16),
                        pltpu.VMEM((IN_CHUNK, D), F32), pltpu.VMEM((IN_CHUNK, D), F32),
                        pltpu.SemaphoreType.DMA((2,))],
        compiler_params=_params(("arbitrary", "arbitrary")),
        name="in_proj",
    )(xt, g1, w_main, w_gate)


def _compress_kernel(x_ref, pos_ref, w1_ref, w2_ref, kg_ref, o_ref, xs_scr, *, S, n_pad):
    kv = pl.program_id(1)
    xs_scr[0:S, :] = x_ref[...]
    xs_scr[S:S + CMP_STRIDE, :] = jnp.zeros((CMP_STRIDE, HEAD_DIM), F32)
    acc = jnp.zeros((n_pad, HEAD_DIM), F32)
    for l in range(CMP_BLOCK):
        tb = xs_scr[pl.ds(l, n_pad, stride=CMP_STRIDE), :] + pos_ref[0, l:l + 1, :]
        acc = acc + _dot(tb.astype(BF16), w1_ref[0, l].astype(BF16))
    hid = acc * jax.nn.sigmoid(acc)
    out = _dot(hid.astype(BF16), w2_ref[0].astype(BF16))
    normed = _rms(out, kg_ref[0:1, :])
    o_ref[0, 0, 0] = jnp.where(kv == 0, normed, out).astype(BF16)


def _compress(proj, cmp_pos, cmp_w1, cmp_w2, k_norm_g, B, S):
    n_pad = S // CMP_STRIDE
    G = NSA_KV_GROUPS
    kern = functools.partial(_compress_kernel, S=S, n_pad=n_pad)
    return pl.pallas_call(
        kern,
        grid=(B, 2, G),
        in_specs=[
            pl.BlockSpec((S, HEAD_DIM), lambda b, kv, g: (b, KC_BLK + 2 * kv + g)),
            pl.BlockSpec((1, CMP_BLOCK, HEAD_DIM), lambda b, kv, g: (kv, 0, 0)),
            pl.BlockSpec((1, CMP_BLOCK, HEAD_DIM, HEAD_DIM), lambda b, kv, g: (kv, 0, 0, 0)),
            pl.BlockSpec((1, HEAD_DIM, HEAD_DIM), lambda b, kv, g: (kv, 0, 0)),
            pl.BlockSpec((3, HEAD_DIM), lambda b, kv, g: (0, 0)),
        ],
        out_specs=pl.BlockSpec((1, 1, 1, n_pad, HEAD_DIM), lambda b, kv, g: (b, kv, g, 0, 0)),
        out_shape=jax.ShapeDtypeStruct((B, 2, G, n_pad, HEAD_DIM), BF16),
        scratch_shapes=[pltpu.VMEM((S + CMP_STRIDE, HEAD_DIM), F32)],
        compiler_params=_params(("parallel", "parallel", "parallel")),
        name="compress",
    )(proj, cmp_pos, cmp_w1, cmp_w2, k_norm_g)


def _nsa_kernel(q_ref, kc_ref, vc_ref, ks_ref, vs_ref, kw_ref, vw_ref, gate_ref, qg_ref, kg_ref,
                ovt_ref, ext_ref, o_ref, ksa, vsa, kwn, vwa, l_scr, acc_scr, *, S, tq, tk, wk):
    i = pl.program_id(2)
    H = NSA_GROUP_HEADS
    scale = HEAD_DIM ** -0.5
    c2 = scale * LOG2E
    n_cp = S // CMP_STRIDE
    n_sblk = S // SEL_BLOCK

    @pl.when(i == 0)
    def _():
        ksa[:, 0:HEAD_DIM] = _rms(ks_ref[...], kg_ref[1:2, :]).astype(BF16)
        ksa[:, HEAD_DIM:2 * HEAD_DIM] = ext_ref[...]
        kwn[...] = _rms(kw_ref[...], kg_ref[2:3, :]).astype(BF16)
        ones = jnp.ones((S, HEAD_DIM), BF16)
        vsa[:, 0:HEAD_DIM] = vs_ref[...].astype(BF16)
        vsa[:, HEAD_DIM:2 * HEAD_DIM] = ones
        vwa[:, 0:HEAD_DIM] = vw_ref[...].astype(BF16)
        vwa[:, HEAD_DIM:2 * HEAD_DIM] = ones

    pos = i * tq + lax.broadcasted_iota(jnp.int32, (tq, 1), 0)
    qg = qg_ref[...]
    q4 = jnp.concatenate([_rms(q_ref[:, h * HEAD_DIM:(h + 1) * HEAD_DIM], qg).astype(BF16)
                          for h in range(H)], axis=0)
    rows = [slice(h * tq, (h + 1) * tq) for h in range(H)]

    def softmax_pv(q_chains, k, v, tail_bias):
        head = k.shape[0] - tail_bias.shape[1]
        outs, sums = [], []
        for q2 in q_chains:
            s2 = _dot_nt(q2, k)
            es = []
            hpc = q2.shape[0] // tq
            for hh in range(hpc):
                t = s2[hh * tq:(hh + 1) * tq] * c2
                tb = t[:, head:] + tail_bias
                m = jnp.max(tb, axis=-1, keepdims=True)
                if head:
                    ta = t[:, :head]
                    m = jnp.maximum(m, jnp.max(ta, axis=-1, keepdims=True))
                    es.append(jnp.concatenate([jnp.exp2(ta - m).astype(BF16),
                                               jnp.exp2(tb - m).astype(BF16)], axis=1))
                else:
                    es.append(jnp.exp2(tb - m).astype(BF16))
            o2 = _dot(es[0] if hpc == 1 else jnp.concatenate(es, axis=0), v)
            outs += [o2[hh * tq:(hh + 1) * tq, 0:HEAD_DIM] for hh in range(hpc)]
            sums += [o2[hh * tq:(hh + 1) * tq, HEAD_DIM:2 * HEAD_DIM] for hh in range(hpc)]
        return outs, sums

    pos_t = i * tq + lax.broadcasted_iota(jnp.int32, (1, tq), 1)
    nrow = lax.broadcasted_iota(jnp.int32, (n_cp, 1), 0)
    cmask_t = (nrow * CMP_STRIDE + (CMP_BLOCK - 1)) <= pos_t
    s_ct = _dot_nt(kc_ref[0, 0, 0], q4) * scale
    vc = vc_ref[0, 0, 0]
    ps_t, o_cs = [], []
    for r in rows:
        sm = jnp.where(cmask_t, s_ct[:, r], NEG_INF)
        e = jnp.where(cmask_t, jnp.exp(sm - jnp.max(sm, axis=0, keepdims=True)), 0.0)
        den = jnp.sum(e, axis=0, keepdims=True)
        p = e / jnp.where(den > 0.0, den, 1.0)
        ps_t.append(p)
        o_cs.append(_dot_tn(p.astype(BF16), vc))
    psum_t = ps_t[0] + ps_t[1] + ps_t[2] + ps_t[3]

    ovt = ovt_ref[...]
    p1, p2, p3 = _split3(psum_t)
    imp_t = (_dot(ovt, p1) + _dot(ovt, p2) + _dot(ovt, p3))[0:n_sblk, :]
    jrow = lax.broadcasted_iota(jnp.int32, (n_sblk, 1), 0)
    cur_t = pos_t // SEL_BLOCK
    valid_t = jrow * SEL_BLOCK <= pos_t
    forced_t = (jrow == 0) | (jrow == cur_t) | (jrow == cur_t - 1)
    score = jnp.where(valid_t, jnp.where(forced_t, FORCED_SCORE, imp_t), -jnp.inf)
    rank = jnp.zeros((n_sblk, tq), jnp.int32)
    for i2 in range(n_sblk):
        si = score[i2:i2 + 1, :]
        ahead = (si > score) | ((si == score) & (jrow > i2))
        rank = rank + jnp.where(ahead, 1, 0)
    drop_t = jnp.where((rank < SEL_TOPK) & valid_t, 0.0, NEG_INF)
    drop_t = jnp.concatenate([drop_t, jnp.full((LANES - n_sblk, tq), NEG_INF, F32)], axis=0)
    drop = drop_t.T.astype(BF16)
    qa_pairs = [jnp.concatenate([q4[r], drop], axis=1) for r in rows]

    kst = pl.multiple_of(jnp.clip(i * tq - WINDOW, 0, S - wk), tq)
    wpos = kst + lax.broadcasted_iota(jnp.int32, (1, wk), 1)
    wbias = jnp.where((wpos <= pos) & (wpos > pos - WINDOW), 0.0, NEG_INF)
    o_ws, lws = softmax_pv([q4[r] for r in rows],
                           kwn[pl.ds(kst, wk), :], vwa[pl.ds(kst, wk), :], wbias)

    def slc_prefix(n):
        kpos = (n - tk) + lax.broadcasted_iota(jnp.int32, (1, tk), 1)
        causal = jnp.where(kpos <= pos, 0.0, NEG_INF)
        outs, sums = softmax_pv(qa_pairs, ksa[0:n, :], vsa[0:n, :], causal)
        for r, o, l in zip(rows, outs, sums):
            acc_scr[r] = o
            l_scr[r] = l

    n_cls = S // tk
    cls = ((i + 1) * tq - 1) // tk
    for c in range(n_cls):
        pl.when(cls == c)(functools.partial(slc_prefix, (c + 1) * tk))

    gates = jax.nn.sigmoid(gate_ref[...])
    for h, r in enumerate(rows):
        c = 3 * h
        o = (o_cs[h] * gates[:, c:c + 1]
             + acc_scr[r] * (gates[:, c + 1:c + 2] / l_scr[r])
             + o_ws[h] * (gates[:, c + 2:c + 3] / lws[h]))
        o_ref[:, h * HEAD_DIM:(h + 1) * HEAD_DIM] = o.astype(BF16)


def _nsa_tables(S):
    n_cp, n_cmp, n_sblk = S // CMP_STRIDE, (S - CMP_BLOCK) // CMP_STRIDE + 1, S // SEL_BLOCK
    n = np.arange(n_cp)[None, :] * CMP_STRIDE
    j = np.arange(LANES)[:, None]
    ovt = ((n < (j + 1) * SEL_BLOCK) & (n + CMP_BLOCK > j * SEL_BLOCK)
           & (np.arange(n_cp)[None, :] < n_cmp) & (j < n_sblk))
    key_block = (np.arange(S) // SEL_BLOCK)[:, None] == np.arange(LANES)[None, :]
    return jnp.asarray(ovt, BF16), jnp.asarray(key_block, BF16)


def _nsa(proj, gate, cmp_kv, q_norm_g, k_norm_g, B, S, tq, tk):
    G = NSA_KV_GROUPS
    H = NSA_GROUP_HEADS
    n_cp = S // CMP_STRIDE
    wk = min(S, WINDOW + tq)
    nq = S // tq
    gw = H * HEAD_DIM
    ovt, key_block = _nsa_tables(S)
    kern = functools.partial(_nsa_kernel, S=S, tq=tq, tk=tk, wk=wk)
    kv_spec = lambda blk: pl.BlockSpec((S, HEAD_DIM), lambda b, g, i: (b, blk + g))
    return pl.pallas_call(
        kern,
        grid=(B, G, nq),
        in_specs=[
            pl.BlockSpec((tq, gw), lambda b, g, i: (b * nq + i, Q_COL0 // gw + g)),
            pl.BlockSpec((1, 1, 1, n_cp, HEAD_DIM), lambda b, g, i: (b, 0, g, 0, 0)),
            pl.BlockSpec((1, 1, 1, n_cp, HEAD_DIM), lambda b, g, i: (b, 1, g, 0, 0)),
            kv_spec(KS_BLK), kv_spec(VS_BLK), kv_spec(KW_BLK), kv_spec(VW_BLK),
            pl.BlockSpec((tq, LANES), lambda b, g, i: (b * nq + i, g)),
            pl.BlockSpec((1, HEAD_DIM), lambda b, g, i: (0, 0)),
            pl.BlockSpec((3, HEAD_DIM), lambda b, g, i: (0, 0)),
            pl.BlockSpec((LANES, n_cp), lambda b, g, i: (0, 0)),
            pl.BlockSpec((S, LANES), lambda b, g, i: (0, 0)),
        ],
        out_specs=pl.BlockSpec((tq, gw), lambda b, g, i: (b * nq + i, g)),
        out_shape=jax.ShapeDtypeStruct((B * S, NSA_Q_WIDTH), BF16),
        scratch_shapes=[
            pltpu.VMEM((S, 2 * HEAD_DIM), BF16), pltpu.VMEM((S, 2 * HEAD_DIM), BF16),
            pltpu.VMEM((S, HEAD_DIM), BF16), pltpu.VMEM((S, 2 * HEAD_DIM), BF16),
            pltpu.VMEM((H * tq, HEAD_DIM), F32), pltpu.VMEM((H * tq, HEAD_DIM), F32)],
        compiler_params=_params(("parallel", "parallel", "arbitrary")),
        name="nsa",
    )(proj, cmp_kv, cmp_kv, proj, proj, proj, proj, gate, q_norm_g, k_norm_g, ovt, key_block)


def _ret_kernel(rq_ref, rk_ref, rv_ref, rg_ref, cos_ref, sin_ref, dec_ref, xi_ref, zeta_ref, cd_ref,
                gg_ref, gb_ref, o_ref, r_scr):
    scale = HEAD_DIM ** -0.5

    @pl.when(pl.program_id(1) == 0)
    def _():
        r_scr[...] = jnp.zeros(r_scr.shape, F32)

    cos = cos_ref[...]
    sin = sin_ref[...]
    for h in range(RET_HEADS):
        sl = slice(h * HEAD_DIM, (h + 1) * HEAD_DIM)
        q = rq_ref[:, sl]
        k = rk_ref[:, sl]
        qf = q * cos + pltpu.roll(q, HEAD_DIM // 2, 1) * sin
        kf = (k * cos + pltpu.roll(k, HEAD_DIM // 2, 1) * sin) * scale
        v = rv_ref[:, sl].astype(BF16)
        qb = qf.astype(BF16)
        r_old = r_scr[h]
        a = _dot_nt(qb, kf.astype(BF16)) * dec_ref[h]
        o = _dot(a.astype(BF16), v) + _dot(qb, r_old.astype(BF16)) * xi_ref[h]
        r_scr[h] = r_old * cd_ref[h:h + 1, :] + _dot_tn((kf * zeta_ref[h]).astype(BF16), v)
        mu = jnp.mean(o, axis=-1, keepdims=True)
        d = o - mu
        var = jnp.mean(d * d, axis=-1, keepdims=True)
        y = d * lax.rsqrt(var + GN_EPS) * gg_ref[h:h + 1, :] + gb_ref[h:h + 1, :]
        gt = rg_ref[:, sl]
        o_ref[:, sl] = (gt * jax.nn.sigmoid(gt) * y).astype(BF16)


def _ret_tables(S):
    C, H, half = RET_CHUNK, RET_HEADS, HEAD_DIM // 2
    inv_freq = ROPE_BASE ** (-jnp.arange(half, dtype=F32) / half)
    ang = jnp.arange(S, dtype=F32)[:, None] * inv_freq[None, :]
    cos2 = jnp.concatenate([jnp.cos(ang), jnp.cos(ang)], axis=1)
    sin2 = jnp.concatenate([-jnp.sin(ang), jnp.sin(ang)], axis=1)
    log_g = jnp.log1p(-jnp.exp2(-5.0 - jnp.arange(H, dtype=F32)))
    n = jnp.arange(C, dtype=F32)
    diff = n[:, None] - n[None, :]
    dec = jnp.where(diff >= 0, jnp.exp(log_g[:, None, None] * jnp.maximum(diff, 0.0)), 0.0)
    lanes = lambda t: jnp.broadcast_to(t[..., None], t.shape + (HEAD_DIM,))
    xi = lanes(jnp.exp(log_g[:, None] * (n + 1.0)))
    zeta = lanes(jnp.exp(log_g[:, None] * (C - 1.0 - n)))
    cd = lanes(jnp.exp(log_g * C))
    return cos2, sin2, dec, xi, zeta, cd


def _retention(proj, gn_g, gn_b, B, S):
    C, H = RET_CHUNK, RET_HEADS
    n_ch = S // C
    cos2, sin2, dec, xi, zeta, cd = _ret_tables(S)
    spec = lambda k: pl.BlockSpec((C, RET_WIDTH), lambda b, c: (b * n_ch + c, k))
    whole = lambda a: pl.BlockSpec(a.shape, lambda b, c: (0,) * a.ndim)
    return pl.pallas_call(
        _ret_kernel,
        grid=(B, n_ch),
        in_specs=[
            spec(0), spec(1), spec(2), spec(3),
            pl.BlockSpec((C, HEAD_DIM), lambda b, c: (c, 0)),
            pl.BlockSpec((C, HEAD_DIM), lambda b, c: (c, 0)),
            whole(dec), whole(xi), whole(zeta), whole(cd), whole(gn_g), whole(gn_b),
        ],
        out_specs=pl.BlockSpec((C, RET_WIDTH), lambda b, c: (b * n_ch + c, 0)),
        out_shape=jax.ShapeDtypeStruct((B * S, RET_WIDTH), BF16),
        scratch_shapes=[pltpu.VMEM((H, HEAD_DIM, HEAD_DIM), F32)],
        compiler_params=_params(("parallel", "arbitrary")),
        name="retention",
    )(proj, proj, proj, proj, cos2, sin2, dec, xi, zeta, cd, gn_g, gn_b)


def _out_kernel(on_ref, or_ref, w_ref, x_ref, g2_ref, wr_ref, br_ref, tri_ref,
                x1_ref, h2_ref, rt_ref, cnt_ref, cnt_scr, ws_scr):
    @pl.when(pl.program_id(0) == 0)
    def _():
        cnt_scr[...] = jnp.zeros(cnt_scr.shape, F32)
        wr = wr_ref[...]
        w_hi = wr.astype(BF16)
        ws_scr[:, 0:LANES] = w_hi
        ws_scr[:, LANES:2 * LANES] = (wr - w_hi.astype(F32)).astype(BF16)

    half = on_ref.shape[1]
    acc = _dot(on_ref[...], w_ref[0:half, :]) + _dot(or_ref[...], w_ref[half:2 * half, :])
    x1 = x_ref[...] + acc
    x1_ref[...] = x1
    h2 = _rms(x1, g2_ref[...])
    h2_ref[...] = h2

    h_hi = h2.astype(BF16)
    h_lo = (h2 - h_hi.astype(F32)).astype(BF16)
    tm = h2.shape[0]
    prod = _dot(jnp.concatenate([h_hi, h_lo], axis=0), ws_scr[...])
    logits = (prod[0:tm, 0:LANES] + prod[tm:2 * tm, 0:LANES] + prod[0:tm, LANES:2 * LANES]
              + br_ref[...])

    tm = logits.shape[0]
    lt = logits.T
    G8 = EXPERTS_PER_GROUP
    row = lax.broadcasted_iota(jnp.int32, (G8, 1), 0).astype(F32)
    big = float(G8)
    col_max = lambda v: jnp.max(v, axis=0, keepdims=True)
    col_min = lambda v: jnp.min(v, axis=0, keepdims=True)
    col_sum = lambda v: jnp.sum(v, axis=0, keepdims=True)
    gm = row < N_GROUPS
    gl = jnp.where(gm, lt[0:G8], -jnp.inf)
    ge = jnp.where(gm, jnp.exp(gl - col_max(gl)), 0.0)
    pg = ge / col_sum(ge)
    g_top = col_max(pg)
    g_idx = col_min(jnp.where(gm & (pg == g_top), row, big))
    el = lt[EXP_ROW0:EXP_ROW0 + G8]
    for g in range(1, N_GROUPS):
        el = jnp.where(g_idx == float(g), lt[EXP_ROW0 + g * G8:EXP_ROW0 + (g + 1) * G8], el)
    ee = jnp.exp(el - col_max(el))
    pe = ee / col_sum(ee)
    t1 = col_max(pe)
    i1 = col_min(jnp.where(pe == t1, row, big))
    rest = row != i1
    t2 = col_max(jnp.where(rest, pe, -1.0))
    i2 = col_min(jnp.where(rest & (pe == t2), row, big))
    tsum = t1 + t2
    w1 = g_top * t1 / tsum
    w2 = g_top * t2 / tsum
    e1 = g_idx * float(G8) + i1
    e2 = g_idx * float(G8) + i2
    erow = lax.broadcasted_iota(jnp.int32, (N_EXPERTS, 1), 0).astype(F32)
    oh1 = jnp.where(erow == e1, 1.0, 0.0)
    oh2 = jnp.where(erow == e2, 1.0, 0.0)
    both = oh1 + oh2
    before = _dot(both.astype(BF16), tri_ref[...]) + cnt_scr[...]
    r1 = col_sum(before * oh1)
    r2 = col_sum(before * oh2)
    cnt_scr[...] = cnt_scr[...] + jnp.sum(both, axis=1, keepdims=True)
    cnt_ref[...] = jnp.broadcast_to(cnt_scr[...], cnt_ref.shape)
    rt_ref[...] = jnp.concatenate([e1, e2, w1, w2, r1, r2, jnp.zeros((2, tm), F32)], axis=0)


def _out_proj(o_nsa, o_ret, w_out, xt, g2, w_router, b_router, tm):
    T, D = xt.shape
    half = o_nsa.shape[1]
    tri = jnp.asarray(np.triu(np.ones((tm, tm), np.float32), 1), BF16)
    return pl.pallas_call(
        _out_kernel,
        grid=(T // tm,),
        in_specs=[
            pl.BlockSpec((tm, half), lambda m: (m, 0)),
            pl.BlockSpec((tm, half), lambda m: (m, 0)),
            pl.BlockSpec((2 * half, D), lambda m: (0, 0), pipeline_mode=pl.Buffered(1)),
            pl.BlockSpec((tm, D), lambda m: (m, 0)),
            pl.BlockSpec((1, D), lambda m: (0, 0)),
            pl.BlockSpec((D, LANES), lambda m: (0, 0)),
            pl.BlockSpec((1, LANES), lambda m: (0, 0)),
            pl.BlockSpec((tm, tm), lambda m: (0, 0)),
        ],
        out_specs=[
            pl.BlockSpec((tm, D), lambda m: (m, 0)),
            pl.BlockSpec((tm, D), lambda m: (m, 0)),
            pl.BlockSpec((8, tm), lambda m: (0, m)),
            pl.BlockSpec((N_EXPERTS, LANES), lambda m: (0, 0)),
        ],
        out_shape=[
            jax.ShapeDtypeStruct((T, D), F32),
            jax.ShapeDtypeStruct((T, D), F32),
            jax.ShapeDtypeStruct((8, T), F32),
            jax.ShapeDtypeStruct((N_EXPERTS, LANES), F32),
        ],
        scratch_shapes=[pltpu.VMEM((N_EXPERTS, 1), F32), pltpu.VMEM((D, 2 * LANES), BF16)],
        compiler_params=_params(("arbitrary",)),
        name="out_proj",
    )(o_nsa, o_ret, w_out, xt, g2, w_router, b_router, tri)


def _row_copy(src, row, dst, slot, sem):
    return pltpu.make_async_copy(src.at[pl.ds(row, 1), :], dst.at[pl.ds(slot, 1), :], sem)


def _dispatch_kernel(pos_ref, pad_ref, h2_ref, xs_hbm, zrow, sem, zsem, *, n_tok):
    i = pl.program_id(0)
    tm = h2_ref.shape[0]
    K = TOPK_IN_GROUP

    def issue(r, carry):
        for k in range(K):
            _row_copy(h2_ref, r, xs_hbm, pos_ref[k * n_tok + i * tm + r], sem).start()
        return carry

    lax.fori_loop(0, tm, issue, 0, unroll=8)

    @pl.when(i == pl.num_programs(0) - 1)
    def _():
        zrow[...] = jnp.zeros(zrow.shape, F32)
        for e in range(N_EXPERTS):
            lo, hi = pad_ref[e], pad_ref[N_EXPERTS + e]
            mid = jnp.minimum(lax.shift_left(lax.shift_right_logical(lo + 7, 3), 3), hi)

            def fill(r, carry):
                _row_copy(zrow, 0, xs_hbm, r, zsem).start()
                return carry

            lax.fori_loop(lo, mid, fill, 0)

            def fill_wait(r, carry):
                _row_copy(zrow, 0, xs_hbm, r, zsem).wait()
                return carry

            lax.fori_loop(lo, mid, fill_wait, 0)

            def piece(j):
                r8 = pl.multiple_of(mid + j * 8, 8)
                return pltpu.make_async_copy(zrow.at[pl.ds(0, 8), :], xs_hbm.at[pl.ds(r8, 8), :], zsem)

            n8 = lax.shift_right_logical(hi - mid, 3)

            def fill8(j, carry):
                piece(j).start()
                return carry

            lax.fori_loop(0, n8, fill8, 0)

            def fill8_wait(j, carry):
                piece(j).wait()
                return carry

            lax.fori_loop(0, n8, fill8_wait, 0)

        M = zrow.shape[0]

        def tail_copy(t):
            return pltpu.make_async_copy(zrow, xs_hbm.at[pl.ds(pl.multiple_of(t * M, M), M), :], zsem)

        def tail(t, carry):
            tail_copy(t).start()
            return carry

        lax.fori_loop(pad_ref[2 * N_EXPERTS], xs_hbm.shape[0] // M, tail, 0)

        def tail_wait(t, carry):
            tail_copy(t).wait()
            return carry

        lax.fori_loop(pad_ref[2 * N_EXPERTS], xs_hbm.shape[0] // M, tail_wait, 0)

    for k in range(K):
        pltpu.make_async_copy(h2_ref, xs_hbm.at[pl.ds(0, tm), :], sem).wait()


def _dispatch(pos, pad_rows, h2, n_rows, tm):
    T, D = h2.shape
    grid_spec = pltpu.PrefetchScalarGridSpec(
        num_scalar_prefetch=2,
        grid=(T // tm,),
        in_specs=[pl.BlockSpec((tm, D), lambda i, pos, pad: (i, 0))],
        out_specs=pl.BlockSpec(memory_space=pl.ANY),
        scratch_shapes=[pltpu.VMEM((MOE_BLOCK, D), F32), pltpu.SemaphoreType.DMA, pltpu.SemaphoreType.DMA],
    )
    return pl.pallas_call(
        functools.partial(_dispatch_kernel, n_tok=T),
        grid_spec=grid_spec,
        out_shape=jax.ShapeDtypeStruct((n_rows, D), F32),
        compiler_params=_params(("arbitrary",)),
        name="dispatch",
    )(pos, pad_rows, h2)


def _expert_kernel(ts_ref, xs_hbm, wg_ref, wu_ref, wd_ref, y_hbm,
                   xb0, xb1, yb0, yb1, wg_b, wu_b, wd_b, gsem, osem, *, n_blocks):
    e = pl.program_id(0)
    M = MOE_BLOCK
    t0 = ts_ref[e]
    t1 = ts_ref[e + 1]
    n_used = ts_ref[N_EXPERTS]
    xbufs = (xb0, xb1)
    ybufs = (yb0, yb1)

    def rows(t):
        return pl.ds(pl.multiple_of(t * M, M), M)

    def in_copy(t, p):
        return pltpu.make_async_copy(xs_hbm.at[rows(t), :], xbufs[p], gsem.at[p])

    def out_copy(t, p):
        return pltpu.make_async_copy(ybufs[p], y_hbm.at[rows(t), :], osem.at[p])

    @pl.when(e == 0)
    def _():
        in_copy(0, 0).start()

    @pl.when(t1 > t0)
    def _():
        wg_b[...] = wg_ref[0].astype(BF16)
        wu_b[...] = wu_ref[0].astype(BF16)
        wd_b[...] = wd_ref[0].astype(BF16)

    def tile_body(t, p):
        in_copy(t, p).wait()

        @pl.when(t + 1 < n_used)
        def _():
            in_copy(t + 1, 1 - p).start()

        @pl.when(t >= 2)
        def _():
            out_copy(t - 2, p).wait()

        xb = xbufs[p][...].astype(BF16)
        hg = _dot(xb, wg_b[...])
        hu = _dot(xb, wu_b[...])
        hb = (hg * jax.nn.sigmoid(hg) * hu).astype(BF16)
        ybufs[p][...] = _dot(hb, wd_b[...])
        out_copy(t, p).start()

    def tile(t, carry):
        parity = lax.rem(t, 2)
        for p in range(2):
            pl.when(parity == p)(functools.partial(tile_body, t, p))
        return carry

    lax.fori_loop(t0, t1, tile, 0)

    @pl.when(e == N_EXPERTS - 1)
    def _():
        parity = lax.rem(n_used, 2)
        for p in range(2):
            @pl.when(parity == p)
            def _(p=p):
                out_copy(n_used - 1, 1 - p).wait()

                @pl.when(n_used >= 2)
                def _():
                    out_copy(n_used - 2, p).wait()

        yb0[...] = jnp.zeros(yb0.shape, F32)

        def clear(t, carry):
            out_copy(t, 0).start()
            return carry

        lax.fori_loop(n_used, n_blocks, clear, 0)

        def clear_wait(t, carry):
            out_copy(t, 0).wait()
            return carry

        lax.fori_loop(n_used, n_blocks, clear_wait, 0)


def _experts(xs, tile_start, w_gate, w_up, w_down):
    n_rows, D = xs.shape
    n_blocks = n_rows // MOE_BLOCK
    grid_spec = pltpu.PrefetchScalarGridSpec(
        num_scalar_prefetch=1,
        grid=(N_EXPERTS,),
        in_specs=[
            pl.BlockSpec(memory_space=pl.ANY),
            pl.BlockSpec((1, D, D_EXPERT), lambda e, ts: (e, 0, 0)),
            pl.BlockSpec((1, D, D_EXPERT), lambda e, ts: (e, 0, 0)),
            pl.BlockSpec((1, D_EXPERT, D), lambda e, ts: (e, 0, 0)),
        ],
        out_specs=pl.BlockSpec(memory_space=pl.ANY),
        scratch_shapes=[pltpu.VMEM((MOE_BLOCK, D), F32)] * 4 + [
            pltpu.VMEM((D, D_EXPERT), BF16), pltpu.VMEM((D, D_EXPERT), BF16),
            pltpu.VMEM((D_EXPERT, D), BF16),
            pltpu.SemaphoreType.DMA((2,)), pltpu.SemaphoreType.DMA((2,)),
        ],
    )
    return pl.pallas_call(
        functools.partial(_expert_kernel, n_blocks=n_blocks),
        grid_spec=grid_spec,
        out_shape=jax.ShapeDtypeStruct((n_rows, D), F32),
        compiler_params=_params(("arbitrary",)),
        name="experts",
    )(tile_start, xs, w_gate, w_up, w_down)


def _combine_kernel(pos_ref, y_hbm, x1_ref, rt_ref, o_ref, ybuf, sems, *, n_tok):
    i = pl.program_id(0)
    n = pl.num_programs(0)
    tm = x1_ref.shape[0]
    K = TOPK_IN_GROUP
    slot = lax.rem(i, 2)

    def issue_tile(t, s):
        def issue(r, carry):
            for k in range(K):
                _row_copy(y_hbm, pos_ref[k * n_tok + t * tm + r], ybuf.at[s, k], r,
                          sems.at[s]).start(priority=k)
            return carry

        lax.fori_loop(0, tm, issue, 0, unroll=8)

    @pl.when(i == 0)
    def _():
        issue_tile(0, 0)

    @pl.when(i + 1 < n)
    def _():
        issue_tile(i + 1, 1 - slot)

    for k in range(K):
        pltpu.make_async_copy(y_hbm.at[pl.ds(0, tm), :], ybuf.at[slot, k], sems.at[slot]).wait()
    w = rt_ref[...].T
    yb = ybuf[slot]
    o_ref[...] = x1_ref[...] + (yb[0] * w[:, K:K + 1] + yb[1] * w[:, K + 1:K + 2])


def _combine(pos, y_rows, x1, rt, tm):
    T, D = x1.shape
    grid_spec = pltpu.PrefetchScalarGridSpec(
        num_scalar_prefetch=1,
        grid=(T // tm,),
        in_specs=[
            pl.BlockSpec(memory_space=pl.ANY),
            pl.BlockSpec((tm, D), lambda i, pos: (i, 0)),
            pl.BlockSpec((8, tm), lambda i, pos: (0, i)),
        ],
        out_specs=pl.BlockSpec((tm, D), lambda i, pos: (i, 0)),
        scratch_shapes=[pltpu.VMEM((2, TOPK_IN_GROUP, tm, D), F32), pltpu.SemaphoreType.DMA((2,))],
    )
    return pl.pallas_call(
        functools.partial(_combine_kernel, n_tok=T),
        grid_spec=grid_spec,
        out_shape=jax.ShapeDtypeStruct((T, D), F32),
        compiler_params=_params(("arbitrary",)),
        name="combine",
    )(pos, y_rows, x1, rt)


def _block_layout(rt, counts, T):
    K = TOPK_IN_GROUP
    n_rows = (K * T + N_EXPERTS * (MOE_BLOCK - 1) + MOE_BLOCK - 1) // MOE_BLOCK * MOE_BLOCK
    n_blocks = n_rows // MOE_BLOCK
    cnt = counts[:, 0].astype(jnp.int32)
    padded = (cnt + MOE_BLOCK - 1) // MOE_BLOCK * MOE_BLOCK
    pad_end = jnp.cumsum(padded)
    pad_start = pad_end - padded
    e = rt[0:K].astype(jnp.int32).reshape(-1)
    rank = rt[2 * K:3 * K].astype(jnp.int32).reshape(-1)
    hit = e[:, None] == jnp.arange(N_EXPERTS, dtype=jnp.int32)[None, :]
    pos = jnp.sum(jnp.where(hit, pad_start[None, :].astype(jnp.int32), 0), axis=1) + rank
    tile_start = (jnp.concatenate([pad_start, pad_end[-1:]]) // MOE_BLOCK).astype(jnp.int32)
    pad_rows = jnp.concatenate([pad_start + cnt, pad_end, pad_end[-1:] // MOE_BLOCK]).astype(jnp.int32)
    return tile_start, pos, pad_rows, n_rows


def _layer(x, norm1_g, w_in, cmp_pos, cmp_w1, cmp_w2, q_norm_g, k_norm_g, ret_gn_g, ret_gn_b,
           w_out, norm2_g, w_rg, b_rg, w_re, b_re, w_eg, w_eu, w_ed, tiles):
    B, S, D = x.shape
    T = B * S
    xt = x.reshape(T, D)

    w_main = jnp.concatenate([w_in[:, GATE_COL0 + NSA_GATE_WIDTH:], w_in[:, :GATE_COL0]], axis=1).astype(BF16)
    gpg = NSA_GATE_WIDTH // NSA_KV_GROUPS
    w_gate = jnp.concatenate(
        [jnp.pad(w_in[:, GATE_COL0 + g * gpg:GATE_COL0 + (g + 1) * gpg], ((0, 0), (0, LANES - gpg)))
         for g in range(NSA_KV_GROUPS)], axis=1).astype(BF16)
    proj, gate = _in_proj(xt, norm1_g.reshape(1, D), w_main, w_gate, tiles["tm_in"], tiles["tn_in"])

    cmp_kv = _compress(proj, cmp_pos, cmp_w1, cmp_w2, k_norm_g, B, S)
    o_nsa = _nsa(proj, gate, cmp_kv, q_norm_g.reshape(1, HEAD_DIM), k_norm_g, B, S, tiles["tq"], tiles["tk"])

    o_ret = _retention(proj, ret_gn_g, ret_gn_b, B, S)

    gap, tail = EXP_ROW0 - N_GROUPS, LANES - EXP_ROW0 - N_EXPERTS
    w_router = jnp.concatenate([w_rg, jnp.zeros((D, gap), F32), w_re, jnp.zeros((D, tail), F32)], axis=1)
    b_router = jnp.concatenate([b_rg, jnp.zeros((gap,), F32), b_re, jnp.zeros((tail,), F32)]).reshape(1, LANES)
    x1, h2, rt, counts = _out_proj(o_nsa, o_ret, w_out.astype(BF16), xt, norm2_g.reshape(1, D),
                                   w_router, b_router, tiles["tm_out"])

    tile_start, pos, pad_rows, n_rows = _block_layout(rt, counts, T)
    xs = _dispatch(pos, pad_rows, h2, n_rows, tiles["tm_dsp"])
    y_rows = _experts(xs, tile_start, w_eg, w_eu, w_ed)
    out = _combine(pos, y_rows, x1, rt, tiles["tm_cmb"])
    return out.reshape(B, S, D)


def _tiles(T, S):
    return {
        "tm_in": min(2048, T), "tn_in": 512,
        "tq": min(256, S), "tk": min(256, S),
        "tm_out": min(512, T),
        "tm_dsp": min(1024, T),
        "tm_cmb": min(256, T),
    }


def kernel(x, norm1_g, w_in, cmp_pos, cmp_w1, cmp_w2, q_norm_g, k_norm_g, ret_gn_g, ret_gn_b, w_out, norm2_g, w_router_group, b_router_group, w_router_expert, b_router_expert, w_exp_gate, w_exp_up, w_exp_down):
    B, S, _ = x.shape
    tiles = _tiles(B * S, S)
    for l in range(norm1_g.shape[0]):
        x = _layer(x, norm1_g[l], w_in[l], cmp_pos[l], cmp_w1[l], cmp_w2[l], q_norm_g[l], k_norm_g[l],
                   ret_gn_g[l], ret_gn_b[l], w_out[l], norm2_g[l], w_router_group[l], b_router_group[l],
                   w_router_expert[l], b_router_expert[l], w_exp_gate[l], w_exp_up[l], w_exp_down[l], tiles)
    return x
```

```python
import functools

import numpy as np
import jax
import jax.numpy as jnp
from jax import lax
from jax.experimental import pallas as pl
from jax.experimental.pallas import tpu as pltpu

F32 = jnp.float32
BF16 = jnp.bfloat16

D_MODEL = 2048
NSA_HEADS = 8
NSA_KV_GROUPS = 2
NSA_GROUP_HEADS = NSA_HEADS // NSA_KV_GROUPS
HEAD_DIM = 128
RET_HEADS = 8
CMP_BLOCK = 32
CMP_STRIDE = 16
SEL_BLOCK = 64
SEL_TOPK = 8
WINDOW = 512
RET_CHUNK = 128
ROPE_BASE = 10000.0
N_GROUPS = 4
EXPERTS_PER_GROUP = 8
N_EXPERTS = N_GROUPS * EXPERTS_PER_GROUP
TOPK_IN_GROUP = 2
D_EXPERT = 512
MOE_BLOCK = 256
RMS_EPS = 1e-6
GN_EPS = 1e-5
NEG_INF = -1e30
FORCED_SCORE = 1e6
EXP_ROW0 = 8
LOG2E = 1.4426950408889634

NSA_Q_WIDTH = NSA_HEADS * HEAD_DIM
NSA_KV_WIDTH = NSA_KV_GROUPS * HEAD_DIM
NSA_GATE_WIDTH = 3 * NSA_HEADS
RET_WIDTH = RET_HEADS * HEAD_DIM
GATE_COL0 = NSA_Q_WIDTH + 6 * NSA_KV_WIDTH
LANES = 128
VMEM_LIMIT = 56 * 1024 * 1024

Q_COL0 = 4 * RET_WIDTH
KV_COL0 = Q_COL0 + NSA_Q_WIDTH
KC_BLK, VC_BLK, KS_BLK, VS_BLK, KW_BLK, VW_BLK = [KV_COL0 // LANES + 2 * t for t in range(6)]


def _rms(xf, g):
    return xf * lax.rsqrt(jnp.mean(xf * xf, axis=-1, keepdims=True) + RMS_EPS) * g


def _dot(a, b):
    return jnp.dot(a, b, preferred_element_type=F32)


def _dot_nt(a, b):
    return lax.dot_general(a, b, (((1,), (1,)), ((), ())), preferred_element_type=F32)


def _dot_tn(a, b):
    return lax.dot_general(a, b, (((0,), (0,)), ((), ())), preferred_element_type=F32)


def _split3(p):
    p1 = p.astype(BF16)
    r1 = p - p1.astype(F32)
    p2 = r1.astype(BF16)
    p3 = (r1 - p2.astype(F32)).astype(BF16)
    return p1, p2, p3


def _params(sem):
    return pltpu.CompilerParams(dimension_semantics=sem, vmem_limit_bytes=VMEM_LIMIT)


IN_CHUNK = 256


def _in_proj_kernel(x_hbm, g_ref, w_ref, wg_ref, o_ref, og_ref, h0, h1, xc0, xc1, sem, *, tm):
    m = pl.program_id(0)
    n = pl.program_id(1)
    ck = IN_CHUNK
    nc = tm // ck
    hs = (h0, h1)
    xcs = (xc0, xc1)
    g = g_ref[...]

    def chunk_copy(tile, c, slot):
        row0 = pl.multiple_of(tile * tm + c * ck, ck)
        return pltpu.make_async_copy(x_hbm.at[pl.ds(row0, ck), :], xcs[slot], sem.at[slot])

    def normalise(slot, dst, c):
        dst[pl.ds(pl.multiple_of(c * ck, ck), ck), :] = _rms(xcs[slot][...], g).astype(BF16)

    @pl.when((m == 0) & (n == 0))
    def _():
        chunk_copy(0, 0, 0).start()
        for c in range(nc):
            if c + 1 < nc:
                chunk_copy(0, c + 1, (c + 1) % 2).start()
            chunk_copy(0, c, c % 2).wait()
            normalise(c % 2, h0, c)

    has_next = m + 1 < pl.num_programs(0)
    for slot in range(2):
        @pl.when(has_next & (n >= 1) & (n <= nc) & (lax.rem(n - 1, 2) == slot))
        def _(slot=slot):
            chunk_copy(m + 1, n - 1, slot).wait()

        @pl.when(has_next & (n < nc) & (lax.rem(n, 2) == slot))
        def _(slot=slot):
            chunk_copy(m + 1, n, slot).start()

    c_prev = jnp.where(n == 0, nc - 1, jnp.clip(n - 1, 0, nc - 1))
    even_chunk = lax.rem(c_prev, 2) == 0
    for cur in range(2):
        for slot in range(2):
            @pl.when((lax.rem(m, 2) == cur) & (even_chunk == (slot == 0)))
            def _(cur=cur, slot=slot):
                @pl.when(n == 0)
                def _():
                    og_ref[...] = _dot(hs[cur][...], wg_ref[...])

                normalise(slot, hs[1 - cur], c_prev)
                o_ref[...] = _dot(hs[cur][...], w_ref[...])


def _in_proj(xt, g1, w_main, w_gate, tm, tn):
    T, D = xt.shape
    n_main = w_main.shape[1]
    n_gate = w_gate.shape[1]
    assert tm % (2 * IN_CHUNK) == 0 and tm // IN_CHUNK < n_main // tn
    return pl.pallas_call(
        functools.partial(_in_proj_kernel, tm=tm),
        grid=(T // tm, n_main // tn),
        in_specs=[
            pl.BlockSpec(memory_space=pl.ANY),
            pl.BlockSpec((1, D), lambda m, n: (0, 0)),
            pl.BlockSpec((D, tn), lambda m, n: (0, n)),
            pl.BlockSpec((D, n_gate), lambda m, n: (0, 0)),
        ],
        out_specs=[
            pl.BlockSpec((tm, tn), lambda m, n: (m, n)),
            pl.BlockSpec((tm, n_gate), lambda m, n: (m, 0)),
        ],
        out_shape=[
            jax.ShapeDtypeStruct((T, n_main), F32),
            jax.ShapeDtypeStruct((T, n_gate), F32),
        ],
        scratch_shapes=[pltpu.VMEM((tm, D), BF16), pltpu.VMEM((tm, D), BF16),
                        pltpu.VMEM((IN_CHUNK, D), F32), pltpu.VMEM((IN_CHUNK, D), F32),
                        pltpu.SemaphoreType.DMA((2,))],
        compiler_params=_params(("arbitrary", "arbitrary")),
        name="in_proj",
    )(xt, g1, w_main, w_gate)


def _compress_kernel(x_ref, pos_ref, w1_ref, w2_ref, kg_ref, o_ref, xs_scr, col_scr, *, S, n_pad):
    kv = pl.program_id(1)
    xs_scr[0:S, :] = x_ref[...]
    xs_scr[S:S + CMP_STRIDE, :] = jnp.zeros((CMP_STRIDE, HEAD_DIM), F32)
    for l in range(CMP_BLOCK):
        tb = xs_scr[pl.ds(l, n_pad, stride=CMP_STRIDE), :] + pos_ref[0, l:l + 1, :]
        col_scr[:, l * HEAD_DIM:(l + 1) * HEAD_DIM] = tb.astype(BF16)
    w1 = w1_ref[0].reshape(CMP_BLOCK * HEAD_DIM, HEAD_DIM).astype(BF16)
    acc = _dot(col_scr[...], w1)
    hid = acc * jax.nn.sigmoid(acc)
    out = _dot(hid.astype(BF16), w2_ref[0].astype(BF16))
    normed = _rms(out, kg_ref[0:1, :])
    o_ref[0, 0, 0] = jnp.where(kv == 0, normed, out).astype(BF16)


def _compress(proj, cmp_pos, cmp_w1, cmp_w2, k_norm_g, B, S):
    n_pad = S // CMP_STRIDE
    G = NSA_KV_GROUPS
    kern = functools.partial(_compress_kernel, S=S, n_pad=n_pad)
    return pl.pallas_call(
        kern,
        grid=(B, 2, G),
        in_specs=[
            pl.BlockSpec((S, HEAD_DIM), lambda b, kv, g: (b, KC_BLK + 2 * kv + g)),
            pl.BlockSpec((1, CMP_BLOCK, HEAD_DIM), lambda b, kv, g: (kv, 0, 0)),
            pl.BlockSpec((1, CMP_BLOCK, HEAD_DIM, HEAD_DIM), lambda b, kv, g: (kv, 0, 0, 0)),
            pl.BlockSpec((1, HEAD_DIM, HEAD_DIM), lambda b, kv, g: (kv, 0, 0)),
            pl.BlockSpec((3, HEAD_DIM), lambda b, kv, g: (0, 0)),
        ],
        out_specs=pl.BlockSpec((1, 1, 1, n_pad, HEAD_DIM), lambda b, kv, g: (b, kv, g, 0, 0)),
        out_shape=jax.ShapeDtypeStruct((B, 2, G, n_pad, HEAD_DIM), BF16),
        scratch_shapes=[pltpu.VMEM((S + CMP_STRIDE, HEAD_DIM), F32),
                        pltpu.VMEM((n_pad, CMP_BLOCK * HEAD_DIM), BF16)],
        compiler_params=_params(("parallel", "parallel", "parallel")),
        name="compress",
    )(proj, cmp_pos, cmp_w1, cmp_w2, k_norm_g)


def _nsa_kernel(q_ref, kc_ref, vc_ref, ks_ref, vs_ref, kw_ref, vw_ref, gate_ref, qg_ref, kg_ref,
                ovt_ref, ext_ref, o_ref, ksa, vsa, kwn, vwa, l_scr, acc_scr, *, S, tq, tk, wk):
    i = pl.program_id(2)
    H = NSA_GROUP_HEADS
    scale = HEAD_DIM ** -0.5
    c2 = scale * LOG2E
    n_cp = S // CMP_STRIDE
    n_sblk = S // SEL_BLOCK

    @pl.when(i == 0)
    def _():
        ksa[:, 0:HEAD_DIM] = _rms(ks_ref[...], kg_ref[1:2, :]).astype(BF16)
        ksa[:, HEAD_DIM:2 * HEAD_DIM] = ext_ref[...]
        kwn[...] = _rms(kw_ref[...], kg_ref[2:3, :]).astype(BF16)
        ones = jnp.ones((S, HEAD_DIM), BF16)
        vsa[:, 0:HEAD_DIM] = vs_ref[...].astype(BF16)
        vsa[:, HEAD_DIM:2 * HEAD_DIM] = ones
        vwa[:, 0:HEAD_DIM] = vw_ref[...].astype(BF16)
        vwa[:, HEAD_DIM:2 * HEAD_DIM] = ones

    pos = i * tq + lax.broadcasted_iota(jnp.int32, (tq, 1), 0)
    qg = qg_ref[...]
    q4 = jnp.concatenate([_rms(q_ref[:, h * HEAD_DIM:(h + 1) * HEAD_DIM], qg).astype(BF16)
                          for h in range(H)], axis=0)
    rows = [slice(h * tq, (h + 1) * tq) for h in range(H)]

    def softmax_pv(q_chains, k, v, tail_bias):
        head = k.shape[0] - tail_bias.shape[1]
        outs, sums = [], []
        for q2 in q_chains:
            s2 = _dot_nt(q2, k)
            es = []
            hpc = q2.shape[0] // tq
            for hh in range(hpc):
                t = s2[hh * tq:(hh + 1) * tq] * c2
                tb = t[:, head:] + tail_bias
                m = jnp.max(tb, axis=-1, keepdims=True)
                if head:
                    ta = t[:, :head]
                    m = jnp.maximum(m, jnp.max(ta, axis=-1, keepdims=True))
                    es.append(jnp.concatenate([jnp.exp2(ta - m).astype(BF16),
                                               jnp.exp2(tb - m).astype(BF16)], axis=1))
                else:
                    es.append(jnp.exp2(tb - m).astype(BF16))
            o2 = _dot(es[0] if hpc == 1 else jnp.concatenate(es, axis=0), v)
            outs += [o2[hh * tq:(hh + 1) * tq, 0:HEAD_DIM] for hh in range(hpc)]
            sums += [o2[hh * tq:(hh + 1) * tq, HEAD_DIM:2 * HEAD_DIM] for hh in range(hpc)]
        return outs, sums

    pos_t = i * tq + lax.broadcasted_iota(jnp.int32, (1, tq), 1)
    nrow = lax.broadcasted_iota(jnp.int32, (n_cp, 1), 0)
    cmask_t = (nrow * CMP_STRIDE + (CMP_BLOCK - 1)) <= pos_t
    s_ct = _dot_nt(kc_ref[0, 0, 0], q4) * scale
    vc = vc_ref[0, 0, 0]
    ps_t, o_cs = [], []
    for r in rows:
        sm = jnp.where(cmask_t, s_ct[:, r], NEG_INF)
        e = jnp.where(cmask_t, jnp.exp(sm - jnp.max(sm, axis=0, keepdims=True)), 0.0)
        den = jnp.sum(e, axis=0, keepdims=True)
        p = e / jnp.where(den > 0.0, den, 1.0)
        ps_t.append(p)
        o_cs.append(_dot_tn(p.astype(BF16), vc))
    psum_t = ps_t[0] + ps_t[1] + ps_t[2] + ps_t[3]

    ovt = ovt_ref[...]
    p1, p2, p3 = _split3(psum_t)
    imp_t = (_dot(ovt, p1) + _dot(ovt, p2) + _dot(ovt, p3))[0:n_sblk, :]
    jrow = lax.broadcasted_iota(jnp.int32, (n_sblk, 1), 0)
    cur_t = pos_t // SEL_BLOCK
    valid_t = jrow * SEL_BLOCK <= pos_t
    forced_t = (jrow == 0) | (jrow == cur_t) | (jrow == cur_t - 1)
    score = jnp.where(valid_t, jnp.where(forced_t, FORCED_SCORE, imp_t), -jnp.inf)
    rank = jnp.zeros((n_sblk, tq), jnp.int32)
    for i2 in range(n_sblk):
        si = score[i2:i2 + 1, :]
        ahead = (si > score) | ((si == score) & (jrow > i2))
        rank = rank + jnp.where(ahead, 1, 0)
    drop_t = jnp.where((rank < SEL_TOPK) & valid_t, 0.0, NEG_INF)
    drop_t = jnp.concatenate([drop_t, jnp.full((LANES - n_sblk, tq), NEG_INF, F32)], axis=0)
    drop = drop_t.T.astype(BF16)
    qa_pairs = [jnp.concatenate([q4[r], drop], axis=1) for r in rows]

    kst = pl.multiple_of(jnp.clip(i * tq - WINDOW, 0, S - wk), tq)
    wpos = kst + lax.broadcasted_iota(jnp.int32, (1, wk), 1)
    wbias = jnp.where((wpos <= pos) & (wpos > pos - WINDOW), 0.0, NEG_INF)
    o_ws, lws = softmax_pv([q4[r] for r in rows],
                           kwn[pl.ds(kst, wk), :], vwa[pl.ds(kst, wk), :], wbias)

    def slc_prefix(n):
        kpos = (n - tk) + lax.broadcasted_iota(jnp.int32, (1, tk), 1)
        causal = jnp.where(kpos <= pos, 0.0, NEG_INF)
        outs, sums = softmax_pv(qa_pairs, ksa[0:n, :], vsa[0:n, :], causal)
        for r, o, l in zip(rows, outs, sums):
            acc_scr[r] = o
            l_scr[r] = l

    n_cls = S // tk
    cls = ((i + 1) * tq - 1) // tk
    for c in range(n_cls):
        pl.when(cls == c)(functools.partial(slc_prefix, (c + 1) * tk))

    gates = jax.nn.sigmoid(gate_ref[...])
    for h, r in enumerate(rows):
        c = 3 * h
        o = (o_cs[h] * gates[:, c:c + 1]
             + acc_scr[r] * (gates[:, c + 1:c + 2] / l_scr[r])
             + o_ws[h] * (gates[:, c + 2:c + 3] / lws[h]))
        o_ref[:, h * HEAD_DIM:(h + 1) * HEAD_DIM] = o.astype(BF16)


def _nsa_tables(S):
    n_cp, n_cmp, n_sblk = S // CMP_STRIDE, (S - CMP_BLOCK) // CMP_STRIDE + 1, S // SEL_BLOCK
    n = np.arange(n_cp)[None, :] * CMP_STRIDE
    j = np.arange(LANES)[:, None]
    ovt = ((n < (j + 1) * SEL_BLOCK) & (n + CMP_BLOCK > j * SEL_BLOCK)
           & (np.arange(n_cp)[None, :] < n_cmp) & (j < n_sblk))
    key_block = (np.arange(S) // SEL_BLOCK)[:, None] == np.arange(LANES)[None, :]
    return jnp.asarray(ovt, BF16), jnp.asarray(key_block, BF16)


def _nsa(proj, gate, cmp_kv, q_norm_g, k_norm_g, B, S, tq, tk):
    G = NSA_KV_GROUPS
    H = NSA_GROUP_HEADS
    n_cp = S // CMP_STRIDE
    wk = min(S, WINDOW + tq)
    nq = S // tq
    gw = H * HEAD_DIM
    ovt, key_block = _nsa_tables(S)
    kern = functools.partial(_nsa_kernel, S=S, tq=tq, tk=tk, wk=wk)
    kv_spec = lambda blk: pl.BlockSpec((S, HEAD_DIM), lambda b, g, i: (b, blk + g))
    return pl.pallas_call(
        kern,
        grid=(B, G, nq),
        in_specs=[
            pl.BlockSpec((tq, gw), lambda b, g, i: (b * nq + i, Q_COL0 // gw + g)),
            pl.BlockSpec((1, 1, 1, n_cp, HEAD_DIM), lambda b, g, i: (b, 0, g, 0, 0)),
            pl.BlockSpec((1, 1, 1, n_cp, HEAD_DIM), lambda b, g, i: (b, 1, g, 0, 0)),
            kv_spec(KS_BLK), kv_spec(VS_BLK), kv_spec(KW_BLK), kv_spec(VW_BLK),
            pl.BlockSpec((tq, LANES), lambda b, g, i: (b * nq + i, g)),
            pl.BlockSpec((1, HEAD_DIM), lambda b, g, i: (0, 0)),
            pl.BlockSpec((3, HEAD_DIM), lambda b, g, i: (0, 0)),
            pl.BlockSpec((LANES, n_cp), lambda b, g, i: (0, 0)),
            pl.BlockSpec((S, LANES), lambda b, g, i: (0, 0)),
        ],
        out_specs=pl.BlockSpec((tq, gw), lambda b, g, i: (b * nq + i, g)),
        out_shape=jax.ShapeDtypeStruct((B * S, NSA_Q_WIDTH), BF16),
        scratch_shapes=[
            pltpu.VMEM((S, 2 * HEAD_DIM), BF16), pltpu.VMEM((S, 2 * HEAD_DIM), BF16),
            pltpu.VMEM((S, HEAD_DIM), BF16), pltpu.VMEM((S, 2 * HEAD_DIM), BF16),
            pltpu.VMEM((H * tq, HEAD_DIM), F32), pltpu.VMEM((H * tq, HEAD_DIM), F32)],
        compiler_params=_params(("parallel", "parallel", "arbitrary")),
        name="nsa",
    )(proj, cmp_kv, cmp_kv, proj, proj, proj, proj, gate, q_norm_g, k_norm_g, ovt, key_block)


def _ret_kernel(rq_ref, rk_ref, rv_ref, rg_ref, cos_ref, sin_ref, dec_ref, xi_ref, zeta_ref, cd_ref,
                gg_ref, gb_ref, o_ref, r_scr):
    scale = HEAD_DIM ** -0.5

    @pl.when(pl.program_id(1) == 0)
    def _():
        r_scr[...] = jnp.zeros(r_scr.shape, F32)

    cos = cos_ref[...]
    sin = sin_ref[...]
    for h in range(RET_HEADS):
        sl = slice(h * HEAD_DIM, (h + 1) * HEAD_DIM)
        q = rq_ref[:, sl]
        k = rk_ref[:, sl]
        qf = q * cos + pltpu.roll(q, HEAD_DIM // 2, 1) * sin
        kf = (k * cos + pltpu.roll(k, HEAD_DIM // 2, 1) * sin) * scale
        v = rv_ref[:, sl].astype(BF16)
        qb = qf.astype(BF16)
        r_old = r_scr[h]
        a = _dot_nt(qb, kf.astype(BF16)) * dec_ref[h]
        o = _dot(a.astype(BF16), v) + _dot(qb, r_old.astype(BF16)) * xi_ref[h]
        r_scr[h] = r_old * cd_ref[h:h + 1, :] + _dot_tn((kf * zeta_ref[h]).astype(BF16), v)
        mu = jnp.mean(o, axis=-1, keepdims=True)
        d = o - mu
        var = jnp.mean(d * d, axis=-1, keepdims=True)
        y = d * lax.rsqrt(var + GN_EPS) * gg_ref[h:h + 1, :] + gb_ref[h:h + 1, :]
        gt = rg_ref[:, sl]
        o_ref[:, sl] = (gt * jax.nn.sigmoid(gt) * y).astype(BF16)


def _ret_tables(S):
    C, H, half = RET_CHUNK, RET_HEADS, HEAD_DIM // 2
    inv_freq = ROPE_BASE ** (-jnp.arange(half, dtype=F32) / half)
    ang = jnp.arange(S, dtype=F32)[:, None] * inv_freq[None, :]
    cos2 = jnp.concatenate([jnp.cos(ang), jnp.cos(ang)], axis=1)
    sin2 = jnp.concatenate([-jnp.sin(ang), jnp.sin(ang)], axis=1)
    log_g = jnp.log1p(-jnp.exp2(-5.0 - jnp.arange(H, dtype=F32)))
    n = jnp.arange(C, dtype=F32)
    diff = n[:, None] - n[None, :]
    dec = jnp.where(diff >= 0, jnp.exp(log_g[:, None, None] * jnp.maximum(diff, 0.0)), 0.0)
    lanes = lambda t: jnp.broadcast_to(t[..., None], t.shape + (HEAD_DIM,))
    xi = lanes(jnp.exp(log_g[:, None] * (n + 1.0)))
    zeta = lanes(jnp.exp(log_g[:, None] * (C - 1.0 - n)))
    cd = lanes(jnp.exp(log_g * C))
    return cos2, sin2, dec, xi, zeta, cd


def _retention(proj, gn_g, gn_b, B, S):
    C, H = RET_CHUNK, RET_HEADS
    n_ch = S // C
    cos2, sin2, dec, xi, zeta, cd = _ret_tables(S)
    spec = lambda k: pl.BlockSpec((C, RET_WIDTH), lambda b, c: (b * n_ch + c, k))
    whole = lambda a: pl.BlockSpec(a.shape, lambda b, c: (0,) * a.ndim)
    return pl.pallas_call(
        _ret_kernel,
        grid=(B, n_ch),
        in_specs=[
            spec(0), spec(1), spec(2), spec(3),
            pl.BlockSpec((C, HEAD_DIM), lambda b, c: (c, 0)),
            pl.BlockSpec((C, HEAD_DIM), lambda b, c: (c, 0)),
            whole(dec), whole(xi), whole(zeta), whole(cd), whole(gn_g), whole(gn_b),
        ],
        out_specs=pl.BlockSpec((C, RET_WIDTH), lambda b, c: (b * n_ch + c, 0)),
        out_shape=jax.ShapeDtypeStruct((B * S, RET_WIDTH), BF16),
        scratch_shapes=[pltpu.VMEM((H, HEAD_DIM, HEAD_DIM), F32)],
        compiler_params=_params(("parallel", "arbitrary")),
        name="retention",
    )(proj, proj, proj, proj, cos2, sin2, dec, xi, zeta, cd, gn_g, gn_b)


def _out_kernel(on_ref, or_ref, w_ref, x_ref, g2_ref, wr_ref, br_ref, tri_ref,
                x1_ref, h2_ref, rt_ref, cnt_ref, cnt_scr, ws_scr):
    @pl.when(pl.program_id(0) == 0)
    def _():
        cnt_scr[...] = jnp.zeros(cnt_scr.shape, F32)
        wr = wr_ref[...]
        w_hi = wr.astype(BF16)
        ws_scr[:, 0:LANES] = w_hi
        ws_scr[:, LANES:2 * LANES] = (wr - w_hi.astype(F32)).astype(BF16)

    half = on_ref.shape[1]
    acc = _dot(on_ref[...], w_ref[0:half, :]) + _dot(or_ref[...], w_ref[half:2 * half, :])
    x1 = x_ref[...] + acc
    x1_ref[...] = x1
    h2 = _rms(x1, g2_ref[...])
    h2_ref[...] = h2

    h_hi = h2.astype(BF16)
    h_lo = (h2 - h_hi.astype(F32)).astype(BF16)
    tm = h2.shape[0]
    prod = _dot(jnp.concatenate([h_hi, h_lo], axis=0), ws_scr[...])
    logits = (prod[0:tm, 0:LANES] + prod[tm:2 * tm, 0:LANES] + prod[0:tm, LANES:2 * LANES]
              + br_ref[...])

    tm = logits.shape[0]
    lt = logits.T
    G8 = EXPERTS_PER_GROUP
    row = lax.broadcasted_iota(jnp.int32, (G8, 1), 0).astype(F32)
    big = float(G8)
    col_max = lambda v: jnp.max(v, axis=0, keepdims=True)
    col_min = lambda v: jnp.min(v, axis=0, keepdims=True)
    col_sum = lambda v: jnp.sum(v, axis=0, keepdims=True)
    gm = row < N_GROUPS
    gl = jnp.where(gm, lt[0:G8], -jnp.inf)
    ge = jnp.where(gm, jnp.exp(gl - col_max(gl)), 0.0)
    pg = ge / col_sum(ge)
    g_top = col_max(pg)
    g_idx = col_min(jnp.where(gm & (pg == g_top), row, big))
    el = lt[EXP_ROW0:EXP_ROW0 + G8]
    for g in range(1, N_GROUPS):
        el = jnp.where(g_idx == float(g), lt[EXP_ROW0 + g * G8:EXP_ROW0 + (g + 1) * G8], el)
    ee = jnp.exp(el - col_max(el))
    pe = ee / col_sum(ee)
    t1 = col_max(pe)
    i1 = col_min(jnp.where(pe == t1, row, big))
    rest = row != i1
    t2 = col_max(jnp.where(rest, pe, -1.0))
    i2 = col_min(jnp.where(rest & (pe == t2), row, big))
    tsum = t1 + t2
    w1 = g_top * t1 / tsum
    w2 = g_top * t2 / tsum
    e1 = g_idx * float(G8) + i1
    e2 = g_idx * float(G8) + i2
    erow = lax.broadcasted_iota(jnp.int32, (N_EXPERTS, 1), 0).astype(F32)
    oh1 = jnp.where(erow == e1, 1.0, 0.0)
    oh2 = jnp.where(erow == e2, 1.0, 0.0)
    both = oh1 + oh2
    before = _dot(both.astype(BF16), tri_ref[...]) + cnt_scr[...]
    r1 = col_sum(before * oh1)
    r2 = col_sum(before * oh2)
    cnt_scr[...] = cnt_scr[...] + jnp.sum(both, axis=1, keepdims=True)
    cnt_ref[...] = jnp.broadcast_to(cnt_scr[...], cnt_ref.shape)
    rt_ref[...] = jnp.concatenate([e1, e2, w1, w2, r1, r2, jnp.zeros((2, tm), F32)], axis=0)


def _out_proj(o_nsa, o_ret, w_out, xt, g2, w_router, b_router, tm):
    T, D = xt.shape
    half = o_nsa.shape[1]
    tri = jnp.asarray(np.triu(np.ones((tm, tm), np.float32), 1), BF16)
    return pl.pallas_call(
        _out_kernel,
        grid=(T // tm,),
        in_specs=[
            pl.BlockSpec((tm, half), lambda m: (m, 0)),
            pl.BlockSpec((tm, half), lambda m: (m, 0)),
            pl.BlockSpec((2 * half, D), lambda m: (0, 0), pipeline_mode=pl.Buffered(1)),
            pl.BlockSpec((tm, D), lambda m: (m, 0)),
            pl.BlockSpec((1, D), lambda m: (0, 0)),
            pl.BlockSpec((D, LANES), lambda m: (0, 0)),
            pl.BlockSpec((1, LANES), lambda m: (0, 0)),
            pl.BlockSpec((tm, tm), lambda m: (0, 0)),
        ],
        out_specs=[
            pl.BlockSpec((tm, D), lambda m: (m, 0)),
            pl.BlockSpec((tm, D), lambda m: (m, 0)),
            pl.BlockSpec((8, tm), lambda m: (0, m)),
            pl.BlockSpec((N_EXPERTS, LANES), lambda m: (0, 0)),
        ],
        out_shape=[
            jax.ShapeDtypeStruct((T, D), F32),
            jax.ShapeDtypeStruct((T, D), F32),
            jax.ShapeDtypeStruct((8, T), F32),
            jax.ShapeDtypeStruct((N_EXPERTS, LANES), F32),
        ],
        scratch_shapes=[pltpu.VMEM((N_EXPERTS, 1), F32), pltpu.VMEM((D, 2 * LANES), BF16)],
        compiler_params=_params(("arbitrary",)),
        name="out_proj",
    )(o_nsa, o_ret, w_out, xt, g2, w_router, b_router, tri)


def _row_copy(src, row, dst, slot, sem):
    return pltpu.make_async_copy(src.at[pl.ds(row, 1), :], dst.at[pl.ds(slot, 1), :], sem)


def _dispatch_kernel(pos_ref, pad_ref, h2_ref, xs_hbm, zrow, sem, zsem, *, n_tok):
    i = pl.program_id(0)
    tm = h2_ref.shape[0]
    K = TOPK_IN_GROUP

    def issue(r, carry):
        for k in range(K):
            _row_copy(h2_ref, r, xs_hbm, pos_ref[k * n_tok + i * tm + r], sem).start()
        return carry

    lax.fori_loop(0, tm, issue, 0, unroll=8)

    @pl.when(i == pl.num_programs(0) - 1)
    def _():
        zrow[...] = jnp.zeros(zrow.shape, F32)
        for e in range(N_EXPERTS):
            lo, hi = pad_ref[e], pad_ref[N_EXPERTS + e]
            mid = jnp.minimum(lax.shift_left(lax.shift_right_logical(lo + 7, 3), 3), hi)

            def fill(r, carry):
                _row_copy(zrow, 0, xs_hbm, r, zsem).start()
                return carry

            lax.fori_loop(lo, mid, fill, 0)

            def fill_wait(r, carry):
                _row_copy(zrow, 0, xs_hbm, r, zsem).wait()
                return carry

            lax.fori_loop(lo, mid, fill_wait, 0)

            def piece(j):
                r8 = pl.multiple_of(mid + j * 8, 8)
                return pltpu.make_async_copy(zrow.at[pl.ds(0, 8), :], xs_hbm.at[pl.ds(r8, 8), :], zsem)

            n8 = lax.shift_right_logical(hi - mid, 3)

            def fill8(j, carry):
                piece(j).start()
                return carry

            lax.fori_loop(0, n8, fill8, 0)

            def fill8_wait(j, carry):
                piece(j).wait()
                return carry

            lax.fori_loop(0, n8, fill8_wait, 0)

        M = zrow.shape[0]

        def tail_copy(t):
            return pltpu.make_async_copy(zrow, xs_hbm.at[pl.ds(pl.multiple_of(t * M, M), M), :], zsem)

        def tail(t, carry):
            tail_copy(t).start()
            return carry

        lax.fori_loop(pad_ref[2 * N_EXPERTS], xs_hbm.shape[0] // M, tail, 0)

        def tail_wait(t, carry):
            tail_copy(t).wait()
            return carry

        lax.fori_loop(pad_ref[2 * N_EXPERTS], xs_hbm.shape[0] // M, tail_wait, 0)

    for k in range(K):
        pltpu.make_async_copy(h2_ref, xs_hbm.at[pl.ds(0, tm), :], sem).wait()


def _dispatch(pos, pad_rows, h2, n_rows, tm):
    T, D = h2.shape
    grid_spec = pltpu.PrefetchScalarGridSpec(
        num_scalar_prefetch=2,
        grid=(T // tm,),
        in_specs=[pl.BlockSpec((tm, D), lambda i, pos, pad: (i, 0))],
        out_specs=pl.BlockSpec(memory_space=pl.ANY),
        scratch_shapes=[pltpu.VMEM((MOE_BLOCK, D), F32), pltpu.SemaphoreType.DMA, pltpu.SemaphoreType.DMA],
    )
    return pl.pallas_call(
        functools.partial(_dispatch_kernel, n_tok=T),
        grid_spec=grid_spec,
        out_shape=jax.ShapeDtypeStruct((n_rows, D), F32),
        compiler_params=_params(("arbitrary",)),
        name="dispatch",
    )(pos, pad_rows, h2)


def _expert_kernel(ts_ref, xs_hbm, wg_ref, wu_ref, wd_ref, y_hbm,
                   xb0, xb1, yb0, yb1, wg_b, wu_b, wd_b, gsem, osem, *, n_blocks):
    e = pl.program_id(0)
    M = MOE_BLOCK
    t0 = ts_ref[e]
    t1 = ts_ref[e + 1]
    n_used = ts_ref[N_EXPERTS]
    xbufs = (xb0, xb1)
    ybufs = (yb0, yb1)

    def rows(t):
        return pl.ds(pl.multiple_of(t * M, M), M)

    def in_copy(t, p):
        return pltpu.make_async_copy(xs_hbm.at[rows(t), :], xbufs[p], gsem.at[p])

    def out_copy(t, p):
        return pltpu.make_async_copy(ybufs[p], y_hbm.at[rows(t), :], osem.at[p])

    @pl.when(e == 0)
    def _():
        in_copy(0, 0).start()

    @pl.when(t1 > t0)
    def _():
        wg_b[...] = wg_ref[0].astype(BF16)
        wu_b[...] = wu_ref[0].astype(BF16)
        wd_b[...] = wd_ref[0].astype(BF16)

    def tile_body(t, p):
        in_copy(t, p).wait()

        @pl.when(t + 1 < n_used)
        def _():
            in_copy(t + 1, 1 - p).start()

        @pl.when(t >= 2)
        def _():
            out_copy(t - 2, p).wait()

        xb = xbufs[p][...].astype(BF16)
        hg = _dot(xb, wg_b[...])
        hu = _dot(xb, wu_b[...])
        hb = (hg * jax.nn.sigmoid(hg) * hu).astype(BF16)
        ybufs[p][...] = _dot(hb, wd_b[...])
        out_copy(t, p).start()

    def tile(t, carry):
        parity = lax.rem(t, 2)
        for p in range(2):
            pl.when(parity == p)(functools.partial(tile_body, t, p))
        return carry

    lax.fori_loop(t0, t1, tile, 0)

    @pl.when(e == N_EXPERTS - 1)
    def _():
        parity = lax.rem(n_used, 2)
        for p in range(2):
            @pl.when(parity == p)
            def _(p=p):
                out_copy(n_used - 1, 1 - p).wait()

                @pl.when(n_used >= 2)
                def _():
                    out_copy(n_used - 2, p).wait()

        yb0[...] = jnp.zeros(yb0.shape, F32)

        def clear(t, carry):
            out_copy(t, 0).start()
            return carry

        lax.fori_loop(n_used, n_blocks, clear, 0)

        def clear_wait(t, carry):
            out_copy(t, 0).wait()
            return carry

        lax.fori_loop(n_used, n_blocks, clear_wait, 0)


def _experts(xs, tile_start, w_gate, w_up, w_down):
    n_rows, D = xs.shape
    n_blocks = n_rows // MOE_BLOCK
    grid_spec = pltpu.PrefetchScalarGridSpec(
        num_scalar_prefetch=1,
        grid=(N_EXPERTS,),
        in_specs=[
            pl.BlockSpec(memory_space=pl.ANY),
            pl.BlockSpec((1, D, D_EXPERT), lambda e, ts: (e, 0, 0)),
            pl.BlockSpec((1, D, D_EXPERT), lambda e, ts: (e, 0, 0)),
            pl.BlockSpec((1, D_EXPERT, D), lambda e, ts: (e, 0, 0)),
        ],
        out_specs=pl.BlockSpec(memory_space=pl.ANY),
        scratch_shapes=[pltpu.VMEM((MOE_BLOCK, D), F32)] * 4 + [
            pltpu.VMEM((D, D_EXPERT), BF16), pltpu.VMEM((D, D_EXPERT), BF16),
            pltpu.VMEM((D_EXPERT, D), BF16),
            pltpu.SemaphoreType.DMA((2,)), pltpu.SemaphoreType.DMA((2,)),
        ],
    )
    return pl.pallas_call(
        functools.partial(_expert_kernel, n_blocks=n_blocks),
        grid_spec=grid_spec,
        out_shape=jax.ShapeDtypeStruct((n_rows, D), F32),
        compiler_params=_params(("arbitrary",)),
        name="experts",
    )(tile_start, xs, w_gate, w_up, w_down)


def _combine_kernel(pos_ref, y_hbm, x1_ref, rt_ref, o_ref, ybuf, sems, *, n_tok):
    i = pl.program_id(0)
    n = pl.num_programs(0)
    tm = x1_ref.shape[0]
    K = TOPK_IN_GROUP
    slot = lax.rem(i, 2)

    def issue_tile(t, s):
        def issue(r, carry):
            for k in range(K):
                _row_copy(y_hbm, pos_ref[k * n_tok + t * tm + r], ybuf.at[s, k], r,
                          sems.at[s]).start(priority=k)
            return carry

        lax.fori_loop(0, tm, issue, 0, unroll=8)

    @pl.when(i == 0)
    def _():
        issue_tile(0, 0)

    @pl.when(i + 1 < n)
    def _():
        issue_tile(i + 1, 1 - slot)

    for k in range(K):
        pltpu.make_async_copy(y_hbm.at[pl.ds(0, tm), :], ybuf.at[slot, k], sems.at[slot]).wait()
    w = rt_ref[...].T
    yb = ybuf[slot]
    o_ref[...] = x1_ref[...] + (yb[0] * w[:, K:K + 1] + yb[1] * w[:, K + 1:K + 2])


def _combine(pos, y_rows, x1, rt, tm):
    T, D = x1.shape
    grid_spec = pltpu.PrefetchScalarGridSpec(
        num_scalar_prefetch=1,
        grid=(T // tm,),
        in_specs=[
            pl.BlockSpec(memory_space=pl.ANY),
            pl.BlockSpec((tm, D), lambda i, pos: (i, 0)),
            pl.BlockSpec((8, tm), lambda i, pos: (0, i)),
        ],
        out_specs=pl.BlockSpec((tm, D), lambda i, pos: (i, 0)),
        scratch_shapes=[pltpu.VMEM((2, TOPK_IN_GROUP, tm, D), F32), pltpu.SemaphoreType.DMA((2,))],
    )
    return pl.pallas_call(
        functools.partial(_combine_kernel, n_tok=T),
        grid_spec=grid_spec,
        out_shape=jax.ShapeDtypeStruct((T, D), F32),
        compiler_params=_params(("arbitrary",)),
        name="combine",
    )(pos, y_rows, x1, rt)


def _block_layout(rt, counts, T):
    K = TOPK_IN_GROUP
    n_rows = (K * T + N_EXPERTS * (MOE_BLOCK - 1) + MOE_BLOCK - 1) // MOE_BLOCK * MOE_BLOCK
    n_blocks = n_rows // MOE_BLOCK
    cnt = counts[:, 0].astype(jnp.int32)
    padded = (cnt + MOE_BLOCK - 1) // MOE_BLOCK * MOE_BLOCK
    pad_end = jnp.cumsum(padded)
    pad_start = pad_end - padded
    e = rt[0:K].astype(jnp.int32).reshape(-1)
    rank = rt[2 * K:3 * K].astype(jnp.int32).reshape(-1)
    hit = e[:, None] == jnp.arange(N_EXPERTS, dtype=jnp.int32)[None, :]
    pos = jnp.sum(jnp.where(hit, pad_start[None, :].astype(jnp.int32), 0), axis=1) + rank
    tile_start = (jnp.concatenate([pad_start, pad_end[-1:]]) // MOE_BLOCK).astype(jnp.int32)
    pad_rows = jnp.concatenate([pad_start + cnt, pad_end, pad_end[-1:] // MOE_BLOCK]).astype(jnp.int32)
    return tile_start, pos, pad_rows, n_rows


def _layer(x, norm1_g, w_in, cmp_pos, cmp_w1, cmp_w2, q_norm_g, k_norm_g, ret_gn_g, ret_gn_b,
           w_out, norm2_g, w_rg, b_rg, w_re, b_re, w_eg, w_eu, w_ed, tiles):
    B, S, D = x.shape
    T = B * S
    xt = x.reshape(T, D)

    w_main = jnp.concatenate([w_in[:, GATE_COL0 + NSA_GATE_WIDTH:], w_in[:, :GATE_COL0]], axis=1).astype(BF16)
    gpg = NSA_GATE_WIDTH // NSA_KV_GROUPS
    w_gate = jnp.concatenate(
        [jnp.pad(w_in[:, GATE_COL0 + g * gpg:GATE_COL0 + (g + 1) * gpg], ((0, 0), (0, LANES - gpg)))
         for g in range(NSA_KV_GROUPS)], axis=1).astype(BF16)
    proj, gate = _in_proj(xt, norm1_g.reshape(1, D), w_main, w_gate, tiles["tm_in"], tiles["tn_in"])

    cmp_kv = _compress(proj, cmp_pos, cmp_w1, cmp_w2, k_norm_g, B, S)
    o_nsa = _nsa(proj, gate, cmp_kv, q_norm_g.reshape(1, HEAD_DIM), k_norm_g, B, S, tiles["tq"], tiles["tk"])

    o_ret = _retention(proj, ret_gn_g, ret_gn_b, B, S)

    gap, tail = EXP_ROW0 - N_GROUPS, LANES - EXP_ROW0 - N_EXPERTS
    w_router = jnp.concatenate([w_rg, jnp.zeros((D, gap), F32), w_re, jnp.zeros((D, tail), F32)], axis=1)
    b_router = jnp.concatenate([b_rg, jnp.zeros((gap,), F32), b_re, jnp.zeros((tail,), F32)]).reshape(1, LANES)
    x1, h2, rt, counts = _out_proj(o_nsa, o_ret, w_out.astype(BF16), xt, norm2_g.reshape(1, D),
                                   w_router, b_router, tiles["tm_out"])

    tile_start, pos, pad_rows, n_rows = _block_layout(rt, counts, T)
    xs = _dispatch(pos, pad_rows, h2, n_rows, tiles["tm_dsp"])
    y_rows = _experts(xs, tile_start, w_eg, w_eu, w_ed)
    out = _combine(pos, y_rows, x1, rt, tiles["tm_cmb"])
    return out.reshape(B, S, D)


def _tiles(T, S):
    return {
        "tm_in": min(2048, T), "tn_in": 512,
        "tq": min(256, S), "tk": min(256, S),
        "tm_out": min(512, T),
        "tm_dsp": min(1024, T),
        "tm_cmb": min(256, T),
    }


def kernel(x, norm1_g, w_in, cmp_pos, cmp_w1, cmp_w2, q_norm_g, k_norm_g, ret_gn_g, ret_gn_b, w_out, norm2_g, w_router_group, b_router_group, w_router_expert, b_router_expert, w_exp_gate, w_exp_up, w_exp_down):
    B, S, _ = x.shape
    tiles = _tiles(B * S, S)
    for l in range(norm1_g.shape[0]):
        x = _layer(x, norm1_g[l], w_in[l], cmp_pos[l], cmp_w1[l], cmp_w2[l], q_norm_g[l], k_norm_g[l],
                   ret_gn_g[l], ret_gn_b[l], w_out[l], norm2_g[l], w_router_group[l], b_router_group[l],
                   w_router_expert[l], b_router_expert[l], w_exp_gate[l], w_exp_up[l], w_exp_down[l], tiles)
    return x
```
